```python
import jax, jax.numpy as jnp
from jax import lax
import numpy as np

D_MODEL = 1024
BATCH = 2
SEQ = 8192
DEPTH = 2

H_A = 8
NOPE = 64
ROPE_D = 32
V_A = 64
Q_LORA = 256
KV_LORA = 128
W_A = H_A * V_A
H_B = 8
HD_B = 64
W_B = H_B * HD_B
H_C = 8
HD_C = 64
W_C = H_C * HD_C
DECAY_LORA = 64
AAA_LORA = 64
MV_LORA = 32
SHIFT_COLS = 3 * W_C + DECAY_LORA + AAA_LORA
ROPE_THETA = 10000.0
Q_BLOCK = 128
RWKV_DECAY_SCALE = 0.606531
GN_EPS = 64e-5
EPS = 1e-6
NEG_INF = -1e30

IN_SIZES = (Q_LORA, KV_LORA, ROPE_D, W_A,
            W_B, W_B, W_B, H_B, W_B,
            SHIFT_COLS, W_C,
            3 * D_MODEL)
IN_COLS = sum(IN_SIZES)
RWKV_SIZES = (W_C, W_C, W_C, DECAY_LORA, AAA_LORA)

kernel_name = "hybrid_mla_fox_rwkv7_gated_merge"


def rmsnorm(x, g, eps=EPS):
    xf = x.astype(jnp.float32)
    y = xf * lax.rsqrt(jnp.mean(xf * xf, axis=-1, keepdims=True) + eps)
    return (y * g.astype(jnp.float32)).astype(x.dtype)


def split_cols(p, sizes):
    idx = np.cumsum(np.array(sizes))[:-1].tolist()
    return jnp.split(p, idx, axis=-1)


def rope_tables(seq):
    inv = ROPE_THETA ** (-jnp.arange(0, ROPE_D, 2, dtype=jnp.float32) / ROPE_D)
    ang = jnp.arange(seq, dtype=jnp.float32)[:, None] * inv[None, :]
    return jnp.cos(ang)[:, None, :], jnp.sin(ang)[:, None, :]


def apply_rope(x, cos, sin):
    half = ROPE_D // 2
    x1 = x[..., :half].astype(jnp.float32)
    x2 = x[..., half:].astype(jnp.float32)
    out = jnp.concatenate([x1 * cos - x2 * sin, x2 * cos + x1 * sin], axis=-1)
    return out.astype(x.dtype)


def causal_block_attention(q, k, v, cum=None):
    B, H, S, Dk = q.shape
    Dv = v.shape[-1]
    nb = S // Q_BLOCK
    scale = Dk ** -0.5
    kpos = jnp.arange(S)
    q_blocks = q.reshape(B, H, nb, Q_BLOCK, Dk).transpose(2, 0, 1, 3, 4)
    use_decay = cum is not None
    xs = (jnp.arange(nb), q_blocks)
    if use_decay:
        xs = xs + (cum.reshape(B, H, nb, Q_BLOCK).transpose(2, 0, 1, 3),)

    def block(args):
        i, qb = args[0], args[1]
        s = jnp.einsum('bhqd,bhkd->bhqk', qb, k).astype(jnp.float32) * scale
        if use_decay:
            s = s + args[2][..., :, None] - cum[:, :, None, :]
        qpos = i * Q_BLOCK + jnp.arange(Q_BLOCK)
        s = jnp.where(kpos[None, :] <= qpos[:, None], s, NEG_INF)
        p = jax.nn.softmax(s, axis=-1)
        return jnp.einsum('bhqk,bhkd->bhqd', p.astype(v.dtype), v)

    o = lax.map(block, xs)
    return o.transpose(1, 0, 3, 2, 4).reshape(B, S, H * Dv)


def mla_branch(c_q, c_kv, k_r, gate, qa_g, w_uq, kva_g, w_ukv, q_g, knope_g, krope_g, cos, sin):
    B, S, _ = c_q.shape
    q = (rmsnorm(c_q, qa_g) @ w_uq).reshape(B, S, H_A, NOPE + ROPE_D)
    q = rmsnorm(q, q_g)
    q = jnp.concatenate([q[..., :NOPE], apply_rope(q[..., NOPE:], cos, sin)], axis=-1)
    kv = (rmsnorm(c_kv, kva_g) @ w_ukv).reshape(B, S, H_A, NOPE + V_A)
    k_nope = rmsnorm(kv[..., :NOPE], knope_g)
    v = kv[..., NOPE:]
    k_rope = apply_rope(rmsnorm(k_r, krope_g)[:, :, None, :], cos, sin)
    k = jnp.concatenate([k_nope, jnp.broadcast_to(k_rope, (B, S, H_A, ROPE_D))], axis=-1)
    o = causal_block_attention(q.transpose(0, 2, 1, 3), k.transpose(0, 2, 1, 3), v.transpose(0, 2, 1, 3))
    return o * jax.nn.silu(gate)


def fox_branch(fq, fk, fv, ff, gate, b_f, q_g, k_g):
    B, S, _ = fq.shape
    q = rmsnorm(fq.reshape(B, S, H_B, HD_B), q_g)
    k = rmsnorm(fk.reshape(B, S, H_B, HD_B), k_g)
    v = fv.reshape(B, S, H_B, HD_B)
    log_f = jax.nn.log_sigmoid((ff + b_f).astype(jnp.float32))
    cum = jnp.cumsum(log_f, axis=1).transpose(0, 2, 1)
    o = causal_block_attention(q.transpose(0, 2, 1, 3), k.transpose(0, 2, 1, 3), v.transpose(0, 2, 1, 3), cum)
    return o * jax.nn.silu(gate)


def rwkv7_scan(r, w, k, v, kk, a):
    B, S, H, N = r.shape

    def step(state, inp):
        r_t, w_t, k_t, v_t, kk_t, a_t = inp
        sa = jnp.einsum('bhvk,bhk->bhv', state, -kk_t)
        state = (state * w_t[:, :, None, :] + sa[..., None] * (kk_t * a_t)[:, :, None, :]
                 + v_t[..., None] * k_t[:, :, None, :])
        return state, jnp.einsum('bhvk,bhk->bhv', state, r_t)

    xs = tuple(t.transpose(1, 0, 2, 3) for t in (r, w, k, v, kk, a))
    _, y = lax.scan(step, jnp.zeros((B, H, N, N), jnp.float32), xs)
    return y.transpose(1, 0, 2, 3)


def rwkv_branch(r, k, v, wl, al, gate, w0, w_up, a0, a_up, k_k, k_a, r_k, lnx_g, lnx_b):
    B, S, _ = r.shape
    f32 = jnp.float32
    heads = lambda t: t.astype(f32).reshape(B, S, H_C, HD_C)
    w = jnp.exp(-RWKV_DECAY_SCALE * jax.nn.sigmoid((w0 + jnp.tanh(wl) @ w_up).astype(f32)))
    a = jax.nn.sigmoid((a0 + al @ a_up).astype(f32))
    kk = heads(k * k_k)
    kk = kk / jnp.maximum(jnp.sqrt(jnp.sum(kk * kk, axis=-1, keepdims=True)), 1e-12)
    k_mod = k.astype(f32) * (1.0 + (a - 1.0) * k_a.astype(f32))
    r_h, w_h, k_h, v_h, a_h = heads(r), heads(w), heads(k_mod), heads(v), heads(a)
    y = rwkv7_scan(r_h, w_h, k_h, v_h, kk, a_h)
    mu = jnp.mean(y, axis=-1, keepdims=True)
    var = jnp.mean(jnp.square(y - mu), axis=-1, keepdims=True)
    y = ((y - mu) * lax.rsqrt(var + GN_EPS)).reshape(B, S, W_C) * lnx_g.astype(f32) + lnx_b.astype(f32)
    bonus = jnp.sum(r_h * k_h * r_k.astype(f32), axis=-1, keepdims=True) * v_h
    out = (y + bonus.reshape(B, S, W_C)) * jax.nn.silu(gate.astype(f32))
    return out.astype(gate.dtype)


def setup_inputs(seed: int = 0) -> dict:
    key = jax.random.key(seed)
    ks = iter(jax.random.split(key, 48))
    f32 = jnp.float32
    L, Lv = DEPTH, DEPTH - 1

    def nrm(shape, scale):
        return scale * jax.random.normal(next(ks), shape, f32)

    def gain(shape):
        return 1.0 + nrm(shape, 0.02)

    x = jax.random.normal(next(ks), (BATCH, SEQ, D_MODEL), f32)
    return {
        "x": x,
        "norm_g": gain((L, D_MODEL)),
        "w_in": nrm((L, D_MODEL, IN_COLS), D_MODEL ** -0.5),
        "mla_qa_g": gain((L, Q_LORA)),
        "mla_w_uq": nrm((L, Q_LORA, H_A * (NOPE + ROPE_D)), Q_LORA ** -0.5),
        "mla_kva_g": gain((L, KV_LORA)),
        "mla_w_ukv": nrm((L, KV_LORA, H_A * (NOPE + V_A)), KV_LORA ** -0.5),
        "mla_q_g": gain((L, NOPE + ROPE_D)),
        "mla_knope_g": gain((L, NOPE)),
        "mla_krope_g": gain((L, ROPE_D)),
        "fox_b_f": jnp.broadcast_to(jnp.linspace(1.0, 6.0, H_B, dtype=f32), (L, H_B)) + nrm((L, H_B), 0.1),
        "fox_q_g": gain((L, HD_B)),
        "fox_k_g": gain((L, HD_B)),
        "rwkv_mu": jax.random.uniform(next(ks), (L, SHIFT_COLS), f32),
        "rwkv_w0": nrm((L, W_C), 1.0),
        "rwkv_w_up": nrm((L, DECAY_LORA, W_C), 0.5 * DECAY_LORA ** -0.5),
        "rwkv_a0": nrm((L, W_C), 0.5),
        "rwkv_a_up": nrm((L, AAA_LORA, W_C), 0.5 * AAA_LORA ** -0.5),
        "rwkv_k_k": 0.85 + nrm((L, W_C), 0.1),
        "rwkv_k_a": 1.0 + nrm((L, W_C), 0.1),
        "rwkv_r_k": nrm((L, H_C, HD_C), 0.1),
        "rwkv_lnx_g": gain((L, W_C)),
        "rwkv_lnx_b": nrm((L, W_C), 0.02),
        "rwkv_v0": nrm((Lv, W_C), 0.5),
        "rwkv_v_down": nrm((Lv, W_C, MV_LORA), W_C ** -0.5),
        "rwkv_v_up": nrm((Lv, MV_LORA, W_C), 0.5 * MV_LORA ** -0.5),
        "w_pa": nrm((L, W_A, D_MODEL), W_A ** -0.5),
        "w_pb": nrm((L, W_B, D_MODEL), W_B ** -0.5),
        "w_pc": nrm((L, W_C, D_MODEL), W_C ** -0.5),
        "w_out": nrm((L, D_MODEL, D_MODEL), D_MODEL ** -0.5),
    }


def reference(x, norm_g, w_in, mla_qa_g, mla_w_uq, mla_kva_g, mla_w_ukv, mla_q_g, mla_knope_g,
              mla_krope_g, fox_b_f, fox_q_g, fox_k_g, rwkv_mu, rwkv_w0, rwkv_w_up, rwkv_a0, rwkv_a_up,
              rwkv_k_k, rwkv_k_a, rwkv_r_k, rwkv_lnx_g, rwkv_lnx_b, rwkv_v0, rwkv_v_down, rwkv_v_up,
              w_pa, w_pb, w_pc, w_out):
    S = x.shape[1]
    cos, sin = rope_tables(S)
    v_first = None
    for l in range(DEPTH):
        h = rmsnorm(x, norm_g[l])
        p = h @ w_in[l]
        (c_q, c_kv, k_r, gate_a, fq, fk, fv, ff, gate_b,
         shift_cols, gate_c, merge_cols) = split_cols(p, IN_SIZES)

        o_a = mla_branch(c_q, c_kv, k_r, gate_a, mla_qa_g[l], mla_w_uq[l], mla_kva_g[l], mla_w_ukv[l],
                         mla_q_g[l], mla_knope_g[l], mla_krope_g[l], cos, sin)
        o_b = fox_branch(fq, fk, fv, ff, gate_b, fox_b_f[l], fox_q_g[l], fox_k_g[l])
        prev = jnp.pad(shift_cols, ((0, 0), (1, 0), (0, 0)))[:, :-1]
        shifted = shift_cols + rwkv_mu[l] * (prev - shift_cols)
        r, k, v, wl, al = split_cols(shifted, RWKV_SIZES)
        if l == 0:
            v_first = v
        else:
            nu = jax.nn.sigmoid(rwkv_v0[l - 1] + (v @ rwkv_v_down[l - 1]) @ rwkv_v_up[l - 1])
            v = v + (v_first - v) * nu
        o_c = rwkv_branch(r, k, v, wl, al, gate_c, rwkv_w0[l], rwkv_w_up[l], rwkv_a0[l], rwkv_a_up[l],
                          rwkv_k_k[l], rwkv_k_a[l], rwkv_r_k[l], rwkv_lnx_g[l], rwkv_lnx_b[l])

        g_a, g_b, g_c = jnp.split(merge_cols, 3, axis=-1)
        merged = (jax.nn.sigmoid(g_a) * (o_a @ w_pa[l])
                  + jax.nn.sigmoid(g_b) * (o_b @ w_pb[l])
                  + jax.nn.sigmoid(g_c) * (o_c @ w_pc[l]))
        x = x + (merged @ w_out[l]).astype(x.dtype)
    return x
```

```python
import functools
import math

import jax
import jax.numpy as jnp
import numpy as np
from jax import lax
from jax.experimental import pallas as pl
from jax.experimental.pallas import tpu as pltpu

F32 = jnp.float32
BF16 = jnp.bfloat16

LANES = 128
H = 8
HD = 64
NOPE = 64
ROPE_D = 32
Q_LORA = 256
KV_LORA = 128
D_MODEL = 1024
W_BR = H * HD
DECAY_LORA = 64
AAA_LORA = 64
MV_LORA = 32
ROPE_THETA = 10000.0
RWKV_DECAY_SCALE = 0.606531
GN_EPS = 64e-5
EPS = 1e-6
NEG_INF = -1e30
CHUNK = 64

COL_MERGE = 0
COL_GATE_A = 3072
COL_FQ = 3584
COL_FK = 4096
COL_FV = 4608
COL_GATE_B = 5120
COL_SR = 5632
COL_SK = 6144
COL_SV = 6656
COL_GATE_C = 7168
COL_CQ = 7680
COL_CKV = 7936
COL_KR = 8064
COL_FF = 8192
COL_WA = 8320
NCOLS = 8448
INPROJ_TN = 1408

VMEM_LIMIT = 56 * 1024 * 1024


def _cparams(sem):
    return pltpu.CompilerParams(dimension_semantics=sem, vmem_limit_bytes=VMEM_LIMIT)


def _sigmoid(x):
    return 1.0 / (1.0 + jnp.exp(-x))


def _dot(a, b):
    return jnp.dot(a, b, preferred_element_type=F32)


def _split3(x):
    hi = x.astype(BF16)
    r1 = x - hi.astype(F32)
    mid = r1.astype(BF16)
    lo = (r1 - mid.astype(F32)).astype(BF16)
    return hi, mid, lo


def _dot_exact_lhs(m_bf16, x):
    hi, mid, lo = _split3(x)
    return _dot(m_bf16, hi) + _dot(m_bf16, mid) + _dot(m_bf16, lo)


def _pair_seg_sum(x):
    lane = lax.broadcasted_iota(jnp.int32, x.shape, 1)
    first = lane < HD
    s0 = jnp.sum(jnp.where(first, x, 0.0), axis=-1, keepdims=True)
    s1 = jnp.sum(jnp.where(first, 0.0, x), axis=-1, keepdims=True)
    return jnp.where(first, s0, s1)


def _inproj_kernel(x_ref, g_ref, w_ref, o_ref, h_ref):
    @pl.when(pl.program_id(1) == 0)
    def _():
        x = x_ref[...]
        ms = jnp.mean(x * x, axis=-1, keepdims=True)
        h_ref[...] = (x * lax.rsqrt(ms + EPS) * g_ref[...]).astype(BF16)

    o_ref[...] = _dot(h_ref[...], w_ref[...])


def _inproj(x2, g, w, tm):
    n = x2.shape[0]
    return pl.pallas_call(
        _inproj_kernel,
        grid=(n // tm, NCOLS // INPROJ_TN),
        in_specs=[
            pl.BlockSpec((tm, D_MODEL), lambda i, j: (i, 0)),
            pl.BlockSpec((1, D_MODEL), lambda i, j: (0, 0)),
            pl.BlockSpec((D_MODEL, INPROJ_TN), lambda i, j: (0, j)),
        ],
        out_specs=pl.BlockSpec((tm, INPROJ_TN), lambda i, j: (i, j)),
        out_shape=jax.ShapeDtypeStruct((n, NCOLS), F32),
        scratch_shapes=[pltpu.VMEM((tm, D_MODEL), BF16)],
        compiler_params=_cparams(("parallel", "arbitrary")),
        name="inproj",
    )(x2, g, w)


def _rms(x, g, n):
    ms = jnp.sum(x * x, axis=-1, keepdims=True) / n
    return x * lax.rsqrt(ms + EPS) * g


def _rope(x, c, s1, s2):
    return x * c + pltpu.roll(x, LANES - ROPE_D // 2, 1) * s1 + pltpu.roll(x, ROPE_D // 2, 1) * s2


def _mla_prep_kernel(cq_ref, ckv_ref, kr_ref, c_ref, s1_ref, s2_ref, qag_ref, wuq_ref, kvag_ref, wk_ref, wv_ref,
                     qg_ref, kng_ref, krg_ref, q_out, k_out, v_out):
    c, s1, s2 = c_ref[...], s1_ref[...], s2_ref[...]
    cqn = _rms(cq_ref[...], qag_ref[...], Q_LORA).astype(BF16)
    ckvn = _rms(ckv_ref[...], kvag_ref[...], KV_LORA).astype(BF16)
    q_all = _dot(cqn, wuq_ref[...])
    k_all = _dot(ckvn, wk_ref[...])
    v_out[...] = _dot(ckvn, wv_ref[...]).astype(BF16)
    k_rope = _rope(_rms(kr_ref[...], krg_ref[...], ROPE_D), c, s1, s2)
    qg, kng = qg_ref[...], kng_ref[...]
    for h in range(H):
        sl = slice(h * LANES, (h + 1) * LANES)
        q_out[:, sl] = _rope(_rms(q_all[:, sl], qg, NOPE + ROPE_D), c, s1, s2).astype(BF16)
        k_out[:, sl] = (_rms(k_all[:, sl], kng, NOPE) + k_rope).astype(BF16)


def _mla_prep(p, tabs, wts, seq, t):
    n = p.shape[0]
    ns = seq // t
    row = lambda i: (i, 0)
    const = lambda i: (0, 0)
    tab_spec = pl.BlockSpec((t, LANES), lambda i: (i % ns, 0))
    return pl.pallas_call(
        _mla_prep_kernel,
        grid=(n // t,),
        in_specs=[
            pl.BlockSpec((t, Q_LORA), lambda i: (i, COL_CQ // Q_LORA)),
            pl.BlockSpec((t, LANES), lambda i: (i, COL_CKV // LANES)),
            pl.BlockSpec((t, LANES), lambda i: (i, COL_KR // LANES)),
            tab_spec, tab_spec, tab_spec,
            pl.BlockSpec((1, Q_LORA), const),
            pl.BlockSpec((Q_LORA, H * LANES), const),
            pl.BlockSpec((1, KV_LORA), const),
            pl.BlockSpec((KV_LORA, H * LANES), const),
            pl.BlockSpec((KV_LORA, W_BR), const),
            pl.BlockSpec((1, LANES), const),
            pl.BlockSpec((1, LANES), const),
            pl.BlockSpec((1, LANES), const),
        ],
        out_specs=[
            pl.BlockSpec((t, H * LANES), row),
            pl.BlockSpec((t, H * LANES), row),
            pl.BlockSpec((t, W_BR), row),
        ],
        out_shape=[
            jax.ShapeDtypeStruct((n, H * LANES), BF16),
            jax.ShapeDtypeStruct((n, H * LANES), BF16),
            jax.ShapeDtypeStruct((n, W_BR), BF16),
        ],
        compiler_params=_cparams(("parallel",)),
        name="mla_prep",
    )(p, p, p, *tabs, *wts)


def _fox_prep_kernel(fq_ref, fk_ref, fv_ref, ff_ref, bf_ref, qg_ref, kg_ref, tri_ref, selq_ref, selk_ref,
                     oneq_ref, onek_ref, q_out, k_out, v_out, carry_ref):
    t = fq_ref.shape[0]

    @pl.when(pl.program_id(1) == 0)
    def _():
        carry_ref[...] = jnp.zeros_like(carry_ref)

    z = ff_ref[...] + bf_ref[...]
    logf = jnp.minimum(z, 0.0) - jnp.log1p(jnp.exp(-jnp.abs(z)))
    cum = carry_ref[...] + _dot_exact_lhs(tri_ref[...], logf)
    carry_ref[...] = cum[t - 1:t, :]
    hi, mid, lo = _split3(cum)
    aug_q = _dot(hi, selq_ref[0]) + _dot(mid, selq_ref[1]) + _dot(lo, selq_ref[2]) + oneq_ref[...]
    aug_k = _dot(hi, selk_ref[0]) + _dot(mid, selk_ref[1]) + _dot(lo, selk_ref[2]) + onek_ref[...]
    v_out[...] = fv_ref[...].astype(BF16)
    lane = lax.broadcasted_iota(jnp.int32, (t, LANES), 1)
    first = lane < HD
    for src_ref, g_ref, aug, dst in ((fq_ref, qg_ref, aug_q, q_out), (fk_ref, kg_ref, aug_k, k_out)):
        for p in range(H // 2):
            sl = slice(p * LANES, (p + 1) * LANES)
            x = src_ref[:, sl]
            ms = _pair_seg_sum(x * x) / HD
            xn = x * lax.rsqrt(ms + EPS) * g_ref[:, sl]
            he, ho = 2 * p, 2 * p + 1
            dst[:, he * LANES:(he + 1) * LANES] = jnp.where(first, xn, aug[:, he * LANES:(he + 1) * LANES]).astype(BF16)
            dst[:, ho * LANES:(ho + 1) * LANES] = jnp.where(
                first, pltpu.roll(xn, HD, 1), aug[:, ho * LANES:(ho + 1) * LANES]).astype(BF16)


def _fox_prep(p, wts, batch, seq, t):
    n = p.shape[0]
    ns = seq // t
    row = lambda b, i: (b * ns + i, 0)
    const2 = lambda b, i: (0, 0)
    const3 = lambda b, i: (0, 0, 0)
    return pl.pallas_call(
        _fox_prep_kernel,
        grid=(batch, ns),
        in_specs=[
            pl.BlockSpec((t, W_BR), lambda b, i: (b * ns + i, COL_FQ // W_BR)),
            pl.BlockSpec((t, W_BR), lambda b, i: (b * ns + i, COL_FK // W_BR)),
            pl.BlockSpec((t, W_BR), lambda b, i: (b * ns + i, COL_FV // W_BR)),
            pl.BlockSpec((t, LANES), lambda b, i: (b * ns + i, COL_FF // LANES)),
            pl.BlockSpec((1, LANES), const2),
            pl.BlockSpec((1, W_BR), const2),
            pl.BlockSpec((1, W_BR), const2),
            pl.BlockSpec((t, t), const2),
            pl.BlockSpec((3, LANES, H * LANES), const3),
            pl.BlockSpec((3, LANES, H * LANES), const3),
            pl.BlockSpec((1, H * LANES), const2),
            pl.BlockSpec((1, H * LANES), const2),
        ],
        out_specs=[
            pl.BlockSpec((t, H * LANES), row),
            pl.BlockSpec((t, H * LANES), row),
            pl.BlockSpec((t, W_BR), row),
        ],
        out_shape=[
            jax.ShapeDtypeStruct((n, H * LANES), BF16),
            jax.ShapeDtypeStruct((n, H * LANES), BF16),
            jax.ShapeDtypeStruct((n, W_BR), BF16),
        ],
        scratch_shapes=[pltpu.VMEM((1, LANES), F32)],
        compiler_params=_cparams(("parallel", "arbitrary")),
        name="fox_prep",
    )(p, p, p, p, *wts)


def _attn_kernel(q_ref, k_ref, v_ref, g_ref, o_ref, m_ref, l_ref, acc_ref, *, tq):
    qi = pl.program_id(2)
    nt = (((1,), (1,)), ((), ()))
    row = lax.broadcasted_iota(jnp.int32, (tq, tq), 0)
    col = lax.broadcasted_iota(jnp.int32, (tq, tq), 1)
    causal = col <= row
    outs = []
    for j in range(2):
        q = q_ref[:, j * LANES:(j + 1) * LANES]
        m_ref[...] = jnp.full(m_ref.shape, NEG_INF, F32)
        l_ref[...] = jnp.zeros(l_ref.shape, F32)
        acc_ref[...] = jnp.zeros(acc_ref.shape, F32)

        def step(c, masked):
            start = pl.multiple_of(c * tq, tq)
            k = k_ref[pl.ds(start, tq), j * LANES:(j + 1) * LANES]
            v = v_ref[pl.ds(start, tq), :]
            s = lax.dot_general(q, k, nt, preferred_element_type=F32)
            if masked:
                s = jnp.where(causal, s, NEG_INF)
            m_old = m_ref[...]
            m_new = jnp.maximum(m_old, jnp.max(s, axis=-1, keepdims=True))
            alpha = jnp.exp(m_old - m_new)
            p = jnp.exp(s - m_new)
            l_ref[...] = alpha * l_ref[...] + jnp.sum(p, axis=-1, keepdims=True)
            acc_ref[...] = alpha * acc_ref[...] + _dot(p.astype(BF16), v)
            m_ref[...] = m_new

        def body(c, carry):
            step(c, False)
            return carry

        lax.fori_loop(0, qi, body, 0)
        step(qi, True)
        outs.append(acc_ref[...] / l_ref[...])
    lane = lax.broadcasted_iota(jnp.int32, (tq, LANES), 1)
    o = jnp.where(lane < HD, outs[0], outs[1])
    g = g_ref[...]
    o_ref[...] = (o * (g * _sigmoid(g))).astype(BF16)


def _attention(q, k, v, p, gate_col, batch, seq, tq):
    n = q.shape[0]
    nq = seq // tq
    return pl.pallas_call(
        functools.partial(_attn_kernel, tq=tq),
        grid=(batch, H // 2, nq),
        in_specs=[
            pl.BlockSpec((tq, 2 * LANES), lambda b, hp, i: (b * nq + i, hp)),
            pl.BlockSpec((seq, 2 * LANES), lambda b, hp, i: (b, hp)),
            pl.BlockSpec((seq, LANES), lambda b, hp, i: (b, hp)),
            pl.BlockSpec((tq, LANES), lambda b, hp, i: (b * nq + i, gate_col // LANES + hp)),
        ],
        out_specs=pl.BlockSpec((tq, LANES), lambda b, hp, i: (b * nq + i, hp)),
        out_shape=jax.ShapeDtypeStruct((n, W_BR), BF16),
        scratch_shapes=[
            pltpu.VMEM((tq, 1), F32),
            pltpu.VMEM((tq, 1), F32),
            pltpu.VMEM((tq, LANES), F32),
        ],
        compiler_params=_cparams(("parallel", "parallel", "arbitrary")),
        name="attention",
    )(q, k, v, p)


def _store_heads(o_ref, x):
    for p in range(H // 2):
        blk = x[:, p * LANES:(p + 1) * LANES]
        o_ref[0, 2 * p] = blk[:, :HD].astype(o_ref.dtype)
        o_ref[0, 2 * p + 1] = pltpu.roll(blk, HD, 1)[:, :HD].astype(o_ref.dtype)


def _rwkv_prep_kernel(*refs, first_layer):
    if first_layer:
        (sr_ref, sk_ref, sv_ref, swa_ref, gc_ref, mur_ref, muk_ref, muv_ref, muwa_ref, w0_ref, a0_ref, wup_ref,
         aup_ref, kk_ref, ka_ref, rk_ref, bt_ref, bo_ref,
         at_o, rt_o, bt_o, kt_o, v_o, bh_o, kh_o, wc_o, bonus_o, sg_o, vfirst_o,
         cr_ref, ck_ref, cv_ref, cwa_ref) = refs
    else:
        (sr_ref, sk_ref, sv_ref, swa_ref, gc_ref, vf_ref, mur_ref, muk_ref, muv_ref, muwa_ref, w0_ref, a0_ref,
         wup_ref, aup_ref, kk_ref, ka_ref, rk_ref, v0_ref, vdown_ref, vup_ref, bt_ref, bo_ref,
         at_o, rt_o, bt_o, kt_o, v_o, bh_o, kh_o, wc_o, bonus_o, sg_o,
         cr_ref, ck_ref, cv_ref, cwa_ref) = refs
    t = sr_ref.shape[0]

    @pl.when(pl.program_id(1) == 0)
    def _():
        for c_ref in (cr_ref, ck_ref, cv_ref, cwa_ref):
            c_ref[...] = jnp.zeros_like(c_ref)

    def shift(x_ref, c_ref, mu_ref):
        x = x_ref[...]
        rowid = lax.broadcasted_iota(jnp.int32, x.shape, 0)
        prev = jnp.where(rowid == 0, c_ref[...], pltpu.roll(x, 1, 0))
        c_ref[...] = x[t - 1:t, :]
        return x + mu_ref[...] * (prev - x)

    r = shift(sr_ref, cr_ref, mur_ref)
    k = shift(sk_ref, ck_ref, muk_ref)
    v = shift(sv_ref, cv_ref, muv_ref)
    wa = shift(swa_ref, cwa_ref, muwa_ref)

    logw = -RWKV_DECAY_SCALE * _sigmoid(w0_ref[...] + _dot(jnp.tanh(wa).astype(BF16), wup_ref[...]))
    a = _sigmoid(a0_ref[...] + _dot(wa.astype(BF16), aup_ref[...]))
    if first_layer:
        vfirst_o[...] = v
    else:
        low = _dot(v.astype(BF16), vdown_ref[...]).astype(BF16)
        nu = _sigmoid(v0_ref[...] + _dot(low, vup_ref[...]))
        v = v + (vf_ref[...] - v) * nu

    kk = k * kk_ref[...]
    k_mod = k * (1.0 + (a - 1.0) * ka_ref[...])
    rk = r * k_mod * rk_ref[...]
    kk_n, bonus = [], []
    for p in range(H // 2):
        sl = slice(p * LANES, (p + 1) * LANES)
        nrm = jnp.sqrt(_pair_seg_sum(kk[:, sl] * kk[:, sl]))
        kk_n.append(kk[:, sl] / jnp.maximum(nrm, 1e-12))
        bonus.append(_pair_seg_sum(rk[:, sl]) * v[:, sl])
    kk = jnp.concatenate(kk_n, axis=-1)
    bonus = jnp.concatenate(bonus, axis=-1)
    a_vec = -kk
    b_vec = kk * a

    lcum = _dot_exact_lhs(bt_ref[...], logw)
    ltot = _dot_exact_lhs(bo_ref[...], logw)
    e_minus = jnp.exp(-lcum)
    e_rem = jnp.exp(ltot - lcum)
    _store_heads(at_o, a_vec * jnp.exp(lcum - logw))
    _store_heads(rt_o, r * jnp.exp(lcum))
    _store_heads(bt_o, b_vec * e_minus)
    _store_heads(kt_o, k_mod * e_minus)
    _store_heads(v_o, v)
    _store_heads(bh_o, b_vec * e_rem)
    _store_heads(kh_o, k_mod * e_rem)
    _store_heads(wc_o, jnp.exp(ltot))
    _store_heads(bonus_o, bonus)
    g = gc_ref[...]
    _store_heads(sg_o, g * _sigmoid(g))


def _rwkv_prep(p, vfirst, wts, batch, seq, t, first_layer):
    n = p.shape[0]
    ns = seq // t
    const2 = lambda b, i: (0, 0)
    pcol = lambda col, w: pl.BlockSpec((t, w), lambda b, i: (b * ns + i, col // w))
    vec = pl.BlockSpec((1, W_BR), const2)
    in_specs = [pcol(COL_SR, W_BR), pcol(COL_SK, W_BR), pcol(COL_SV, W_BR), pcol(COL_WA, LANES),
                pcol(COL_GATE_C, W_BR)]
    args = [p, p, p, p, p]
    if not first_layer:
        in_specs.append(pl.BlockSpec((t, W_BR), lambda b, i: (b * ns + i, 0)))
        args.append(vfirst)
    in_specs += [vec, vec, vec, pl.BlockSpec((1, LANES), const2), vec, vec,
                 pl.BlockSpec((LANES, W_BR), const2), pl.BlockSpec((LANES, W_BR), const2), vec, vec, vec]
    if not first_layer:
        in_specs += [vec, pl.BlockSpec((W_BR, LANES), const2), pl.BlockSpec((LANES, W_BR), const2)]
    in_specs += [pl.BlockSpec((t, t), const2), pl.BlockSpec((t, t), const2)]
    args += list(wts)
    hm_spec = pl.BlockSpec((1, H, t, HD), lambda b, i: (b, 0, i, 0))
    hm_shape = jax.ShapeDtypeStruct((batch, H, seq, HD), F32)
    out_specs = [hm_spec] * 10
    out_shape = [hm_shape] * 10
    if first_layer:
        out_specs.append(pl.BlockSpec((t, W_BR), lambda b, i: (b * ns + i, 0)))
        out_shape.append(jax.ShapeDtypeStruct((n, W_BR), F32))
    return pl.pallas_call(
        functools.partial(_rwkv_prep_kernel, first_layer=first_layer),
        grid=(batch, ns),
        in_specs=in_specs,
        out_specs=out_specs,
        out_shape=out_shape,
        scratch_shapes=[pltpu.VMEM((1, W_BR), F32), pltpu.VMEM((1, W_BR), F32), pltpu.VMEM((1, W_BR), F32),
                        pltpu.VMEM((1, LANES), F32)],
        compiler_params=_cparams(("parallel", "arbitrary")),
        name="rwkv_prep",
    )(*args)


def _mm(a, b, dims, passes):
    if passes == 1:
        return lax.dot_general(a.astype(BF16), b.astype(BF16), dims, preferred_element_type=F32)
    if passes == 3:
        ah = a.astype(BF16)
        al = (a - ah.astype(F32)).astype(BF16)
        bh = b.astype(BF16)
        bl = (b - bh.astype(F32)).astype(BF16)
        d = lambda x, y: lax.dot_general(x, y, dims, preferred_element_type=F32)
        return d(ah, bh) + (d(ah, bl) + d(al, bh))
    return lax.dot_general(a, b, dims, precision=lax.Precision.HIGHEST, preferred_element_type=F32)


_NN = (((1,), (0,)), ((), ()))
_NT = (((1,), (1,)), ((), ()))
_TN = (((0,), (0,)), ((), ()))


def _rwkv_chunk_kernel(at_ref, rt_ref, bt_ref, kt_ref, v_ref, bh_ref, kh_ref, wc_ref, bonus_ref, sg_ref,
                       lg_ref, lb_ref, o_ref, s_ref, *, nchunk, passes):
    @pl.when(pl.program_id(2) == 0)
    def _():
        s_ref[...] = jnp.zeros_like(s_ref)

    mm = functools.partial(_mm, passes=passes)
    row = lax.broadcasted_iota(jnp.int32, (CHUNK, CHUNK), 0)
    col = lax.broadcasted_iota(jnp.int32, (CHUNK, CHUNK), 1)
    strict = col < row
    incl = col <= row
    eye = col == row
    lg, lb = lg_ref[0], lb_ref[0]
    for c in range(nchunk):
        sl = slice(c * CHUNK, (c + 1) * CHUNK)
        at, rt, bt, kt = at_ref[0, 0, sl, :], rt_ref[0, 0, sl, :], bt_ref[0, 0, sl, :], kt_ref[0, 0, sl, :]
        v, bh, kh = v_ref[0, 0, sl, :], bh_ref[0, 0, sl, :], kh_ref[0, 0, sl, :]
        wc = wc_ref[0, 0, c * CHUNK:c * CHUNK + 1, :]
        x1 = jnp.concatenate([at, rt], axis=0)
        gb = mm(x1, bt, _NT)
        gk = mm(x1, kt, _NT)
        a_ab = jnp.where(strict, gb[:CHUNK], 0.0)
        a_rb = jnp.where(incl, gb[CHUNK:], 0.0)
        a_ak = jnp.where(strict, gk[:CHUNK], 0.0)
        a_rk = jnp.where(incl, gk[CHUNK:], 0.0)
        tinv = jnp.where(eye, 1.0, a_ab)
        pw = a_ab
        for _ in range(int(math.log2(CHUNK)) - 1):
            pw = mm(pw, pw, _NN)
            tinv = tinv + mm(tinv, pw, _NN)
        u = mm(tinv, mm(a_ak, v, _NN), _NN)
        a_til = mm(tinv, at, _NN)
        s0 = s_ref[...]
        r_hat = rt + mm(a_rb, a_til, _NN)
        y = mm(r_hat, s0, _NT) + mm(a_rb, u, _NN) + mm(a_rk, v, _NN)
        m_new = jnp.where(eye, wc, 0.0) + mm(a_til, bh, _TN)
        n_new = mm(u, bh, _TN) + mm(v, kh, _TN)
        s_ref[...] = mm(s0, m_new, _NN) + n_new
        mu = jnp.mean(y, axis=-1, keepdims=True)
        d = y - mu
        var = jnp.mean(d * d, axis=-1, keepdims=True)
        yn = d * lax.rsqrt(var + GN_EPS) * lg + lb
        o_ref[0, 0, sl, :] = ((yn + bonus_ref[0, 0, sl, :]) * sg_ref[0, 0, sl, :]).astype(o_ref.dtype)


def _rwkv_chunk(hm, lnx_g, lnx_b, batch, seq, tc, passes):
    spec = pl.BlockSpec((1, 1, tc, HD), lambda b, h, i: (b, h, i, 0))
    gspec = pl.BlockSpec((1, 1, HD), lambda b, h, i: (h, 0, 0))
    return pl.pallas_call(
        functools.partial(_rwkv_chunk_kernel, nchunk=tc // CHUNK, passes=passes),
        grid=(batch, H, seq // tc),
        in_specs=[spec] * 10 + [gspec, gspec],
        out_specs=spec,
        out_shape=jax.ShapeDtypeStruct((batch, H, seq, HD), BF16),
        scratch_shapes=[pltpu.VMEM((HD, HD), F32)],
        compiler_params=_cparams(("parallel", "parallel", "arbitrary")),
        name="rwkv_chunk",
    )(*hm, lnx_g, lnx_b)


def _merge_kernel(oa_ref, ob_ref, oc_ref, ga_ref, gb_ref, gc_ref, x_ref, wpa_ref, wpb_ref, wpc_ref, wout_ref,
                  o_ref):
    pa = _dot(oa_ref[...], wpa_ref[...])
    pb = _dot(ob_ref[...], wpb_ref[...])
    pc = _dot(oc_ref[0, 0], wpc_ref[0])
    for h in range(1, H):
        pc = pc + _dot(oc_ref[0, h], wpc_ref[h])
    merged = _sigmoid(ga_ref[...]) * pa + _sigmoid(gb_ref[...]) * pb + _sigmoid(gc_ref[...]) * pc
    o_ref[...] = x_ref[...] + _dot(merged.astype(BF16), wout_ref[...])


def _merge(oa, ob, oc, p, x2, wpa, wpb, wpc, wout, batch, seq, t):
    n = x2.shape[0]
    ns = seq // t
    row = lambda b, i: (b * ns + i, 0)
    const2 = lambda b, i: (0, 0)
    gate = lambda g: pl.BlockSpec((t, D_MODEL), lambda b, i: (b * ns + i, COL_MERGE // D_MODEL + g))
    return pl.pallas_call(
        _merge_kernel,
        grid=(batch, ns),
        in_specs=[
            pl.BlockSpec((t, W_BR), row),
            pl.BlockSpec((t, W_BR), row),
            pl.BlockSpec((1, H, t, HD), lambda b, i: (b, 0, i, 0)),
            gate(0), gate(1), gate(2),
            pl.BlockSpec((t, D_MODEL), row),
            pl.BlockSpec((W_BR, D_MODEL), const2),
            pl.BlockSpec((W_BR, D_MODEL), const2),
            pl.BlockSpec((H, HD, D_MODEL), lambda b, i: (0, 0, 0)),
            pl.BlockSpec((D_MODEL, D_MODEL), const2),
        ],
        out_specs=pl.BlockSpec((t, D_MODEL), row),
        out_shape=jax.ShapeDtypeStruct((n, D_MODEL), F32),
        compiler_params=_cparams(("parallel", "parallel")),
        name="merge",
    )(oa, ob, oc, p, p, p, x2, wpa, wpb, wpc, wout)


def _regroup_w_in(w):
    d = w.shape[0]
    sizes = (Q_LORA, KV_LORA, ROPE_D, W_BR, W_BR, W_BR, W_BR, H, W_BR,
             3 * W_BR + DECAY_LORA + AAA_LORA, W_BR, 3 * D_MODEL)
    offs = np.concatenate([[0], np.cumsum(sizes)])
    (c_q, c_kv, k_r, gate_a, fq, fk, fv, ff, gate_b, shift, gate_c, merge) = [
        w[:, offs[i]:offs[i + 1]] for i in range(len(sizes))]
    z = lambda n: jnp.zeros((d, n), w.dtype)
    out = jnp.concatenate([
        merge, gate_a, fq, fk, fv, gate_b,
        shift[:, :W_BR], shift[:, W_BR:2 * W_BR], shift[:, 2 * W_BR:3 * W_BR], gate_c,
        c_q, c_kv,
        z(NOPE), k_r, z(LANES - NOPE - ROPE_D),
        ff, z(LANES - H),
        shift[:, 3 * W_BR:],
    ], axis=1)
    return out.astype(BF16)


def _rope_tables(seq):
    half = ROPE_D // 2
    inv = ROPE_THETA ** (-jnp.arange(0, ROPE_D, 2, dtype=F32) / ROPE_D)
    ang = jnp.arange(seq, dtype=F32)[:, None] * inv[None, :]
    cos, sin = jnp.cos(ang), jnp.sin(ang)
    z = lambda n: jnp.zeros((seq, n), F32)
    c = jnp.concatenate([jnp.ones((seq, NOPE), F32), cos, cos, z(LANES - NOPE - ROPE_D)], axis=1)
    s1 = jnp.concatenate([z(NOPE), -sin, z(LANES - NOPE - half)], axis=1)
    s2 = jnp.concatenate([z(NOPE + half), sin, z(LANES - NOPE - ROPE_D)], axis=1)
    return c, s1, s2


def _pad_lanes(v, lo, total=LANES):
    return jnp.zeros((1, total), F32).at[0, lo:lo + v.shape[0]].set(v)


def _fox_selectors():
    selq = np.zeros((3, LANES, H * LANES), np.float32)
    selk = np.zeros((3, LANES, H * LANES), np.float32)
    oneq = np.zeros((1, H * LANES), np.float32)
    onek = np.zeros((1, H * LANES), np.float32)
    for h in range(H):
        for j in range(3):
            selq[j, h, h * LANES + HD + j] = 1.0
            selk[j, h, h * LANES + HD + 3 + j] = -1.0
            oneq[0, h * LANES + HD + 3 + j] = 1.0
            onek[0, h * LANES + HD + j] = 1.0
    return (jnp.asarray(selq, BF16), jnp.asarray(selk, BF16), jnp.asarray(oneq), jnp.asarray(onek))


def _tri(t):
    r = np.arange(t)
    return jnp.asarray((r[None, :] <= r[:, None]).astype(np.float32), BF16)


def _chunk_tri(t):
    r = np.arange(t)
    same = (r[None, :] // CHUNK) == (r[:, None] // CHUNK)
    lower = r[None, :] <= r[:, None]
    return (jnp.asarray((same & lower).astype(np.float32), BF16), jnp.asarray(same.astype(np.float32), BF16))


def _tile(seq, pref):
    return pref if seq % pref == 0 else seq


RWKV_PASSES = 3


def kernel(x, norm_g, w_in, mla_qa_g, mla_w_uq, mla_kva_g, mla_w_ukv, mla_q_g, mla_knope_g, mla_krope_g, fox_b_f, fox_q_g, fox_k_g, rwkv_mu, rwkv_w0, rwkv_w_up, rwkv_a0, rwkv_a_up, rwkv_k_k, rwkv_k_a, rwkv_r_k, rwkv_lnx_g, rwkv_lnx_b, rwkv_v0, rwkv_v_down, rwkv_v_up, w_pa, w_pb, w_pc, w_out):
    batch, seq, d = x.shape
    depth = w_in.shape[0]
    n = batch * seq
    t_row = _tile(seq, 512)
    t_rwkv = _tile(seq, 256)
    x2 = x.reshape(n, d)

    tabs = _rope_tables(seq)
    selq, selk, oneq, onek = _fox_selectors()
    tri = _tri(t_row)
    btri, bones = _chunk_tri(t_rwkv)
    row = lambda v: v.reshape(1, -1).astype(F32)

    vfirst = None
    for l in range(depth):
        p = _inproj(x2, row(norm_g[l]), _regroup_w_in(w_in[l]), t_row)

        wuq = mla_w_uq[l].reshape(Q_LORA, H, NOPE + ROPE_D)
        wuq = jnp.pad(wuq, ((0, 0), (0, 0), (0, LANES - NOPE - ROPE_D))).reshape(Q_LORA, H * LANES).astype(BF16)
        wukv = mla_w_ukv[l].reshape(KV_LORA, H, NOPE + HD)
        wk = jnp.pad(wukv[:, :, :NOPE], ((0, 0), (0, 0), (0, LANES - NOPE))).reshape(KV_LORA, H * LANES).astype(BF16)
        wv = wukv[:, :, NOPE:].reshape(KV_LORA, W_BR).astype(BF16)
        mla_scale = float(NOPE + ROPE_D) ** -0.5
        mla_wts = (row(mla_qa_g[l]), wuq, row(mla_kva_g[l]), wk, wv,
                   _pad_lanes(mla_q_g[l] * mla_scale, 0), _pad_lanes(mla_knope_g[l], 0),
                   _pad_lanes(mla_krope_g[l], NOPE))
        qa, ka, va = _mla_prep(p, tabs, mla_wts, seq, t_row)
        o_a = _attention(qa, ka, va, p, COL_GATE_A, batch, seq, t_row)

        fox_wts = (_pad_lanes(fox_b_f[l], 0), row(jnp.tile(fox_q_g[l] * float(HD) ** -0.5, H)),
                   row(jnp.tile(fox_k_g[l], H)), tri, selq, selk, oneq, onek)
        qb, kb, vb = _fox_prep(p, fox_wts, batch, seq, t_row)
        o_b = _attention(qb, kb, vb, p, COL_GATE_B, batch, seq, t_row)

        mu = rwkv_mu[l]
        wup = jnp.zeros((LANES, W_BR), F32).at[:DECAY_LORA].set(rwkv_w_up[l]).astype(BF16)
        aup = jnp.zeros((LANES, W_BR), F32).at[DECAY_LORA:].set(rwkv_a_up[l]).astype(BF16)
        rw = [row(mu[:W_BR]), row(mu[W_BR:2 * W_BR]), row(mu[2 * W_BR:3 * W_BR]), row(mu[3 * W_BR:]),
              row(rwkv_w0[l]), row(rwkv_a0[l]), wup, aup, row(rwkv_k_k[l]), row(rwkv_k_a[l]),
              row(rwkv_r_k[l])]
        if l > 0:
            vdown = jnp.zeros((W_BR, LANES), F32).at[:, :MV_LORA].set(rwkv_v_down[l - 1]).astype(BF16)
            vup = jnp.zeros((LANES, W_BR), F32).at[:MV_LORA].set(rwkv_v_up[l - 1]).astype(BF16)
            rw += [row(rwkv_v0[l - 1]), vdown, vup]
        rw += [btri, bones]
        outs = _rwkv_prep(p, vfirst, rw, batch, seq, t_rwkv, first_layer=(l == 0))
        if l == 0:
            vfirst = outs[10]
        o_c = _rwkv_chunk(outs[:10], rwkv_lnx_g[l].reshape(H, 1, HD), rwkv_lnx_b[l].reshape(H, 1, HD),
                          batch, seq, t_row, RWKV_PASSES)

        x2 = _merge(o_a, o_b, o_c, p, x2, w_pa[l].astype(BF16), w_pb[l].astype(BF16),
                    w_pc[l].reshape(H, HD, D_MODEL).astype(BF16), w_out[l].astype(BF16), batch, seq, t_row)
    return x2.reshape(batch, seq, d)
```

```python
import functools
import math

import jax
import jax.numpy as jnp
import numpy as np
from jax import lax
from jax.experimental import pallas as pl
from jax.experimental.pallas import tpu as pltpu

F32 = jnp.float32
BF16 = jnp.bfloat16

LANES = 128
H = 8
HD = 64
NOPE = 64
ROPE_D = 32
Q_LORA = 256
KV_LORA = 128
D_MODEL = 1024
W_BR = H * HD
DECAY_LORA = 64
AAA_LORA = 64
MV_LORA = 32
ROPE_THETA = 10000.0
RWKV_DECAY_SCALE = 0.606531
GN_EPS = 64e-5
EPS = 1e-6
NEG_INF = -1e30
CHUNK = 64

COL_MERGE = 0
COL_GATE_A = 3072
COL_FQ = 3584
COL_FK = 4096
COL_FV = 4608
COL_GATE_B = 5120
COL_SR = 5632
COL_SK = 6144
COL_SV = 6656
COL_GATE_C = 7168
COL_CQ = 7680
COL_CKV = 7936
COL_KR = 8064
COL_FF = 8192
COL_WA = 8320
NCOLS = 8448
INPROJ_TN = 1408

VMEM_LIMIT = 56 * 1024 * 1024


def _cparams(sem):
    return pltpu.CompilerParams(dimension_semantics=sem, vmem_limit_bytes=VMEM_LIMIT)


def _sigmoid(x):
    return 1.0 / (1.0 + jnp.exp(-x))


def _dot(a, b):
    return jnp.dot(a, b, preferred_element_type=F32)


def _split3(x):
    hi = x.astype(BF16)
    r1 = x - hi.astype(F32)
    mid = r1.astype(BF16)
    lo = (r1 - mid.astype(F32)).astype(BF16)
    return hi, mid, lo


def _dot_exact_lhs(m_bf16, x):
    hi, mid, lo = _split3(x)
    return _dot(m_bf16, hi) + _dot(m_bf16, mid) + _dot(m_bf16, lo)


def _pair_seg_sum(x):
    lane = lax.broadcasted_iota(jnp.int32, x.shape, 1)
    first = lane < HD
    s0 = jnp.sum(jnp.where(first, x, 0.0), axis=-1, keepdims=True)
    s1 = jnp.sum(jnp.where(first, 0.0, x), axis=-1, keepdims=True)
    return jnp.where(first, s0, s1)


def _inproj_kernel(x_ref, g_ref, w_ref, o_ref, h_ref):
    @pl.when(pl.program_id(1) == 0)
    def _():
        x = x_ref[...]
        ms = jnp.mean(x * x, axis=-1, keepdims=True)
        h_ref[...] = (x * lax.rsqrt(ms + EPS) * g_ref[...]).astype(BF16)

    o_ref[...] = _dot(h_ref[...], w_ref[...])


def _inproj(x2, g, w, tm):
    n = x2.shape[0]
    return pl.pallas_call(
        _inproj_kernel,
        grid=(n // tm, NCOLS // INPROJ_TN),
        in_specs=[
            pl.BlockSpec((tm, D_MODEL), lambda i, j: (i, 0)),
            pl.BlockSpec((1, D_MODEL), lambda i, j: (0, 0)),
            pl.BlockSpec((D_MODEL, INPROJ_TN), lambda i, j: (0, j)),
        ],
        out_specs=pl.BlockSpec((tm, INPROJ_TN), lambda i, j: (i, j)),
        out_shape=jax.ShapeDtypeStruct((n, NCOLS), F32),
        scratch_shapes=[pltpu.VMEM((tm, D_MODEL), BF16)],
        compiler_params=_cparams(("parallel", "arbitrary")),
        name="inproj",
    )(x2, g, w)


def _rms(x, g, n):
    ms = jnp.sum(x * x, axis=-1, keepdims=True) / n
    return x * lax.rsqrt(ms + EPS) * g


def _rope(x, c, s1, s2):
    return x * c + pltpu.roll(x, LANES - ROPE_D // 2, 1) * s1 + pltpu.roll(x, ROPE_D // 2, 1) * s2


def _mla_prep_kernel(cq_ref, ckv_ref, kr_ref, c_ref, s1_ref, s2_ref, qag_ref, wuq_ref, kvag_ref, wk_ref, wv_ref,
                     qg_ref, kng_ref, krg_ref, q_out, k_out, v_out):
    c, s1, s2 = c_ref[...], s1_ref[...], s2_ref[...]
    cqn = _rms(cq_ref[...], qag_ref[...], Q_LORA).astype(BF16)
    ckvn = _rms(ckv_ref[...], kvag_ref[...], KV_LORA).astype(BF16)
    q_all = _dot(cqn, wuq_ref[...])
    k_all = _dot(ckvn, wk_ref[...])
    v_out[...] = _dot(ckvn, wv_ref[...]).astype(BF16)
    k_rope = _rope(_rms(kr_ref[...], krg_ref[...], ROPE_D), c, s1, s2)
    qg, kng = qg_ref[...], kng_ref[...]
    for h in range(H):
        sl = slice(h * LANES, (h + 1) * LANES)
        q_out[:, sl] = _rope(_rms(q_all[:, sl], qg, NOPE + ROPE_D), c, s1, s2).astype(BF16)
        k_out[:, sl] = (_rms(k_all[:, sl], kng, NOPE) + k_rope).astype(BF16)


def _mla_prep(p, tabs, wts, seq, t):
    n = p.shape[0]
    ns = seq // t
    row = lambda i: (i, 0)
    const = lambda i: (0, 0)
    tab_spec = pl.BlockSpec((t, LANES), lambda i: (i % ns, 0))
    return pl.pallas_call(
        _mla_prep_kernel,
        grid=(n // t,),
        in_specs=[
            pl.BlockSpec((t, Q_LORA), lambda i: (i, COL_CQ // Q_LORA)),
            pl.BlockSpec((t, LANES), lambda i: (i, COL_CKV // LANES)),
            pl.BlockSpec((t, LANES), lambda i: (i, COL_KR // LANES)),
            tab_spec, tab_spec, tab_spec,
            pl.BlockSpec((1, Q_LORA), const),
            pl.BlockSpec((Q_LORA, H * LANES), const),
            pl.BlockSpec((1, KV_LORA), const),
            pl.BlockSpec((KV_LORA, H * LANES), const),
            pl.BlockSpec((KV_LORA, W_BR), const),
            pl.BlockSpec((1, LANES), const),
            pl.BlockSpec((1, LANES), const),
            pl.BlockSpec((1, LANES), const),
        ],
        out_specs=[
            pl.BlockSpec((t, H * LANES), row),
            pl.BlockSpec((t, H * LANES), row),
            pl.BlockSpec((t, W_BR), row),
        ],
        out_shape=[
            jax.ShapeDtypeStruct((n, H * LANES), BF16),
            jax.ShapeDtypeStruct((n, H * LANES), BF16),
            jax.ShapeDtypeStruct((n, W_BR), BF16),
        ],
        compiler_params=_cparams(("parallel",)),
        name="mla_prep",
    )(p, p, p, *tabs, *wts)


def _fox_prep_kernel(fq_ref, fk_ref, fv_ref, ff_ref, bf_ref, qg_ref, kg_ref, tri_ref, selq_ref, selk_ref,
                     oneq_ref, onek_ref, q_out, k_out, v_out, carry_ref):
    t = fq_ref.shape[0]

    @pl.when(pl.program_id(1) == 0)
    def _():
        carry_ref[...] = jnp.zeros_like(carry_ref)

    z = ff_ref[...] + bf_ref[...]
    logf = jnp.minimum(z, 0.0) - jnp.log1p(jnp.exp(-jnp.abs(z)))
    cum = carry_ref[...] + _dot_exact_lhs(tri_ref[...], logf)
    carry_ref[...] = cum[t - 1:t, :]
    hi, mid, lo = _split3(cum)
    aug_q = _dot(hi, selq_ref[0]) + _dot(mid, selq_ref[1]) + _dot(lo, selq_ref[2]) + oneq_ref[...]
    aug_k = _dot(hi, selk_ref[0]) + _dot(mid, selk_ref[1]) + _dot(lo, selk_ref[2]) + onek_ref[...]
    v_out[...] = fv_ref[...].astype(BF16)
    lane = lax.broadcasted_iota(jnp.int32, (t, LANES), 1)
    first = lane < HD
    for src_ref, g_ref, aug, dst in ((fq_ref, qg_ref, aug_q, q_out), (fk_ref, kg_ref, aug_k, k_out)):
        for p in range(H // 2):
            sl = slice(p * LANES, (p + 1) * LANES)
            x = src_ref[:, sl]
            ms = _pair_seg_sum(x * x) / HD
            xn = x * lax.rsqrt(ms + EPS) * g_ref[:, sl]
            he, ho = 2 * p, 2 * p + 1
            dst[:, he * LANES:(he + 1) * LANES] = jnp.where(first, xn, aug[:, he * LANES:(he + 1) * LANES]).astype(BF16)
            dst[:, ho * LANES:(ho + 1) * LANES] = jnp.where(
                first, pltpu.roll(xn, HD, 1), aug[:, ho * LANES:(ho + 1) * LANES]).astype(BF16)


def _fox_prep(p, wts, batch, seq, t):
    n = p.shape[0]
    ns = seq // t
    row = lambda b, i: (b * ns + i, 0)
    const2 = lambda b, i: (0, 0)
    const3 = lambda b, i: (0, 0, 0)
    return pl.pallas_call(
        _fox_prep_kernel,
        grid=(batch, ns),
        in_specs=[
            pl.BlockSpec((t, W_BR), lambda b, i: (b * ns + i, COL_FQ // W_BR)),
            pl.BlockSpec((t, W_BR), lambda b, i: (b * ns + i, COL_FK // W_BR)),
            pl.BlockSpec((t, W_BR), lambda b, i: (b * ns + i, COL_FV // W_BR)),
            pl.BlockSpec((t, LANES), lambda b, i: (b * ns + i, COL_FF // LANES)),
            pl.BlockSpec((1, LANES), const2),
            pl.BlockSpec((1, W_BR), const2),
            pl.BlockSpec((1, W_BR), const2),
            pl.BlockSpec((t, t), const2),
            pl.BlockSpec((3, LANES, H * LANES), const3),
            pl.BlockSpec((3, LANES, H * LANES), const3),
            pl.BlockSpec((1, H * LANES), const2),
            pl.BlockSpec((1, H * LANES), const2),
        ],
        out_specs=[
            pl.BlockSpec((t, H * LANES), row),
            pl.BlockSpec((t, H * LANES), row),
            pl.BlockSpec((t, W_BR), row),
        ],
        out_shape=[
            jax.ShapeDtypeStruct((n, H * LANES), BF16),
            jax.ShapeDtypeStruct((n, H * LANES), BF16),
            jax.ShapeDtypeStruct((n, W_BR), BF16),
        ],
        scratch_shapes=[pltpu.VMEM((1, LANES), F32)],
        compiler_params=_cparams(("parallel", "arbitrary")),
        name="fox_prep",
    )(p, p, p, p, *wts)


def _attn_kernel(q_ref, k_ref, v_ref, g_ref, o_ref, m_ref, acc_ref, *, tq):
    qi = pl.program_id(2)
    nt = (((1,), (1,)), ((), ()))
    row = lax.broadcasted_iota(jnp.int32, (tq, tq), 0)
    col = lax.broadcasted_iota(jnp.int32, (tq, tq), 1)
    causal = col <= row
    ones = jnp.ones((tq, LANES), BF16)
    m_ref[...] = jnp.full(m_ref.shape, NEG_INF, F32)
    acc_ref[...] = jnp.zeros(acc_ref.shape, F32)

    def step(c, masked):
        start = pl.multiple_of(c * tq, tq)
        v = jnp.concatenate([v_ref[pl.ds(start, tq), :], ones], axis=1)
        for j in range(2):
            q = q_ref[:, j * LANES:(j + 1) * LANES]
            k = k_ref[pl.ds(start, tq), j * LANES:(j + 1) * LANES]
            s = lax.dot_general(q, k, nt, preferred_element_type=F32)
            if masked:
                s = jnp.where(causal, s, NEG_INF)
            m_old = m_ref[j]
            m_new = jnp.maximum(m_old, jnp.max(s, axis=-1, keepdims=True))
            alpha = jnp.exp(m_old - m_new)
            p = jnp.exp(s - jnp.concatenate([m_new] * (tq // LANES), axis=1))
            acc_ref[j] = jnp.concatenate([alpha, alpha], axis=1) * acc_ref[j] + _dot(p.astype(BF16), v)
            m_ref[j] = m_new

    def body(c, carry):
        step(c, False)
        return carry

    lax.fori_loop(0, qi, body, 0)
    step(qi, True)
    outs = [acc_ref[j, :, :LANES] / acc_ref[j, :, LANES:] for j in range(2)]
    lane = lax.broadcasted_iota(jnp.int32, (tq, LANES), 1)
    o = jnp.where(lane < HD, outs[0], outs[1])
    g = g_ref[...]
    o_ref[...] = (o * (g * _sigmoid(g))).astype(BF16)


def _attention(q, k, v, p, gate_col, batch, seq, tq):
    n = q.shape[0]
    nq = seq // tq
    return pl.pallas_call(
        functools.partial(_attn_kernel, tq=tq),
        grid=(batch, H // 2, nq),
        in_specs=[
            pl.BlockSpec((tq, 2 * LANES), lambda b, hp, i: (b * nq + i, hp)),
            pl.BlockSpec((seq, 2 * LANES), lambda b, hp, i: (b, hp)),
            pl.BlockSpec((seq, LANES), lambda b, hp, i: (b, hp)),
            pl.BlockSpec((tq, LANES), lambda b, hp, i: (b * nq + i, gate_col // LANES + hp)),
        ],
        out_specs=pl.BlockSpec((tq, LANES), lambda b, hp, i: (b * nq + i, hp)),
        out_shape=jax.ShapeDtypeStruct((n, W_BR), BF16),
        scratch_shapes=[
            pltpu.VMEM((2, tq, LANES), F32),
            pltpu.VMEM((2, tq, 2 * LANES), F32),
        ],
        compiler_params=_cparams(("parallel", "parallel", "arbitrary")),
        name="attention",
    )(q, k, v, p)


def _store_heads(o_ref, x):
    for p in range(H // 2):
        blk = x[:, p * LANES:(p + 1) * LANES]
        o_ref[0, 2 * p] = blk[:, :HD].astype(o_ref.dtype)
        o_ref[0, 2 * p + 1] = pltpu.roll(blk, HD, 1)[:, :HD].astype(o_ref.dtype)


def _rwkv_prep_kernel(*refs, first_layer):
    if first_layer:
        (sr_ref, sk_ref, sv_ref, swa_ref, gc_ref, mur_ref, muk_ref, muv_ref, muwa_ref, w0_ref, a0_ref, wup_ref,
         aup_ref, kk_ref, ka_ref, rk_ref, bt_ref, bo_ref,
         at_o, rt_o, bt_o, kt_o, v_o, bh_o, kh_o, wc_o, bonus_o, sg_o, vfirst_o,
         cr_ref, ck_ref, cv_ref, cwa_ref) = refs
    else:
        (sr_ref, sk_ref, sv_ref, swa_ref, gc_ref, vf_ref, mur_ref, muk_ref, muv_ref, muwa_ref, w0_ref, a0_ref,
         wup_ref, aup_ref, kk_ref, ka_ref, rk_ref, v0_ref, vdown_ref, vup_ref, bt_ref, bo_ref,
         at_o, rt_o, bt_o, kt_o, v_o, bh_o, kh_o, wc_o, bonus_o, sg_o,
         cr_ref, ck_ref, cv_ref, cwa_ref) = refs
    t = sr_ref.shape[0]

    @pl.when(pl.program_id(1) == 0)
    def _():
        for c_ref in (cr_ref, ck_ref, cv_ref, cwa_ref):
            c_ref[...] = jnp.zeros_like(c_ref)

    def shift(x_ref, c_ref, mu_ref):
        x = x_ref[...]
        rowid = lax.broadcasted_iota(jnp.int32, x.shape, 0)
        prev = jnp.where(rowid == 0, c_ref[...], pltpu.roll(x, 1, 0))
        c_ref[...] = x[t - 1:t, :]
        return x + mu_ref[...] * (prev - x)

    r = shift(sr_ref, cr_ref, mur_ref)
    k = shift(sk_ref, ck_ref, muk_ref)
    v = shift(sv_ref, cv_ref, muv_ref)
    wa = shift(swa_ref, cwa_ref, muwa_ref)

    logw = -RWKV_DECAY_SCALE * _sigmoid(w0_ref[...] + _dot(jnp.tanh(wa).astype(BF16), wup_ref[...]))
    a = _sigmoid(a0_ref[...] + _dot(wa.astype(BF16), aup_ref[...]))
    if first_layer:
        vfirst_o[...] = v
    else:
        low = _dot(v.astype(BF16), vdown_ref[...]).astype(BF16)
        nu = _sigmoid(v0_ref[...] + _dot(low, vup_ref[...]))
        v = v + (vf_ref[...] - v) * nu

    kk = k * kk_ref[...]
    k_mod = k * (1.0 + (a - 1.0) * ka_ref[...])
    rk = r * k_mod * rk_ref[...]
    kk_n, bonus = [], []
    for p in range(H // 2):
        sl = slice(p * LANES, (p + 1) * LANES)
        nrm = jnp.sqrt(_pair_seg_sum(kk[:, sl] * kk[:, sl]))
        kk_n.append(kk[:, sl] / jnp.maximum(nrm, 1e-12))
        bonus.append(_pair_seg_sum(rk[:, sl]) * v[:, sl])
    kk = jnp.concatenate(kk_n, axis=-1)
    bonus = jnp.concatenate(bonus, axis=-1)
    a_vec = -kk
    b_vec = kk * a

    lcum = _dot_exact_lhs(bt_ref[...], logw)
    ltot = _dot_exact_lhs(bo_ref[...], logw)
    e_minus = jnp.exp(-lcum)
    e_rem = jnp.exp(ltot - lcum)
    _store_heads(at_o, a_vec * jnp.exp(lcum - logw))
    _store_heads(rt_o, r * jnp.exp(lcum))
    _store_heads(bt_o, b_vec * e_minus)
    _store_heads(kt_o, k_mod * e_minus)
    _store_heads(v_o, v)
    _store_heads(bh_o, b_vec * e_rem)
    _store_heads(kh_o, k_mod * e_rem)
    _store_heads(wc_o, jnp.exp(ltot))
    _store_heads(bonus_o, bonus)
    g = gc_ref[...]
    _store_heads(sg_o, g * _sigmoid(g))


def _rwkv_prep(p, vfirst, wts, batch, seq, t, first_layer):
    n = p.shape[0]
    ns = seq // t
    const2 = lambda b, i: (0, 0)
    pcol = lambda col, w: pl.BlockSpec((t, w), lambda b, i: (b * ns + i, col // w))
    vec = pl.BlockSpec((1, W_BR), const2)
    in_specs = [pcol(COL_SR, W_BR), pcol(COL_SK, W_BR), pcol(COL_SV, W_BR), pcol(COL_WA, LANES),
                pcol(COL_GATE_C, W_BR)]
    args = [p, p, p, p, p]
    if not first_layer:
        in_specs.append(pl.BlockSpec((t, W_BR), lambda b, i: (b * ns + i, 0)))
        args.append(vfirst)
    in_specs += [vec, vec, vec, pl.BlockSpec((1, LANES), const2), vec, vec,
                 pl.BlockSpec((LANES, W_BR), const2), pl.BlockSpec((LANES, W_BR), const2), vec, vec, vec]
    if not first_layer:
        in_specs += [vec, pl.BlockSpec((W_BR, LANES), const2), pl.BlockSpec((LANES, W_BR), const2)]
    in_specs += [pl.BlockSpec((t, t), const2), pl.BlockSpec((t, t), const2)]
    args += list(wts)
    hm_spec = pl.BlockSpec((1, H, t, HD), lambda b, i: (b, 0, i, 0))
    hm_shape = jax.ShapeDtypeStruct((batch, H, seq, HD), F32)
    out_specs = [hm_spec] * 10
    out_shape = [hm_shape] * 10
    if first_layer:
        out_specs.append(pl.BlockSpec((t, W_BR), lambda b, i: (b * ns + i, 0)))
        out_shape.append(jax.ShapeDtypeStruct((n, W_BR), F32))
    return pl.pallas_call(
        functools.partial(_rwkv_prep_kernel, first_layer=first_layer),
        grid=(batch, ns),
        in_specs=in_specs,
        out_specs=out_specs,
        out_shape=out_shape,
        scratch_shapes=[pltpu.VMEM((1, W_BR), F32), pltpu.VMEM((1, W_BR), F32), pltpu.VMEM((1, W_BR), F32),
                        pltpu.VMEM((1, LANES), F32)],
        compiler_params=_cparams(("parallel", "arbitrary")),
        name="rwkv_prep",
    )(*args)


def _mm(a, b, dims, passes):
    if passes == 1:
        return lax.dot_general(a.astype(BF16), b.astype(BF16), dims, preferred_element_type=F32)
    if passes == 3:
        ah = a.astype(BF16)
        al = (a - ah.astype(F32)).astype(BF16)
        bh = b.astype(BF16)
        bl = (b - bh.astype(F32)).astype(BF16)
        d = lambda x, y: lax.dot_general(x, y, dims, preferred_element_type=F32)
        return d(ah, bh) + (d(ah, bl) + d(al, bh))
    return lax.dot_general(a, b, dims, precision=lax.Precision.HIGHEST, preferred_element_type=F32)


_NN = (((1,), (0,)), ((), ()))
_NT = (((1,), (1,)), ((), ()))
_TN = (((0,), (0,)), ((), ()))


def _rwkv_chunk_kernel(at_ref, rt_ref, bt_ref, kt_ref, v_ref, bh_ref, kh_ref, wc_ref, bonus_ref, sg_ref,
                       lg_ref, lb_ref, o_ref, s_all_ref, *, nchunk, passes):
    @pl.when(pl.program_id(1) == 0)
    def _():
        s_all_ref[...] = jnp.zeros_like(s_all_ref)

    mm = functools.partial(_mm, passes=passes)
    row = lax.broadcasted_iota(jnp.int32, (CHUNK, CHUNK), 0)
    col = lax.broadcasted_iota(jnp.int32, (CHUNK, CHUNK), 1)
    strict = col < row
    incl = col <= row
    eye = col == row
    items = [(c, h) for c in range(nchunk) for h in range(H)]
    each = lambda f, *cols: [f(*xs) for xs in zip(*cols)]
    sls = [slice(c * CHUNK, (c + 1) * CHUNK) for c, _ in items]
    ld = lambda ref: [ref[0, h, sl, :] for (_, h), sl in zip(items, sls)]
    at, rt, bt, kt, v, bh, kh = ld(at_ref), ld(rt_ref), ld(bt_ref), ld(kt_ref), ld(v_ref), ld(bh_ref), ld(kh_ref)
    x1 = each(lambda a, r: jnp.concatenate([a, r], axis=0), at, rt)
    gb = each(lambda x, b: mm(x, b, _NT), x1, bt)
    gk = each(lambda x, k: mm(x, k, _NT), x1, kt)
    a_ab = [jnp.where(strict, g[:CHUNK], 0.0) for g in gb]
    a_rb = [jnp.where(incl, g[CHUNK:], 0.0) for g in gb]
    a_ak = [jnp.where(strict, g[:CHUNK], 0.0) for g in gk]
    a_rk = [jnp.where(incl, g[CHUNK:], 0.0) for g in gk]
    tinv = [jnp.where(eye, 1.0, a) for a in a_ab]
    pw = a_ab
    for _ in range(int(math.log2(CHUNK)) - 1):
        pw = each(lambda p: mm(p, p, _NN), pw)
        tinv = each(lambda t, p: t + mm(t, p, _NN), tinv, pw)
    av = each(lambda a, x: mm(a, x, _NN), a_ak, v)
    u = each(lambda t, x: mm(t, x, _NN), tinv, av)
    a_til = each(lambda t, x: mm(t, x, _NN), tinv, at)
    r_hat = each(lambda r, a, x: r + mm(a, x, _NN), rt, a_rb, a_til)
    y_in = each(lambda a, x: mm(a, x, _NN), a_rb, u)
    y_in = each(lambda y, a, x: y + mm(a, x, _NN), y_in, a_rk, v)
    m_new = each(lambda a, b: mm(a, b, _TN), a_til, bh)
    n_new = each(lambda a, b: mm(a, b, _TN), u, bh)
    n_new = each(lambda n, a, b: n + mm(a, b, _TN), n_new, v, kh)
    for c in range(nchunk):
        idx = [i for i, (ci, _) in enumerate(items) if ci == c]
        s0 = [s_all_ref[h] for h in range(H)]
        y = [y_in[i] + mm(r_hat[i], s0[h], _NT) for h, i in enumerate(idx)]
        for h, i in enumerate(idx):
            wc = wc_ref[0, h, c * CHUNK:c * CHUNK + 1, :]
            s_all_ref[h] = mm(s0[h], jnp.where(eye, wc, 0.0) + m_new[i], _NN) + n_new[i]
        for h, i in enumerate(idx):
            mu = jnp.mean(y[h], axis=-1, keepdims=True)
            d = y[h] - mu
            var = jnp.mean(d * d, axis=-1, keepdims=True)
            yn = d * lax.rsqrt(var + GN_EPS) * lg_ref[h] + lb_ref[h]
            o_ref[0, h, sls[i], :] = ((yn + bonus_ref[0, h, sls[i], :]) * sg_ref[0, h, sls[i], :]).astype(o_ref.dtype)


def _rwkv_chunk(hm, lnx_g, lnx_b, batch, seq, tc, passes):
    spec = pl.BlockSpec((1, H, tc, HD), lambda b, i: (b, 0, i, 0))
    gspec = pl.BlockSpec((H, 1, HD), lambda b, i: (0, 0, 0))
    return pl.pallas_call(
        functools.partial(_rwkv_chunk_kernel, nchunk=tc // CHUNK, passes=passes),
        grid=(batch, seq // tc),
        in_specs=[spec] * 10 + [gspec, gspec],
        out_specs=spec,
        out_shape=jax.ShapeDtypeStruct((batch, H, seq, HD), BF16),
        scratch_shapes=[pltpu.VMEM((H, HD, HD), F32)],
        compiler_params=_cparams(("parallel", "arbitrary")),
        name="rwkv_chunk",
    )(*hm, lnx_g, lnx_b)


def _merge_kernel(oa_ref, ob_ref, oc_ref, ga_ref, gb_ref, gc_ref, x_ref, wpa_ref, wpb_ref, wpc_ref, wout_ref,
                  o_ref):
    pa = _dot(oa_ref[...], wpa_ref[...])
    pb = _dot(ob_ref[...], wpb_ref[...])
    pc = _dot(oc_ref[0, 0], wpc_ref[0])
    for h in range(1, H):
        pc = pc + _dot(oc_ref[0, h], wpc_ref[h])
    merged = _sigmoid(ga_ref[...]) * pa + _sigmoid(gb_ref[...]) * pb + _sigmoid(gc_ref[...]) * pc
    o_ref[...] = x_ref[...] + _dot(merged.astype(BF16), wout_ref[...])


def _merge(oa, ob, oc, p, x2, wpa, wpb, wpc, wout, batch, seq, t):
    n = x2.shape[0]
    ns = seq // t
    row = lambda b, i: (b * ns + i, 0)
    const2 = lambda b, i: (0, 0)
    gate = lambda g: pl.BlockSpec((t, D_MODEL), lambda b, i: (b * ns + i, COL_MERGE // D_MODEL + g))
    return pl.pallas_call(
        _merge_kernel,
        grid=(batch, ns),
        in_specs=[
            pl.BlockSpec((t, W_BR), row),
            pl.BlockSpec((t, W_BR), row),
            pl.BlockSpec((1, H, t, HD), lambda b, i: (b, 0, i, 0)),
            gate(0), gate(1), gate(2),
            pl.BlockSpec((t, D_MODEL), row),
            pl.BlockSpec((W_BR, D_MODEL), const2),
            pl.BlockSpec((W_BR, D_MODEL), const2),
            pl.BlockSpec((H, HD, D_MODEL), lambda b, i: (0, 0, 0)),
            pl.BlockSpec((D_MODEL, D_MODEL), const2),
        ],
        out_specs=pl.BlockSpec((t, D_MODEL), row),
        out_shape=jax.ShapeDtypeStruct((n, D_MODEL), F32),
        compiler_params=_cparams(("parallel", "parallel")),
        name="merge",
    )(oa, ob, oc, p, p, p, x2, wpa, wpb, wpc, wout)


def _regroup_w_in(w):
    d = w.shape[0]
    sizes = (Q_LORA, KV_LORA, ROPE_D, W_BR, W_BR, W_BR, W_BR, H, W_BR,
             3 * W_BR + DECAY_LORA + AAA_LORA, W_BR, 3 * D_MODEL)
    offs = np.concatenate([[0], np.cumsum(sizes)])
    (c_q, c_kv, k_r, gate_a, fq, fk, fv, ff, gate_b, shift, gate_c, merge) = [
        w[:, offs[i]:offs[i + 1]] for i in range(len(sizes))]
    z = lambda n: jnp.zeros((d, n), w.dtype)
    out = jnp.concatenate([
        merge, gate_a, fq, fk, fv, gate_b,
        shift[:, :W_BR], shift[:, W_BR:2 * W_BR], shift[:, 2 * W_BR:3 * W_BR], gate_c,
        c_q, c_kv,
        z(NOPE), k_r, z(LANES - NOPE - ROPE_D),
        ff, z(LANES - H),
        shift[:, 3 * W_BR:],
    ], axis=1)
    return out.astype(BF16)


def _rope_tables(seq):
    half = ROPE_D // 2
    inv = ROPE_THETA ** (-jnp.arange(0, ROPE_D, 2, dtype=F32) / ROPE_D)
    ang = jnp.arange(seq, dtype=F32)[:, None] * inv[None, :]
    cos, sin = jnp.cos(ang), jnp.sin(ang)
    z = lambda n: jnp.zeros((seq, n), F32)
    c = jnp.concatenate([jnp.ones((seq, NOPE), F32), cos, cos, z(LANES - NOPE - ROPE_D)], axis=1)
    s1 = jnp.concatenate([z(NOPE), -sin, z(LANES - NOPE - half)], axis=1)
    s2 = jnp.concatenate([z(NOPE + half), sin, z(LANES - NOPE - ROPE_D)], axis=1)
    return c, s1, s2


def _pad_lanes(v, lo, total=LANES):
    return jnp.zeros((1, total), F32).at[0, lo:lo + v.shape[0]].set(v)


def _fox_selectors():
    selq = np.zeros((3, LANES, H * LANES), np.float32)
    selk = np.zeros((3, LANES, H * LANES), np.float32)
    oneq = np.zeros((1, H * LANES), np.float32)
    onek = np.zeros((1, H * LANES), np.float32)
    for h in range(H):
        for j in range(3):
            selq[j, h, h * LANES + HD + j] = 1.0
            selk[j, h, h * LANES + HD + 3 + j] = -1.0
            oneq[0, h * LANES + HD + 3 + j] = 1.0
            onek[0, h * LANES + HD + j] = 1.0
    return (jnp.asarray(selq, BF16), jnp.asarray(selk, BF16), jnp.asarray(oneq), jnp.asarray(onek))


def _tri(t):
    r = np.arange(t)
    return jnp.asarray((r[None, :] <= r[:, None]).astype(np.float32), BF16)


def _chunk_tri(t):
    r = np.arange(t)
    same = (r[None, :] // CHUNK) == (r[:, None] // CHUNK)
    lower = r[None, :] <= r[:, None]
    return (jnp.asarray((same & lower).astype(np.float32), BF16), jnp.asarray(same.astype(np.float32), BF16))


def _tile(seq, pref):
    return pref if seq % pref == 0 else seq


RWKV_PASSES = 1
RWKV_TC = 128


def kernel(x, norm_g, w_in, mla_qa_g, mla_w_uq, mla_kva_g, mla_w_ukv, mla_q_g, mla_knope_g, mla_krope_g, fox_b_f, fox_q_g, fox_k_g, rwkv_mu, rwkv_w0, rwkv_w_up, rwkv_a0, rwkv_a_up, rwkv_k_k, rwkv_k_a, rwkv_r_k, rwkv_lnx_g, rwkv_lnx_b, rwkv_v0, rwkv_v_down, rwkv_v_up, w_pa, w_pb, w_pc, w_out):
    batch, seq, d = x.shape
    depth = w_in.shape[0]
    n = batch * seq
    t_row = _tile(seq, 512)
    t_rwkv = _tile(seq, 256)
    x2 = x.reshape(n, d)

    tabs = _rope_tables(seq)
    selq, selk, oneq, onek = _fox_selectors()
    tri = _tri(t_row)
    btri, bones = _chunk_tri(t_rwkv)
    row = lambda v: v.reshape(1, -1).astype(F32)

    vfirst = None
    for l in range(depth):
        p = _inproj(x2, row(norm_g[l]), _regroup_w_in(w_in[l]), t_row)

        wuq = mla_w_uq[l].reshape(Q_LORA, H, NOPE + ROPE_D)
        wuq = jnp.pad(wuq, ((0, 0), (0, 0), (0, LANES - NOPE - ROPE_D))).reshape(Q_LORA, H * LANES).astype(BF16)
        wukv = mla_w_ukv[l].reshape(KV_LORA, H, NOPE + HD)
        wk = jnp.pad(wukv[:, :, :NOPE], ((0, 0), (0, 0), (0, LANES - NOPE))).reshape(KV_LORA, H * LANES).astype(BF16)
        wv = wukv[:, :, NOPE:].reshape(KV_LORA, W_BR).astype(BF16)
        mla_scale = float(NOPE + ROPE_D) ** -0.5
        mla_wts = (row(mla_qa_g[l]), wuq, row(mla_kva_g[l]), wk, wv,
                   _pad_lanes(mla_q_g[l] * mla_scale, 0), _pad_lanes(mla_knope_g[l], 0),
                   _pad_lanes(mla_krope_g[l], NOPE))
        qa, ka, va = _mla_prep(p, tabs, mla_wts, seq, t_row)
        o_a = _attention(qa, ka, va, p, COL_GATE_A, batch, seq, t_row)

        fox_wts = (_pad_lanes(fox_b_f[l], 0), row(jnp.tile(fox_q_g[l] * float(HD) ** -0.5, H)),
                   row(jnp.tile(fox_k_g[l], H)), tri, selq, selk, oneq, onek)
        qb, kb, vb = _fox_prep(p, fox_wts, batch, seq, t_row)
        o_b = _attention(qb, kb, vb, p, COL_GATE_B, batch, seq, t_row)

        mu = rwkv_mu[l]
        wup = jnp.zeros((LANES, W_BR), F32).at[:DECAY_LORA].set(rwkv_w_up[l]).astype(BF16)
        aup = jnp.zeros((LANES, W_BR), F32).at[DECAY_LORA:].set(rwkv_a_up[l]).astype(BF16)
        rw = [row(mu[:W_BR]), row(mu[W_BR:2 * W_BR]), row(mu[2 * W_BR:3 * W_BR]), row(mu[3 * W_BR:]),
              row(rwkv_w0[l]), row(rwkv_a0[l]), wup, aup, row(rwkv_k_k[l]), row(rwkv_k_a[l]),
              row(rwkv_r_k[l])]
        if l > 0:
            vdown = jnp.zeros((W_BR, LANES), F32).at[:, :MV_LORA].set(rwkv_v_down[l - 1]).astype(BF16)
            vup = jnp.zeros((LANES, W_BR), F32).at[:MV_LORA].set(rwkv_v_up[l - 1]).astype(BF16)
            rw += [row(rwkv_v0[l - 1]), vdown, vup]
        rw += [btri, bones]
        outs = _rwkv_prep(p, vfirst, rw, batch, seq, t_rwkv, first_layer=(l == 0))
        if l == 0:
            vfirst = outs[10]
        o_c = _rwkv_chunk(outs[:10], rwkv_lnx_g[l].reshape(H, 1, HD), rwkv_lnx_b[l].reshape(H, 1, HD),
                          batch, seq, _tile(seq, RWKV_TC), RWKV_PASSES)

        x2 = _merge(o_a, o_b, o_c, p, x2, w_pa[l].astype(BF16), w_pb[l].astype(BF16),
                    w_pc[l].reshape(H, HD, D_MODEL).astype(BF16), w_out[l].astype(BF16), batch, seq, t_row)
    return x2.reshape(batch, seq, d)
```

```python
import functools
import math

import jax
import jax.numpy as jnp
import numpy as np
from jax import lax
from jax.experimental import pallas as pl
from jax.experimental.pallas import tpu as pltpu

F32 = jnp.float32
BF16 = jnp.bfloat16

LANES = 128
H = 8
HD = 64
NOPE = 64
ROPE_D = 32
Q_LORA = 256
KV_LORA = 128
D_MODEL = 1024
W_BR = H * HD
DECAY_LORA = 64
AAA_LORA = 64
MV_LORA = 32
ROPE_THETA = 10000.0
RWKV_DECAY_SCALE = 0.606531
GN_EPS = 64e-5
EPS = 1e-6
NEG_INF = -1e30
CHUNK = 64

COL_MERGE = 0
COL_GATE_A = 3072
COL_FQ = 3584
COL_FK = 4096
COL_FV = 4608
COL_GATE_B = 5120
COL_SR = 5632
COL_SK = 6144
COL_SV = 6656
COL_GATE_C = 7168
NCOLS_WIDE = 7680
INPROJ_TN = 1280
COL_CQ = 0
COL_CKV = 256
COL_KR = 384
COL_FF = 512
COL_WA = 640
NCOLS_NARROW = 768

VMEM_LIMIT = 56 * 1024 * 1024


def _cparams(sem):
    return pltpu.CompilerParams(dimension_semantics=sem, vmem_limit_bytes=VMEM_LIMIT)


def _sigmoid(x):
    return 1.0 / (1.0 + jnp.exp(-x))


def _dot(a, b):
    return jnp.dot(a, b, preferred_element_type=F32)


def _split3(x):
    hi = x.astype(BF16)
    r1 = x - hi.astype(F32)
    mid = r1.astype(BF16)
    lo = (r1 - mid.astype(F32)).astype(BF16)
    return hi, mid, lo


def _dot_exact_lhs(m_bf16, x):
    hi, mid, lo = _split3(x)
    return _dot(m_bf16, hi) + _dot(m_bf16, mid) + _dot(m_bf16, lo)


def _pair_seg_sum(x):
    lane = lax.broadcasted_iota(jnp.int32, x.shape, 1)
    first = lane < HD
    s0 = jnp.sum(jnp.where(first, x, 0.0), axis=-1, keepdims=True)
    s1 = jnp.sum(jnp.where(first, 0.0, x), axis=-1, keepdims=True)
    return jnp.where(first, s0, s1)


def _inproj_kernel(x_ref, g_ref, w_ref, o_ref, h_ref):
    @pl.when(pl.program_id(1) == 0)
    def _():
        x = x_ref[...]
        ms = jnp.mean(x * x, axis=-1, keepdims=True)
        h_ref[...] = (x * lax.rsqrt(ms + EPS) * g_ref[...]).astype(BF16)

    o_ref[...] = _dot(h_ref[...], w_ref[...]).astype(o_ref.dtype)


def _inproj(x2, g, w, tm, tn, out_dtype):
    n, ncols = x2.shape[0], w.shape[1]
    return pl.pallas_call(
        _inproj_kernel,
        grid=(n // tm, ncols // tn),
        in_specs=[
            pl.BlockSpec((tm, D_MODEL), lambda i, j: (i, 0)),
            pl.BlockSpec((1, D_MODEL), lambda i, j: (0, 0)),
            pl.BlockSpec((D_MODEL, tn), lambda i, j: (0, j)),
        ],
        out_specs=pl.BlockSpec((tm, tn), lambda i, j: (i, j)),
        out_shape=jax.ShapeDtypeStruct((n, ncols), out_dtype),
        scratch_shapes=[pltpu.VMEM((tm, D_MODEL), BF16)],
        compiler_params=_cparams(("parallel", "arbitrary")),
        name="inproj",
    )(x2, g, w)


def _rms(x, g, n):
    ms = jnp.sum(x * x, axis=-1, keepdims=True) / n
    return x * lax.rsqrt(ms + EPS) * g


def _rope(x, c, s1, s2):
    return x * c + pltpu.roll(x, LANES - ROPE_D // 2, 1) * s1 + pltpu.roll(x, ROPE_D // 2, 1) * s2


def _mla_prep_kernel(cq_ref, ckv_ref, kr_ref, c_ref, s1_ref, s2_ref, qag_ref, wuq_ref, kvag_ref, wk_ref, wv_ref,
                     qg_ref, kng_ref, krg_ref, q_out, k_out, v_out):
    c, s1, s2 = c_ref[...], s1_ref[...], s2_ref[...]
    cqn = _rms(cq_ref[...], qag_ref[...], Q_LORA).astype(BF16)
    ckvn = _rms(ckv_ref[...], kvag_ref[...], KV_LORA).astype(BF16)
    q_all = _dot(cqn, wuq_ref[...])
    k_all = _dot(ckvn, wk_ref[...])
    v_out[...] = _dot(ckvn, wv_ref[...]).astype(BF16)
    k_rope = _rope(_rms(kr_ref[...], krg_ref[...], ROPE_D), c, s1, s2)
    qg, kng = qg_ref[...], kng_ref[...]
    for h in range(H):
        sl = slice(h * LANES, (h + 1) * LANES)
        q_out[:, sl] = _rope(_rms(q_all[:, sl], qg, NOPE + ROPE_D), c, s1, s2).astype(BF16)
        k_out[:, sl] = (_rms(k_all[:, sl], kng, NOPE) + k_rope).astype(BF16)


def _mla_prep(pn, tabs, wts, seq, t):
    n = pn.shape[0]
    ns = seq // t
    row = lambda i: (i, 0)
    const = lambda i: (0, 0)
    tab_spec = pl.BlockSpec((t, LANES), lambda i: (i % ns, 0))
    return pl.pallas_call(
        _mla_prep_kernel,
        grid=(n // t,),
        in_specs=[
            pl.BlockSpec((t, Q_LORA), lambda i: (i, COL_CQ // Q_LORA)),
            pl.BlockSpec((t, LANES), lambda i: (i, COL_CKV // LANES)),
            pl.BlockSpec((t, LANES), lambda i: (i, COL_KR // LANES)),
            tab_spec, tab_spec, tab_spec,
            pl.BlockSpec((1, Q_LORA), const),
            pl.BlockSpec((Q_LORA, H * LANES), const),
            pl.BlockSpec((1, KV_LORA), const),
            pl.BlockSpec((KV_LORA, H * LANES), const),
            pl.BlockSpec((KV_LORA, W_BR), const),
            pl.BlockSpec((1, LANES), const),
            pl.BlockSpec((1, LANES), const),
            pl.BlockSpec((1, LANES), const),
        ],
        out_specs=[
            pl.BlockSpec((t, H * LANES), row),
            pl.BlockSpec((t, H * LANES), row),
            pl.BlockSpec((t, W_BR), row),
        ],
        out_shape=[
            jax.ShapeDtypeStruct((n, H * LANES), BF16),
            jax.ShapeDtypeStruct((n, H * LANES), BF16),
            jax.ShapeDtypeStruct((n, W_BR), BF16),
        ],
        compiler_params=_cparams(("parallel",)),
        name="mla_prep",
    )(pn, pn, pn, *tabs, *wts)


def _fox_prep_kernel(fq_ref, fk_ref, ff_ref, bf_ref, qg_ref, kg_ref, tri_ref, selq_ref, selk_ref,
                     oneq_ref, onek_ref, q_out, k_out, carry_ref):
    t = fq_ref.shape[0]

    @pl.when(pl.program_id(1) == 0)
    def _():
        carry_ref[...] = jnp.zeros_like(carry_ref)

    z = ff_ref[...] + bf_ref[...]
    logf = jnp.minimum(z, 0.0) - jnp.log1p(jnp.exp(-jnp.abs(z)))
    cum = carry_ref[...] + _dot_exact_lhs(tri_ref[...], logf)
    carry_ref[...] = cum[t - 1:t, :]
    hi, mid, lo = _split3(cum)
    aug_q = _dot(hi, selq_ref[0]) + _dot(mid, selq_ref[1]) + _dot(lo, selq_ref[2]) + oneq_ref[...]
    aug_k = _dot(hi, selk_ref[0]) + _dot(mid, selk_ref[1]) + _dot(lo, selk_ref[2]) + onek_ref[...]
    lane = lax.broadcasted_iota(jnp.int32, (t, LANES), 1)
    first = lane < HD
    for src_ref, g_ref, aug, dst in ((fq_ref, qg_ref, aug_q, q_out), (fk_ref, kg_ref, aug_k, k_out)):
        for p in range(H // 2):
            sl = slice(p * LANES, (p + 1) * LANES)
            x = src_ref[:, sl].astype(F32)
            ms = _pair_seg_sum(x * x) / HD
            xn = x * lax.rsqrt(ms + EPS) * g_ref[:, sl]
            he, ho = 2 * p, 2 * p + 1
            dst[:, he * LANES:(he + 1) * LANES] = jnp.where(first, xn, aug[:, he * LANES:(he + 1) * LANES]).astype(BF16)
            dst[:, ho * LANES:(ho + 1) * LANES] = jnp.where(
                first, pltpu.roll(xn, HD, 1), aug[:, ho * LANES:(ho + 1) * LANES]).astype(BF16)


def _fox_prep(pw, pn, wts, batch, seq, t):
    n = pw.shape[0]
    ns = seq // t
    row = lambda b, i: (b * ns + i, 0)
    const2 = lambda b, i: (0, 0)
    const3 = lambda b, i: (0, 0, 0)
    return pl.pallas_call(
        _fox_prep_kernel,
        grid=(batch, ns),
        in_specs=[
            pl.BlockSpec((t, W_BR), lambda b, i: (b * ns + i, COL_FQ // W_BR)),
            pl.BlockSpec((t, W_BR), lambda b, i: (b * ns + i, COL_FK // W_BR)),
            pl.BlockSpec((t, LANES), lambda b, i: (b * ns + i, COL_FF // LANES)),
            pl.BlockSpec((1, LANES), const2),
            pl.BlockSpec((1, W_BR), const2),
            pl.BlockSpec((1, W_BR), const2),
            pl.BlockSpec((t, t), const2),
            pl.BlockSpec((3, LANES, H * LANES), const3),
            pl.BlockSpec((3, LANES, H * LANES), const3),
            pl.BlockSpec((1, H * LANES), const2),
            pl.BlockSpec((1, H * LANES), const2),
        ],
        out_specs=[
            pl.BlockSpec((t, H * LANES), row),
            pl.BlockSpec((t, H * LANES), row),
        ],
        out_shape=[
            jax.ShapeDtypeStruct((n, H * LANES), BF16),
            jax.ShapeDtypeStruct((n, H * LANES), BF16),
        ],
        scratch_shapes=[pltpu.VMEM((1, LANES), F32)],
        compiler_params=_cparams(("parallel", "arbitrary")),
        name="fox_prep",
    )(pw, pw, pn, *wts)


def _attn_kernel(q_ref, k_ref, v_ref, g_ref, o_ref, m_ref, acc_ref, sa_ref, sb_ref, *, tq):
    qi = pl.program_id(2)
    nt = (((1,), (1,)), ((), ()))
    row = lax.broadcasted_iota(jnp.int32, (tq, tq), 0)
    col = lax.broadcasted_iota(jnp.int32, (tq, tq), 1)
    causal = col <= row
    ones = jnp.ones((tq, LANES), BF16)
    m_ref[...] = jnp.full(m_ref.shape, NEG_INF, F32)
    acc_ref[...] = jnp.zeros(acc_ref.shape, F32)

    def scores(c, s_ref):
        start = pl.multiple_of(c * tq, tq)
        for j in range(2):
            q = q_ref[:, j * LANES:(j + 1) * LANES]
            k = k_ref[pl.ds(start, tq), j * LANES:(j + 1) * LANES]
            s_ref[j] = lax.dot_general(q, k, nt, preferred_element_type=F32)

    def consume(c, s_ref, masked):
        start = pl.multiple_of(c * tq, tq)
        v = jnp.concatenate([v_ref[pl.ds(start, tq), :], ones], axis=1)
        for j in range(2):
            s = s_ref[j]
            if masked:
                s = jnp.where(causal, s, NEG_INF)
            m_old = m_ref[j]
            m_new = jnp.maximum(m_old, jnp.max(s, axis=-1, keepdims=True))
            alpha = jnp.exp(m_old - m_new)
            p = jnp.exp(s - jnp.concatenate([m_new] * (tq // LANES), axis=1))
            acc_ref[j] = jnp.concatenate([alpha, alpha], axis=1) * acc_ref[j] + _dot(p.astype(BF16), v)
            m_ref[j] = m_new

    scores(0, sa_ref)

    def body(i, carry):
        c = 2 * i
        scores(c + 1, sb_ref)
        consume(c, sa_ref, False)
        scores(c + 2, sa_ref)
        consume(c + 1, sb_ref, False)
        return carry

    lax.fori_loop(0, qi // 2, body, 0)

    @pl.when(qi % 2 == 1)
    def _():
        scores(qi, sb_ref)
        consume(qi - 1, sa_ref, False)
        consume(qi, sb_ref, True)

    @pl.when(qi % 2 == 0)
    def _():
        consume(qi, sa_ref, True)

    outs = [acc_ref[j, :, :LANES] / acc_ref[j, :, LANES:] for j in range(2)]
    lane = lax.broadcasted_iota(jnp.int32, (tq, LANES), 1)
    o = jnp.where(lane < HD, outs[0], outs[1])
    g = g_ref[...].astype(F32)
    o_ref[...] = (o * (g * _sigmoid(g))).astype(BF16)


def _attention(q, k, v, v_col, pw, gate_col, batch, seq, tq):
    n = q.shape[0]
    nq = seq // tq
    return pl.pallas_call(
        functools.partial(_attn_kernel, tq=tq),
        grid=(batch, H // 2, nq),
        in_specs=[
            pl.BlockSpec((tq, 2 * LANES), lambda b, hp, i: (b * nq + i, hp)),
            pl.BlockSpec((seq, 2 * LANES), lambda b, hp, i: (b, hp)),
            pl.BlockSpec((seq, LANES), lambda b, hp, i: (b, v_col // LANES + hp)),
            pl.BlockSpec((tq, LANES), lambda b, hp, i: (b * nq + i, gate_col // LANES + hp)),
        ],
        out_specs=pl.BlockSpec((tq, LANES), lambda b, hp, i: (b * nq + i, hp)),
        out_shape=jax.ShapeDtypeStruct((n, W_BR), BF16),
        scratch_shapes=[
            pltpu.VMEM((2, tq, LANES), F32),
            pltpu.VMEM((2, tq, 2 * LANES), F32),
            pltpu.VMEM((2, tq, tq), F32),
            pltpu.VMEM((2, tq, tq), F32),
        ],
        compiler_params=_cparams(("parallel", "parallel", "arbitrary")),
        name="attention",
    )(q, k, v, pw)


def _store_heads(o_ref, x):
    for p in range(H // 2):
        blk = x[:, p * LANES:(p + 1) * LANES]
        o_ref[0, 2 * p] = blk[:, :HD].astype(o_ref.dtype)
        o_ref[0, 2 * p + 1] = pltpu.roll(blk, HD, 1)[:, :HD].astype(o_ref.dtype)


def _rwkv_prep_kernel(*refs, first_layer):
    if first_layer:
        (sr_ref, sk_ref, sv_ref, swa_ref, gc_ref, mur_ref, muk_ref, muv_ref, muwa_ref, w0_ref, a0_ref, wup_ref,
         aup_ref, kk_ref, ka_ref, rk_ref, bt_ref, bo_ref,
         at_o, rt_o, bt_o, kt_o, v_o, bh_o, kh_o, wc_o, bonus_o, sg_o, vfirst_o,
         cr_ref, ck_ref, cv_ref, cwa_ref) = refs
    else:
        (sr_ref, sk_ref, sv_ref, swa_ref, gc_ref, vf_ref, mur_ref, muk_ref, muv_ref, muwa_ref, w0_ref, a0_ref,
         wup_ref, aup_ref, kk_ref, ka_ref, rk_ref, v0_ref, vdown_ref, vup_ref, bt_ref, bo_ref,
         at_o, rt_o, bt_o, kt_o, v_o, bh_o, kh_o, wc_o, bonus_o, sg_o,
         cr_ref, ck_ref, cv_ref, cwa_ref) = refs
    t = sr_ref.shape[0]

    @pl.when(pl.program_id(1) == 0)
    def _():
        for c_ref in (cr_ref, ck_ref, cv_ref, cwa_ref):
            c_ref[...] = jnp.zeros_like(c_ref)

    def shift(x_ref, c_ref, mu_ref):
        x = x_ref[...].astype(F32)
        rowid = lax.broadcasted_iota(jnp.int32, x.shape, 0)
        prev = jnp.where(rowid == 0, c_ref[...], pltpu.roll(x, 1, 0))
        c_ref[...] = x[t - 1:t, :]
        return x + mu_ref[...] * (prev - x)

    r = shift(sr_ref, cr_ref, mur_ref)
    k = shift(sk_ref, ck_ref, muk_ref)
    v = shift(sv_ref, cv_ref, muv_ref)
    wa = shift(swa_ref, cwa_ref, muwa_ref)

    logw = -RWKV_DECAY_SCALE * _sigmoid(w0_ref[...] + _dot(jnp.tanh(wa).astype(BF16), wup_ref[...]))
    a = _sigmoid(a0_ref[...] + _dot(wa.astype(BF16), aup_ref[...]))
    if first_layer:
        vfirst_o[...] = v
    else:
        low = _dot(v.astype(BF16), vdown_ref[...]).astype(BF16)
        nu = _sigmoid(v0_ref[...] + _dot(low, vup_ref[...]))
        v = v + (vf_ref[...] - v) * nu

    kk = k * kk_ref[...]
    k_mod = k * (1.0 + (a - 1.0) * ka_ref[...])
    rk = r * k_mod * rk_ref[...]
    kk_n, bonus = [], []
    for p in range(H // 2):
        sl = slice(p * LANES, (p + 1) * LANES)
        nrm = jnp.sqrt(_pair_seg_sum(kk[:, sl] * kk[:, sl]))
        kk_n.append(kk[:, sl] / jnp.maximum(nrm, 1e-12))
        bonus.append(_pair_seg_sum(rk[:, sl]) * v[:, sl])
    kk = jnp.concatenate(kk_n, axis=-1)
    bonus = jnp.concatenate(bonus, axis=-1)
    a_vec = -kk
    b_vec = kk * a

    lcum = _dot_exact_lhs(bt_ref[...], logw)
    ltot = _dot_exact_lhs(bo_ref[...], logw)
    e_minus = jnp.exp(-lcum)
    e_rem = jnp.exp(ltot - lcum)
    _store_heads(at_o, a_vec * jnp.exp(lcum - logw))
    _store_heads(rt_o, r * jnp.exp(lcum))
    _store_heads(bt_o, b_vec * e_minus)
    _store_heads(kt_o, k_mod * e_minus)
    _store_heads(v_o, v)
    _store_heads(bh_o, b_vec * e_rem)
    _store_heads(kh_o, k_mod * e_rem)
    _store_heads(wc_o, jnp.exp(ltot))
    _store_heads(bonus_o, bonus)
    g = gc_ref[...].astype(F32)
    _store_heads(sg_o, g * _sigmoid(g))


def _rwkv_prep(pw, pn, vfirst, wts, batch, seq, t, first_layer):
    n = pw.shape[0]
    ns = seq // t
    const2 = lambda b, i: (0, 0)
    pcol = lambda col, w: pl.BlockSpec((t, w), lambda b, i: (b * ns + i, col // w))
    vec = pl.BlockSpec((1, W_BR), const2)
    in_specs = [pcol(COL_SR, W_BR), pcol(COL_SK, W_BR), pcol(COL_SV, W_BR), pcol(COL_WA, LANES),
                pcol(COL_GATE_C, W_BR)]
    args = [pw, pw, pw, pn, pw]
    if not first_layer:
        in_specs.append(pl.BlockSpec((t, W_BR), lambda b, i: (b * ns + i, 0)))
        args.append(vfirst)
    in_specs += [vec, vec, vec, pl.BlockSpec((1, LANES), const2), vec, vec,
                 pl.BlockSpec((LANES, W_BR), const2), pl.BlockSpec((LANES, W_BR), const2), vec, vec, vec]
    if not first_layer:
        in_specs += [vec, pl.BlockSpec((W_BR, LANES), const2), pl.BlockSpec((LANES, W_BR), const2)]
    in_specs += [pl.BlockSpec((t, t), const2), pl.BlockSpec((t, t), const2)]
    args += list(wts)
    hm_spec = pl.BlockSpec((1, H, t, HD), lambda b, i: (b, 0, i, 0))
    hm_dtypes = (BF16, F32, BF16, BF16, BF16, BF16, BF16, F32, F32, F32)
    out_specs = [hm_spec] * 10
    out_shape = [jax.ShapeDtypeStruct((batch, H, seq, HD), dt) for dt in hm_dtypes]
    if first_layer:
        out_specs.append(pl.BlockSpec((t, W_BR), lambda b, i: (b * ns + i, 0)))
        out_shape.append(jax.ShapeDtypeStruct((n, W_BR), F32))
    return pl.pallas_call(
        functools.partial(_rwkv_prep_kernel, first_layer=first_layer),
        grid=(batch, ns),
        in_specs=in_specs,
        out_specs=out_specs,
        out_shape=out_shape,
        scratch_shapes=[pltpu.VMEM((1, W_BR), F32), pltpu.VMEM((1, W_BR), F32), pltpu.VMEM((1, W_BR), F32),
                        pltpu.VMEM((1, LANES), F32)],
        compiler_params=_cparams(("parallel", "arbitrary")),
        name="rwkv_prep",
    )(*args)


def _mm(a, b, dims):
    return lax.dot_general(a.astype(BF16), b.astype(BF16), dims, preferred_element_type=F32)


_NN = (((1,), (0,)), ((), ()))
_NT = (((1,), (1,)), ((), ()))
_TN = (((0,), (0,)), ((), ()))


def _rwkv_chunk_kernel(at_ref, rt_ref, bt_ref, kt_ref, v_ref, bh_ref, kh_ref, wc_ref, bonus_ref, sg_ref,
                       lg_ref, lb_ref, o_ref, s_all_ref, *, nchunk):
    @pl.when(pl.program_id(1) == 0)
    def _():
        s_all_ref[...] = jnp.zeros_like(s_all_ref)

    mm = _mm
    row = lax.broadcasted_iota(jnp.int32, (CHUNK, CHUNK), 0)
    col = lax.broadcasted_iota(jnp.int32, (CHUNK, CHUNK), 1)
    strict = col < row
    incl = col <= row
    eye = col == row
    items = [(c, h) for c in range(nchunk) for h in range(H)]
    each = lambda f, *cols: [f(*xs) for xs in zip(*cols)]
    sls = [slice(c * CHUNK, (c + 1) * CHUNK) for c, _ in items]
    ld = lambda ref: [ref[0, h, sl, :] for (_, h), sl in zip(items, sls)]
    at, rt, bt, kt, v, bh, kh = ld(at_ref), ld(rt_ref), ld(bt_ref), ld(kt_ref), ld(v_ref), ld(bh_ref), ld(kh_ref)
    x1 = each(lambda a, r: jnp.concatenate([a, r.astype(BF16)], axis=0), at, rt)
    gb = each(lambda x, b: mm(x, b, _NT), x1, bt)
    gk = each(lambda x, k: mm(x, k, _NT), x1, kt)
    a_ab = [jnp.where(strict, g[:CHUNK], 0.0) for g in gb]
    a_rb = [jnp.where(incl, g[CHUNK:], 0.0) for g in gb]
    a_ak = [jnp.where(strict, g[:CHUNK], 0.0) for g in gk]
    a_rk = [jnp.where(incl, g[CHUNK:], 0.0) for g in gk]
    tinv = [jnp.where(eye, 1.0, a) for a in a_ab]
    pw = a_ab
    for _ in range(int(math.log2(CHUNK)) - 1):
        pw = each(lambda p: mm(p, p, _NN), pw)
        tinv = each(lambda t, p: t + mm(t, p, _NN), tinv, pw)
    av = each(lambda a, x: mm(a, x, _NN), a_ak, v)
    u = each(lambda t, x: mm(t, x, _NN), tinv, av)
    a_til = each(lambda t, x: mm(t, x, _NN), tinv, at)
    r_hat = each(lambda r, a, x: r + mm(a, x, _NN), rt, a_rb, a_til)
    y_in = each(lambda a, x: mm(a, x, _NN), a_rb, u)
    y_in = each(lambda y, a, x: y + mm(a, x, _NN), y_in, a_rk, v)
    m_new = each(lambda a, b: mm(a, b, _TN), a_til, bh)
    n_new = each(lambda a, b: mm(a, b, _TN), u, bh)
    n_new = each(lambda n, a, b: n + mm(a, b, _TN), n_new, v, kh)
    for c in range(nchunk):
        idx = [i for i, (ci, _) in enumerate(items) if ci == c]
        s0 = [s_all_ref[h] for h in range(H)]
        y = [y_in[i] + mm(r_hat[i], s0[h], _NT) for h, i in enumerate(idx)]
        for h, i in enumerate(idx):
            wc = wc_ref[0, h, c * CHUNK:c * CHUNK + 1, :]
            s_all_ref[h] = mm(s0[h], jnp.where(eye, wc, 0.0) + m_new[i], _NN) + n_new[i]
        for h, i in enumerate(idx):
            mu = jnp.mean(y[h], axis=-1, keepdims=True)
            d = y[h] - mu
            var = jnp.mean(d * d, axis=-1, keepdims=True)
            yn = d * lax.rsqrt(var + GN_EPS) * lg_ref[h] + lb_ref[h]
            o_ref[0, h, sls[i], :] = ((yn + bonus_ref[0, h, sls[i], :]) * sg_ref[0, h, sls[i], :]).astype(o_ref.dtype)


def _rwkv_chunk(hm, lnx_g, lnx_b, batch, seq, tc):
    spec = pl.BlockSpec((1, H, tc, HD), lambda b, i: (b, 0, i, 0))
    gspec = pl.BlockSpec((H, 1, HD), lambda b, i: (0, 0, 0))
    return pl.pallas_call(
        functools.partial(_rwkv_chunk_kernel, nchunk=tc // CHUNK),
        grid=(batch, seq // tc),
        in_specs=[spec] * 10 + [gspec, gspec],
        out_specs=spec,
        out_shape=jax.ShapeDtypeStruct((batch, H, seq, HD), BF16),
        scratch_shapes=[pltpu.VMEM((H, HD, HD), F32)],
        compiler_params=_cparams(("parallel", "arbitrary")),
        name="rwkv_chunk",
    )(*hm, lnx_g, lnx_b)


def _merge_kernel(oa_ref, ob_ref, oc_ref, ga_ref, gb_ref, gc_ref, x_ref, wpa_ref, wpb_ref, wpc_ref, wout_ref,
                  o_ref):
    pa = _dot(oa_ref[...], wpa_ref[...])
    pb = _dot(ob_ref[...], wpb_ref[...])
    pc = _dot(oc_ref[0, 0], wpc_ref[0])
    for h in range(1, H):
        pc = pc + _dot(oc_ref[0, h], wpc_ref[h])
    sig = lambda ref: _sigmoid(ref[...].astype(F32))
    merged = sig(ga_ref) * pa + sig(gb_ref) * pb + sig(gc_ref) * pc
    o_ref[...] = x_ref[...] + _dot(merged.astype(BF16), wout_ref[...])


def _merge(oa, ob, oc, p, x2, wpa, wpb, wpc, wout, batch, seq, t):
    n = x2.shape[0]
    ns = seq // t
    row = lambda b, i: (b * ns + i, 0)
    const2 = lambda b, i: (0, 0)
    gate = lambda g: pl.BlockSpec((t, D_MODEL), lambda b, i: (b * ns + i, COL_MERGE // D_MODEL + g))
    return pl.pallas_call(
        _merge_kernel,
        grid=(batch, ns),
        in_specs=[
            pl.BlockSpec((t, W_BR), row),
            pl.BlockSpec((t, W_BR), row),
            pl.BlockSpec((1, H, t, HD), lambda b, i: (b, 0, i, 0)),
            gate(0), gate(1), gate(2),
            pl.BlockSpec((t, D_MODEL), row),
            pl.BlockSpec((W_BR, D_MODEL), const2),
            pl.BlockSpec((W_BR, D_MODEL), const2),
            pl.BlockSpec((H, HD, D_MODEL), lambda b, i: (0, 0, 0)),
            pl.BlockSpec((D_MODEL, D_MODEL), const2),
        ],
        out_specs=pl.BlockSpec((t, D_MODEL), row),
        out_shape=jax.ShapeDtypeStruct((n, D_MODEL), F32),
        compiler_params=_cparams(("parallel", "parallel")),
        name="merge",
    )(oa, ob, oc, p, p, p, x2, wpa, wpb, wpc, wout)


def _regroup_w_in(w):
    d = w.shape[0]
    sizes = (Q_LORA, KV_LORA, ROPE_D, W_BR, W_BR, W_BR, W_BR, H, W_BR,
             3 * W_BR + DECAY_LORA + AAA_LORA, W_BR, 3 * D_MODEL)
    offs = np.concatenate([[0], np.cumsum(sizes)])
    (c_q, c_kv, k_r, gate_a, fq, fk, fv, ff, gate_b, shift, gate_c, merge) = [
        w[:, offs[i]:offs[i + 1]] for i in range(len(sizes))]
    z = lambda n: jnp.zeros((d, n), w.dtype)
    wide = jnp.concatenate([
        merge, gate_a, fq, fk, fv, gate_b,
        shift[:, :W_BR], shift[:, W_BR:2 * W_BR], shift[:, 2 * W_BR:3 * W_BR], gate_c,
    ], axis=1)
    narrow = jnp.concatenate([
        c_q, c_kv,
        z(NOPE), k_r, z(LANES - NOPE - ROPE_D),
        ff, z(LANES - H),
        shift[:, 3 * W_BR:],
    ], axis=1)
    return wide.astype(BF16), narrow.astype(BF16)


def _rope_tables(seq):
    half = ROPE_D // 2
    inv = ROPE_THETA ** (-jnp.arange(0, ROPE_D, 2, dtype=F32) / ROPE_D)
    ang = jnp.arange(seq, dtype=F32)[:, None] * inv[None, :]
    cos, sin = jnp.cos(ang), jnp.sin(ang)
    z = lambda n: jnp.zeros((seq, n), F32)
    c = jnp.concatenate([jnp.ones((seq, NOPE), F32), cos, cos, z(LANES - NOPE - ROPE_D)], axis=1)
    s1 = jnp.concatenate([z(NOPE), -sin, z(LANES - NOPE - half)], axis=1)
    s2 = jnp.concatenate([z(NOPE + half), sin, z(LANES - NOPE - ROPE_D)], axis=1)
    return c, s1, s2


def _pad_lanes(v, lo, total=LANES):
    return jnp.zeros((1, total), F32).at[0, lo:lo + v.shape[0]].set(v)


def _fox_selectors():
    selq = np.zeros((3, LANES, H * LANES), np.float32)
    selk = np.zeros((3, LANES, H * LANES), np.float32)
    oneq = np.zeros((1, H * LANES), np.float32)
    onek = np.zeros((1, H * LANES), np.float32)
    for h in range(H):
        for j in range(3):
            selq[j, h, h * LANES + HD + j] = 1.0
            selk[j, h, h * LANES + HD + 3 + j] = -1.0
            oneq[0, h * LANES + HD + 3 + j] = 1.0
            onek[0, h * LANES + HD + j] = 1.0
    return (jnp.asarray(selq, BF16), jnp.asarray(selk, BF16), jnp.asarray(oneq), jnp.asarray(onek))


def _tri(t):
    r = np.arange(t)
    return jnp.asarray((r[None, :] <= r[:, None]).astype(np.float32), BF16)


def _chunk_tri(t):
    r = np.arange(t)
    same = (r[None, :] // CHUNK) == (r[:, None] // CHUNK)
    lower = r[None, :] <= r[:, None]
    return (jnp.asarray((same & lower).astype(np.float32), BF16), jnp.asarray(same.astype(np.float32), BF16))


def _tile(seq, pref):
    return pref if seq % pref == 0 else seq


RWKV_TC = 128


def kernel(x, norm_g, w_in, mla_qa_g, mla_w_uq, mla_kva_g, mla_w_ukv, mla_q_g, mla_knope_g, mla_krope_g, fox_b_f, fox_q_g, fox_k_g, rwkv_mu, rwkv_w0, rwkv_w_up, rwkv_a0, rwkv_a_up, rwkv_k_k, rwkv_k_a, rwkv_r_k, rwkv_lnx_g, rwkv_lnx_b, rwkv_v0, rwkv_v_down, rwkv_v_up, w_pa, w_pb, w_pc, w_out):
    batch, seq, d = x.shape
    depth = w_in.shape[0]
    n = batch * seq
    t_row = _tile(seq, 512)
    t_in = _tile(n, 1024)
    t_rwkv = _tile(seq, 256)
    x2 = x.reshape(n, d)

    tabs = _rope_tables(seq)
    selq, selk, oneq, onek = _fox_selectors()
    tri = _tri(t_row)
    btri, bones = _chunk_tri(t_rwkv)
    row = lambda v: v.reshape(1, -1).astype(F32)

    vfirst = None
    for l in range(depth):
        w_wide, w_narrow = _regroup_w_in(w_in[l])
        pw = _inproj(x2, row(norm_g[l]), w_wide, t_in, INPROJ_TN, BF16)
        pn = _inproj(x2, row(norm_g[l]), w_narrow, t_in, NCOLS_NARROW, F32)

        wuq = mla_w_uq[l].reshape(Q_LORA, H, NOPE + ROPE_D)
        wuq = jnp.pad(wuq, ((0, 0), (0, 0), (0, LANES - NOPE - ROPE_D))).reshape(Q_LORA, H * LANES).astype(BF16)
        wukv = mla_w_ukv[l].reshape(KV_LORA, H, NOPE + HD)
        wk = jnp.pad(wukv[:, :, :NOPE], ((0, 0), (0, 0), (0, LANES - NOPE))).reshape(KV_LORA, H * LANES).astype(BF16)
        wv = wukv[:, :, NOPE:].reshape(KV_LORA, W_BR).astype(BF16)
        mla_scale = float(NOPE + ROPE_D) ** -0.5
        mla_wts = (row(mla_qa_g[l]), wuq, row(mla_kva_g[l]), wk, wv,
                   _pad_lanes(mla_q_g[l] * mla_scale, 0), _pad_lanes(mla_knope_g[l], 0),
                   _pad_lanes(mla_krope_g[l], NOPE))
        qa, ka, va = _mla_prep(pn, tabs, mla_wts, seq, t_row)
        o_a = _attention(qa, ka, va, 0, pw, COL_GATE_A, batch, seq, t_row)

        fox_wts = (_pad_lanes(fox_b_f[l], 0), row(jnp.tile(fox_q_g[l] * float(HD) ** -0.5, H)),
                   row(jnp.tile(fox_k_g[l], H)), tri, selq, selk, oneq, onek)
        qb, kb = _fox_prep(pw, pn, fox_wts, batch, seq, t_row)
        o_b = _attention(qb, kb, pw, COL_FV, pw, COL_GATE_B, batch, seq, t_row)

        mu = rwkv_mu[l]
        wup = jnp.zeros((LANES, W_BR), F32).at[:DECAY_LORA].set(rwkv_w_up[l]).astype(BF16)
        aup = jnp.zeros((LANES, W_BR), F32).at[DECAY_LORA:].set(rwkv_a_up[l]).astype(BF16)
        rw = [row(mu[:W_BR]), row(mu[W_BR:2 * W_BR]), row(mu[2 * W_BR:3 * W_BR]), row(mu[3 * W_BR:]),
              row(rwkv_w0[l]), row(rwkv_a0[l]), wup, aup, row(rwkv_k_k[l]), row(rwkv_k_a[l]),
              row(rwkv_r_k[l])]
        if l > 0:
            vdown = jnp.zeros((W_BR, LANES), F32).at[:, :MV_LORA].set(rwkv_v_down[l - 1]).astype(BF16)
            vup = jnp.zeros((LANES, W_BR), F32).at[:MV_LORA].set(rwkv_v_up[l - 1]).astype(BF16)
            rw += [row(rwkv_v0[l - 1]), vdown, vup]
        rw += [btri, bones]
        outs = _rwkv_prep(pw, pn, vfirst, rw, batch, seq, t_rwkv, first_layer=(l == 0))
        if l == 0:
            vfirst = outs[10]
        o_c = _rwkv_chunk(outs[:10], rwkv_lnx_g[l].reshape(H, 1, HD), rwkv_lnx_b[l].reshape(H, 1, HD),
                          batch, seq, _tile(seq, RWKV_TC))

        x2 = _merge(o_a, o_b, o_c, pw, x2, w_pa[l].astype(BF16), w_pb[l].astype(BF16),
                    w_pc[l].reshape(H, HD, D_MODEL).astype(BF16), w_out[l].astype(BF16), batch, seq, t_row)
    return x2.reshape(batch, seq, d)
```

```python
import functools
import math

import jax
import jax.numpy as jnp
import numpy as np
from jax import lax
from jax.experimental import pallas as pl
from jax.experimental.pallas import tpu as pltpu

F32 = jnp.float32
BF16 = jnp.bfloat16

LANES = 128
H = 8
HD = 64
NOPE = 64
ROPE_D = 32
Q_LORA = 256
KV_LORA = 128
D_MODEL = 1024
W_BR = H * HD
DECAY_LORA = 64
AAA_LORA = 64
MV_LORA = 32
ROPE_THETA = 10000.0
RWKV_DECAY_SCALE = 0.606531
GN_EPS = 64e-5
EPS = 1e-6
NEG_INF = -1e30
CHUNK = 64
ONES_ROWS = 16

COL_MERGE = 0
COL_GATE_A = 3072
COL_FQ = 3584
COL_FK = 4096
COL_FV = 4608
COL_GATE_B = 5120
COL_SR = 5632
COL_SK = 6144
COL_SV = 6656
COL_GATE_C = 7168
NCOLS_WIDE = 7680
INPROJ_TN = 1280
COL_CQ = 0
COL_CKV = 256
COL_KR = 384
COL_FF = 512
COL_WA = 640
NCOLS_NARROW = 768

VMEM_LIMIT = 56 * 1024 * 1024


def _cparams(sem):
    return pltpu.CompilerParams(dimension_semantics=sem, vmem_limit_bytes=VMEM_LIMIT)


def _sigmoid(x):
    return 1.0 / (1.0 + jnp.exp(-x))


def _dot(a, b):
    return jnp.dot(a, b, preferred_element_type=F32)


def _split3(x):
    hi = x.astype(BF16)
    r1 = x - hi.astype(F32)
    mid = r1.astype(BF16)
    lo = (r1 - mid.astype(F32)).astype(BF16)
    return hi, mid, lo


def _dot_exact_lhs(m_bf16, x):
    hi, mid, lo = _split3(x)
    return _dot(m_bf16, hi) + _dot(m_bf16, mid) + _dot(m_bf16, lo)


def _pair_seg_sum(x):
    lane = lax.broadcasted_iota(jnp.int32, x.shape, 1)
    first = lane < HD
    s0 = jnp.sum(jnp.where(first, x, 0.0), axis=-1, keepdims=True)
    s1 = jnp.sum(jnp.where(first, 0.0, x), axis=-1, keepdims=True)
    return jnp.where(first, s0, s1)


def _inproj_kernel(x_ref, g_ref, w_ref, o_ref, h_ref):
    @pl.when(pl.program_id(1) == 0)
    def _():
        x = x_ref[...]
        ms = jnp.mean(x * x, axis=-1, keepdims=True)
        h_ref[...] = (x * lax.rsqrt(ms + EPS) * g_ref[...]).astype(BF16)

    o_ref[...] = _dot(h_ref[...], w_ref[...]).astype(o_ref.dtype)


def _inproj(x2, g, w, tm, tn, out_dtype):
    n, ncols = x2.shape[0], w.shape[1]
    return pl.pallas_call(
        _inproj_kernel,
        grid=(n // tm, ncols // tn),
        in_specs=[
            pl.BlockSpec((tm, D_MODEL), lambda i, j: (i, 0)),
            pl.BlockSpec((1, D_MODEL), lambda i, j: (0, 0)),
            pl.BlockSpec((D_MODEL, tn), lambda i, j: (0, j)),
        ],
        out_specs=pl.BlockSpec((tm, tn), lambda i, j: (i, j)),
        out_shape=jax.ShapeDtypeStruct((n, ncols), out_dtype),
        scratch_shapes=[pltpu.VMEM((tm, D_MODEL), BF16)],
        compiler_params=_cparams(("parallel", "arbitrary")),
        name="inproj",
    )(x2, g, w)


def _rms(x, g, n):
    ms = jnp.sum(x * x, axis=-1, keepdims=True) / n
    return x * lax.rsqrt(ms + EPS) * g


def _rope(x, c, s1, s2):
    return x * c + pltpu.roll(x, LANES - ROPE_D // 2, 1) * s1 + pltpu.roll(x, ROPE_D // 2, 1) * s2


def _mla_prep_kernel(cq_ref, ckv_ref, kr_ref, c_ref, s1_ref, s2_ref, qag_ref, wuq_ref, kvag_ref, wk_ref, wv_ref,
                     qg_ref, kng_ref, krg_ref, q_out, k_out, vt_out):
    c, s1, s2 = c_ref[...], s1_ref[...], s2_ref[...]
    cqn = _rms(cq_ref[...], qag_ref[...], Q_LORA).astype(BF16)
    ckvn = _rms(ckv_ref[...], kvag_ref[...], KV_LORA).astype(BF16)
    q_all = _dot(cqn, wuq_ref[...])
    k_all = _dot(ckvn, wk_ref[...])
    vt_out[0, 0] = _dot(ckvn, wv_ref[...]).T.astype(BF16)
    k_rope = _rope(_rms(kr_ref[...], krg_ref[...], ROPE_D), c, s1, s2)
    qg, kng = qg_ref[...], kng_ref[...]
    for h in range(H):
        sl = slice(h * LANES, (h + 1) * LANES)
        q_out[:, sl] = _rope(_rms(q_all[:, sl], qg, NOPE + ROPE_D), c, s1, s2).astype(BF16)
        k_out[:, sl] = (_rms(k_all[:, sl], kng, NOPE) + k_rope).astype(BF16)


def _mla_prep(pn, tabs, wts, seq, t):
    n = pn.shape[0]
    ns = seq // t
    row = lambda i: (i, 0)
    const = lambda i: (0, 0)
    tab_spec = pl.BlockSpec((t, LANES), lambda i: (i % ns, 0))
    return pl.pallas_call(
        _mla_prep_kernel,
        grid=(n // t,),
        in_specs=[
            pl.BlockSpec((t, Q_LORA), lambda i: (i, COL_CQ // Q_LORA)),
            pl.BlockSpec((t, LANES), lambda i: (i, COL_CKV // LANES)),
            pl.BlockSpec((t, LANES), lambda i: (i, COL_KR // LANES)),
            tab_spec, tab_spec, tab_spec,
            pl.BlockSpec((1, Q_LORA), const),
            pl.BlockSpec((Q_LORA, H * LANES), const),
            pl.BlockSpec((1, KV_LORA), const),
            pl.BlockSpec((KV_LORA, H * LANES), const),
            pl.BlockSpec((KV_LORA, W_BR), const),
            pl.BlockSpec((1, LANES), const),
            pl.BlockSpec((1, LANES), const),
            pl.BlockSpec((1, LANES), const),
        ],
        out_specs=[
            pl.BlockSpec((t, H * LANES), row),
            pl.BlockSpec((t, H * LANES), row),
            pl.BlockSpec((1, 1, W_BR, t), lambda i: (i // ns, i % ns, 0, 0)),
        ],
        out_shape=[
            jax.ShapeDtypeStruct((n, H * LANES), BF16),
            jax.ShapeDtypeStruct((n, H * LANES), BF16),
            jax.ShapeDtypeStruct((n // seq, ns, W_BR, t), BF16),
        ],
        compiler_params=_cparams(("parallel",)),
        name="mla_prep",
    )(pn, pn, pn, *tabs, *wts)


def _fox_prep_kernel(fq_ref, fk_ref, fv_ref, ff_ref, bf_ref, qg_ref, kg_ref, tri_ref, selq_ref, selk_ref,
                     oneq_ref, onek_ref, q_out, k_out, vt_out, carry_ref):
    t = fq_ref.shape[0]

    @pl.when(pl.program_id(1) == 0)
    def _():
        carry_ref[...] = jnp.zeros_like(carry_ref)

    z = ff_ref[...] + bf_ref[...]
    logf = jnp.minimum(z, 0.0) - jnp.log1p(jnp.exp(-jnp.abs(z)))
    cum = carry_ref[...] + _dot_exact_lhs(tri_ref[...], logf)
    carry_ref[...] = cum[t - 1:t, :]
    hi, mid, lo = _split3(cum)
    aug_q = _dot(hi, selq_ref[0]) + _dot(mid, selq_ref[1]) + _dot(lo, selq_ref[2]) + oneq_ref[...]
    aug_k = _dot(hi, selk_ref[0]) + _dot(mid, selk_ref[1]) + _dot(lo, selk_ref[2]) + onek_ref[...]
    vt_out[0, 0] = fv_ref[...].astype(F32).T.astype(BF16)
    lane = lax.broadcasted_iota(jnp.int32, (t, LANES), 1)
    first = lane < HD
    for src_ref, g_ref, aug, dst in ((fq_ref, qg_ref, aug_q, q_out), (fk_ref, kg_ref, aug_k, k_out)):
        for p in range(H // 2):
            sl = slice(p * LANES, (p + 1) * LANES)
            x = src_ref[:, sl].astype(F32)
            ms = _pair_seg_sum(x * x) / HD
            xn = x * lax.rsqrt(ms + EPS) * g_ref[:, sl]
            he, ho = 2 * p, 2 * p + 1
            dst[:, he * LANES:(he + 1) * LANES] = jnp.where(first, xn, aug[:, he * LANES:(he + 1) * LANES]).astype(BF16)
            dst[:, ho * LANES:(ho + 1) * LANES] = jnp.where(
                first, pltpu.roll(xn, HD, 1), aug[:, ho * LANES:(ho + 1) * LANES]).astype(BF16)


def _fox_prep(pw, pn, wts, batch, seq, t):
    n = pw.shape[0]
    ns = seq // t
    row = lambda b, i: (b * ns + i, 0)
    const2 = lambda b, i: (0, 0)
    const3 = lambda b, i: (0, 0, 0)
    return pl.pallas_call(
        _fox_prep_kernel,
        grid=(batch, ns),
        in_specs=[
            pl.BlockSpec((t, W_BR), lambda b, i: (b * ns + i, COL_FQ // W_BR)),
            pl.BlockSpec((t, W_BR), lambda b, i: (b * ns + i, COL_FK // W_BR)),
            pl.BlockSpec((t, W_BR), lambda b, i: (b * ns + i, COL_FV // W_BR)),
            pl.BlockSpec((t, LANES), lambda b, i: (b * ns + i, COL_FF // LANES)),
            pl.BlockSpec((1, LANES), const2),
            pl.BlockSpec((1, W_BR), const2),
            pl.BlockSpec((1, W_BR), const2),
            pl.BlockSpec((t, t), const2),
            pl.BlockSpec((3, LANES, H * LANES), const3),
            pl.BlockSpec((3, LANES, H * LANES), const3),
            pl.BlockSpec((1, H * LANES), const2),
            pl.BlockSpec((1, H * LANES), const2),
        ],
        out_specs=[
            pl.BlockSpec((t, H * LANES), row),
            pl.BlockSpec((t, H * LANES), row),
            pl.BlockSpec((1, 1, W_BR, t), lambda b, i: (b, i, 0, 0)),
        ],
        out_shape=[
            jax.ShapeDtypeStruct((n, H * LANES), BF16),
            jax.ShapeDtypeStruct((n, H * LANES), BF16),
            jax.ShapeDtypeStruct((batch, ns, W_BR, t), BF16),
        ],
        scratch_shapes=[pltpu.VMEM((1, LANES), F32)],
        compiler_params=_cparams(("parallel", "arbitrary")),
        name="fox_prep",
    )(pw, pw, pw, pn, *wts)


def _attn_kernel(q_ref, k_ref, vt_ref, g_ref, o_ref, m_ref, acc_ref, sa_ref, sb_ref, *, tq):
    qi = pl.program_id(2)
    nt = (((1,), (1,)), ((), ()))
    kv_pos = lax.broadcasted_iota(jnp.int32, (tq, tq), 0)
    q_pos = lax.broadcasted_iota(jnp.int32, (tq, tq), 1)
    causal = kv_pos <= q_pos
    ones = jnp.ones((ONES_ROWS, tq), BF16)
    m_ref[...] = jnp.full(m_ref.shape, NEG_INF, F32)
    acc_ref[...] = jnp.zeros(acc_ref.shape, F32)

    def scores(c, s_ref):
        start = pl.multiple_of(c * tq, tq)
        for j in range(2):
            q = q_ref[:, j * LANES:(j + 1) * LANES]
            k = k_ref[pl.ds(start, tq), j * LANES:(j + 1) * LANES]
            s_ref[j] = lax.dot_general(k, q, nt, preferred_element_type=F32)

    def consume(c, s_ref, masked):
        for j in range(2):
            vt = jnp.concatenate([vt_ref[0, c, j * HD:(j + 1) * HD, :], ones], axis=0)
            s = s_ref[j]
            if masked:
                s = jnp.where(causal, s, NEG_INF)
            m_old = m_ref[j]
            m_new = jnp.maximum(m_old, jnp.max(s, axis=0, keepdims=True))
            alpha = jnp.exp(m_old - m_new)
            p = jnp.exp(s - m_new)
            acc_ref[j] = alpha * acc_ref[j] + _dot(vt, p.astype(BF16))
            m_ref[j] = m_new

    scores(0, sa_ref)

    def body(i, carry):
        c = 2 * i
        scores(c + 1, sb_ref)
        consume(c, sa_ref, False)
        scores(c + 2, sa_ref)
        consume(c + 1, sb_ref, False)
        return carry

    lax.fori_loop(0, qi // 2, body, 0)

    @pl.when(qi % 2 == 1)
    def _():
        scores(qi, sb_ref)
        consume(qi - 1, sa_ref, False)
        consume(qi, sb_ref, True)

    @pl.when(qi % 2 == 0)
    def _():
        consume(qi, sa_ref, True)

    o_t = jnp.concatenate([acc_ref[j, :HD, :] / acc_ref[j, HD:HD + 1, :] for j in range(2)], axis=0)
    g = g_ref[...].astype(F32)
    o_ref[...] = (o_t.T * (g * _sigmoid(g))).astype(BF16)


def _attention(q, k, vt, pw, gate_col, batch, seq, tq):
    n = q.shape[0]
    nq = seq // tq
    return pl.pallas_call(
        functools.partial(_attn_kernel, tq=tq),
        grid=(batch, H // 2, nq),
        in_specs=[
            pl.BlockSpec((tq, 2 * LANES), lambda b, hp, i: (b * nq + i, hp)),
            pl.BlockSpec((seq, 2 * LANES), lambda b, hp, i: (b, hp)),
            pl.BlockSpec((1, nq, LANES, tq), lambda b, hp, i: (b, 0, hp, 0)),
            pl.BlockSpec((tq, LANES), lambda b, hp, i: (b * nq + i, gate_col // LANES + hp)),
        ],
        out_specs=pl.BlockSpec((tq, LANES), lambda b, hp, i: (b * nq + i, hp)),
        out_shape=jax.ShapeDtypeStruct((n, W_BR), BF16),
        scratch_shapes=[
            pltpu.VMEM((2, 1, tq), F32),
            pltpu.VMEM((2, HD + ONES_ROWS, tq), F32),
            pltpu.VMEM((2, tq, tq), F32),
            pltpu.VMEM((2, tq, tq), F32),
        ],
        compiler_params=_cparams(("parallel", "parallel", "arbitrary")),
        name="attention",
    )(q, k, vt, pw)


def _store_heads(o_ref, x):
    for p in range(H // 2):
        blk = x[:, p * LANES:(p + 1) * LANES]
        o_ref[0, 2 * p] = blk[:, :HD].astype(o_ref.dtype)
        o_ref[0, 2 * p + 1] = pltpu.roll(blk, HD, 1)[:, :HD].astype(o_ref.dtype)


def _rwkv_prep_kernel(*refs, first_layer):
    if first_layer:
        (sr_ref, sk_ref, sv_ref, swa_ref, gc_ref, mur_ref, muk_ref, muv_ref, muwa_ref, w0_ref, a0_ref, wup_ref,
         aup_ref, kk_ref, ka_ref, rk_ref, bt_ref, bo_ref,
         at_o, rt_o, bt_o, kt_o, v_o, bh_o, kh_o, wc_o, bonus_o, sg_o, vfirst_o,
         cr_ref, ck_ref, cv_ref, cwa_ref) = refs
    else:
        (sr_ref, sk_ref, sv_ref, swa_ref, gc_ref, vf_ref, mur_ref, muk_ref, muv_ref, muwa_ref, w0_ref, a0_ref,
         wup_ref, aup_ref, kk_ref, ka_ref, rk_ref, v0_ref, vdown_ref, vup_ref, bt_ref, bo_ref,
         at_o, rt_o, bt_o, kt_o, v_o, bh_o, kh_o, wc_o, bonus_o, sg_o,
         cr_ref, ck_ref, cv_ref, cwa_ref) = refs
    t = sr_ref.shape[0]

    @pl.when(pl.program_id(1) == 0)
    def _():
        for c_ref in (cr_ref, ck_ref, cv_ref, cwa_ref):
            c_ref[...] = jnp.zeros_like(c_ref)

    def shift(x_ref, c_ref, mu_ref):
        x = x_ref[...].astype(F32)
        rowid = lax.broadcasted_iota(jnp.int32, x.shape, 0)
        prev = jnp.where(rowid == 0, c_ref[...], pltpu.roll(x, 1, 0))
        c_ref[...] = x[t - 1:t, :]
        return x + mu_ref[...] * (prev - x)

    r = shift(sr_ref, cr_ref, mur_ref)
    k = shift(sk_ref, ck_ref, muk_ref)
    v = shift(sv_ref, cv_ref, muv_ref)
    wa = shift(swa_ref, cwa_ref, muwa_ref)

    logw = -RWKV_DECAY_SCALE * _sigmoid(w0_ref[...] + _dot(jnp.tanh(wa).astype(BF16), wup_ref[...]))
    a = _sigmoid(a0_ref[...] + _dot(wa.astype(BF16), aup_ref[...]))
    if first_layer:
        vfirst_o[...] = v
    else:
        low = _dot(v.astype(BF16), vdown_ref[...]).astype(BF16)
        nu = _sigmoid(v0_ref[...] + _dot(low, vup_ref[...]))
        v = v + (vf_ref[...] - v) * nu

    kk = k * kk_ref[...]
    k_mod = k * (1.0 + (a - 1.0) * ka_ref[...])
    rk = r * k_mod * rk_ref[...]
    kk_n, bonus = [], []
    for p in range(H // 2):
        sl = slice(p * LANES, (p + 1) * LANES)
        nrm = jnp.sqrt(_pair_seg_sum(kk[:, sl] * kk[:, sl]))
        kk_n.append(kk[:, sl] / jnp.maximum(nrm, 1e-12))
        bonus.append(_pair_seg_sum(rk[:, sl]) * v[:, sl])
    kk = jnp.concatenate(kk_n, axis=-1)
    bonus = jnp.concatenate(bonus, axis=-1)
    a_vec = -kk
    b_vec = kk * a

    lcum = _dot_exact_lhs(bt_ref[...], logw)
    ltot = _dot_exact_lhs(bo_ref[...], logw)
    e_minus = jnp.exp(-lcum)
    e_rem = jnp.exp(ltot - lcum)
    _store_heads(at_o, a_vec * jnp.exp(lcum - logw))
    _store_heads(rt_o, r * jnp.exp(lcum))
    _store_heads(bt_o, b_vec * e_minus)
    _store_heads(kt_o, k_mod * e_minus)
    _store_heads(v_o, v)
    _store_heads(bh_o, b_vec * e_rem)
    _store_heads(kh_o, k_mod * e_rem)
    _store_heads(wc_o, jnp.exp(ltot))
    _store_heads(bonus_o, bonus)
    g = gc_ref[...].astype(F32)
    _store_heads(sg_o, g * _sigmoid(g))


def _rwkv_prep(pw, pn, vfirst, wts, batch, seq, t, first_layer):
    n = pw.shape[0]
    ns = seq // t
    const2 = lambda b, i: (0, 0)
    pcol = lambda col, w: pl.BlockSpec((t, w), lambda b, i: (b * ns + i, col // w))
    vec = pl.BlockSpec((1, W_BR), const2)
    in_specs = [pcol(COL_SR, W_BR), pcol(COL_SK, W_BR), pcol(COL_SV, W_BR), pcol(COL_WA, LANES),
                pcol(COL_GATE_C, W_BR)]
    args = [pw, pw, pw, pn, pw]
    if not first_layer:
        in_specs.append(pl.BlockSpec((t, W_BR), lambda b, i: (b * ns + i, 0)))
        args.append(vfirst)
    in_specs += [vec, vec, vec, pl.BlockSpec((1, LANES), const2), vec, vec,
                 pl.BlockSpec((LANES, W_BR), const2), pl.BlockSpec((LANES, W_BR), const2), vec, vec, vec]
    if not first_layer:
        in_specs += [vec, pl.BlockSpec((W_BR, LANES), const2), pl.BlockSpec((LANES, W_BR), const2)]
    in_specs += [pl.BlockSpec((t, t), const2), pl.BlockSpec((t, t), const2)]
    args += list(wts)
    hm_spec = pl.BlockSpec((1, H, t, HD), lambda b, i: (b, 0, i, 0))
    hm_dtypes = (BF16, F32, BF16, BF16, BF16, BF16, BF16, F32, F32, F32)
    out_specs = [hm_spec] * 10
    out_shape = [jax.ShapeDtypeStruct((batch, H, seq, HD), dt) for dt in hm_dtypes]
    if first_layer:
        out_specs.append(pl.BlockSpec((t, W_BR), lambda b, i: (b * ns + i, 0)))
        out_shape.append(jax.ShapeDtypeStruct((n, W_BR), F32))
    return pl.pallas_call(
        functools.partial(_rwkv_prep_kernel, first_layer=first_layer),
        grid=(batch, ns),
        in_specs=in_specs,
        out_specs=out_specs,
        out_shape=out_shape,
        scratch_shapes=[pltpu.VMEM((1, W_BR), F32), pltpu.VMEM((1, W_BR), F32), pltpu.VMEM((1, W_BR), F32),
                        pltpu.VMEM((1, LANES), F32)],
        compiler_params=_cparams(("parallel", "arbitrary")),
        name="rwkv_prep",
    )(*args)


def _mm(a, b, dims):
    return lax.dot_general(a.astype(BF16), b.astype(BF16), dims, preferred_element_type=F32)


_NN = (((1,), (0,)), ((), ()))
_NT = (((1,), (1,)), ((), ()))
_TN = (((0,), (0,)), ((), ()))


def _rwkv_chunk_kernel(at_ref, rt_ref, bt_ref, kt_ref, v_ref, bh_ref, kh_ref, wc_ref, bonus_ref, sg_ref,
                       lg_ref, lb_ref, o_ref, s_all_ref, *, nchunk):
    @pl.when(pl.program_id(1) == 0)
    def _():
        s_all_ref[...] = jnp.zeros_like(s_all_ref)

    mm = _mm
    row = lax.broadcasted_iota(jnp.int32, (CHUNK, CHUNK), 0)
    col = lax.broadcasted_iota(jnp.int32, (CHUNK, CHUNK), 1)
    strict = col < row
    incl = col <= row
    eye = col == row
    items = [(c, h) for c in range(nchunk) for h in range(H)]
    each = lambda f, *cols: [f(*xs) for xs in zip(*cols)]
    sls = [slice(c * CHUNK, (c + 1) * CHUNK) for c, _ in items]
    ld = lambda ref: [ref[0, h, sl, :] for (_, h), sl in zip(items, sls)]
    at, rt, bt, kt, v, bh, kh = ld(at_ref), ld(rt_ref), ld(bt_ref), ld(kt_ref), ld(v_ref), ld(bh_ref), ld(kh_ref)
    x1 = each(lambda a, r: jnp.concatenate([a, r.astype(BF16)], axis=0), at, rt)
    gb = each(lambda x, b: mm(x, b, _NT), x1, bt)
    gk = each(lambda x, k: mm(x, k, _NT), x1, kt)
    a_ab = [jnp.where(strict, g[:CHUNK], 0.0) for g in gb]
    a_rb = [jnp.where(incl, g[CHUNK:], 0.0) for g in gb]
    a_ak = [jnp.where(strict, g[:CHUNK], 0.0) for g in gk]
    a_rk = [jnp.where(incl, g[CHUNK:], 0.0) for g in gk]
    tinv = [jnp.where(eye, 1.0, a) for a in a_ab]
    pw = a_ab
    for _ in range(int(math.log2(CHUNK)) - 1):
        pw = each(lambda p: mm(p, p, _NN), pw)
        tinv = each(lambda t, p: t + mm(t, p, _NN), tinv, pw)
    av = each(lambda a, x: mm(a, x, _NN), a_ak, v)
    u = each(lambda t, x: mm(t, x, _NN), tinv, av)
    a_til = each(lambda t, x: mm(t, x, _NN), tinv, at)
    r_hat = each(lambda r, a, x: r + mm(a, x, _NN), rt, a_rb, a_til)
    y_in = each(lambda a, x: mm(a, x, _NN), a_rb, u)
    y_in = each(lambda y, a, x: y + mm(a, x, _NN), y_in, a_rk, v)
    m_new = each(lambda a, b: mm(a, b, _TN), a_til, bh)
    n_new = each(lambda a, b: mm(a, b, _TN), u, bh)
    n_new = each(lambda n, a, b: n + mm(a, b, _TN), n_new, v, kh)
    for c in range(nchunk):
        idx = [i for i, (ci, _) in enumerate(items) if ci == c]
        s0 = [s_all_ref[h] for h in range(H)]
        y = [y_in[i] + mm(r_hat[i], s0[h], _NT) for h, i in enumerate(idx)]
        for h, i in enumerate(idx):
            wc = wc_ref[0, h, c * CHUNK:c * CHUNK + 1, :]
            s_all_ref[h] = mm(s0[h], jnp.where(eye, wc, 0.0) + m_new[i], _NN) + n_new[i]
        for h, i in enumerate(idx):
            mu = jnp.mean(y[h], axis=-1, keepdims=True)
            d = y[h] - mu
            var = jnp.mean(d * d, axis=-1, keepdims=True)
            yn = d * lax.rsqrt(var + GN_EPS) * lg_ref[h] + lb_ref[h]
            o_ref[0, h, sls[i], :] = ((yn + bonus_ref[0, h, sls[i], :]) * sg_ref[0, h, sls[i], :]).astype(o_ref.dtype)


def _rwkv_chunk(hm, lnx_g, lnx_b, batch, seq, tc):
    spec = pl.BlockSpec((1, H, tc, HD), lambda b, i: (b, 0, i, 0))
    gspec = pl.BlockSpec((H, 1, HD), lambda b, i: (0, 0, 0))
    return pl.pallas_call(
        functools.partial(_rwkv_chunk_kernel, nchunk=tc // CHUNK),
        grid=(batch, seq // tc),
        in_specs=[spec] * 10 + [gspec, gspec],
        out_specs=spec,
        out_shape=jax.ShapeDtypeStruct((batch, H, seq, HD), BF16),
        scratch_shapes=[pltpu.VMEM((H, HD, HD), F32)],
        compiler_params=_cparams(("parallel", "arbitrary")),
        name="rwkv_chunk",
    )(*hm, lnx_g, lnx_b)


def _merge_kernel(oa_ref, ob_ref, oc_ref, ga_ref, gb_ref, gc_ref, x_ref, wpa_ref, wpb_ref, wpc_ref, wout_ref,
                  o_ref):
    pa = _dot(oa_ref[...], wpa_ref[...])
    pb = _dot(ob_ref[...], wpb_ref[...])
    pc = _dot(oc_ref[0, 0], wpc_ref[0])
    for h in range(1, H):
        pc = pc + _dot(oc_ref[0, h], wpc_ref[h])
    sig = lambda ref: _sigmoid(ref[...].astype(F32))
    merged = sig(ga_ref) * pa + sig(gb_ref) * pb + sig(gc_ref) * pc
    o_ref[...] = x_ref[...] + _dot(merged.astype(BF16), wout_ref[...])


def _merge(oa, ob, oc, p, x2, wpa, wpb, wpc, wout, batch, seq, t):
    n = x2.shape[0]
    ns = seq // t
    row = lambda b, i: (b * ns + i, 0)
    const2 = lambda b, i: (0, 0)
    gate = lambda g: pl.BlockSpec((t, D_MODEL), lambda b, i: (b * ns + i, COL_MERGE // D_MODEL + g))
    return pl.pallas_call(
        _merge_kernel,
        grid=(batch, ns),
        in_specs=[
            pl.BlockSpec((t, W_BR), row),
            pl.BlockSpec((t, W_BR), row),
            pl.BlockSpec((1, H, t, HD), lambda b, i: (b, 0, i, 0)),
            gate(0), gate(1), gate(2),
            pl.BlockSpec((t, D_MODEL), row),
            pl.BlockSpec((W_BR, D_MODEL), const2),
            pl.BlockSpec((W_BR, D_MODEL), const2),
            pl.BlockSpec((H, HD, D_MODEL), lambda b, i: (0, 0, 0)),
            pl.BlockSpec((D_MODEL, D_MODEL), const2),
        ],
        out_specs=pl.BlockSpec((t, D_MODEL), row),
        out_shape=jax.ShapeDtypeStruct((n, D_MODEL), F32),
        compiler_params=_cparams(("parallel", "parallel")),
        name="merge",
    )(oa, ob, oc, p, p, p, x2, wpa, wpb, wpc, wout)


def _regroup_w_in(w):
    d = w.shape[0]
    sizes = (Q_LORA, KV_LORA, ROPE_D, W_BR, W_BR, W_BR, W_BR, H, W_BR,
             3 * W_BR + DECAY_LORA + AAA_LORA, W_BR, 3 * D_MODEL)
    offs = np.concatenate([[0], np.cumsum(sizes)])
    (c_q, c_kv, k_r, gate_a, fq, fk, fv, ff, gate_b, shift, gate_c, merge) = [
        w[:, offs[i]:offs[i + 1]] for i in range(len(sizes))]
    z = lambda n: jnp.zeros((d, n), w.dtype)
    wide = jnp.concatenate([
        merge, gate_a, fq, fk, fv, gate_b,
        shift[:, :W_BR], shift[:, W_BR:2 * W_BR], shift[:, 2 * W_BR:3 * W_BR], gate_c,
    ], axis=1)
    narrow = jnp.concatenate([
        c_q, c_kv,
        z(NOPE), k_r, z(LANES - NOPE - ROPE_D),
        ff, z(LANES - H),
        shift[:, 3 * W_BR:],
    ], axis=1)
    return wide.astype(BF16), narrow.astype(BF16)


def _rope_tables(seq):
    half = ROPE_D // 2
    inv = ROPE_THETA ** (-jnp.arange(0, ROPE_D, 2, dtype=F32) / ROPE_D)
    ang = jnp.arange(seq, dtype=F32)[:, None] * inv[None, :]
    cos, sin = jnp.cos(ang), jnp.sin(ang)
    z = lambda n: jnp.zeros((seq, n), F32)
    c = jnp.concatenate([jnp.ones((seq, NOPE), F32), cos, cos, z(LANES - NOPE - ROPE_D)], axis=1)
    s1 = jnp.concatenate([z(NOPE), -sin, z(LANES - NOPE - half)], axis=1)
    s2 = jnp.concatenate([z(NOPE + half), sin, z(LANES - NOPE - ROPE_D)], axis=1)
    return c, s1, s2


def _pad_lanes(v, lo, total=LANES):
    return jnp.zeros((1, total), F32).at[0, lo:lo + v.shape[0]].set(v)


def _fox_selectors():
    selq = np.zeros((3, LANES, H * LANES), np.float32)
    selk = np.zeros((3, LANES, H * LANES), np.float32)
    oneq = np.zeros((1, H * LANES), np.float32)
    onek = np.zeros((1, H * LANES), np.float32)
    for h in range(H):
        for j in range(3):
            selq[j, h, h * LANES + HD + j] = 1.0
            selk[j, h, h * LANES + HD + 3 + j] = -1.0
            oneq[0, h * LANES + HD + 3 + j] = 1.0
            onek[0, h * LANES + HD + j] = 1.0
    return (jnp.asarray(selq, BF16), jnp.asarray(selk, BF16), jnp.asarray(oneq), jnp.asarray(onek))


def _tri(t):
    r = np.arange(t)
    return jnp.asarray((r[None, :] <= r[:, None]).astype(np.float32), BF16)


def _chunk_tri(t):
    r = np.arange(t)
    same = (r[None, :] // CHUNK) == (r[:, None] // CHUNK)
    lower = r[None, :] <= r[:, None]
    return (jnp.asarray((same & lower).astype(np.float32), BF16), jnp.asarray(same.astype(np.float32), BF16))


def _tile(seq, pref):
    return pref if seq % pref == 0 else seq


RWKV_TC = 128


def kernel(x, norm_g, w_in, mla_qa_g, mla_w_uq, mla_kva_g, mla_w_ukv, mla_q_g, mla_knope_g, mla_krope_g, fox_b_f, fox_q_g, fox_k_g, rwkv_mu, rwkv_w0, rwkv_w_up, rwkv_a0, rwkv_a_up, rwkv_k_k, rwkv_k_a, rwkv_r_k, rwkv_lnx_g, rwkv_lnx_b, rwkv_v0, rwkv_v_down, rwkv_v_up, w_pa, w_pb, w_pc, w_out):
    batch, seq, d = x.shape
    depth = w_in.shape[0]
    n = batch * seq
    t_row = _tile(seq, 512)
    t_in = _tile(n, 1024)
    t_rwkv = _tile(seq, 256)
    x2 = x.reshape(n, d)

    tabs = _rope_tables(seq)
    selq, selk, oneq, onek = _fox_selectors()
    tri = _tri(t_row)
    btri, bones = _chunk_tri(t_rwkv)
    row = lambda v: v.reshape(1, -1).astype(F32)

    vfirst = None
    for l in range(depth):
        w_wide, w_narrow = _regroup_w_in(w_in[l])
        pw = _inproj(x2, row(norm_g[l]), w_wide, t_in, INPROJ_TN, BF16)
        pn = _inproj(x2, row(norm_g[l]), w_narrow, t_in, NCOLS_NARROW, F32)

        wuq = mla_w_uq[l].reshape(Q_LORA, H, NOPE + ROPE_D)
        wuq = jnp.pad(wuq, ((0, 0), (0, 0), (0, LANES - NOPE - ROPE_D))).reshape(Q_LORA, H * LANES).astype(BF16)
        wukv = mla_w_ukv[l].reshape(KV_LORA, H, NOPE + HD)
        wk = jnp.pad(wukv[:, :, :NOPE], ((0, 0), (0, 0), (0, LANES - NOPE))).reshape(KV_LORA, H * LANES).astype(BF16)
        wv = wukv[:, :, NOPE:].reshape(KV_LORA, W_BR).astype(BF16)
        mla_scale = float(NOPE + ROPE_D) ** -0.5
        mla_wts = (row(mla_qa_g[l]), wuq, row(mla_kva_g[l]), wk, wv,
                   _pad_lanes(mla_q_g[l] * mla_scale, 0), _pad_lanes(mla_knope_g[l], 0),
                   _pad_lanes(mla_krope_g[l], NOPE))
        qa, ka, va = _mla_prep(pn, tabs, mla_wts, seq, t_row)
        o_a = _attention(qa, ka, va, pw, COL_GATE_A, batch, seq, t_row)

        fox_wts = (_pad_lanes(fox_b_f[l], 0), row(jnp.tile(fox_q_g[l] * float(HD) ** -0.5, H)),
                   row(jnp.tile(fox_k_g[l], H)), tri, selq, selk, oneq, onek)
        qb, kb, vb = _fox_prep(pw, pn, fox_wts, batch, seq, t_row)
        o_b = _attention(qb, kb, vb, pw, COL_GATE_B, batch, seq, t_row)

        mu = rwkv_mu[l]
        wup = jnp.zeros((LANES, W_BR), F32).at[:DECAY_LORA].set(rwkv_w_up[l]).astype(BF16)
        aup = jnp.zeros((LANES, W_BR), F32).at[DECAY_LORA:].set(rwkv_a_up[l]).astype(BF16)
        rw = [row(mu[:W_BR]), row(mu[W_BR:2 * W_BR]), row(mu[2 * W_BR:3 * W_BR]), row(mu[3 * W_BR:]),
              row(rwkv_w0[l]), row(rwkv_a0[l]), wup, aup, row(rwkv_k_k[l]), row(rwkv_k_a[l]),
              row(rwkv_r_k[l])]
        if l > 0:
            vdown = jnp.zeros((W_BR, LANES), F32).at[:, :MV_LORA].set(rwkv_v_down[l - 1]).astype(BF16)
            vup = jnp.zeros((LANES, W_BR), F32).at[:MV_LORA].set(rwkv_v_up[l - 1]).astype(BF16)
            rw += [row(rwkv_v0[l - 1]), vdown, vup]
        rw += [btri, bones]
        outs = _rwkv_prep(pw, pn, vfirst, rw, batch, seq, t_rwkv, first_layer=(l == 0))
        if l == 0:
            vfirst = outs[10]
        o_c = _rwkv_chunk(outs[:10], rwkv_lnx_g[l].reshape(H, 1, HD), rwkv_lnx_b[l].reshape(H, 1, HD),
                          batch, seq, _tile(seq, RWKV_TC))

        x2 = _merge(o_a, o_b, o_c, pw, x2, w_pa[l].astype(BF16), w_pb[l].astype(BF16),
                    w_pc[l].reshape(H, HD, D_MODEL).astype(BF16), w_out[l].astype(BF16), batch, seq, t_row)
    return x2.reshape(batch, seq, d)
```

```python
import functools
import math

import jax
import jax.numpy as jnp
import numpy as np
from jax import lax
from jax.experimental import pallas as pl
from jax.experimental.pallas import tpu as pltpu

F32 = jnp.float32
BF16 = jnp.bfloat16

LANES = 128
H = 8
HD = 64
NOPE = 64
ROPE_D = 32
Q_LORA = 256
KV_LORA = 128
D_MODEL = 1024
W_BR = H * HD
DECAY_LORA = 64
AAA_LORA = 64
MV_LORA = 32
ROPE_THETA = 10000.0
RWKV_DECAY_SCALE = 0.606531
GN_EPS = 64e-5
EPS = 1e-6
NEG_INF = -1e30
LOG2E = math.log2(math.e)
CHUNK = 64
ONES_ROWS = 16

COL_MERGE = 0
COL_GATE_A = 3072
COL_FQ = 3584
COL_FK = 4096
COL_FV = 4608
COL_GATE_B = 5120
COL_SR = 5632
COL_SK = 6144
COL_SV = 6656
COL_GATE_C = 7168
NCOLS_WIDE = 7680
INPROJ_TN = 1280
COL_CQ = 0
COL_CKV = 256
COL_KR = 384
COL_FF = 512
COL_WA = 640
NCOLS_NARROW = 768

VMEM_LIMIT = 56 * 1024 * 1024


def _cparams(sem):
    return pltpu.CompilerParams(dimension_semantics=sem, vmem_limit_bytes=VMEM_LIMIT)


def _sigmoid(x):
    return 1.0 / (1.0 + jnp.exp(-x))


def _dot(a, b):
    return jnp.dot(a, b, preferred_element_type=F32)


def _split3(x):
    hi = x.astype(BF16)
    r1 = x - hi.astype(F32)
    mid = r1.astype(BF16)
    lo = (r1 - mid.astype(F32)).astype(BF16)
    return hi, mid, lo


def _dot_exact_lhs(m_bf16, x):
    hi, mid, lo = _split3(x)
    return _dot(m_bf16, hi) + _dot(m_bf16, mid) + _dot(m_bf16, lo)


def _pair_seg_sum(x):
    lane = lax.broadcasted_iota(jnp.int32, x.shape, 1)
    first = lane < HD
    s0 = jnp.sum(jnp.where(first, x, 0.0), axis=-1, keepdims=True)
    s1 = jnp.sum(jnp.where(first, 0.0, x), axis=-1, keepdims=True)
    return jnp.where(first, s0, s1)


def _inproj_kernel(x_ref, g_ref, w_ref, o_ref, h_ref):
    @pl.when(pl.program_id(1) == 0)
    def _():
        x = x_ref[...]
        ms = jnp.mean(x * x, axis=-1, keepdims=True)
        h_ref[...] = (x * lax.rsqrt(ms + EPS) * g_ref[...]).astype(BF16)

    o_ref[...] = _dot(h_ref[...], w_ref[...]).astype(o_ref.dtype)


def _inproj(x2, g, w, tm, tn, out_dtype):
    n, ncols = x2.shape[0], w.shape[1]
    return pl.pallas_call(
        _inproj_kernel,
        grid=(n // tm, ncols // tn),
        in_specs=[
            pl.BlockSpec((tm, D_MODEL), lambda i, j: (i, 0)),
            pl.BlockSpec((1, D_MODEL), lambda i, j: (0, 0)),
            pl.BlockSpec((D_MODEL, tn), lambda i, j: (0, j)),
        ],
        out_specs=pl.BlockSpec((tm, tn), lambda i, j: (i, j)),
        out_shape=jax.ShapeDtypeStruct((n, ncols), out_dtype),
        scratch_shapes=[pltpu.VMEM((tm, D_MODEL), BF16)],
        compiler_params=_cparams(("parallel", "arbitrary")),
        name="inproj",
    )(x2, g, w)


def _rms(x, g, n):
    ms = jnp.sum(x * x, axis=-1, keepdims=True) / n
    return x * lax.rsqrt(ms + EPS) * g


def _rope(x, c, s1, s2):
    return x * c + pltpu.roll(x, LANES - ROPE_D // 2, 1) * s1 + pltpu.roll(x, ROPE_D // 2, 1) * s2


def _mla_prep_kernel(cq_ref, ckv_ref, kr_ref, c_ref, s1_ref, s2_ref, qag_ref, wuq_ref, kvag_ref, wk_ref, wv_ref,
                     qg_ref, kng_ref, krg_ref, q_out, k_out, vt_out):
    c, s1, s2 = c_ref[...], s1_ref[...], s2_ref[...]
    cqn = _rms(cq_ref[...], qag_ref[...], Q_LORA).astype(BF16)
    ckvn = _rms(ckv_ref[...], kvag_ref[...], KV_LORA).astype(BF16)
    q_all = _dot(cqn, wuq_ref[...])
    k_all = _dot(ckvn, wk_ref[...])
    vt_out[0, 0] = _dot(ckvn, wv_ref[...]).T.astype(BF16)
    k_rope = _rope(_rms(kr_ref[...], krg_ref[...], ROPE_D), c, s1, s2)
    qg, kng = qg_ref[...], kng_ref[...]
    for h in range(H):
        sl = slice(h * LANES, (h + 1) * LANES)
        q_out[:, sl] = _rope(_rms(q_all[:, sl], qg, NOPE + ROPE_D), c, s1, s2).astype(BF16)
        k_out[:, sl] = (_rms(k_all[:, sl], kng, NOPE) + k_rope).astype(BF16)


def _mla_prep(pn, tabs, wts, seq, t):
    n = pn.shape[0]
    ns = seq // t
    row = lambda i: (i, 0)
    const = lambda i: (0, 0)
    tab_spec = pl.BlockSpec((t, LANES), lambda i: (i % ns, 0))
    return pl.pallas_call(
        _mla_prep_kernel,
        grid=(n // t,),
        in_specs=[
            pl.BlockSpec((t, Q_LORA), lambda i: (i, COL_CQ // Q_LORA)),
            pl.BlockSpec((t, LANES), lambda i: (i, COL_CKV // LANES)),
            pl.BlockSpec((t, LANES), lambda i: (i, COL_KR // LANES)),
            tab_spec, tab_spec, tab_spec,
            pl.BlockSpec((1, Q_LORA), const),
            pl.BlockSpec((Q_LORA, H * LANES), const),
            pl.BlockSpec((1, KV_LORA), const),
            pl.BlockSpec((KV_LORA, H * LANES), const),
            pl.BlockSpec((KV_LORA, W_BR), const),
            pl.BlockSpec((1, LANES), const),
            pl.BlockSpec((1, LANES), const),
            pl.BlockSpec((1, LANES), const),
        ],
        out_specs=[
            pl.BlockSpec((t, H * LANES), row),
            pl.BlockSpec((t, H * LANES), row),
            pl.BlockSpec((1, 1, W_BR, t), lambda i: (i // ns, i % ns, 0, 0)),
        ],
        out_shape=[
            jax.ShapeDtypeStruct((n, H * LANES), BF16),
            jax.ShapeDtypeStruct((n, H * LANES), BF16),
            jax.ShapeDtypeStruct((n // seq, ns, W_BR, t), BF16),
        ],
        compiler_params=_cparams(("parallel",)),
        name="mla_prep",
    )(pn, pn, pn, *tabs, *wts)


def _fox_prep_kernel(fq_ref, fk_ref, fv_ref, ff_ref, bf_ref, qg_ref, kg_ref, tri_ref, selq_ref, selk_ref,
                     oneq_ref, onek_ref, q_out, k_out, vt_out, carry_ref):
    t = fq_ref.shape[0]

    @pl.when(pl.program_id(1) == 0)
    def _():
        carry_ref[...] = jnp.zeros_like(carry_ref)

    z = ff_ref[...] + bf_ref[...]
    logf = jnp.minimum(z, 0.0) - jnp.log1p(jnp.exp(-jnp.abs(z)))
    cum = carry_ref[...] + _dot_exact_lhs(tri_ref[...], logf)
    carry_ref[...] = cum[t - 1:t, :]
    hi, mid, lo = _split3(cum * LOG2E)
    aug_q = _dot(hi, selq_ref[0]) + _dot(mid, selq_ref[1]) + _dot(lo, selq_ref[2]) + oneq_ref[...]
    aug_k = _dot(hi, selk_ref[0]) + _dot(mid, selk_ref[1]) + _dot(lo, selk_ref[2]) + onek_ref[...]
    vt_out[0, 0] = fv_ref[...].astype(F32).T.astype(BF16)
    lane = lax.broadcasted_iota(jnp.int32, (t, LANES), 1)
    first = lane < HD
    for src_ref, g_ref, aug, dst in ((fq_ref, qg_ref, aug_q, q_out), (fk_ref, kg_ref, aug_k, k_out)):
        for p in range(H // 2):
            sl = slice(p * LANES, (p + 1) * LANES)
            x = src_ref[:, sl].astype(F32)
            ms = _pair_seg_sum(x * x) / HD
            xn = x * lax.rsqrt(ms + EPS) * g_ref[:, sl]
            he, ho = 2 * p, 2 * p + 1
            dst[:, he * LANES:(he + 1) * LANES] = jnp.where(first, xn, aug[:, he * LANES:(he + 1) * LANES]).astype(BF16)
            dst[:, ho * LANES:(ho + 1) * LANES] = jnp.where(
                first, pltpu.roll(xn, HD, 1), aug[:, ho * LANES:(ho + 1) * LANES]).astype(BF16)


def _fox_prep(pw, pn, wts, batch, seq, t):
    n = pw.shape[0]
    ns = seq // t
    row = lambda b, i: (b * ns + i, 0)
    const2 = lambda b, i: (0, 0)
    const3 = lambda b, i: (0, 0, 0)
    return pl.pallas_call(
        _fox_prep_kernel,
        grid=(batch, ns),
        in_specs=[
            pl.BlockSpec((t, W_BR), lambda b, i: (b * ns + i, COL_FQ // W_BR)),
            pl.BlockSpec((t, W_BR), lambda b, i: (b * ns + i, COL_FK // W_BR)),
            pl.BlockSpec((t, W_BR), lambda b, i: (b * ns + i, COL_FV // W_BR)),
            pl.BlockSpec((t, LANES), lambda b, i: (b * ns + i, COL_FF // LANES)),
            pl.BlockSpec((1, LANES), const2),
            pl.BlockSpec((1, W_BR), const2),
            pl.BlockSpec((1, W_BR), const2),
            pl.BlockSpec((t, t), const2),
            pl.BlockSpec((3, LANES, H * LANES), const3),
            pl.BlockSpec((3, LANES, H * LANES), const3),
            pl.BlockSpec((1, H * LANES), const2),
            pl.BlockSpec((1, H * LANES), const2),
        ],
        out_specs=[
            pl.BlockSpec((t, H * LANES), row),
            pl.BlockSpec((t, H * LANES), row),
            pl.BlockSpec((1, 1, W_BR, t), lambda b, i: (b, i, 0, 0)),
        ],
        out_shape=[
            jax.ShapeDtypeStruct((n, H * LANES), BF16),
            jax.ShapeDtypeStruct((n, H * LANES), BF16),
            jax.ShapeDtypeStruct((batch, ns, W_BR, t), BF16),
        ],
        scratch_shapes=[pltpu.VMEM((1, LANES), F32)],
        compiler_params=_cparams(("parallel", "arbitrary")),
        name="fox_prep",
    )(pw, pw, pw, pn, *wts)


def _attn_kernel(q_ref, k_ref, vt_ref, g_ref, o_ref, m_ref, acc_ref, sa_ref, sb_ref, mxa_ref, mxb_ref, *, tq):
    qi = pl.program_id(2)
    nt = (((1,), (1,)), ((), ()))
    kv_pos = lax.broadcasted_iota(jnp.int32, (tq, tq), 0)
    q_pos = lax.broadcasted_iota(jnp.int32, (tq, tq), 1)
    causal = kv_pos <= q_pos
    ones = jnp.ones((ONES_ROWS, tq), BF16)
    m_ref[...] = jnp.full(m_ref.shape, NEG_INF, F32)
    acc_ref[...] = jnp.zeros(acc_ref.shape, F32)

    def scores(c, s_ref, mx_ref):
        start = pl.multiple_of(c * tq, tq)
        for j in range(2):
            q = q_ref[:, j * LANES:(j + 1) * LANES]
            k = k_ref[pl.ds(start, tq), j * LANES:(j + 1) * LANES]
            s = lax.dot_general(k, q, nt, preferred_element_type=F32)
            s_ref[j] = s
            mx_ref[j] = jnp.max(s, axis=0, keepdims=True)

    def consume(c, s_ref, mx_ref, masked):
        for j in range(2):
            vt = jnp.concatenate([vt_ref[0, c, j * HD:(j + 1) * HD, :], ones], axis=0)
            s = s_ref[j]
            if masked:
                s = jnp.where(causal, s, NEG_INF)
                m_cur = jnp.max(s, axis=0, keepdims=True)
            else:
                m_cur = mx_ref[j]
            m_old = m_ref[j]
            m_new = jnp.maximum(m_old, m_cur)
            alpha = jnp.exp2(m_old - m_new)
            p = jnp.exp2(s - m_new)
            acc_ref[j] = alpha * acc_ref[j] + _dot(vt, p.astype(BF16))
            m_ref[j] = m_new

    scores(0, sa_ref, mxa_ref)

    def body(i, carry):
        c = 2 * i
        scores(c + 1, sb_ref, mxb_ref)
        consume(c, sa_ref, mxa_ref, False)
        scores(c + 2, sa_ref, mxa_ref)
        consume(c + 1, sb_ref, mxb_ref, False)
        return carry

    lax.fori_loop(0, qi // 2, body, 0)

    @pl.when(qi % 2 == 1)
    def _():
        scores(qi, sb_ref, mxb_ref)
        consume(qi - 1, sa_ref, mxa_ref, False)
        consume(qi, sb_ref, mxb_ref, True)

    @pl.when(qi % 2 == 0)
    def _():
        consume(qi, sa_ref, mxa_ref, True)

    o_t = jnp.concatenate([acc_ref[j, :HD, :] / acc_ref[j, HD:HD + 1, :] for j in range(2)], axis=0)
    g = g_ref[...].astype(F32)
    o_ref[...] = (o_t.T * (g * _sigmoid(g))).astype(BF16)


def _attention(q, k, vt, pw, gate_col, batch, seq, tq):
    n = q.shape[0]
    nq = seq // tq
    return pl.pallas_call(
        functools.partial(_attn_kernel, tq=tq),
        grid=(batch, H // 2, nq),
        in_specs=[
            pl.BlockSpec((tq, 2 * LANES), lambda b, hp, i: (b * nq + i, hp)),
            pl.BlockSpec((seq, 2 * LANES), lambda b, hp, i: (b, hp)),
            pl.BlockSpec((1, nq, LANES, tq), lambda b, hp, i: (b, 0, hp, 0)),
            pl.BlockSpec((tq, LANES), lambda b, hp, i: (b * nq + i, gate_col // LANES + hp)),
        ],
        out_specs=pl.BlockSpec((tq, LANES), lambda b, hp, i: (b * nq + i, hp)),
        out_shape=jax.ShapeDtypeStruct((n, W_BR), BF16),
        scratch_shapes=[
            pltpu.VMEM((2, 1, tq), F32),
            pltpu.VMEM((2, HD + ONES_ROWS, tq), F32),
            pltpu.VMEM((2, tq, tq), F32),
            pltpu.VMEM((2, tq, tq), F32),
            pltpu.VMEM((2, 1, tq), F32),
            pltpu.VMEM((2, 1, tq), F32),
        ],
        compiler_params=_cparams(("parallel", "parallel", "arbitrary")),
        name="attention",
    )(q, k, vt, pw)


def _store_heads(o_ref, x):
    for p in range(H // 2):
        blk = x[:, p * LANES:(p + 1) * LANES]
        o_ref[0, 2 * p] = blk[:, :HD].astype(o_ref.dtype)
        o_ref[0, 2 * p + 1] = pltpu.roll(blk, HD, 1)[:, :HD].astype(o_ref.dtype)


def _rwkv_prep_kernel(*refs, first_layer):
    if first_layer:
        (sr_ref, sk_ref, sv_ref, swa_ref, gc_ref, mur_ref, muk_ref, muv_ref, muwa_ref, w0_ref, a0_ref, wup_ref,
         aup_ref, kk_ref, ka_ref, rk_ref, bt_ref, bo_ref,
         at_o, rt_o, bt_o, kt_o, v_o, bh_o, kh_o, wc_o, bonus_o, sg_o, vfirst_o,
         cr_ref, ck_ref, cv_ref, cwa_ref) = refs
    else:
        (sr_ref, sk_ref, sv_ref, swa_ref, gc_ref, vf_ref, mur_ref, muk_ref, muv_ref, muwa_ref, w0_ref, a0_ref,
         wup_ref, aup_ref, kk_ref, ka_ref, rk_ref, v0_ref, vdown_ref, vup_ref, bt_ref, bo_ref,
         at_o, rt_o, bt_o, kt_o, v_o, bh_o, kh_o, wc_o, bonus_o, sg_o,
         cr_ref, ck_ref, cv_ref, cwa_ref) = refs
    t = sr_ref.shape[0]

    @pl.when(pl.program_id(1) == 0)
    def _():
        for c_ref in (cr_ref, ck_ref, cv_ref, cwa_ref):
            c_ref[...] = jnp.zeros_like(c_ref)

    def shift(x_ref, c_ref, mu_ref):
        x = x_ref[...].astype(F32)
        rowid = lax.broadcasted_iota(jnp.int32, x.shape, 0)
        prev = jnp.where(rowid == 0, c_ref[...], pltpu.roll(x, 1, 0))
        c_ref[...] = x[t - 1:t, :]
        return x + mu_ref[...] * (prev - x)

    r = shift(sr_ref, cr_ref, mur_ref)
    k = shift(sk_ref, ck_ref, muk_ref)
    v = shift(sv_ref, cv_ref, muv_ref)
    wa = shift(swa_ref, cwa_ref, muwa_ref)

    logw = -RWKV_DECAY_SCALE * _sigmoid(w0_ref[...] + _dot(jnp.tanh(wa).astype(BF16), wup_ref[...]))
    a = _sigmoid(a0_ref[...] + _dot(wa.astype(BF16), aup_ref[...]))
    if first_layer:
        vfirst_o[...] = v
    else:
        low = _dot(v.astype(BF16), vdown_ref[...]).astype(BF16)
        nu = _sigmoid(v0_ref[...] + _dot(low, vup_ref[...]))
        v = v + (vf_ref[...] - v) * nu

    kk = k * kk_ref[...]
    k_mod = k * (1.0 + (a - 1.0) * ka_ref[...])
    rk = r * k_mod * rk_ref[...]
    kk_n, bonus = [], []
    for p in range(H // 2):
        sl = slice(p * LANES, (p + 1) * LANES)
        nrm = jnp.sqrt(_pair_seg_sum(kk[:, sl] * kk[:, sl]))
        kk_n.append(kk[:, sl] / jnp.maximum(nrm, 1e-12))
        bonus.append(_pair_seg_sum(rk[:, sl]) * v[:, sl])
    kk = jnp.concatenate(kk_n, axis=-1)
    bonus = jnp.concatenate(bonus, axis=-1)
    a_vec = -kk
    b_vec = kk * a

    lcum = _dot_exact_lhs(bt_ref[...], logw)
    ltot = _dot_exact_lhs(bo_ref[...], logw)
    e_minus = jnp.exp(-lcum)
    e_rem = jnp.exp(ltot - lcum)
    _store_heads(at_o, a_vec * jnp.exp(lcum - logw))
    _store_heads(rt_o, r * jnp.exp(lcum))
    _store_heads(bt_o, b_vec * e_minus)
    _store_heads(kt_o, k_mod * e_minus)
    _store_heads(v_o, v)
    _store_heads(bh_o, b_vec * e_rem)
    _store_heads(kh_o, k_mod * e_rem)
    _store_heads(wc_o, jnp.exp(ltot))
    _store_heads(bonus_o, bonus)
    g = gc_ref[...].astype(F32)
    _store_heads(sg_o, g * _sigmoid(g))


def _rwkv_prep(pw, pn, vfirst, wts, batch, seq, t, first_layer):
    n = pw.shape[0]
    ns = seq // t
    const2 = lambda b, i: (0, 0)
    pcol = lambda col, w: pl.BlockSpec((t, w), lambda b, i: (b * ns + i, col // w))
    vec = pl.BlockSpec((1, W_BR), const2)
    in_specs = [pcol(COL_SR, W_BR), pcol(COL_SK, W_BR), pcol(COL_SV, W_BR), pcol(COL_WA, LANES),
                pcol(COL_GATE_C, W_BR)]
    args = [pw, pw, pw, pn, pw]
    if not first_layer:
        in_specs.append(pl.BlockSpec((t, W_BR), lambda b, i: (b * ns + i, 0)))
        args.append(vfirst)
    in_specs += [vec, vec, vec, pl.BlockSpec((1, LANES), const2), vec, vec,
                 pl.BlockSpec((LANES, W_BR), const2), pl.BlockSpec((LANES, W_BR), const2), vec, vec, vec]
    if not first_layer:
        in_specs += [vec, pl.BlockSpec((W_BR, LANES), const2), pl.BlockSpec((LANES, W_BR), const2)]
    in_specs += [pl.BlockSpec((t, t), const2), pl.BlockSpec((t, t), const2)]
    args += list(wts)
    hm_spec = pl.BlockSpec((1, H, t, HD), lambda b, i: (b, 0, i, 0))
    hm_dtypes = (BF16, F32, BF16, BF16, BF16, BF16, BF16, F32, F32, F32)
    out_specs = [hm_spec] * 10
    out_shape = [jax.ShapeDtypeStruct((batch, H, seq, HD), dt) for dt in hm_dtypes]
    if first_layer:
        out_specs.append(pl.BlockSpec((t, W_BR), lambda b, i: (b * ns + i, 0)))
        out_shape.append(jax.ShapeDtypeStruct((n, W_BR), F32))
    return pl.pallas_call(
        functools.partial(_rwkv_prep_kernel, first_layer=first_layer),
        grid=(batch, ns),
        in_specs=in_specs,
        out_specs=out_specs,
        out_shape=out_shape,
        scratch_shapes=[pltpu.VMEM((1, W_BR), F32), pltpu.VMEM((1, W_BR), F32), pltpu.VMEM((1, W_BR), F32),
                        pltpu.VMEM((1, LANES), F32)],
        compiler_params=_cparams(("parallel", "arbitrary")),
        name="rwkv_prep",
    )(*args)


def _mm(a, b, dims):
    return lax.dot_general(a.astype(BF16), b.astype(BF16), dims, preferred_element_type=F32)


_NN = (((1,), (0,)), ((), ()))
_NT = (((1,), (1,)), ((), ()))
_TN = (((0,), (0,)), ((), ()))


def _rwkv_chunk_kernel(at_ref, rt_ref, bt_ref, kt_ref, v_ref, bh_ref, kh_ref, wc_ref, bonus_ref, sg_ref,
                       lg_ref, lb_ref, o_ref, s_all_ref, *, nchunk):
    @pl.when(pl.program_id(1) == 0)
    def _():
        s_all_ref[...] = jnp.zeros_like(s_all_ref)

    mm = _mm
    row = lax.broadcasted_iota(jnp.int32, (CHUNK, CHUNK), 0)
    col = lax.broadcasted_iota(jnp.int32, (CHUNK, CHUNK), 1)
    strict = col < row
    incl = col <= row
    eye = col == row
    items = [(c, h) for c in range(nchunk) for h in range(H)]
    each = lambda f, *cols: [f(*xs) for xs in zip(*cols)]
    sls = [slice(c * CHUNK, (c + 1) * CHUNK) for c, _ in items]
    ld = lambda ref: [ref[0, h, sl, :] for (_, h), sl in zip(items, sls)]
    at, rt, bt, kt, v, bh, kh = ld(at_ref), ld(rt_ref), ld(bt_ref), ld(kt_ref), ld(v_ref), ld(bh_ref), ld(kh_ref)
    x1 = each(lambda a, r: jnp.concatenate([a, r.astype(BF16)], axis=0), at, rt)
    gb = each(lambda x, b: mm(x, b, _NT), x1, bt)
    gk = each(lambda x, k: mm(x, k, _NT), x1, kt)
    a_ab = [jnp.where(strict, g[:CHUNK], 0.0) for g in gb]
    a_rb = [jnp.where(incl, g[CHUNK:], 0.0) for g in gb]
    a_ak = [jnp.where(strict, g[:CHUNK], 0.0) for g in gk]
    a_rk = [jnp.where(incl, g[CHUNK:], 0.0) for g in gk]
    tinv = [jnp.where(eye, 1.0, a) for a in a_ab]
    pw = a_ab
    for _ in range(int(math.log2(CHUNK)) - 1):
        pw = each(lambda p: mm(p, p, _NN), pw)
        tinv = each(lambda t, p: t + mm(t, p, _NN), tinv, pw)
    av = each(lambda a, x: mm(a, x, _NN), a_ak, v)
    u = each(lambda t, x: mm(t, x, _NN), tinv, av)
    a_til = each(lambda t, x: mm(t, x, _NN), tinv, at)
    r_hat = each(lambda r, a, x: r + mm(a, x, _NN), rt, a_rb, a_til)
    y_in = each(lambda a, x: mm(a, x, _NN), a_rb, u)
    y_in = each(lambda y, a, x: y + mm(a, x, _NN), y_in, a_rk, v)
    m_new = each(lambda a, b: mm(a, b, _TN), a_til, bh)
    n_new = each(lambda a, b: mm(a, b, _TN), u, bh)
    n_new = each(lambda n, a, b: n + mm(a, b, _TN), n_new, v, kh)
    for c in range(nchunk):
        idx = [i for i, (ci, _) in enumerate(items) if ci == c]
        s0 = [s_all_ref[h] for h in range(H)]
        y = [y_in[i] + mm(r_hat[i], s0[h], _NT) for h, i in enumerate(idx)]
        for h, i in enumerate(idx):
            wc = wc_ref[0, h, c * CHUNK:c * CHUNK + 1, :]
            s_all_ref[h] = mm(s0[h], jnp.where(eye, wc, 0.0) + m_new[i], _NN) + n_new[i]
        for h, i in enumerate(idx):
            mu = jnp.mean(y[h], axis=-1, keepdims=True)
            d = y[h] - mu
            var = jnp.mean(d * d, axis=-1, keepdims=True)
            yn = d * lax.rsqrt(var + GN_EPS) * lg_ref[h] + lb_ref[h]
            o_ref[0, h, sls[i], :] = ((yn + bonus_ref[0, h, sls[i], :]) * sg_ref[0, h, sls[i], :]).astype(o_ref.dtype)


def _rwkv_chunk(hm, lnx_g, lnx_b, batch, seq, tc):
    spec = pl.BlockSpec((1, H, tc, HD), lambda b, i: (b, 0, i, 0))
    gspec = pl.BlockSpec((H, 1, HD), lambda b, i: (0, 0, 0))
    return pl.pallas_call(
        functools.partial(_rwkv_chunk_kernel, nchunk=tc // CHUNK),
        grid=(batch, seq // tc),
        in_specs=[spec] * 10 + [gspec, gspec],
        out_specs=spec,
        out_shape=jax.ShapeDtypeStruct((batch, H, seq, HD), BF16),
        scratch_shapes=[pltpu.VMEM((H, HD, HD), F32)],
        compiler_params=_cparams(("parallel", "arbitrary")),
        name="rwkv_chunk",
    )(*hm, lnx_g, lnx_b)


def _merge_kernel(oa_ref, ob_ref, oc_ref, ga_ref, gb_ref, gc_ref, x_ref, wpa_ref, wpb_ref, wpc_ref, wout_ref,
                  o_ref):
    pa = _dot(oa_ref[...], wpa_ref[...])
    pb = _dot(ob_ref[...], wpb_ref[...])
    pc = _dot(oc_ref[0, 0], wpc_ref[0])
    for h in range(1, H):
        pc = pc + _dot(oc_ref[0, h], wpc_ref[h])
    sig = lambda ref: _sigmoid(ref[...].astype(F32))
    merged = sig(ga_ref) * pa + sig(gb_ref) * pb + sig(gc_ref) * pc
    o_ref[...] = x_ref[...] + _dot(merged.astype(BF16), wout_ref[...])


def _merge(oa, ob, oc, p, x2, wpa, wpb, wpc, wout, batch, seq, t):
    n = x2.shape[0]
    ns = seq // t
    row = lambda b, i: (b * ns + i, 0)
    const2 = lambda b, i: (0, 0)
    gate = lambda g: pl.BlockSpec((t, D_MODEL), lambda b, i: (b * ns + i, COL_MERGE // D_MODEL + g))
    return pl.pallas_call(
        _merge_kernel,
        grid=(batch, ns),
        in_specs=[
            pl.BlockSpec((t, W_BR), row),
            pl.BlockSpec((t, W_BR), row),
            pl.BlockSpec((1, H, t, HD), lambda b, i: (b, 0, i, 0)),
            gate(0), gate(1), gate(2),
            pl.BlockSpec((t, D_MODEL), row),
            pl.BlockSpec((W_BR, D_MODEL), const2),
            pl.BlockSpec((W_BR, D_MODEL), const2),
            pl.BlockSpec((H, HD, D_MODEL), lambda b, i: (0, 0, 0)),
            pl.BlockSpec((D_MODEL, D_MODEL), const2),
        ],
        out_specs=pl.BlockSpec((t, D_MODEL), row),
        out_shape=jax.ShapeDtypeStruct((n, D_MODEL), F32),
        compiler_params=_cparams(("parallel", "parallel")),
        name="merge",
    )(oa, ob, oc, p, p, p, x2, wpa, wpb, wpc, wout)


def _regroup_w_in(w):
    d = w.shape[0]
    sizes = (Q_LORA, KV_LORA, ROPE_D, W_BR, W_BR, W_BR, W_BR, H, W_BR,
             3 * W_BR + DECAY_LORA + AAA_LORA, W_BR, 3 * D_MODEL)
    offs = np.concatenate([[0], np.cumsum(sizes)])
    (c_q, c_kv, k_r, gate_a, fq, fk, fv, ff, gate_b, shift, gate_c, merge) = [
        w[:, offs[i]:offs[i + 1]] for i in range(len(sizes))]
    z = lambda n: jnp.zeros((d, n), w.dtype)
    wide = jnp.concatenate([
        merge, gate_a, fq, fk, fv, gate_b,
        shift[:, :W_BR], shift[:, W_BR:2 * W_BR], shift[:, 2 * W_BR:3 * W_BR], gate_c,
    ], axis=1)
    narrow = jnp.concatenate([
        c_q, c_kv,
        z(NOPE), k_r, z(LANES - NOPE - ROPE_D),
        ff, z(LANES - H),
        shift[:, 3 * W_BR:],
    ], axis=1)
    return wide.astype(BF16), narrow.astype(BF16)


def _rope_tables(seq):
    half = ROPE_D // 2
    inv = ROPE_THETA ** (-jnp.arange(0, ROPE_D, 2, dtype=F32) / ROPE_D)
    ang = jnp.arange(seq, dtype=F32)[:, None] * inv[None, :]
    cos, sin = jnp.cos(ang), jnp.sin(ang)
    z = lambda n: jnp.zeros((seq, n), F32)
    c = jnp.concatenate([jnp.ones((seq, NOPE), F32), cos, cos, z(LANES - NOPE - ROPE_D)], axis=1)
    s1 = jnp.concatenate([z(NOPE), -sin, z(LANES - NOPE - half)], axis=1)
    s2 = jnp.concatenate([z(NOPE + half), sin, z(LANES - NOPE - ROPE_D)], axis=1)
    return c, s1, s2


def _pad_lanes(v, lo, total=LANES):
    return jnp.zeros((1, total), F32).at[0, lo:lo + v.shape[0]].set(v)


def _fox_selectors():
    selq = np.zeros((3, LANES, H * LANES), np.float32)
    selk = np.zeros((3, LANES, H * LANES), np.float32)
    oneq = np.zeros((1, H * LANES), np.float32)
    onek = np.zeros((1, H * LANES), np.float32)
    for h in range(H):
        for j in range(3):
            selq[j, h, h * LANES + HD + j] = 1.0
            selk[j, h, h * LANES + HD + 3 + j] = -1.0
            oneq[0, h * LANES + HD + 3 + j] = 1.0
            onek[0, h * LANES + HD + j] = 1.0
    return (jnp.asarray(selq, BF16), jnp.asarray(selk, BF16), jnp.asarray(oneq), jnp.asarray(onek))


def _tri(t):
    r = np.arange(t)
    return jnp.asarray((r[None, :] <= r[:, None]).astype(np.float32), BF16)


def _chunk_tri(t):
    r = np.arange(t)
    same = (r[None, :] // CHUNK) == (r[:, None] // CHUNK)
    lower = r[None, :] <= r[:, None]
    return (jnp.asarray((same & lower).astype(np.float32), BF16), jnp.asarray(same.astype(np.float32), BF16))


def _tile(seq, pref):
    return pref if seq % pref == 0 else seq


RWKV_TC = 128


def kernel(x, norm_g, w_in, mla_qa_g, mla_w_uq, mla_kva_g, mla_w_ukv, mla_q_g, mla_knope_g, mla_krope_g, fox_b_f, fox_q_g, fox_k_g, rwkv_mu, rwkv_w0, rwkv_w_up, rwkv_a0, rwkv_a_up, rwkv_k_k, rwkv_k_a, rwkv_r_k, rwkv_lnx_g, rwkv_lnx_b, rwkv_v0, rwkv_v_down, rwkv_v_up, w_pa, w_pb, w_pc, w_out):
    batch, seq, d = x.shape
    depth = w_in.shape[0]
    n = batch * seq
    t_row = _tile(seq, 512)
    t_in = _tile(n, 1024)
    t_rwkv = _tile(seq, 256)
    x2 = x.reshape(n, d)

    tabs = _rope_tables(seq)
    selq, selk, oneq, onek = _fox_selectors()
    tri = _tri(t_row)
    btri, bones = _chunk_tri(t_rwkv)
    row = lambda v: v.reshape(1, -1).astype(F32)

    vfirst = None
    for l in range(depth):
        w_wide, w_narrow = _regroup_w_in(w_in[l])
        pw = _inproj(x2, row(norm_g[l]), w_wide, t_in, INPROJ_TN, BF16)
        pn = _inproj(x2, row(norm_g[l]), w_narrow, t_in, NCOLS_NARROW, F32)

        wuq = mla_w_uq[l].reshape(Q_LORA, H, NOPE + ROPE_D)
        wuq = jnp.pad(wuq, ((0, 0), (0, 0), (0, LANES - NOPE - ROPE_D))).reshape(Q_LORA, H * LANES).astype(BF16)
        wukv = mla_w_ukv[l].reshape(KV_LORA, H, NOPE + HD)
        wk = jnp.pad(wukv[:, :, :NOPE], ((0, 0), (0, 0), (0, LANES - NOPE))).reshape(KV_LORA, H * LANES).astype(BF16)
        wv = wukv[:, :, NOPE:].reshape(KV_LORA, W_BR).astype(BF16)
        mla_scale = float(NOPE + ROPE_D) ** -0.5 * LOG2E
        mla_wts = (row(mla_qa_g[l]), wuq, row(mla_kva_g[l]), wk, wv,
                   _pad_lanes(mla_q_g[l] * mla_scale, 0), _pad_lanes(mla_knope_g[l], 0),
                   _pad_lanes(mla_krope_g[l], NOPE))
        qa, ka, va = _mla_prep(pn, tabs, mla_wts, seq, t_row)
        o_a = _attention(qa, ka, va, pw, COL_GATE_A, batch, seq, t_row)

        fox_wts = (_pad_lanes(fox_b_f[l], 0), row(jnp.tile(fox_q_g[l] * (float(HD) ** -0.5 * LOG2E), H)),
                   row(jnp.tile(fox_k_g[l], H)), tri, selq, selk, oneq, onek)
        qb, kb, vb = _fox_prep(pw, pn, fox_wts, batch, seq, t_row)
        o_b = _attention(qb, kb, vb, pw, COL_GATE_B, batch, seq, t_row)

        mu = rwkv_mu[l]
        wup = jnp.zeros((LANES, W_BR), F32).at[:DECAY_LORA].set(rwkv_w_up[l]).astype(BF16)
        aup = jnp.zeros((LANES, W_BR), F32).at[DECAY_LORA:].set(rwkv_a_up[l]).astype(BF16)
        rw = [row(mu[:W_BR]), row(mu[W_BR:2 * W_BR]), row(mu[2 * W_BR:3 * W_BR]), row(mu[3 * W_BR:]),
              row(rwkv_w0[l]), row(rwkv_a0[l]), wup, aup, row(rwkv_k_k[l]), row(rwkv_k_a[l]),
              row(rwkv_r_k[l])]
        if l > 0:
            vdown = jnp.zeros((W_BR, LANES), F32).at[:, :MV_LORA].set(rwkv_v_down[l - 1]).astype(BF16)
            vup = jnp.zeros((LANES, W_BR), F32).at[:MV_LORA].set(rwkv_v_up[l - 1]).astype(BF16)
            rw += [row(rwkv_v0[l - 1]), vdown, vup]
        rw += [btri, bones]
        outs = _rwkv_prep(pw, pn, vfirst, rw, batch, seq, t_rwkv, first_layer=(l == 0))
        if l == 0:
            vfirst = outs[10]
        o_c = _rwkv_chunk(outs[:10], rwkv_lnx_g[l].reshape(H, 1, HD), rwkv_lnx_b[l].reshape(H, 1, HD),
                          batch, seq, _tile(seq, RWKV_TC))

        x2 = _merge(o_a, o_b, o_c, pw, x2, w_pa[l].astype(BF16), w_pb[l].astype(BF16),
                    w_pc[l].reshape(H, HD, D_MODEL).astype(BF16), w_out[l].astype(BF16), batch, seq, t_row)
    return x2.reshape(batch, seq, d)
```

```python
import functools
import math

import jax
import jax.numpy as jnp
import numpy as np
from jax import lax
from jax.experimental import pallas as pl
from jax.experimental.pallas import tpu as pltpu

F32 = jnp.float32
BF16 = jnp.bfloat16

LANES = 128
H = 8
HD = 64
NOPE = 64
ROPE_D = 32
Q_LORA = 256
KV_LORA = 128
D_MODEL = 1024
W_BR = H * HD
DECAY_LORA = 64
AAA_LORA = 64
MV_LORA = 32
ROPE_THETA = 10000.0
RWKV_DECAY_SCALE = 0.606531
GN_EPS = 64e-5
EPS = 1e-6
NEG_INF = -1e30
LOG2E = math.log2(math.e)
CHUNK = 64
ONES_ROWS = 16

COL_MERGE = 0
COL_GATE_A = 3072
COL_FQ = 3584
COL_FK = 4096
COL_FV = 4608
COL_GATE_B = 5120
COL_SR = 5632
COL_SK = 6144
COL_SV = 6656
COL_GATE_C = 7168
NCOLS_WIDE = 7680
INPROJ_TN = 1280
COL_CQ = 0
COL_CKV = 256
COL_KR = 384
COL_FF = 512
COL_WA = 640
NCOLS_NARROW = 768

VMEM_LIMIT = 56 * 1024 * 1024


def _cparams(sem):
    return pltpu.CompilerParams(dimension_semantics=sem, vmem_limit_bytes=VMEM_LIMIT)


def _sigmoid(x):
    return 1.0 / (1.0 + jnp.exp(-x))


def _dot(a, b):
    return jnp.dot(a, b, preferred_element_type=F32)


def _split3(x):
    hi = x.astype(BF16)
    r1 = x - hi.astype(F32)
    mid = r1.astype(BF16)
    lo = (r1 - mid.astype(F32)).astype(BF16)
    return hi, mid, lo


def _dot_exact_lhs(m_bf16, x):
    hi, mid, lo = _split3(x)
    return _dot(m_bf16, hi) + _dot(m_bf16, mid) + _dot(m_bf16, lo)


def _pair_seg_sum(x):
    lane = lax.broadcasted_iota(jnp.int32, x.shape, 1)
    first = lane < HD
    s0 = jnp.sum(jnp.where(first, x, 0.0), axis=-1, keepdims=True)
    s1 = jnp.sum(jnp.where(first, 0.0, x), axis=-1, keepdims=True)
    return jnp.where(first, s0, s1)


def _inproj_kernel(x_ref, g_ref, w_ref, o_ref, h_ref):
    @pl.when(pl.program_id(1) == 0)
    def _():
        x = x_ref[...]
        ms = jnp.mean(x * x, axis=-1, keepdims=True)
        h_ref[...] = (x * lax.rsqrt(ms + EPS) * g_ref[...]).astype(BF16)

    o_ref[...] = _dot(h_ref[...], w_ref[...]).astype(o_ref.dtype)


def _inproj(x2, g, w, tm, tn, out_dtype):
    n, ncols = x2.shape[0], w.shape[1]
    return pl.pallas_call(
        _inproj_kernel,
        grid=(n // tm, ncols // tn),
        in_specs=[
            pl.BlockSpec((tm, D_MODEL), lambda i, j: (i, 0)),
            pl.BlockSpec((1, D_MODEL), lambda i, j: (0, 0)),
            pl.BlockSpec((D_MODEL, tn), lambda i, j: (0, j)),
        ],
        out_specs=pl.BlockSpec((tm, tn), lambda i, j: (i, j)),
        out_shape=jax.ShapeDtypeStruct((n, ncols), out_dtype),
        scratch_shapes=[pltpu.VMEM((tm, D_MODEL), BF16)],
        compiler_params=_cparams(("parallel", "arbitrary")),
        name="inproj",
    )(x2, g, w)


def _rms(x, g, n):
    ms = jnp.sum(x * x, axis=-1, keepdims=True) / n
    return x * lax.rsqrt(ms + EPS) * g


def _rope(x, c, s1, s2):
    return x * c + pltpu.roll(x, LANES - ROPE_D // 2, 1) * s1 + pltpu.roll(x, ROPE_D // 2, 1) * s2


def _mla_prep_kernel(cq_ref, ckv_ref, kr_ref, c_ref, s1_ref, s2_ref, qag_ref, wuq_ref, kvag_ref, wk_ref, wv_ref,
                     qg_ref, kng_ref, krg_ref, q_out, k_out, vt_out):
    c, s1, s2 = c_ref[...], s1_ref[...], s2_ref[...]
    cqn = _rms(cq_ref[...], qag_ref[...], Q_LORA).astype(BF16)
    ckvn = _rms(ckv_ref[...], kvag_ref[...], KV_LORA).astype(BF16)
    q_all = _dot(cqn, wuq_ref[...])
    k_all = _dot(ckvn, wk_ref[...])
    vt_out[0, 0] = _dot(ckvn, wv_ref[...]).T.astype(BF16)
    k_rope = _rope(_rms(kr_ref[...], krg_ref[...], ROPE_D), c, s1, s2)
    qg, kng = qg_ref[...], kng_ref[...]
    for h in range(H):
        sl = slice(h * LANES, (h + 1) * LANES)
        q_out[:, sl] = _rope(_rms(q_all[:, sl], qg, NOPE + ROPE_D), c, s1, s2).astype(BF16)
        k_out[:, sl] = (_rms(k_all[:, sl], kng, NOPE) + k_rope).astype(BF16)


def _mla_prep(pn, tabs, wts, seq, t):
    n = pn.shape[0]
    ns = seq // t
    row = lambda i: (i, 0)
    const = lambda i: (0, 0)
    tab_spec = pl.BlockSpec((t, LANES), lambda i: (i % ns, 0))
    return pl.pallas_call(
        _mla_prep_kernel,
        grid=(n // t,),
        in_specs=[
            pl.BlockSpec((t, Q_LORA), lambda i: (i, COL_CQ // Q_LORA)),
            pl.BlockSpec((t, LANES), lambda i: (i, COL_CKV // LANES)),
            pl.BlockSpec((t, LANES), lambda i: (i, COL_KR // LANES)),
            tab_spec, tab_spec, tab_spec,
            pl.BlockSpec((1, Q_LORA), const),
            pl.BlockSpec((Q_LORA, H * LANES), const),
            pl.BlockSpec((1, KV_LORA), const),
            pl.BlockSpec((KV_LORA, H * LANES), const),
            pl.BlockSpec((KV_LORA, W_BR), const),
            pl.BlockSpec((1, LANES), const),
            pl.BlockSpec((1, LANES), const),
            pl.BlockSpec((1, LANES), const),
        ],
        out_specs=[
            pl.BlockSpec((t, H * LANES), row),
            pl.BlockSpec((t, H * LANES), row),
            pl.BlockSpec((1, 1, W_BR, t), lambda i: (i // ns, i % ns, 0, 0)),
        ],
        out_shape=[
            jax.ShapeDtypeStruct((n, H * LANES), BF16),
            jax.ShapeDtypeStruct((n, H * LANES), BF16),
            jax.ShapeDtypeStruct((n // seq, ns, W_BR, t), BF16),
        ],
        compiler_params=_cparams(("parallel",)),
        name="mla_prep",
    )(pn, pn, pn, *tabs, *wts)


def _fox_prep_kernel(fq_ref, fk_ref, fv_ref, ff_ref, bf_ref, qg_ref, kg_ref, tri_ref, selq_ref, selk_ref,
                     oneq_ref, onek_ref, q_out, k_out, vt_out, carry_ref):
    t = fq_ref.shape[0]

    @pl.when(pl.program_id(1) == 0)
    def _():
        carry_ref[...] = jnp.zeros_like(carry_ref)

    z = ff_ref[...] + bf_ref[...]
    logf = jnp.minimum(z, 0.0) - jnp.log1p(jnp.exp(-jnp.abs(z)))
    cum = carry_ref[...] + _dot_exact_lhs(tri_ref[...], logf)
    carry_ref[...] = cum[t - 1:t, :]
    hi, mid, lo = _split3(cum * LOG2E)
    aug_q = _dot(hi, selq_ref[0]) + _dot(mid, selq_ref[1]) + _dot(lo, selq_ref[2]) + oneq_ref[...]
    aug_k = _dot(hi, selk_ref[0]) + _dot(mid, selk_ref[1]) + _dot(lo, selk_ref[2]) + onek_ref[...]
    vt_out[0, 0] = fv_ref[...].astype(F32).T.astype(BF16)
    lane = lax.broadcasted_iota(jnp.int32, (t, LANES), 1)
    first = lane < HD
    for src_ref, g_ref, aug, dst in ((fq_ref, qg_ref, aug_q, q_out), (fk_ref, kg_ref, aug_k, k_out)):
        for p in range(H // 2):
            sl = slice(p * LANES, (p + 1) * LANES)
            x = src_ref[:, sl].astype(F32)
            ms = _pair_seg_sum(x * x) / HD
            xn = x * lax.rsqrt(ms + EPS) * g_ref[:, sl]
            he, ho = 2 * p, 2 * p + 1
            dst[:, he * LANES:(he + 1) * LANES] = jnp.where(first, xn, aug[:, he * LANES:(he + 1) * LANES]).astype(BF16)
            dst[:, ho * LANES:(ho + 1) * LANES] = jnp.where(
                first, pltpu.roll(xn, HD, 1), aug[:, ho * LANES:(ho + 1) * LANES]).astype(BF16)


def _fox_prep(pw, pn, wts, batch, seq, t):
    n = pw.shape[0]
    ns = seq // t
    row = lambda b, i: (b * ns + i, 0)
    const2 = lambda b, i: (0, 0)
    const3 = lambda b, i: (0, 0, 0)
    return pl.pallas_call(
        _fox_prep_kernel,
        grid=(batch, ns),
        in_specs=[
            pl.BlockSpec((t, W_BR), lambda b, i: (b * ns + i, COL_FQ // W_BR)),
            pl.BlockSpec((t, W_BR), lambda b, i: (b * ns + i, COL_FK // W_BR)),
            pl.BlockSpec((t, W_BR), lambda b, i: (b * ns + i, COL_FV // W_BR)),
            pl.BlockSpec((t, LANES), lambda b, i: (b * ns + i, COL_FF // LANES)),
            pl.BlockSpec((1, LANES), const2),
            pl.BlockSpec((1, W_BR), const2),
            pl.BlockSpec((1, W_BR), const2),
            pl.BlockSpec((t, t), const2),
            pl.BlockSpec((3, LANES, H * LANES), const3),
            pl.BlockSpec((3, LANES, H * LANES), const3),
            pl.BlockSpec((1, H * LANES), const2),
            pl.BlockSpec((1, H * LANES), const2),
        ],
        out_specs=[
            pl.BlockSpec((t, H * LANES), row),
            pl.BlockSpec((t, H * LANES), row),
            pl.BlockSpec((1, 1, W_BR, t), lambda b, i: (b, i, 0, 0)),
        ],
        out_shape=[
            jax.ShapeDtypeStruct((n, H * LANES), BF16),
            jax.ShapeDtypeStruct((n, H * LANES), BF16),
            jax.ShapeDtypeStruct((batch, ns, W_BR, t), BF16),
        ],
        scratch_shapes=[pltpu.VMEM((1, LANES), F32)],
        compiler_params=_cparams(("parallel", "arbitrary")),
        name="fox_prep",
    )(pw, pw, pw, pn, *wts)


def _attn_kernel(q_ref, k_ref, vt_ref, g_ref, o_ref, m_ref, acc_ref, sa_ref, sb_ref, mxa_ref, mxb_ref, *, tq):
    qi = pl.program_id(2)
    nt = (((1,), (1,)), ((), ()))
    kv_pos = lax.broadcasted_iota(jnp.int32, (tq, tq), 0)
    q_pos = lax.broadcasted_iota(jnp.int32, (tq, tq), 1)
    causal = kv_pos <= q_pos
    ones = jnp.ones((ONES_ROWS, tq), BF16)
    m_ref[...] = jnp.full(m_ref.shape, NEG_INF, F32)
    acc_ref[...] = jnp.zeros(acc_ref.shape, F32)

    def scores(c, s_ref, mx_ref):
        start = pl.multiple_of(c * tq, tq)
        for j in range(2):
            q = q_ref[:, j * LANES:(j + 1) * LANES]
            k = k_ref[pl.ds(start, tq), j * LANES:(j + 1) * LANES]
            s = lax.dot_general(k, q, nt, preferred_element_type=F32)
            s_ref[j] = s
            mx_ref[j] = jnp.max(s, axis=0, keepdims=True)

    def consume(c, s_ref, mx_ref, masked):
        for j in range(2):
            vt = jnp.concatenate([vt_ref[0, c, j * HD:(j + 1) * HD, :], ones], axis=0)
            s = s_ref[j]
            if masked:
                s = jnp.where(causal, s, NEG_INF)
                m_cur = jnp.max(s, axis=0, keepdims=True)
            else:
                m_cur = mx_ref[j]
            m_old = m_ref[j]
            m_new = jnp.maximum(m_old, m_cur)
            alpha = jnp.exp2(m_old - m_new)
            p = jnp.exp2(s - m_new)
            acc_ref[j] = alpha * acc_ref[j] + _dot(vt, p.astype(BF16))
            m_ref[j] = m_new

    scores(0, sa_ref, mxa_ref)

    def body(i, carry):
        c = 2 * i
        scores(c + 1, sb_ref, mxb_ref)
        consume(c, sa_ref, mxa_ref, False)
        scores(c + 2, sa_ref, mxa_ref)
        consume(c + 1, sb_ref, mxb_ref, False)
        return carry

    lax.fori_loop(0, qi // 2, body, 0)

    @pl.when(qi % 2 == 1)
    def _():
        scores(qi, sb_ref, mxb_ref)
        consume(qi - 1, sa_ref, mxa_ref, False)
        consume(qi, sb_ref, mxb_ref, True)

    @pl.when(qi % 2 == 0)
    def _():
        consume(qi, sa_ref, mxa_ref, True)

    o_t = jnp.concatenate([acc_ref[j, :HD, :] / acc_ref[j, HD:HD + 1, :] for j in range(2)], axis=0)
    g = g_ref[...].astype(F32)
    o_ref[...] = (o_t.T * (g * _sigmoid(g))).astype(BF16)


def _attention(q, k, vt, pw, gate_col, batch, seq, tq):
    n = q.shape[0]
    nq = seq // tq
    return pl.pallas_call(
        functools.partial(_attn_kernel, tq=tq),
        grid=(batch, H // 2, nq),
        in_specs=[
            pl.BlockSpec((tq, 2 * LANES), lambda b, hp, i: (b * nq + i, hp)),
            pl.BlockSpec((seq, 2 * LANES), lambda b, hp, i: (b, hp)),
            pl.BlockSpec((1, nq, LANES, tq), lambda b, hp, i: (b, 0, hp, 0)),
            pl.BlockSpec((tq, LANES), lambda b, hp, i: (b * nq + i, gate_col // LANES + hp)),
        ],
        out_specs=pl.BlockSpec((tq, LANES), lambda b, hp, i: (b * nq + i, hp)),
        out_shape=jax.ShapeDtypeStruct((n, W_BR), BF16),
        scratch_shapes=[
            pltpu.VMEM((2, 1, tq), F32),
            pltpu.VMEM((2, HD + ONES_ROWS, tq), F32),
            pltpu.VMEM((2, tq, tq), F32),
            pltpu.VMEM((2, tq, tq), F32),
            pltpu.VMEM((2, 1, tq), F32),
            pltpu.VMEM((2, 1, tq), F32),
        ],
        compiler_params=_cparams(("parallel", "parallel", "arbitrary")),
        name="attention",
    )(q, k, vt, pw)


def _rwkv_prep_kernel(*refs, first_layer):
    if first_layer:
        (sr_ref, sk_ref, sv_ref, swa_ref, gc_ref, mur_ref, muk_ref, muv_ref, muwa_ref, w0_ref, a0_ref, wup_ref,
         aup_ref, kk_ref, ka_ref, rk_ref, bt_ref, bo_ref,
         at_o, rt_o, bt_o, kt_o, v_o, bh_o, kh_o, wc_o, bonus_o, sg_o, vfirst_o,
         cr_ref, ck_ref, cv_ref, cwa_ref) = refs
    else:
        (sr_ref, sk_ref, sv_ref, swa_ref, gc_ref, vf_ref, mur_ref, muk_ref, muv_ref, muwa_ref, w0_ref, a0_ref,
         wup_ref, aup_ref, kk_ref, ka_ref, rk_ref, v0_ref, vdown_ref, vup_ref, bt_ref, bo_ref,
         at_o, rt_o, bt_o, kt_o, v_o, bh_o, kh_o, wc_o, bonus_o, sg_o,
         cr_ref, ck_ref, cv_ref, cwa_ref) = refs
    t = sr_ref.shape[0]

    @pl.when(pl.program_id(1) == 0)
    def _():
        for c_ref in (cr_ref, ck_ref, cv_ref, cwa_ref):
            c_ref[...] = jnp.zeros_like(c_ref)

    def shift(x_ref, c_ref, mu_ref):
        x = x_ref[...].astype(F32)
        rowid = lax.broadcasted_iota(jnp.int32, x.shape, 0)
        prev = jnp.where(rowid == 0, c_ref[...], pltpu.roll(x, 1, 0))
        c_ref[...] = x[t - 1:t, :]
        return x + mu_ref[...] * (prev - x)

    r = shift(sr_ref, cr_ref, mur_ref)
    k = shift(sk_ref, ck_ref, muk_ref)
    v = shift(sv_ref, cv_ref, muv_ref)
    wa = shift(swa_ref, cwa_ref, muwa_ref)

    logw = -RWKV_DECAY_SCALE * _sigmoid(w0_ref[...] + _dot(jnp.tanh(wa).astype(BF16), wup_ref[...]))
    a = _sigmoid(a0_ref[...] + _dot(wa.astype(BF16), aup_ref[...]))
    if first_layer:
        vfirst_o[...] = v
    else:
        low = _dot(v.astype(BF16), vdown_ref[...]).astype(BF16)
        nu = _sigmoid(v0_ref[...] + _dot(low, vup_ref[...]))
        v = v + (vf_ref[...] - v) * nu

    kk = k * kk_ref[...]
    k_mod = k * (1.0 + (a - 1.0) * ka_ref[...])
    rk = r * k_mod * rk_ref[...]
    kk_n, bonus = [], []
    for p in range(H // 2):
        sl = slice(p * LANES, (p + 1) * LANES)
        nrm = jnp.sqrt(_pair_seg_sum(kk[:, sl] * kk[:, sl]))
        kk_n.append(kk[:, sl] / jnp.maximum(nrm, 1e-12))
        bonus.append(_pair_seg_sum(rk[:, sl]) * v[:, sl])
    kk = jnp.concatenate(kk_n, axis=-1)
    bonus = jnp.concatenate(bonus, axis=-1)
    a_vec = -kk
    b_vec = kk * a

    lcum = _dot_exact_lhs(bt_ref[...], logw)
    ltot = _dot_exact_lhs(bo_ref[...], logw)
    e_minus = jnp.exp(-lcum)
    e_rem = jnp.exp(ltot - lcum)
    g = gc_ref[...].astype(F32)
    for o_ref, val in ((at_o, a_vec * jnp.exp(lcum - logw)), (rt_o, r * jnp.exp(lcum)), (bt_o, b_vec * e_minus),
                       (kt_o, k_mod * e_minus), (v_o, v), (bh_o, b_vec * e_rem), (kh_o, k_mod * e_rem),
                       (wc_o, jnp.exp(ltot)), (bonus_o, bonus), (sg_o, g * _sigmoid(g))):
        o_ref[...] = val.astype(o_ref.dtype)


def _rwkv_prep(pw, pn, vfirst, wts, batch, seq, t, first_layer):
    n = pw.shape[0]
    ns = seq // t
    const2 = lambda b, i: (0, 0)
    pcol = lambda col, w: pl.BlockSpec((t, w), lambda b, i: (b * ns + i, col // w))
    vec = pl.BlockSpec((1, W_BR), const2)
    in_specs = [pcol(COL_SR, W_BR), pcol(COL_SK, W_BR), pcol(COL_SV, W_BR), pcol(COL_WA, LANES),
                pcol(COL_GATE_C, W_BR)]
    args = [pw, pw, pw, pn, pw]
    if not first_layer:
        in_specs.append(pl.BlockSpec((t, W_BR), lambda b, i: (b * ns + i, 0)))
        args.append(vfirst)
    in_specs += [vec, vec, vec, pl.BlockSpec((1, LANES), const2), vec, vec,
                 pl.BlockSpec((LANES, W_BR), const2), pl.BlockSpec((LANES, W_BR), const2), vec, vec, vec]
    if not first_layer:
        in_specs += [vec, pl.BlockSpec((W_BR, LANES), const2), pl.BlockSpec((LANES, W_BR), const2)]
    in_specs += [pl.BlockSpec((t, t), const2), pl.BlockSpec((t, t), const2)]
    args += list(wts)
    row_spec = pl.BlockSpec((t, W_BR), lambda b, i: (b * ns + i, 0))
    out_dtypes = (BF16, F32, BF16, BF16, BF16, BF16, BF16, F32, F32, F32) + ((F32,) if first_layer else ())
    out_specs = [row_spec] * len(out_dtypes)
    out_shape = [jax.ShapeDtypeStruct((n, W_BR), dt) for dt in out_dtypes]
    return pl.pallas_call(
        functools.partial(_rwkv_prep_kernel, first_layer=first_layer),
        grid=(batch, ns),
        in_specs=in_specs,
        out_specs=out_specs,
        out_shape=out_shape,
        scratch_shapes=[pltpu.VMEM((1, W_BR), F32), pltpu.VMEM((1, W_BR), F32), pltpu.VMEM((1, W_BR), F32),
                        pltpu.VMEM((1, LANES), F32)],
        compiler_params=_cparams(("parallel", "arbitrary")),
        name="rwkv_prep",
    )(*args)


def _mm(a, b, dims):
    return lax.dot_general(a.astype(BF16), b.astype(BF16), dims, preferred_element_type=F32)


_NN = (((1,), (0,)), ((), ()))
_NT = (((1,), (1,)), ((), ()))
_TN = (((0,), (0,)), ((), ()))


def _rwkv_chunk_kernel(at_ref, rt_ref, bt_ref, kt_ref, v_ref, bh_ref, kh_ref, wc_ref, bonus_ref, sg_ref,
                       lg_ref, lb_ref, o_ref, s_all_ref, *, nchunk):
    @pl.when(pl.program_id(1) == 0)
    def _():
        s_all_ref[...] = jnp.zeros_like(s_all_ref)

    mm = _mm
    row = lax.broadcasted_iota(jnp.int32, (CHUNK, CHUNK), 0)
    col = lax.broadcasted_iota(jnp.int32, (CHUNK, CHUNK), 1)
    strict = col < row
    incl = col <= row
    eye = col == row
    first = lax.broadcasted_iota(jnp.int32, (CHUNK, LANES), 1) < HD
    r2 = lax.broadcasted_iota(jnp.int32, (LANES, LANES), 0)
    c2 = lax.broadcasted_iota(jnp.int32, (LANES, LANES), 1)
    same_head = (r2 < HD) == (c2 < HD)
    eye2 = r2 == c2
    each = lambda f, *cols: [f(*xs) for xs in zip(*cols)]
    pairs = [(c, p) for c in range(nchunk) for p in range(H // 2)]
    ld = lambda ref: [ref[c * CHUNK:(c + 1) * CHUNK, p * LANES:(p + 1) * LANES] for c, p in pairs]
    at, rt, bt, kt, v, bh, kh = ld(at_ref), ld(rt_ref), ld(bt_ref), ld(kt_ref), ld(v_ref), ld(bh_ref), ld(kh_ref)
    x1 = each(lambda a, r: jnp.concatenate([a, r.astype(BF16)], axis=0), at, rt)
    heads = [(i, j) for i in range(len(pairs)) for j in range(2)]
    hmask = lambda j: first if j == 0 else jnp.logical_not(first)
    own = lambda x, j: jnp.where(hmask(j), x, jnp.zeros_like(x))
    gb = [mm(x1[i], own(bt[i], j), _NT) for i, j in heads]
    gk = [mm(x1[i], own(kt[i], j), _NT) for i, j in heads]
    a_ab = [jnp.where(strict, g[:CHUNK], 0.0) for g in gb]
    a_rb = [jnp.where(incl, g[CHUNK:], 0.0) for g in gb]
    a_ak = [jnp.where(strict, g[:CHUNK], 0.0) for g in gk]
    a_rk = [jnp.where(incl, g[CHUNK:], 0.0) for g in gk]
    tinv = [jnp.where(eye, 1.0, a) for a in a_ab]
    pw = a_ab
    for _ in range(int(math.log2(CHUNK)) - 1):
        pw = each(lambda p: mm(p, p, _NN), pw)
        tinv = each(lambda t, p: t + mm(t, p, _NN), tinv, pw)
    av = [mm(a_ak[n], v[i], _NN) for n, (i, _) in enumerate(heads)]
    tx = [mm(tinv[n], jnp.concatenate([av[n].astype(BF16), at[i]], axis=1), _NN)
          for n, (i, _) in enumerate(heads)]
    ry = each(lambda a, x: mm(a, x, _NN), a_rb, tx)
    yk = [mm(a_rk[n], v[i], _NN) for n, (i, _) in enumerate(heads)]
    npair = range(len(pairs))
    pick = lambda xs: [jnp.where(first, xs[2 * i], xs[2 * i + 1]) for i in npair]
    u = pick([t[:, :LANES] for t in tx])
    a_til = pick([t[:, LANES:] for t in tx])
    y_in = pick([r[:, :LANES] + k for r, k in zip(ry, yk)])
    r_hat = each(lambda r, x: r + x, rt, pick([r[:, LANES:] for r in ry]))
    m_new = [jnp.where(same_head, mm(a_til[i], bh[i], _TN), 0.0) for i in npair]
    n_new = [jnp.where(same_head, mm(jnp.concatenate([u[i].astype(BF16), v[i]], axis=0),
                                     jnp.concatenate([bh[i], kh[i]], axis=0), _TN), 0.0) for i in npair]
    lg, lb = lg_ref[...], lb_ref[...]
    for c in range(nchunk):
        idx = [i for i, (ci, _) in enumerate(pairs) if ci == c]
        s0 = [s_all_ref[p] for p in range(H // 2)]
        y = [y_in[i] + mm(r_hat[i], s0[p], _NT) for p, i in enumerate(idx)]
        for p, i in enumerate(idx):
            wc = wc_ref[c * CHUNK:c * CHUNK + 1, p * LANES:(p + 1) * LANES]
            s_all_ref[p] = mm(s0[p], jnp.where(eye2, wc, 0.0) + m_new[i], _NN) + n_new[i]
        for p, i in enumerate(idx):
            rows, lanes = slice(c * CHUNK, (c + 1) * CHUNK), slice(p * LANES, (p + 1) * LANES)
            mu = _pair_seg_sum(y[p]) / HD
            d = y[p] - mu
            var = _pair_seg_sum(d * d) / HD
            yn = d * lax.rsqrt(var + GN_EPS) * lg[:, lanes] + lb[:, lanes]
            o_ref[rows, lanes] = ((yn + bonus_ref[rows, lanes]) * sg_ref[rows, lanes]).astype(o_ref.dtype)


def _rwkv_chunk(arrs, lnx_g, lnx_b, batch, seq, tc):
    n = batch * seq
    ns = seq // tc
    spec = pl.BlockSpec((tc, W_BR), lambda b, i: (b * ns + i, 0))
    gspec = pl.BlockSpec((1, W_BR), lambda b, i: (0, 0))
    return pl.pallas_call(
        functools.partial(_rwkv_chunk_kernel, nchunk=tc // CHUNK),
        grid=(batch, ns),
        in_specs=[spec] * 10 + [gspec, gspec],
        out_specs=spec,
        out_shape=jax.ShapeDtypeStruct((n, W_BR), BF16),
        scratch_shapes=[pltpu.VMEM((H // 2, LANES, LANES), F32)],
        compiler_params=_cparams(("parallel", "arbitrary")),
        name="rwkv_chunk",
    )(*arrs, lnx_g, lnx_b)


def _merge_kernel(oa_ref, ob_ref, oc_ref, ga_ref, gb_ref, gc_ref, x_ref, wpa_ref, wpb_ref, wpc_ref, wout_ref,
                  o_ref):
    pa = _dot(oa_ref[...], wpa_ref[...])
    pb = _dot(ob_ref[...], wpb_ref[...])
    pc = _dot(oc_ref[...], wpc_ref[...])
    sig = lambda ref: _sigmoid(ref[...].astype(F32))
    merged = sig(ga_ref) * pa + sig(gb_ref) * pb + sig(gc_ref) * pc
    o_ref[...] = x_ref[...] + _dot(merged.astype(BF16), wout_ref[...])


def _merge(oa, ob, oc, p, x2, wpa, wpb, wpc, wout, batch, seq, t):
    n = x2.shape[0]
    ns = seq // t
    row = lambda b, i: (b * ns + i, 0)
    const2 = lambda b, i: (0, 0)
    gate = lambda g: pl.BlockSpec((t, D_MODEL), lambda b, i: (b * ns + i, COL_MERGE // D_MODEL + g))
    return pl.pallas_call(
        _merge_kernel,
        grid=(batch, ns),
        in_specs=[
            pl.BlockSpec((t, W_BR), row),
            pl.BlockSpec((t, W_BR), row),
            pl.BlockSpec((t, W_BR), row),
            gate(0), gate(1), gate(2),
            pl.BlockSpec((t, D_MODEL), row),
            pl.BlockSpec((W_BR, D_MODEL), const2),
            pl.BlockSpec((W_BR, D_MODEL), const2),
            pl.BlockSpec((W_BR, D_MODEL), const2),
            pl.BlockSpec((D_MODEL, D_MODEL), const2),
        ],
        out_specs=pl.BlockSpec((t, D_MODEL), row),
        out_shape=jax.ShapeDtypeStruct((n, D_MODEL), F32),
        compiler_params=_cparams(("parallel", "parallel")),
        name="merge",
    )(oa, ob, oc, p, p, p, x2, wpa, wpb, wpc, wout)


def _regroup_w_in(w):
    d = w.shape[0]
    sizes = (Q_LORA, KV_LORA, ROPE_D, W_BR, W_BR, W_BR, W_BR, H, W_BR,
             3 * W_BR + DECAY_LORA + AAA_LORA, W_BR, 3 * D_MODEL)
    offs = np.concatenate([[0], np.cumsum(sizes)])
    (c_q, c_kv, k_r, gate_a, fq, fk, fv, ff, gate_b, shift, gate_c, merge) = [
        w[:, offs[i]:offs[i + 1]] for i in range(len(sizes))]
    z = lambda n: jnp.zeros((d, n), w.dtype)
    wide = jnp.concatenate([
        merge, gate_a, fq, fk, fv, gate_b,
        shift[:, :W_BR], shift[:, W_BR:2 * W_BR], shift[:, 2 * W_BR:3 * W_BR], gate_c,
    ], axis=1)
    narrow = jnp.concatenate([
        c_q, c_kv,
        z(NOPE), k_r, z(LANES - NOPE - ROPE_D),
        ff, z(LANES - H),
        shift[:, 3 * W_BR:],
    ], axis=1)
    return wide.astype(BF16), narrow.astype(BF16)


def _rope_tables(seq):
    half = ROPE_D // 2
    inv = ROPE_THETA ** (-jnp.arange(0, ROPE_D, 2, dtype=F32) / ROPE_D)
    ang = jnp.arange(seq, dtype=F32)[:, None] * inv[None, :]
    cos, sin = jnp.cos(ang), jnp.sin(ang)
    z = lambda n: jnp.zeros((seq, n), F32)
    c = jnp.concatenate([jnp.ones((seq, NOPE), F32), cos, cos, z(LANES - NOPE - ROPE_D)], axis=1)
    s1 = jnp.concatenate([z(NOPE), -sin, z(LANES - NOPE - half)], axis=1)
    s2 = jnp.concatenate([z(NOPE + half), sin, z(LANES - NOPE - ROPE_D)], axis=1)
    return c, s1, s2


def _pad_lanes(v, lo, total=LANES):
    return jnp.zeros((1, total), F32).at[0, lo:lo + v.shape[0]].set(v)


def _fox_selectors():
    selq = np.zeros((3, LANES, H * LANES), np.float32)
    selk = np.zeros((3, LANES, H * LANES), np.float32)
    oneq = np.zeros((1, H * LANES), np.float32)
    onek = np.zeros((1, H * LANES), np.float32)
    for h in range(H):
        for j in range(3):
            selq[j, h, h * LANES + HD + j] = 1.0
            selk[j, h, h * LANES + HD + 3 + j] = -1.0
            oneq[0, h * LANES + HD + 3 + j] = 1.0
            onek[0, h * LANES + HD + j] = 1.0
    return (jnp.asarray(selq, BF16), jnp.asarray(selk, BF16), jnp.asarray(oneq), jnp.asarray(onek))


def _tri(t):
    r = np.arange(t)
    return jnp.asarray((r[None, :] <= r[:, None]).astype(np.float32), BF16)


def _chunk_tri(t):
    r = np.arange(t)
    same = (r[None, :] // CHUNK) == (r[:, None] // CHUNK)
    lower = r[None, :] <= r[:, None]
    return (jnp.asarray((same & lower).astype(np.float32), BF16), jnp.asarray(same.astype(np.float32), BF16))


def _tile(seq, pref):
    return pref if seq % pref == 0 else seq


RWKV_TC = 128


def kernel(x, norm_g, w_in, mla_qa_g, mla_w_uq, mla_kva_g, mla_w_ukv, mla_q_g, mla_knope_g, mla_krope_g, fox_b_f, fox_q_g, fox_k_g, rwkv_mu, rwkv_w0, rwkv_w_up, rwkv_a0, rwkv_a_up, rwkv_k_k, rwkv_k_a, rwkv_r_k, rwkv_lnx_g, rwkv_lnx_b, rwkv_v0, rwkv_v_down, rwkv_v_up, w_pa, w_pb, w_pc, w_out):
    batch, seq, d = x.shape
    depth = w_in.shape[0]
    n = batch * seq
    t_row = _tile(seq, 512)
    t_in = _tile(n, 1024)
    t_rwkv = _tile(seq, 256)
    x2 = x.reshape(n, d)

    tabs = _rope_tables(seq)
    selq, selk, oneq, onek = _fox_selectors()
    tri = _tri(t_row)
    btri, bones = _chunk_tri(t_rwkv)
    row = lambda v: v.reshape(1, -1).astype(F32)

    vfirst = None
    for l in range(depth):
        w_wide, w_narrow = _regroup_w_in(w_in[l])
        pw = _inproj(x2, row(norm_g[l]), w_wide, t_in, INPROJ_TN, BF16)
        pn = _inproj(x2, row(norm_g[l]), w_narrow, t_in, NCOLS_NARROW, F32)

        wuq = mla_w_uq[l].reshape(Q_LORA, H, NOPE + ROPE_D)
        wuq = jnp.pad(wuq, ((0, 0), (0, 0), (0, LANES - NOPE - ROPE_D))).reshape(Q_LORA, H * LANES).astype(BF16)
        wukv = mla_w_ukv[l].reshape(KV_LORA, H, NOPE + HD)
        wk = jnp.pad(wukv[:, :, :NOPE], ((0, 0), (0, 0), (0, LANES - NOPE))).reshape(KV_LORA, H * LANES).astype(BF16)
        wv = wukv[:, :, NOPE:].reshape(KV_LORA, W_BR).astype(BF16)
        mla_scale = float(NOPE + ROPE_D) ** -0.5 * LOG2E
        mla_wts = (row(mla_qa_g[l]), wuq, row(mla_kva_g[l]), wk, wv,
                   _pad_lanes(mla_q_g[l] * mla_scale, 0), _pad_lanes(mla_knope_g[l], 0),
                   _pad_lanes(mla_krope_g[l], NOPE))
        qa, ka, va = _mla_prep(pn, tabs, mla_wts, seq, t_row)
        o_a = _attention(qa, ka, va, pw, COL_GATE_A, batch, seq, t_row)

        fox_wts = (_pad_lanes(fox_b_f[l], 0), row(jnp.tile(fox_q_g[l] * (float(HD) ** -0.5 * LOG2E), H)),
                   row(jnp.tile(fox_k_g[l], H)), tri, selq, selk, oneq, onek)
        qb, kb, vb = _fox_prep(pw, pn, fox_wts, batch, seq, t_row)
        o_b = _attention(qb, kb, vb, pw, COL_GATE_B, batch, seq, t_row)

        mu = rwkv_mu[l]
        wup = jnp.zeros((LANES, W_BR), F32).at[:DECAY_LORA].set(rwkv_w_up[l]).astype(BF16)
        aup = jnp.zeros((LANES, W_BR), F32).at[DECAY_LORA:].set(rwkv_a_up[l]).astype(BF16)
        rw = [row(mu[:W_BR]), row(mu[W_BR:2 * W_BR]), row(mu[2 * W_BR:3 * W_BR]), row(mu[3 * W_BR:]),
              row(rwkv_w0[l]), row(rwkv_a0[l]), wup, aup, row(rwkv_k_k[l]), row(rwkv_k_a[l]),
              row(rwkv_r_k[l])]
        if l > 0:
            vdown = jnp.zeros((W_BR, LANES), F32).at[:, :MV_LORA].set(rwkv_v_down[l - 1]).astype(BF16)
            vup = jnp.zeros((LANES, W_BR), F32).at[:MV_LORA].set(rwkv_v_up[l - 1]).astype(BF16)
            rw += [row(rwkv_v0[l - 1]), vdown, vup]
        rw += [btri, bones]
        outs = _rwkv_prep(pw, pn, vfirst, rw, batch, seq, t_rwkv, first_layer=(l == 0))
        if l == 0:
            vfirst = outs[10]
        o_c = _rwkv_chunk(outs[:10], row(rwkv_lnx_g[l]), row(rwkv_lnx_b[l]), batch, seq, _tile(seq, RWKV_TC))

        x2 = _merge(o_a, o_b, o_c, pw, x2, w_pa[l].astype(BF16), w_pb[l].astype(BF16),
                    w_pc[l].astype(BF16), w_out[l].astype(BF16), batch, seq, t_row)
    return x2.reshape(batch, seq, d)
```

```python
import functools
import math

import jax
import jax.numpy as jnp
import numpy as np
from jax import lax
from jax.experimental import pallas as pl
from jax.experimental.pallas import tpu as pltpu

F32 = jnp.float32
BF16 = jnp.bfloat16

LANES = 128
H = 8
HD = 64
NOPE = 64
ROPE_D = 32
Q_LORA = 256
KV_LORA = 128
D_MODEL = 1024
W_BR = H * HD
DECAY_LORA = 64
AAA_LORA = 64
MV_LORA = 32
ROPE_THETA = 10000.0
RWKV_DECAY_SCALE = 0.606531
GN_EPS = 64e-5
EPS = 1e-6
NEG_INF = -1e30
LOG2E = math.log2(math.e)
CHUNK = 64
ONES_ROWS = 16

COL_MERGE = 0
COL_GATE_A = 3072
COL_FQ = 3584
COL_FK = 4096
COL_FV = 4608
COL_GATE_B = 5120
COL_SR = 5632
COL_SK = 6144
COL_SV = 6656
COL_GATE_C = 7168
NCOLS_WIDE = 7680
INPROJ_TN = 1280
COL_CQ = 0
COL_CKV = 256
COL_KR = 384
COL_FF = 512
COL_WA = 640
NCOLS_NARROW = 768

VMEM_LIMIT = 56 * 1024 * 1024


def _cparams(sem):
    return pltpu.CompilerParams(dimension_semantics=sem, vmem_limit_bytes=VMEM_LIMIT)


def _sigmoid(x):
    return 1.0 / (1.0 + jnp.exp(-x))


def _dot(a, b):
    return jnp.dot(a, b, preferred_element_type=F32)


def _split3(x):
    hi = x.astype(BF16)
    r1 = x - hi.astype(F32)
    mid = r1.astype(BF16)
    lo = (r1 - mid.astype(F32)).astype(BF16)
    return hi, mid, lo


def _dot_exact_lhs(m_bf16, x):
    hi, mid, lo = _split3(x)
    return _dot(m_bf16, hi) + _dot(m_bf16, mid) + _dot(m_bf16, lo)


def _pair_seg_sum(x):
    lane = lax.broadcasted_iota(jnp.int32, x.shape, 1)
    first = lane < HD
    s0 = jnp.sum(jnp.where(first, x, 0.0), axis=-1, keepdims=True)
    s1 = jnp.sum(jnp.where(first, 0.0, x), axis=-1, keepdims=True)
    return jnp.where(first, s0, s1)


def _inproj_kernel(x_ref, g_ref, w_ref, o_ref, h_ref):
    @pl.when(pl.program_id(1) == 0)
    def _():
        x = x_ref[...]
        ms = jnp.mean(x * x, axis=-1, keepdims=True)
        h_ref[...] = (x * lax.rsqrt(ms + EPS) * g_ref[...]).astype(BF16)

    o_ref[...] = _dot(h_ref[...], w_ref[...]).astype(o_ref.dtype)


def _inproj(x2, g, w, tm, tn, out_dtype):
    n, ncols = x2.shape[0], w.shape[1]
    return pl.pallas_call(
        _inproj_kernel,
        grid=(n // tm, ncols // tn),
        in_specs=[
            pl.BlockSpec((tm, D_MODEL), lambda i, j: (i, 0)),
            pl.BlockSpec((1, D_MODEL), lambda i, j: (0, 0)),
            pl.BlockSpec((D_MODEL, tn), lambda i, j: (0, j)),
        ],
        out_specs=pl.BlockSpec((tm, tn), lambda i, j: (i, j)),
        out_shape=jax.ShapeDtypeStruct((n, ncols), out_dtype),
        scratch_shapes=[pltpu.VMEM((tm, D_MODEL), BF16)],
        compiler_params=_cparams(("parallel", "arbitrary")),
        name="inproj",
    )(x2, g, w)


def _rms(x, g, n):
    ms = jnp.sum(x * x, axis=-1, keepdims=True) / n
    return x * lax.rsqrt(ms + EPS) * g


def _rope(x, c, s1, s2):
    return x * c + pltpu.roll(x, LANES - ROPE_D // 2, 1) * s1 + pltpu.roll(x, ROPE_D // 2, 1) * s2


def _mla_prep_kernel(cq_ref, ckv_ref, kr_ref, c_ref, s1_ref, s2_ref, qag_ref, wuq_ref, kvag_ref, wk_ref, wv_ref,
                     qg_ref, kng_ref, krg_ref, q_out, k_out, vt_out):
    c, s1, s2 = c_ref[...], s1_ref[...], s2_ref[...]
    cqn = _rms(cq_ref[...], qag_ref[...], Q_LORA).astype(BF16)
    ckvn = _rms(ckv_ref[...], kvag_ref[...], KV_LORA).astype(BF16)
    q_all = _dot(cqn, wuq_ref[...])
    k_all = _dot(ckvn, wk_ref[...])
    vt_out[0, 0] = _dot(ckvn, wv_ref[...]).T.astype(BF16)
    k_rope = _rope(_rms(kr_ref[...], krg_ref[...], ROPE_D), c, s1, s2)
    qg, kng = qg_ref[...], kng_ref[...]
    for h in range(H):
        sl = slice(h * LANES, (h + 1) * LANES)
        q_out[:, sl] = _rope(_rms(q_all[:, sl], qg, NOPE + ROPE_D), c, s1, s2).astype(BF16)
        k_out[:, sl] = (_rms(k_all[:, sl], kng, NOPE) + k_rope).astype(BF16)


def _mla_prep(pn, tabs, wts, seq, t):
    n = pn.shape[0]
    ns = seq // t
    row = lambda i: (i, 0)
    const = lambda i: (0, 0)
    tab_spec = pl.BlockSpec((t, LANES), lambda i: (i % ns, 0))
    return pl.pallas_call(
        _mla_prep_kernel,
        grid=(n // t,),
        in_specs=[
            pl.BlockSpec((t, Q_LORA), lambda i: (i, COL_CQ // Q_LORA)),
            pl.BlockSpec((t, LANES), lambda i: (i, COL_CKV // LANES)),
            pl.BlockSpec((t, LANES), lambda i: (i, COL_KR // LANES)),
            tab_spec, tab_spec, tab_spec,
            pl.BlockSpec((1, Q_LORA), const),
            pl.BlockSpec((Q_LORA, H * LANES), const),
            pl.BlockSpec((1, KV_LORA), const),
            pl.BlockSpec((KV_LORA, H * LANES), const),
            pl.BlockSpec((KV_LORA, W_BR), const),
            pl.BlockSpec((1, LANES), const),
            pl.BlockSpec((1, LANES), const),
            pl.BlockSpec((1, LANES), const),
        ],
        out_specs=[
            pl.BlockSpec((t, H * LANES), row),
            pl.BlockSpec((t, H * LANES), row),
            pl.BlockSpec((1, 1, W_BR, t), lambda i: (i // ns, i % ns, 0, 0)),
        ],
        out_shape=[
            jax.ShapeDtypeStruct((n, H * LANES), BF16),
            jax.ShapeDtypeStruct((n, H * LANES), BF16),
            jax.ShapeDtypeStruct((n // seq, ns, W_BR, t), BF16),
        ],
        compiler_params=_cparams(("parallel",)),
        name="mla_prep",
    )(pn, pn, pn, *tabs, *wts)


def _fox_prep_kernel(fq_ref, fk_ref, fv_ref, ff_ref, bf_ref, qg_ref, kg_ref, tri_ref, selq_ref, selk_ref,
                     oneq_ref, onek_ref, q_out, k_out, vt_out, carry_ref):
    t = fq_ref.shape[0]

    @pl.when(pl.program_id(1) == 0)
    def _():
        carry_ref[...] = jnp.zeros_like(carry_ref)

    z = ff_ref[...] + bf_ref[...]
    logf = jnp.minimum(z, 0.0) - jnp.log1p(jnp.exp(-jnp.abs(z)))
    cum = carry_ref[...] + _dot_exact_lhs(tri_ref[...], logf)
    carry_ref[...] = cum[t - 1:t, :]
    hi, mid, lo = _split3(cum * LOG2E)
    aug_q = _dot(hi, selq_ref[0]) + _dot(mid, selq_ref[1]) + _dot(lo, selq_ref[2]) + oneq_ref[...]
    aug_k = _dot(hi, selk_ref[0]) + _dot(mid, selk_ref[1]) + _dot(lo, selk_ref[2]) + onek_ref[...]
    vt_out[0, 0] = fv_ref[...].astype(F32).T.astype(BF16)
    lane = lax.broadcasted_iota(jnp.int32, (t, LANES), 1)
    first = lane < HD
    for src_ref, g_ref, aug, dst in ((fq_ref, qg_ref, aug_q, q_out), (fk_ref, kg_ref, aug_k, k_out)):
        for p in range(H // 2):
            sl = slice(p * LANES, (p + 1) * LANES)
            x = src_ref[:, sl].astype(F32)
            ms = _pair_seg_sum(x * x) / HD
            xn = x * lax.rsqrt(ms + EPS) * g_ref[:, sl]
            he, ho = 2 * p, 2 * p + 1
            dst[:, he * LANES:(he + 1) * LANES] = jnp.where(first, xn, aug[:, he * LANES:(he + 1) * LANES]).astype(BF16)
            dst[:, ho * LANES:(ho + 1) * LANES] = jnp.where(
                first, pltpu.roll(xn, HD, 1), aug[:, ho * LANES:(ho + 1) * LANES]).astype(BF16)


def _fox_prep(pw, pn, wts, batch, seq, t):
    n = pw.shape[0]
    ns = seq // t
    row = lambda b, i: (b * ns + i, 0)
    const2 = lambda b, i: (0, 0)
    const3 = lambda b, i: (0, 0, 0)
    return pl.pallas_call(
        _fox_prep_kernel,
        grid=(batch, ns),
        in_specs=[
            pl.BlockSpec((t, W_BR), lambda b, i: (b * ns + i, COL_FQ // W_BR)),
            pl.BlockSpec((t, W_BR), lambda b, i: (b * ns + i, COL_FK // W_BR)),
            pl.BlockSpec((t, W_BR), lambda b, i: (b * ns + i, COL_FV // W_BR)),
            pl.BlockSpec((t, LANES), lambda b, i: (b * ns + i, COL_FF // LANES)),
            pl.BlockSpec((1, LANES), const2),
            pl.BlockSpec((1, W_BR), const2),
            pl.BlockSpec((1, W_BR), const2),
            pl.BlockSpec((t, t), const2),
            pl.BlockSpec((3, LANES, H * LANES), const3),
            pl.BlockSpec((3, LANES, H * LANES), const3),
            pl.BlockSpec((1, H * LANES), const2),
            pl.BlockSpec((1, H * LANES), const2),
        ],
        out_specs=[
            pl.BlockSpec((t, H * LANES), row),
            pl.BlockSpec((t, H * LANES), row),
            pl.BlockSpec((1, 1, W_BR, t), lambda b, i: (b, i, 0, 0)),
        ],
        out_shape=[
            jax.ShapeDtypeStruct((n, H * LANES), BF16),
            jax.ShapeDtypeStruct((n, H * LANES), BF16),
            jax.ShapeDtypeStruct((batch, ns, W_BR, t), BF16),
        ],
        scratch_shapes=[pltpu.VMEM((1, LANES), F32)],
        compiler_params=_cparams(("parallel", "arbitrary")),
        name="fox_prep",
    )(pw, pw, pw, pn, *wts)


def _attn_kernel(q_ref, k_ref, vt_ref, g_ref, o_ref, m_ref, acc_ref, sa_ref, sb_ref, mxa_ref, mxb_ref, *, tk):
    tq = 2 * tk
    gi = pl.program_id(2)
    nt = (((1,), (1,)), ((), ()))
    kv_pos = lax.broadcasted_iota(jnp.int32, (tk, tq), 0)
    q_pos = lax.broadcasted_iota(jnp.int32, (tk, tq), 1)
    causal = kv_pos <= q_pos
    both, second = slice(0, tq), slice(tk, tq)
    ones = jnp.ones((ONES_ROWS, tk), BF16)
    m_ref[...] = jnp.full(m_ref.shape, NEG_INF, F32)
    acc_ref[...] = jnp.zeros(acc_ref.shape, F32)

    def scores(c, s_ref, mx_ref, cols):
        start = pl.multiple_of(c * tk, tk)
        for j in range(2):
            q = q_ref[cols, j * LANES:(j + 1) * LANES]
            k = k_ref[pl.ds(start, tk), j * LANES:(j + 1) * LANES]
            s = lax.dot_general(k, q, nt, preferred_element_type=F32)
            s_ref[j, :, cols] = s
            mx_ref[j, :, cols] = jnp.max(s, axis=0, keepdims=True)

    def consume(c, s_ref, mx_ref, cols, mask):
        for j in range(2):
            vt = jnp.concatenate([vt_ref[0, c, j * HD:(j + 1) * HD, :], ones], axis=0)
            s = s_ref[j, :, cols]
            if mask is not None:
                s = jnp.where(mask, s, NEG_INF)
                m_cur = jnp.max(s, axis=0, keepdims=True)
            else:
                m_cur = mx_ref[j, :, cols]
            m_old = m_ref[j, :, cols]
            m_new = jnp.maximum(m_old, m_cur)
            alpha = jnp.exp2(m_old - m_new)
            p = jnp.exp2(s - m_new)
            acc_ref[j, :, cols] = alpha * acc_ref[j, :, cols] + _dot(vt, p.astype(BF16))
            m_ref[j, :, cols] = m_new

    scores(0, sa_ref, mxa_ref, both)

    def body(i, carry):
        c = 2 * i
        scores(c + 1, sb_ref, mxb_ref, both)
        consume(c, sa_ref, mxa_ref, both, None)
        scores(c + 2, sa_ref, mxa_ref, both)
        consume(c + 1, sb_ref, mxb_ref, both, None)
        return carry

    lax.fori_loop(0, gi, body, 0)
    scores(2 * gi + 1, sb_ref, mxb_ref, second)
    consume(2 * gi, sa_ref, mxa_ref, both, causal)
    consume(2 * gi + 1, sb_ref, mxb_ref, second, causal[:, :tk])

    o_t = jnp.concatenate([acc_ref[j, :HD, :] / acc_ref[j, HD:HD + 1, :] for j in range(2)], axis=0)
    g = g_ref[...].astype(F32)
    o_ref[...] = (o_t.T * (g * _sigmoid(g))).astype(BF16)


def _attention(q, k, vt, pw, gate_col, batch, seq, tk):
    n = q.shape[0]
    tq = 2 * tk
    nq, nk = seq // tq, seq // tk
    return pl.pallas_call(
        functools.partial(_attn_kernel, tk=tk),
        grid=(batch, H // 2, nq),
        in_specs=[
            pl.BlockSpec((tq, 2 * LANES), lambda b, hp, i: (b * nq + i, hp)),
            pl.BlockSpec((seq, 2 * LANES), lambda b, hp, i: (b, hp)),
            pl.BlockSpec((1, nk, LANES, tk), lambda b, hp, i: (b, 0, hp, 0)),
            pl.BlockSpec((tq, LANES), lambda b, hp, i: (b * nq + i, gate_col // LANES + hp)),
        ],
        out_specs=pl.BlockSpec((tq, LANES), lambda b, hp, i: (b * nq + i, hp)),
        out_shape=jax.ShapeDtypeStruct((n, W_BR), BF16),
        scratch_shapes=[
            pltpu.VMEM((2, 1, tq), F32),
            pltpu.VMEM((2, HD + ONES_ROWS, tq), F32),
            pltpu.VMEM((2, tk, tq), F32),
            pltpu.VMEM((2, tk, tq), F32),
            pltpu.VMEM((2, 1, tq), F32),
            pltpu.VMEM((2, 1, tq), F32),
        ],
        compiler_params=_cparams(("parallel", "parallel", "arbitrary")),
        name="attention",
    )(q, k, vt, pw)


def _rwkv_prep_kernel(*refs, first_layer):
    if first_layer:
        (sr_ref, sk_ref, sv_ref, swa_ref, gc_ref, mur_ref, muk_ref, muv_ref, muwa_ref, w0_ref, a0_ref, wup_ref,
         aup_ref, kk_ref, ka_ref, rk_ref, bt_ref, bo_ref,
         at_o, rt_o, bt_o, kt_o, v_o, bh_o, kh_o, wc_o, bonus_o, sg_o, vfirst_o,
         cr_ref, ck_ref, cv_ref, cwa_ref) = refs
    else:
        (sr_ref, sk_ref, sv_ref, swa_ref, gc_ref, vf_ref, mur_ref, muk_ref, muv_ref, muwa_ref, w0_ref, a0_ref,
         wup_ref, aup_ref, kk_ref, ka_ref, rk_ref, v0_ref, vdown_ref, vup_ref, bt_ref, bo_ref,
         at_o, rt_o, bt_o, kt_o, v_o, bh_o, kh_o, wc_o, bonus_o, sg_o,
         cr_ref, ck_ref, cv_ref, cwa_ref) = refs
    t = sr_ref.shape[0]

    @pl.when(pl.program_id(1) == 0)
    def _():
        for c_ref in (cr_ref, ck_ref, cv_ref, cwa_ref):
            c_ref[...] = jnp.zeros_like(c_ref)

    def shift(x_ref, c_ref, mu_ref):
        x = x_ref[...].astype(F32)
        rowid = lax.broadcasted_iota(jnp.int32, x.shape, 0)
        prev = jnp.where(rowid == 0, c_ref[...], pltpu.roll(x, 1, 0))
        c_ref[...] = x[t - 1:t, :]
        return x + mu_ref[...] * (prev - x)

    r = shift(sr_ref, cr_ref, mur_ref)
    k = shift(sk_ref, ck_ref, muk_ref)
    v = shift(sv_ref, cv_ref, muv_ref)
    wa = shift(swa_ref, cwa_ref, muwa_ref)

    logw = -RWKV_DECAY_SCALE * _sigmoid(w0_ref[...] + _dot(jnp.tanh(wa).astype(BF16), wup_ref[...]))
    a = _sigmoid(a0_ref[...] + _dot(wa.astype(BF16), aup_ref[...]))
    if first_layer:
        vfirst_o[...] = v
    else:
        low = _dot(v.astype(BF16), vdown_ref[...]).astype(BF16)
        nu = _sigmoid(v0_ref[...] + _dot(low, vup_ref[...]))
        v = v + (vf_ref[...] - v) * nu

    kk = k * kk_ref[...]
    k_mod = k * (1.0 + (a - 1.0) * ka_ref[...])
    rk = r * k_mod * rk_ref[...]
    kk_n, bonus = [], []
    for p in range(H // 2):
        sl = slice(p * LANES, (p + 1) * LANES)
        nrm = jnp.sqrt(_pair_seg_sum(kk[:, sl] * kk[:, sl]))
        kk_n.append(kk[:, sl] / jnp.maximum(nrm, 1e-12))
        bonus.append(_pair_seg_sum(rk[:, sl]) * v[:, sl])
    kk = jnp.concatenate(kk_n, axis=-1)
    bonus = jnp.concatenate(bonus, axis=-1)
    a_vec = -kk
    b_vec = kk * a

    lcum = _dot_exact_lhs(bt_ref[...], logw)
    ltot = _dot_exact_lhs(bo_ref[...], logw)
    e_minus = jnp.exp(-lcum)
    e_rem = jnp.exp(ltot - lcum)
    g = gc_ref[...].astype(F32)
    for o_ref, val in ((at_o, a_vec * jnp.exp(lcum - logw)), (rt_o, r * jnp.exp(lcum)), (bt_o, b_vec * e_minus),
                       (kt_o, k_mod * e_minus), (v_o, v), (bh_o, b_vec * e_rem), (kh_o, k_mod * e_rem),
                       (wc_o, jnp.exp(ltot)), (bonus_o, bonus), (sg_o, g * _sigmoid(g))):
        o_ref[...] = val.astype(o_ref.dtype)


def _rwkv_prep(pw, pn, vfirst, wts, batch, seq, t, first_layer):
    n = pw.shape[0]
    ns = seq // t
    const2 = lambda b, i: (0, 0)
    pcol = lambda col, w: pl.BlockSpec((t, w), lambda b, i: (b * ns + i, col // w))
    vec = pl.BlockSpec((1, W_BR), const2)
    in_specs = [pcol(COL_SR, W_BR), pcol(COL_SK, W_BR), pcol(COL_SV, W_BR), pcol(COL_WA, LANES),
                pcol(COL_GATE_C, W_BR)]
    args = [pw, pw, pw, pn, pw]
    if not first_layer:
        in_specs.append(pl.BlockSpec((t, W_BR), lambda b, i: (b * ns + i, 0)))
        args.append(vfirst)
    in_specs += [vec, vec, vec, pl.BlockSpec((1, LANES), const2), vec, vec,
                 pl.BlockSpec((LANES, W_BR), const2), pl.BlockSpec((LANES, W_BR), const2), vec, vec, vec]
    if not first_layer:
        in_specs += [vec, pl.BlockSpec((W_BR, LANES), const2), pl.BlockSpec((LANES, W_BR), const2)]
    in_specs += [pl.BlockSpec((t, t), const2), pl.BlockSpec((t, t), const2)]
    args += list(wts)
    row_spec = pl.BlockSpec((t, W_BR), lambda b, i: (b * ns + i, 0))
    out_dtypes = (BF16, F32, BF16, BF16, BF16, BF16, BF16, F32, F32, F32) + ((F32,) if first_layer else ())
    out_specs = [row_spec] * len(out_dtypes)
    out_shape = [jax.ShapeDtypeStruct((n, W_BR), dt) for dt in out_dtypes]
    return pl.pallas_call(
        functools.partial(_rwkv_prep_kernel, first_layer=first_layer),
        grid=(batch, ns),
        in_specs=in_specs,
        out_specs=out_specs,
        out_shape=out_shape,
        scratch_shapes=[pltpu.VMEM((1, W_BR), F32), pltpu.VMEM((1, W_BR), F32), pltpu.VMEM((1, W_BR), F32),
                        pltpu.VMEM((1, LANES), F32)],
        compiler_params=_cparams(("parallel", "arbitrary")),
        name="rwkv_prep",
    )(*args)


def _mm(a, b, dims):
    return lax.dot_general(a.astype(BF16), b.astype(BF16), dims, preferred_element_type=F32)


_NN = (((1,), (0,)), ((), ()))
_NT = (((1,), (1,)), ((), ()))
_TN = (((0,), (0,)), ((), ()))


def _rwkv_chunk_kernel(at_ref, rt_ref, bt_ref, kt_ref, v_ref, bh_ref, kh_ref, wc_ref, bonus_ref, sg_ref,
                       lg_ref, lb_ref, o_ref, s_all_ref, *, nchunk):
    @pl.when(pl.program_id(1) == 0)
    def _():
        s_all_ref[...] = jnp.zeros_like(s_all_ref)

    mm = _mm
    row = lax.broadcasted_iota(jnp.int32, (CHUNK, CHUNK), 0)
    col = lax.broadcasted_iota(jnp.int32, (CHUNK, CHUNK), 1)
    strict = col < row
    incl = col <= row
    eye = col == row
    first = lax.broadcasted_iota(jnp.int32, (CHUNK, LANES), 1) < HD
    r2 = lax.broadcasted_iota(jnp.int32, (LANES, LANES), 0)
    c2 = lax.broadcasted_iota(jnp.int32, (LANES, LANES), 1)
    same_head = (r2 < HD) == (c2 < HD)
    eye2 = r2 == c2
    each = lambda f, *cols: [f(*xs) for xs in zip(*cols)]
    pairs = [(c, p) for c in range(nchunk) for p in range(H // 2)]
    ld = lambda ref: [ref[c * CHUNK:(c + 1) * CHUNK, p * LANES:(p + 1) * LANES] for c, p in pairs]
    at, rt, bt, kt, v, bh, kh = ld(at_ref), ld(rt_ref), ld(bt_ref), ld(kt_ref), ld(v_ref), ld(bh_ref), ld(kh_ref)
    x1 = each(lambda a, r: jnp.concatenate([a, r.astype(BF16)], axis=0), at, rt)
    heads = [(i, j) for i in range(len(pairs)) for j in range(2)]
    hmask = lambda j: first if j == 0 else jnp.logical_not(first)
    own = lambda x, j: jnp.where(hmask(j), x, jnp.zeros_like(x))
    gb = [mm(x1[i], own(bt[i], j), _NT) for i, j in heads]
    gk = [mm(x1[i], own(kt[i], j), _NT) for i, j in heads]
    a_ab = [jnp.where(strict, g[:CHUNK], 0.0) for g in gb]
    a_rb = [jnp.where(incl, g[CHUNK:], 0.0) for g in gb]
    a_ak = [jnp.where(strict, g[:CHUNK], 0.0) for g in gk]
    a_rk = [jnp.where(incl, g[CHUNK:], 0.0) for g in gk]
    tinv = [jnp.where(eye, 1.0, a) for a in a_ab]
    pw = a_ab
    for _ in range(int(math.log2(CHUNK)) - 1):
        pw = each(lambda p: mm(p, p, _NN), pw)
        tinv = each(lambda t, p: t + mm(t, p, _NN), tinv, pw)
    av = [mm(a_ak[n], v[i], _NN) for n, (i, _) in enumerate(heads)]
    tx = [mm(tinv[n], jnp.concatenate([av[n].astype(BF16), at[i]], axis=1), _NN)
          for n, (i, _) in enumerate(heads)]
    ry = each(lambda a, x: mm(a, x, _NN), a_rb, tx)
    yk = [mm(a_rk[n], v[i], _NN) for n, (i, _) in enumerate(heads)]
    npair = range(len(pairs))
    pick = lambda xs: [jnp.where(first, xs[2 * i], xs[2 * i + 1]) for i in npair]
    u = pick([t[:, :LANES] for t in tx])
    a_til = pick([t[:, LANES:] for t in tx])
    y_in = pick([r[:, :LANES] + k for r, k in zip(ry, yk)])
    r_hat = each(lambda r, x: r + x, rt, pick([r[:, LANES:] for r in ry]))
    m_new = [jnp.where(same_head, mm(a_til[i], bh[i], _TN), 0.0) for i in npair]
    n_new = [jnp.where(same_head, mm(jnp.concatenate([u[i].astype(BF16), v[i]], axis=0),
                                     jnp.concatenate([bh[i], kh[i]], axis=0), _TN), 0.0) for i in npair]
    lg, lb = lg_ref[...], lb_ref[...]
    for c in range(nchunk):
        idx = [i for i, (ci, _) in enumerate(pairs) if ci == c]
        s0 = [s_all_ref[p] for p in range(H // 2)]
        y = [y_in[i] + mm(r_hat[i], s0[p], _NT) for p, i in enumerate(idx)]
        for p, i in enumerate(idx):
            wc = wc_ref[c * CHUNK:c * CHUNK + 1, p * LANES:(p + 1) * LANES]
            s_all_ref[p] = mm(s0[p], jnp.where(eye2, wc, 0.0) + m_new[i], _NN) + n_new[i]
        for p, i in enumerate(idx):
            rows, lanes = slice(c * CHUNK, (c + 1) * CHUNK), slice(p * LANES, (p + 1) * LANES)
            mu = _pair_seg_sum(y[p]) / HD
            d = y[p] - mu
            var = _pair_seg_sum(d * d) / HD
            yn = d * lax.rsqrt(var + GN_EPS) * lg[:, lanes] + lb[:, lanes]
            o_ref[rows, lanes] = ((yn + bonus_ref[rows, lanes]) * sg_ref[rows, lanes]).astype(o_ref.dtype)


def _rwkv_chunk(arrs, lnx_g, lnx_b, batch, seq, tc):
    n = batch * seq
    ns = seq // tc
    spec = pl.BlockSpec((tc, W_BR), lambda b, i: (b * ns + i, 0))
    gspec = pl.BlockSpec((1, W_BR), lambda b, i: (0, 0))
    return pl.pallas_call(
        functools.partial(_rwkv_chunk_kernel, nchunk=tc // CHUNK),
        grid=(batch, ns),
        in_specs=[spec] * 10 + [gspec, gspec],
        out_specs=spec,
        out_shape=jax.ShapeDtypeStruct((n, W_BR), BF16),
        scratch_shapes=[pltpu.VMEM((H // 2, LANES, LANES), F32)],
        compiler_params=_cparams(("parallel", "arbitrary")),
        name="rwkv_chunk",
    )(*arrs, lnx_g, lnx_b)


def _merge_kernel(oa_ref, ob_ref, oc_ref, ga_ref, gb_ref, gc_ref, x_ref, wpa_ref, wpb_ref, wpc_ref, wout_ref,
                  o_ref):
    pa = _dot(oa_ref[...], wpa_ref[...])
    pb = _dot(ob_ref[...], wpb_ref[...])
    pc = _dot(oc_ref[...], wpc_ref[...])
    sig = lambda ref: _sigmoid(ref[...].astype(F32))
    merged = sig(ga_ref) * pa + sig(gb_ref) * pb + sig(gc_ref) * pc
    o_ref[...] = x_ref[...] + _dot(merged.astype(BF16), wout_ref[...])


def _merge(oa, ob, oc, p, x2, wpa, wpb, wpc, wout, batch, seq, t):
    n = x2.shape[0]
    ns = seq // t
    row = lambda b, i: (b * ns + i, 0)
    const2 = lambda b, i: (0, 0)
    gate = lambda g: pl.BlockSpec((t, D_MODEL), lambda b, i: (b * ns + i, COL_MERGE // D_MODEL + g))
    return pl.pallas_call(
        _merge_kernel,
        grid=(batch, ns),
        in_specs=[
            pl.BlockSpec((t, W_BR), row),
            pl.BlockSpec((t, W_BR), row),
            pl.BlockSpec((t, W_BR), row),
            gate(0), gate(1), gate(2),
            pl.BlockSpec((t, D_MODEL), row),
            pl.BlockSpec((W_BR, D_MODEL), const2),
            pl.BlockSpec((W_BR, D_MODEL), const2),
            pl.BlockSpec((W_BR, D_MODEL), const2),
            pl.BlockSpec((D_MODEL, D_MODEL), const2),
        ],
        out_specs=pl.BlockSpec((t, D_MODEL), row),
        out_shape=jax.ShapeDtypeStruct((n, D_MODEL), F32),
        compiler_params=_cparams(("parallel", "parallel")),
        name="merge",
    )(oa, ob, oc, p, p, p, x2, wpa, wpb, wpc, wout)


def _regroup_w_in(w):
    d = w.shape[0]
    sizes = (Q_LORA, KV_LORA, ROPE_D, W_BR, W_BR, W_BR, W_BR, H, W_BR,
             3 * W_BR + DECAY_LORA + AAA_LORA, W_BR, 3 * D_MODEL)
    offs = np.concatenate([[0], np.cumsum(sizes)])
    (c_q, c_kv, k_r, gate_a, fq, fk, fv, ff, gate_b, shift, gate_c, merge) = [
        w[:, offs[i]:offs[i + 1]] for i in range(len(sizes))]
    z = lambda n: jnp.zeros((d, n), w.dtype)
    wide = jnp.concatenate([
        merge, gate_a, fq, fk, fv, gate_b,
        shift[:, :W_BR], shift[:, W_BR:2 * W_BR], shift[:, 2 * W_BR:3 * W_BR], gate_c,
    ], axis=1)
    narrow = jnp.concatenate([
        c_q, c_kv,
        z(NOPE), k_r, z(LANES - NOPE - ROPE_D),
        ff, z(LANES - H),
        shift[:, 3 * W_BR:],
    ], axis=1)
    return wide.astype(BF16), narrow.astype(BF16)


def _rope_tables(seq):
    half = ROPE_D // 2
    inv = ROPE_THETA ** (-jnp.arange(0, ROPE_D, 2, dtype=F32) / ROPE_D)
    ang = jnp.arange(seq, dtype=F32)[:, None] * inv[None, :]
    cos, sin = jnp.cos(ang), jnp.sin(ang)
    z = lambda n: jnp.zeros((seq, n), F32)
    c = jnp.concatenate([jnp.ones((seq, NOPE), F32), cos, cos, z(LANES - NOPE - ROPE_D)], axis=1)
    s1 = jnp.concatenate([z(NOPE), -sin, z(LANES - NOPE - half)], axis=1)
    s2 = jnp.concatenate([z(NOPE + half), sin, z(LANES - NOPE - ROPE_D)], axis=1)
    return c, s1, s2


def _pad_lanes(v, lo, total=LANES):
    return jnp.zeros((1, total), F32).at[0, lo:lo + v.shape[0]].set(v)


def _fox_selectors():
    selq = np.zeros((3, LANES, H * LANES), np.float32)
    selk = np.zeros((3, LANES, H * LANES), np.float32)
    oneq = np.zeros((1, H * LANES), np.float32)
    onek = np.zeros((1, H * LANES), np.float32)
    for h in range(H):
        for j in range(3):
            selq[j, h, h * LANES + HD + j] = 1.0
            selk[j, h, h * LANES + HD + 3 + j] = -1.0
            oneq[0, h * LANES + HD + 3 + j] = 1.0
            onek[0, h * LANES + HD + j] = 1.0
    return (jnp.asarray(selq, BF16), jnp.asarray(selk, BF16), jnp.asarray(oneq), jnp.asarray(onek))


def _tri(t):
    r = np.arange(t)
    return jnp.asarray((r[None, :] <= r[:, None]).astype(np.float32), BF16)


def _chunk_tri(t):
    r = np.arange(t)
    same = (r[None, :] // CHUNK) == (r[:, None] // CHUNK)
    lower = r[None, :] <= r[:, None]
    return (jnp.asarray((same & lower).astype(np.float32), BF16), jnp.asarray(same.astype(np.float32), BF16))


def _tile(seq, pref):
    return pref if seq % pref == 0 else seq


RWKV_TC = 128


def kernel(x, norm_g, w_in, mla_qa_g, mla_w_uq, mla_kva_g, mla_w_ukv, mla_q_g, mla_knope_g, mla_krope_g, fox_b_f, fox_q_g, fox_k_g, rwkv_mu, rwkv_w0, rwkv_w_up, rwkv_a0, rwkv_a_up, rwkv_k_k, rwkv_k_a, rwkv_r_k, rwkv_lnx_g, rwkv_lnx_b, rwkv_v0, rwkv_v_down, rwkv_v_up, w_pa, w_pb, w_pc, w_out):
    batch, seq, d = x.shape
    depth = w_in.shape[0]
    n = batch * seq
    t_row = _tile(seq, 512)
    t_in = _tile(n, 1024)
    t_rwkv = _tile(seq, 256)
    x2 = x.reshape(n, d)

    tabs = _rope_tables(seq)
    selq, selk, oneq, onek = _fox_selectors()
    tri = _tri(t_row)
    btri, bones = _chunk_tri(t_rwkv)
    row = lambda v: v.reshape(1, -1).astype(F32)

    vfirst = None
    for l in range(depth):
        w_wide, w_narrow = _regroup_w_in(w_in[l])
        pw = _inproj(x2, row(norm_g[l]), w_wide, t_in, INPROJ_TN, BF16)
        pn = _inproj(x2, row(norm_g[l]), w_narrow, t_in, NCOLS_NARROW, F32)

        wuq = mla_w_uq[l].reshape(Q_LORA, H, NOPE + ROPE_D)
        wuq = jnp.pad(wuq, ((0, 0), (0, 0), (0, LANES - NOPE - ROPE_D))).reshape(Q_LORA, H * LANES).astype(BF16)
        wukv = mla_w_ukv[l].reshape(KV_LORA, H, NOPE + HD)
        wk = jnp.pad(wukv[:, :, :NOPE], ((0, 0), (0, 0), (0, LANES - NOPE))).reshape(KV_LORA, H * LANES).astype(BF16)
        wv = wukv[:, :, NOPE:].reshape(KV_LORA, W_BR).astype(BF16)
        mla_scale = float(NOPE + ROPE_D) ** -0.5 * LOG2E
        mla_wts = (row(mla_qa_g[l]), wuq, row(mla_kva_g[l]), wk, wv,
                   _pad_lanes(mla_q_g[l] * mla_scale, 0), _pad_lanes(mla_knope_g[l], 0),
                   _pad_lanes(mla_krope_g[l], NOPE))
        qa, ka, va = _mla_prep(pn, tabs, mla_wts, seq, t_row)
        o_a = _attention(qa, ka, va, pw, COL_GATE_A, batch, seq, t_row)

        fox_wts = (_pad_lanes(fox_b_f[l], 0), row(jnp.tile(fox_q_g[l] * (float(HD) ** -0.5 * LOG2E), H)),
                   row(jnp.tile(fox_k_g[l], H)), tri, selq, selk, oneq, onek)
        qb, kb, vb = _fox_prep(pw, pn, fox_wts, batch, seq, t_row)
        o_b = _attention(qb, kb, vb, pw, COL_GATE_B, batch, seq, t_row)

        mu = rwkv_mu[l]
        wup = jnp.zeros((LANES, W_BR), F32).at[:DECAY_LORA].set(rwkv_w_up[l]).astype(BF16)
        aup = jnp.zeros((LANES, W_BR), F32).at[DECAY_LORA:].set(rwkv_a_up[l]).astype(BF16)
        rw = [row(mu[:W_BR]), row(mu[W_BR:2 * W_BR]), row(mu[2 * W_BR:3 * W_BR]), row(mu[3 * W_BR:]),
              row(rwkv_w0[l]), row(rwkv_a0[l]), wup, aup, row(rwkv_k_k[l]), row(rwkv_k_a[l]),
              row(rwkv_r_k[l])]
        if l > 0:
            vdown = jnp.zeros((W_BR, LANES), F32).at[:, :MV_LORA].set(rwkv_v_down[l - 1]).astype(BF16)
            vup = jnp.zeros((LANES, W_BR), F32).at[:MV_LORA].set(rwkv_v_up[l - 1]).astype(BF16)
            rw += [row(rwkv_v0[l - 1]), vdown, vup]
        rw += [btri, bones]
        outs = _rwkv_prep(pw, pn, vfirst, rw, batch, seq, t_rwkv, first_layer=(l == 0))
        if l == 0:
            vfirst = outs[10]
        o_c = _rwkv_chunk(outs[:10], row(rwkv_lnx_g[l]), row(rwkv_lnx_b[l]), batch, seq, _tile(seq, RWKV_TC))

        x2 = _merge(o_a, o_b, o_c, pw, x2, w_pa[l].astype(BF16), w_pb[l].astype(BF16),
                    w_pc[l].astype(BF16), w_out[l].astype(BF16), batch, seq, t_row)
    return x2.reshape(batch, seq, d)
```

```python
import functools
import math

import jax
import jax.numpy as jnp
import numpy as np
from jax import lax
from jax.experimental import pallas as pl
from jax.experimental.pallas import tpu as pltpu

F32 = jnp.float32
BF16 = jnp.bfloat16

LANES = 128
H = 8
HD = 64
NOPE = 64
ROPE_D = 32
Q_LORA = 256
KV_LORA = 128
D_MODEL = 1024
W_BR = H * HD
DECAY_LORA = 64
AAA_LORA = 64
MV_LORA = 32
ROPE_THETA = 10000.0
RWKV_DECAY_SCALE = 0.606531
GN_EPS = 64e-5
EPS = 1e-6
NEG_INF = -1e30
LOG2E = math.log2(math.e)
CHUNK = 64
ONES_ROWS = 16

COL_MERGE = 0
COL_GATE_A = 3072
COL_FQ = 3584
COL_FK = 4096
COL_FV = 4608
COL_GATE_B = 5120
COL_SR = 5632
COL_SK = 6144
COL_SV = 6656
COL_GATE_C = 7168
NCOLS_WIDE = 7680
INPROJ_TN = 1280
COL_CQ = 0
COL_CKV = 256
COL_KR = 384
COL_FF = 512
COL_WA = 640
NCOLS_NARROW = 768

VMEM_LIMIT = 56 * 1024 * 1024


def _cparams(sem):
    return pltpu.CompilerParams(dimension_semantics=sem, vmem_limit_bytes=VMEM_LIMIT)


def _sigmoid(x):
    return 1.0 / (1.0 + jnp.exp(-x))


def _dot(a, b):
    return jnp.dot(a, b, preferred_element_type=F32)


def _split3(x):
    hi = x.astype(BF16)
    r1 = x - hi.astype(F32)
    mid = r1.astype(BF16)
    lo = (r1 - mid.astype(F32)).astype(BF16)
    return hi, mid, lo


def _dot_exact_lhs(m_bf16, x):
    hi, mid, lo = _split3(x)
    return _dot(m_bf16, hi) + _dot(m_bf16, mid) + _dot(m_bf16, lo)


def _pair_seg_sum(x):
    lane = lax.broadcasted_iota(jnp.int32, x.shape, 1)
    first = lane < HD
    s0 = jnp.sum(jnp.where(first, x, 0.0), axis=-1, keepdims=True)
    s1 = jnp.sum(jnp.where(first, 0.0, x), axis=-1, keepdims=True)
    return jnp.where(first, s0, s1)


def _inproj_kernel(x_ref, g_ref, w_ref, o_ref, h_ref):
    @pl.when(pl.program_id(1) == 0)
    def _():
        x = x_ref[...]
        ms = jnp.mean(x * x, axis=-1, keepdims=True)
        h_ref[...] = (x * lax.rsqrt(ms + EPS) * g_ref[...]).astype(BF16)

    o_ref[...] = _dot(h_ref[...], w_ref[...]).astype(o_ref.dtype)


def _inproj(x2, g, w, tm, tn, out_dtype):
    n, ncols = x2.shape[0], w.shape[1]
    return pl.pallas_call(
        _inproj_kernel,
        grid=(n // tm, ncols // tn),
        in_specs=[
            pl.BlockSpec((tm, D_MODEL), lambda i, j: (i, 0)),
            pl.BlockSpec((1, D_MODEL), lambda i, j: (0, 0)),
            pl.BlockSpec((D_MODEL, tn), lambda i, j: (0, j)),
        ],
        out_specs=pl.BlockSpec((tm, tn), lambda i, j: (i, j)),
        out_shape=jax.ShapeDtypeStruct((n, ncols), out_dtype),
        scratch_shapes=[pltpu.VMEM((tm, D_MODEL), BF16)],
        compiler_params=_cparams(("parallel", "arbitrary")),
        name="inproj",
    )(x2, g, w)


def _rms(x, g, n):
    ms = jnp.sum(x * x, axis=-1, keepdims=True) / n
    return x * lax.rsqrt(ms + EPS) * g


def _rope(x, c, s1, s2):
    return x * c + pltpu.roll(x, LANES - ROPE_D // 2, 1) * s1 + pltpu.roll(x, ROPE_D // 2, 1) * s2


def _mla_prep_kernel(cq_ref, ckv_ref, kr_ref, c_ref, s1_ref, s2_ref, qag_ref, wuq_ref, kvag_ref, wk_ref, wv_ref,
                     qg_ref, kng_ref, krg_ref, q_out, k_out, vt_out):
    c, s1, s2 = c_ref[...], s1_ref[...], s2_ref[...]
    cqn = _rms(cq_ref[...], qag_ref[...], Q_LORA).astype(BF16)
    ckvn = _rms(ckv_ref[...], kvag_ref[...], KV_LORA).astype(BF16)
    q_all = _dot(cqn, wuq_ref[...])
    k_all = _dot(ckvn, wk_ref[...])
    vt_out[0, 0] = _dot(ckvn, wv_ref[...]).T.astype(BF16)
    k_rope = _rope(_rms(kr_ref[...], krg_ref[...], ROPE_D), c, s1, s2)
    qg, kng = qg_ref[...], kng_ref[...]
    for h in range(H):
        sl = slice(h * LANES, (h + 1) * LANES)
        q_out[:, sl] = _rope(_rms(q_all[:, sl], qg, NOPE + ROPE_D), c, s1, s2).astype(BF16)
        k_out[:, sl] = (_rms(k_all[:, sl], kng, NOPE) + k_rope).astype(BF16)


def _mla_prep(pn, tabs, wts, seq, t):
    n = pn.shape[0]
    ns = seq // t
    row = lambda i: (i, 0)
    const = lambda i: (0, 0)
    tab_spec = pl.BlockSpec((t, LANES), lambda i: (i % ns, 0))
    return pl.pallas_call(
        _mla_prep_kernel,
        grid=(n // t,),
        in_specs=[
            pl.BlockSpec((t, Q_LORA), lambda i: (i, COL_CQ // Q_LORA)),
            pl.BlockSpec((t, LANES), lambda i: (i, COL_CKV // LANES)),
            pl.BlockSpec((t, LANES), lambda i: (i, COL_KR // LANES)),
            tab_spec, tab_spec, tab_spec,
            pl.BlockSpec((1, Q_LORA), const),
            pl.BlockSpec((Q_LORA, H * LANES), const),
            pl.BlockSpec((1, KV_LORA), const),
            pl.BlockSpec((KV_LORA, H * LANES), const),
            pl.BlockSpec((KV_LORA, W_BR), const),
            pl.BlockSpec((1, LANES), const),
            pl.BlockSpec((1, LANES), const),
            pl.BlockSpec((1, LANES), const),
        ],
        out_specs=[
            pl.BlockSpec((t, H * LANES), row),
            pl.BlockSpec((t, H * LANES), row),
            pl.BlockSpec((1, 1, W_BR, t), lambda i: (i // ns, i % ns, 0, 0)),
        ],
        out_shape=[
            jax.ShapeDtypeStruct((n, H * LANES), BF16),
            jax.ShapeDtypeStruct((n, H * LANES), BF16),
            jax.ShapeDtypeStruct((n // seq, ns, W_BR, t), BF16),
        ],
        compiler_params=_cparams(("parallel",)),
        name="mla_prep",
    )(pn, pn, pn, *tabs, *wts)


def _fox_prep_kernel(fq_ref, fk_ref, fv_ref, ff_ref, bf_ref, qg_ref, kg_ref, tri_ref, selq_ref, selk_ref,
                     oneq_ref, onek_ref, q_out, k_out, vt_out, carry_ref):
    t = fq_ref.shape[0]

    @pl.when(pl.program_id(1) == 0)
    def _():
        carry_ref[...] = jnp.zeros_like(carry_ref)

    z = ff_ref[...] + bf_ref[...]
    logf = jnp.minimum(z, 0.0) - jnp.log1p(jnp.exp(-jnp.abs(z)))
    cum = carry_ref[...] + _dot_exact_lhs(tri_ref[...], logf)
    carry_ref[...] = cum[t - 1:t, :]
    hi, mid, lo = _split3(cum * LOG2E)
    aug_q = _dot(hi, selq_ref[0]) + _dot(mid, selq_ref[1]) + _dot(lo, selq_ref[2]) + oneq_ref[...]
    aug_k = _dot(hi, selk_ref[0]) + _dot(mid, selk_ref[1]) + _dot(lo, selk_ref[2]) + onek_ref[...]
    vt_out[0, 0] = fv_ref[...].astype(F32).T.astype(BF16)
    lane = lax.broadcasted_iota(jnp.int32, (t, LANES), 1)
    first = lane < HD
    for src_ref, g_ref, aug, dst in ((fq_ref, qg_ref, aug_q, q_out), (fk_ref, kg_ref, aug_k, k_out)):
        for p in range(H // 2):
            sl = slice(p * LANES, (p + 1) * LANES)
            x = src_ref[:, sl].astype(F32)
            ms = _pair_seg_sum(x * x) / HD
            xn = x * lax.rsqrt(ms + EPS) * g_ref[:, sl]
            he, ho = 2 * p, 2 * p + 1
            dst[:, he * LANES:(he + 1) * LANES] = jnp.where(first, xn, aug[:, he * LANES:(he + 1) * LANES]).astype(BF16)
            dst[:, ho * LANES:(ho + 1) * LANES] = jnp.where(
                first, pltpu.roll(xn, HD, 1), aug[:, ho * LANES:(ho + 1) * LANES]).astype(BF16)


def _fox_prep(pw, pn, wts, batch, seq, t):
    n = pw.shape[0]
    ns = seq // t
    row = lambda b, i: (b * ns + i, 0)
    const2 = lambda b, i: (0, 0)
    const3 = lambda b, i: (0, 0, 0)
    return pl.pallas_call(
        _fox_prep_kernel,
        grid=(batch, ns),
        in_specs=[
            pl.BlockSpec((t, W_BR), lambda b, i: (b * ns + i, COL_FQ // W_BR)),
            pl.BlockSpec((t, W_BR), lambda b, i: (b * ns + i, COL_FK // W_BR)),
            pl.BlockSpec((t, W_BR), lambda b, i: (b * ns + i, COL_FV // W_BR)),
            pl.BlockSpec((t, LANES), lambda b, i: (b * ns + i, COL_FF // LANES)),
            pl.BlockSpec((1, LANES), const2),
            pl.BlockSpec((1, W_BR), const2),
            pl.BlockSpec((1, W_BR), const2),
            pl.BlockSpec((t, t), const2),
            pl.BlockSpec((3, LANES, H * LANES), const3),
            pl.BlockSpec((3, LANES, H * LANES), const3),
            pl.BlockSpec((1, H * LANES), const2),
            pl.BlockSpec((1, H * LANES), const2),
        ],
        out_specs=[
            pl.BlockSpec((t, H * LANES), row),
            pl.BlockSpec((t, H * LANES), row),
            pl.BlockSpec((1, 1, W_BR, t), lambda b, i: (b, i, 0, 0)),
        ],
        out_shape=[
            jax.ShapeDtypeStruct((n, H * LANES), BF16),
            jax.ShapeDtypeStruct((n, H * LANES), BF16),
            jax.ShapeDtypeStruct((batch, ns, W_BR, t), BF16),
        ],
        scratch_shapes=[pltpu.VMEM((1, LANES), F32)],
        compiler_params=_cparams(("parallel", "arbitrary")),
        name="fox_prep",
    )(pw, pw, pw, pn, *wts)


def _attn_kernel(q_ref, k_ref, vt_ref, g_ref, o_ref, m_ref, acc_ref, sa_ref, sb_ref, mxa_ref, mxb_ref, *, tk):
    tq = 2 * tk
    gi = pl.program_id(2)
    nt = (((1,), (1,)), ((), ()))
    kv_pos = lax.broadcasted_iota(jnp.int32, (tk, tq), 0)
    q_pos = lax.broadcasted_iota(jnp.int32, (tk, tq), 1)
    causal = kv_pos <= q_pos
    both, second = slice(0, tq), slice(tk, tq)
    ones = jnp.ones((ONES_ROWS, tk), BF16)
    m_ref[...] = jnp.full(m_ref.shape, NEG_INF, F32)
    acc_ref[...] = jnp.zeros(acc_ref.shape, F32)

    def scores(c, s_ref, mx_ref, cols):
        start = pl.multiple_of(c * tk, tk)
        for j in range(2):
            q = q_ref[cols, j * LANES:(j + 1) * LANES]
            k = k_ref[pl.ds(start, tk), j * LANES:(j + 1) * LANES]
            s = lax.dot_general(k, q, nt, preferred_element_type=F32)
            s_ref[j, :, cols] = s
            mx_ref[j, :, cols] = jnp.max(s, axis=0, keepdims=True)

    def consume(c, s_ref, mx_ref, cols, mask):
        for j in range(2):
            vt = jnp.concatenate([vt_ref[0, c, j * HD:(j + 1) * HD, :], ones], axis=0)
            s = s_ref[j, :, cols]
            if mask is not None:
                s = jnp.where(mask, s, NEG_INF)
                m_cur = jnp.max(s, axis=0, keepdims=True)
            else:
                m_cur = mx_ref[j, :, cols]
            m_old = m_ref[j, :, cols]
            m_new = jnp.maximum(m_old, m_cur)
            alpha = jnp.exp2(m_old - m_new)
            p = jnp.exp2(s - m_new)
            acc_ref[j, :, cols] = alpha * acc_ref[j, :, cols] + _dot(vt, p.astype(BF16))
            m_ref[j, :, cols] = m_new

    scores(0, sa_ref, mxa_ref, both)

    def body(i, carry):
        c = 2 * i
        scores(c + 1, sb_ref, mxb_ref, both)
        consume(c, sa_ref, mxa_ref, both, None)
        scores(c + 2, sa_ref, mxa_ref, both)
        consume(c + 1, sb_ref, mxb_ref, both, None)
        return carry

    lax.fori_loop(0, gi, body, 0)
    scores(2 * gi + 1, sb_ref, mxb_ref, second)
    consume(2 * gi, sa_ref, mxa_ref, both, causal)
    consume(2 * gi + 1, sb_ref, mxb_ref, second, causal[:, :tk])

    o_t = jnp.concatenate([acc_ref[j, :HD, :] / acc_ref[j, HD:HD + 1, :] for j in range(2)], axis=0)
    g = g_ref[...].astype(F32)
    o_ref[...] = (o_t.T * (g * _sigmoid(g))).astype(BF16)


def _attention(q, k, vt, pw, gate_col, batch, seq, tk):
    n = q.shape[0]
    tq = 2 * tk
    nq, nk = seq // tq, seq // tk
    return pl.pallas_call(
        functools.partial(_attn_kernel, tk=tk),
        grid=(batch, H // 2, nq),
        in_specs=[
            pl.BlockSpec((tq, 2 * LANES), lambda b, hp, i: (b * nq + i, hp)),
            pl.BlockSpec((seq, 2 * LANES), lambda b, hp, i: (b, hp)),
            pl.BlockSpec((1, nk, LANES, tk), lambda b, hp, i: (b, 0, hp, 0)),
            pl.BlockSpec((tq, LANES), lambda b, hp, i: (b * nq + i, gate_col // LANES + hp)),
        ],
        out_specs=pl.BlockSpec((tq, LANES), lambda b, hp, i: (b * nq + i, hp)),
        out_shape=jax.ShapeDtypeStruct((n, W_BR), BF16),
        scratch_shapes=[
            pltpu.VMEM((2, 1, tq), F32),
            pltpu.VMEM((2, HD + ONES_ROWS, tq), F32),
            pltpu.VMEM((2, tk, tq), F32),
            pltpu.VMEM((2, tk, tq), F32),
            pltpu.VMEM((2, 1, tq), F32),
            pltpu.VMEM((2, 1, tq), F32),
        ],
        compiler_params=_cparams(("parallel", "parallel", "arbitrary")),
        name="attention",
    )(q, k, vt, pw)


def _rwkv_prep_kernel(*refs, first_layer):
    if first_layer:
        (sr_ref, sk_ref, sv_ref, swa_ref, gc_ref, mur_ref, muk_ref, muv_ref, muwa_ref, w0_ref, a0_ref, wup_ref,
         aup_ref, kk_ref, ka_ref, rk_ref, bt_ref, bo_ref,
         at_o, rt_o, bt_o, kt_o, v_o, bh_o, kh_o, wc_o, bonus_o, sg_o, vfirst_o,
         cr_ref, ck_ref, cv_ref, cwa_ref) = refs
    else:
        (sr_ref, sk_ref, sv_ref, swa_ref, gc_ref, vf_ref, mur_ref, muk_ref, muv_ref, muwa_ref, w0_ref, a0_ref,
         wup_ref, aup_ref, kk_ref, ka_ref, rk_ref, v0_ref, vdown_ref, vup_ref, bt_ref, bo_ref,
         at_o, rt_o, bt_o, kt_o, v_o, bh_o, kh_o, wc_o, bonus_o, sg_o,
         cr_ref, ck_ref, cv_ref, cwa_ref) = refs
    t = sr_ref.shape[0]

    @pl.when(pl.program_id(1) == 0)
    def _():
        for c_ref in (cr_ref, ck_ref, cv_ref, cwa_ref):
            c_ref[...] = jnp.zeros_like(c_ref)

    def shift(x_ref, c_ref, mu_ref):
        x = x_ref[...].astype(F32)
        rowid = lax.broadcasted_iota(jnp.int32, x.shape, 0)
        prev = jnp.where(rowid == 0, c_ref[...], pltpu.roll(x, 1, 0))
        c_ref[...] = x[t - 1:t, :]
        return x + mu_ref[...] * (prev - x)

    r = shift(sr_ref, cr_ref, mur_ref)
    k = shift(sk_ref, ck_ref, muk_ref)
    v = shift(sv_ref, cv_ref, muv_ref)
    wa = shift(swa_ref, cwa_ref, muwa_ref)

    logw = -RWKV_DECAY_SCALE * _sigmoid(w0_ref[...] + _dot(jnp.tanh(wa).astype(BF16), wup_ref[...]))
    a = _sigmoid(a0_ref[...] + _dot(wa.astype(BF16), aup_ref[...]))
    if first_layer:
        vfirst_o[...] = v
    else:
        low = _dot(v.astype(BF16), vdown_ref[...]).astype(BF16)
        nu = _sigmoid(v0_ref[...] + _dot(low, vup_ref[...]))
        v = v + (vf_ref[...] - v) * nu

    kk = k * kk_ref[...]
    k_mod = k * (1.0 + (a - 1.0) * ka_ref[...])
    rk = r * k_mod * rk_ref[...]
    kk_n, bonus = [], []
    for p in range(H // 2):
        sl = slice(p * LANES, (p + 1) * LANES)
        nrm = jnp.sqrt(_pair_seg_sum(kk[:, sl] * kk[:, sl]))
        kk_n.append(kk[:, sl] / jnp.maximum(nrm, 1e-12))
        bonus.append(_pair_seg_sum(rk[:, sl]) * v[:, sl])
    kk = jnp.concatenate(kk_n, axis=-1)
    bonus = jnp.concatenate(bonus, axis=-1)
    a_vec = -kk
    b_vec = kk * a

    lcum = _dot_exact_lhs(bt_ref[...], logw)
    ltot = _dot_exact_lhs(bo_ref[...], logw)
    e_minus = jnp.exp(-lcum)
    e_rem = jnp.exp(ltot - lcum)
    g = gc_ref[...].astype(F32)
    for o_ref, val in ((at_o, a_vec * jnp.exp(lcum - logw)), (rt_o, r * jnp.exp(lcum)), (bt_o, b_vec * e_minus),
                       (kt_o, k_mod * e_minus), (v_o, v), (bh_o, b_vec * e_rem), (kh_o, k_mod * e_rem),
                       (wc_o, jnp.exp(ltot)), (bonus_o, bonus), (sg_o, g * _sigmoid(g))):
        o_ref[...] = val.astype(o_ref.dtype)


def _rwkv_prep(pw, pn, vfirst, wts, batch, seq, t, first_layer):
    n = pw.shape[0]
    ns = seq // t
    const2 = lambda b, i: (0, 0)
    pcol = lambda col, w: pl.BlockSpec((t, w), lambda b, i: (b * ns + i, col // w))
    vec = pl.BlockSpec((1, W_BR), const2)
    in_specs = [pcol(COL_SR, W_BR), pcol(COL_SK, W_BR), pcol(COL_SV, W_BR), pcol(COL_WA, LANES),
                pcol(COL_GATE_C, W_BR)]
    args = [pw, pw, pw, pn, pw]
    if not first_layer:
        in_specs.append(pl.BlockSpec((t, W_BR), lambda b, i: (b * ns + i, 0)))
        args.append(vfirst)
    in_specs += [vec, vec, vec, pl.BlockSpec((1, LANES), const2), vec, vec,
                 pl.BlockSpec((LANES, W_BR), const2), pl.BlockSpec((LANES, W_BR), const2), vec, vec, vec]
    if not first_layer:
        in_specs += [vec, pl.BlockSpec((W_BR, LANES), const2), pl.BlockSpec((LANES, W_BR), const2)]
    in_specs += [pl.BlockSpec((t, t), const2), pl.BlockSpec((t, t), const2)]
    args += list(wts)
    row_spec = pl.BlockSpec((t, W_BR), lambda b, i: (b * ns + i, 0))
    out_dtypes = (BF16, F32, BF16, BF16, BF16, BF16, BF16, F32, F32, F32) + ((F32,) if first_layer else ())
    out_specs = [row_spec] * len(out_dtypes)
    out_shape = [jax.ShapeDtypeStruct((n, W_BR), dt) for dt in out_dtypes]
    return pl.pallas_call(
        functools.partial(_rwkv_prep_kernel, first_layer=first_layer),
        grid=(batch, ns),
        in_specs=in_specs,
        out_specs=out_specs,
        out_shape=out_shape,
        scratch_shapes=[pltpu.VMEM((1, W_BR), F32), pltpu.VMEM((1, W_BR), F32), pltpu.VMEM((1, W_BR), F32),
                        pltpu.VMEM((1, LANES), F32)],
        compiler_params=_cparams(("parallel", "arbitrary")),
        name="rwkv_prep",
    )(*args)


def _mm(a, b, dims):
    return lax.dot_general(a.astype(BF16), b.astype(BF16), dims, preferred_element_type=F32)


_NN = (((1,), (0,)), ((), ()))
_NT = (((1,), (1,)), ((), ()))
_TN = (((0,), (0,)), ((), ()))


def _rwkv_chunk_kernel(at_ref, rt_ref, bt_ref, kt_ref, v_ref, bh_ref, kh_ref, wc_ref, bonus_ref, sg_ref,
                       lg_ref, lb_ref, o_ref, s_all_ref, *, nchunk):
    @pl.when(pl.program_id(1) == 0)
    def _():
        s_all_ref[...] = jnp.zeros_like(s_all_ref)

    mm = _mm
    first = lax.broadcasted_iota(jnp.int32, (CHUNK, LANES), 1) < HD
    r2 = lax.broadcasted_iota(jnp.int32, (LANES, LANES), 0)
    c2 = lax.broadcasted_iota(jnp.int32, (LANES, LANES), 1)
    same_head = (r2 < HD) == (c2 < HD)
    eye2 = r2 == c2
    t_row, t_col = r2 % CHUNK, c2 % CHUNK
    strict2 = jnp.logical_and(same_head, t_col < t_row)
    incl2 = jnp.logical_and(same_head, t_col <= t_row)
    each = lambda f, *cols: [f(*xs) for xs in zip(*cols)]
    pairs = [(c, p) for c in range(nchunk) for p in range(H // 2)]
    npair = range(len(pairs))
    ld = lambda ref: [ref[c * CHUNK:(c + 1) * CHUNK, p * LANES:(p + 1) * LANES] for c, p in pairs]
    at, rt, bt, kt, v, bh, kh = ld(at_ref), ld(rt_ref), ld(bt_ref), ld(kt_ref), ld(v_ref), ld(bh_ref), ld(kh_ref)
    zero = jnp.zeros((CHUNK, LANES), BF16)
    stack = lambda x: jnp.concatenate([jnp.where(first, x, zero), jnp.where(first, zero, x)], axis=0)
    fold = lambda x: x[:CHUNK] + x[CHUNK:]
    a_s, b_s, k_s, v_s = each(stack, at), each(stack, bt), each(stack, kt), each(stack, v)
    x1 = each(lambda a, r: jnp.concatenate([a, stack(r.astype(BF16))], axis=0), a_s, rt)
    gb = each(lambda x, b: mm(x, b, _NT), x1, b_s)
    gk = each(lambda x, k: mm(x, k, _NT), x1, k_s)
    a_ab = [jnp.where(strict2, g[:LANES], 0.0) for g in gb]
    a_rb = [jnp.where(incl2, g[LANES:], 0.0) for g in gb]
    a_ak = [jnp.where(strict2, g[:LANES], 0.0) for g in gk]
    a_rk = [jnp.where(incl2, g[LANES:], 0.0) for g in gk]
    tinv = [jnp.where(eye2, 1.0, a) for a in a_ab]
    pw = a_ab
    for _ in range(int(math.log2(CHUNK)) - 1):
        pw = each(lambda p: mm(p, p, _NN), pw)
        tinv = each(lambda t, p: t + mm(t, p, _NN), tinv, pw)
    av = each(lambda a, x: mm(a, x, _NN), a_ak, v_s)
    tx = each(lambda t, x, a: mm(t, jnp.concatenate([x.astype(BF16), a], axis=1), _NN), tinv, av, a_s)
    ry = each(lambda a, x: mm(a, x, _NN), a_rb, tx)
    yk = each(lambda a, x: mm(a, x, _NN), a_rk, v_s)
    u = [fold(t[:, :LANES]) for t in tx]
    a_til = [fold(t[:, LANES:]) for t in tx]
    y_in = [fold(r[:, :LANES] + k) for r, k in zip(ry, yk)]
    r_hat = [r + fold(x[:, LANES:]) for r, x in zip(rt, ry)]
    m_new = [jnp.where(same_head, mm(a_til[i], bh[i], _TN), 0.0) for i in npair]
    n_new = [jnp.where(same_head, mm(jnp.concatenate([u[i].astype(BF16), v[i]], axis=0),
                                     jnp.concatenate([bh[i], kh[i]], axis=0), _TN), 0.0) for i in npair]
    lg, lb = lg_ref[...], lb_ref[...]
    for c in range(nchunk):
        idx = [i for i, (ci, _) in enumerate(pairs) if ci == c]
        s0 = [s_all_ref[p] for p in range(H // 2)]
        y = [y_in[i] + mm(r_hat[i], s0[p], _NT) for p, i in enumerate(idx)]
        for p, i in enumerate(idx):
            wc = wc_ref[c * CHUNK:c * CHUNK + 1, p * LANES:(p + 1) * LANES]
            s_all_ref[p] = mm(s0[p], jnp.where(eye2, wc, 0.0) + m_new[i], _NN) + n_new[i]
        for p, i in enumerate(idx):
            rows, lanes = slice(c * CHUNK, (c + 1) * CHUNK), slice(p * LANES, (p + 1) * LANES)
            mu = _pair_seg_sum(y[p]) / HD
            d = y[p] - mu
            var = _pair_seg_sum(d * d) / HD
            yn = d * lax.rsqrt(var + GN_EPS) * lg[:, lanes] + lb[:, lanes]
            o_ref[rows, lanes] = ((yn + bonus_ref[rows, lanes]) * sg_ref[rows, lanes]).astype(o_ref.dtype)


def _rwkv_chunk(arrs, lnx_g, lnx_b, batch, seq, tc):
    n = batch * seq
    ns = seq // tc
    spec = pl.BlockSpec((tc, W_BR), lambda b, i: (b * ns + i, 0))
    gspec = pl.BlockSpec((1, W_BR), lambda b, i: (0, 0))
    return pl.pallas_call(
        functools.partial(_rwkv_chunk_kernel, nchunk=tc // CHUNK),
        grid=(batch, ns),
        in_specs=[spec] * 10 + [gspec, gspec],
        out_specs=spec,
        out_shape=jax.ShapeDtypeStruct((n, W_BR), BF16),
        scratch_shapes=[pltpu.VMEM((H // 2, LANES, LANES), F32)],
        compiler_params=_cparams(("parallel", "arbitrary")),
        name="rwkv_chunk",
    )(*arrs, lnx_g, lnx_b)


def _merge_kernel(oa_ref, ob_ref, oc_ref, ga_ref, gb_ref, gc_ref, x_ref, wpa_ref, wpb_ref, wpc_ref, wout_ref,
                  o_ref):
    pa = _dot(oa_ref[...], wpa_ref[...])
    pb = _dot(ob_ref[...], wpb_ref[...])
    pc = _dot(oc_ref[...], wpc_ref[...])
    sig = lambda ref: _sigmoid(ref[...].astype(F32))
    merged = sig(ga_ref) * pa + sig(gb_ref) * pb + sig(gc_ref) * pc
    o_ref[...] = x_ref[...] + _dot(merged.astype(BF16), wout_ref[...])


def _merge(oa, ob, oc, p, x2, wpa, wpb, wpc, wout, batch, seq, t):
    n = x2.shape[0]
    ns = seq // t
    row = lambda b, i: (b * ns + i, 0)
    const2 = lambda b, i: (0, 0)
    gate = lambda g: pl.BlockSpec((t, D_MODEL), lambda b, i: (b * ns + i, COL_MERGE // D_MODEL + g))
    return pl.pallas_call(
        _merge_kernel,
        grid=(batch, ns),
        in_specs=[
            pl.BlockSpec((t, W_BR), row),
            pl.BlockSpec((t, W_BR), row),
            pl.BlockSpec((t, W_BR), row),
            gate(0), gate(1), gate(2),
            pl.BlockSpec((t, D_MODEL), row),
            pl.BlockSpec((W_BR, D_MODEL), const2),
            pl.BlockSpec((W_BR, D_MODEL), const2),
            pl.BlockSpec((W_BR, D_MODEL), const2),
            pl.BlockSpec((D_MODEL, D_MODEL), const2),
        ],
        out_specs=pl.BlockSpec((t, D_MODEL), row),
        out_shape=jax.ShapeDtypeStruct((n, D_MODEL), F32),
        compiler_params=_cparams(("parallel", "parallel")),
        name="merge",
    )(oa, ob, oc, p, p, p, x2, wpa, wpb, wpc, wout)


def _regroup_w_in(w):
    d = w.shape[0]
    sizes = (Q_LORA, KV_LORA, ROPE_D, W_BR, W_BR, W_BR, W_BR, H, W_BR,
             3 * W_BR + DECAY_LORA + AAA_LORA, W_BR, 3 * D_MODEL)
    offs = np.concatenate([[0], np.cumsum(sizes)])
    (c_q, c_kv, k_r, gate_a, fq, fk, fv, ff, gate_b, shift, gate_c, merge) = [
        w[:, offs[i]:offs[i + 1]] for i in range(len(sizes))]
    z = lambda n: jnp.zeros((d, n), w.dtype)
    wide = jnp.concatenate([
        merge, gate_a, fq, fk, fv, gate_b,
        shift[:, :W_BR], shift[:, W_BR:2 * W_BR], shift[:, 2 * W_BR:3 * W_BR], gate_c,
    ], axis=1)
    narrow = jnp.concatenate([
        c_q, c_kv,
        z(NOPE), k_r, z(LANES - NOPE - ROPE_D),
        ff, z(LANES - H),
        shift[:, 3 * W_BR:],
    ], axis=1)
    return wide.astype(BF16), narrow.astype(BF16)


def _rope_tables(seq):
    half = ROPE_D // 2
    inv = ROPE_THETA ** (-jnp.arange(0, ROPE_D, 2, dtype=F32) / ROPE_D)
    ang = jnp.arange(seq, dtype=F32)[:, None] * inv[None, :]
    cos, sin = jnp.cos(ang), jnp.sin(ang)
    z = lambda n: jnp.zeros((seq, n), F32)
    c = jnp.concatenate([jnp.ones((seq, NOPE), F32), cos, cos, z(LANES - NOPE - ROPE_D)], axis=1)
    s1 = jnp.concatenate([z(NOPE), -sin, z(LANES - NOPE - half)], axis=1)
    s2 = jnp.concatenate([z(NOPE + half), sin, z(LANES - NOPE - ROPE_D)], axis=1)
    return c, s1, s2


def _pad_lanes(v, lo, total=LANES):
    return jnp.zeros((1, total), F32).at[0, lo:lo + v.shape[0]].set(v)


def _fox_selectors():
    selq = np.zeros((3, LANES, H * LANES), np.float32)
    selk = np.zeros((3, LANES, H * LANES), np.float32)
    oneq = np.zeros((1, H * LANES), np.float32)
    onek = np.zeros((1, H * LANES), np.float32)
    for h in range(H):
        for j in range(3):
            selq[j, h, h * LANES + HD + j] = 1.0
            selk[j, h, h * LANES + HD + 3 + j] = -1.0
            oneq[0, h * LANES + HD + 3 + j] = 1.0
            onek[0, h * LANES + HD + j] = 1.0
    return (jnp.asarray(selq, BF16), jnp.asarray(selk, BF16), jnp.asarray(oneq), jnp.asarray(onek))


def _tri(t):
    r = np.arange(t)
    return jnp.asarray((r[None, :] <= r[:, None]).astype(np.float32), BF16)


def _chunk_tri(t):
    r = np.arange(t)
    same = (r[None, :] // CHUNK) == (r[:, None] // CHUNK)
    lower = r[None, :] <= r[:, None]
    return (jnp.asarray((same & lower).astype(np.float32), BF16), jnp.asarray(same.astype(np.float32), BF16))


def _tile(seq, pref):
    return pref if seq % pref == 0 else seq


RWKV_TC = 256


def kernel(x, norm_g, w_in, mla_qa_g, mla_w_uq, mla_kva_g, mla_w_ukv, mla_q_g, mla_knope_g, mla_krope_g, fox_b_f, fox_q_g, fox_k_g, rwkv_mu, rwkv_w0, rwkv_w_up, rwkv_a0, rwkv_a_up, rwkv_k_k, rwkv_k_a, rwkv_r_k, rwkv_lnx_g, rwkv_lnx_b, rwkv_v0, rwkv_v_down, rwkv_v_up, w_pa, w_pb, w_pc, w_out):
    batch, seq, d = x.shape
    depth = w_in.shape[0]
    n = batch * seq
    t_row = _tile(seq, 512)
    t_in = _tile(n, 1024)
    t_rwkv = _tile(seq, 256)
    x2 = x.reshape(n, d)

    tabs = _rope_tables(seq)
    selq, selk, oneq, onek = _fox_selectors()
    tri = _tri(t_row)
    btri, bones = _chunk_tri(t_rwkv)
    row = lambda v: v.reshape(1, -1).astype(F32)

    vfirst = None
    for l in range(depth):
        w_wide, w_narrow = _regroup_w_in(w_in[l])
        pw = _inproj(x2, row(norm_g[l]), w_wide, t_in, INPROJ_TN, BF16)
        pn = _inproj(x2, row(norm_g[l]), w_narrow, t_in, NCOLS_NARROW, F32)

        wuq = mla_w_uq[l].reshape(Q_LORA, H, NOPE + ROPE_D)
        wuq = jnp.pad(wuq, ((0, 0), (0, 0), (0, LANES - NOPE - ROPE_D))).reshape(Q_LORA, H * LANES).astype(BF16)
        wukv = mla_w_ukv[l].reshape(KV_LORA, H, NOPE + HD)
        wk = jnp.pad(wukv[:, :, :NOPE], ((0, 0), (0, 0), (0, LANES - NOPE))).reshape(KV_LORA, H * LANES).astype(BF16)
        wv = wukv[:, :, NOPE:].reshape(KV_LORA, W_BR).astype(BF16)
        mla_scale = float(NOPE + ROPE_D) ** -0.5 * LOG2E
        mla_wts = (row(mla_qa_g[l]), wuq, row(mla_kva_g[l]), wk, wv,
                   _pad_lanes(mla_q_g[l] * mla_scale, 0), _pad_lanes(mla_knope_g[l], 0),
                   _pad_lanes(mla_krope_g[l], NOPE))
        qa, ka, va = _mla_prep(pn, tabs, mla_wts, seq, t_row)
        o_a = _attention(qa, ka, va, pw, COL_GATE_A, batch, seq, t_row)

        fox_wts = (_pad_lanes(fox_b_f[l], 0), row(jnp.tile(fox_q_g[l] * (float(HD) ** -0.5 * LOG2E), H)),
                   row(jnp.tile(fox_k_g[l], H)), tri, selq, selk, oneq, onek)
        qb, kb, vb = _fox_prep(pw, pn, fox_wts, batch, seq, t_row)
        o_b = _attention(qb, kb, vb, pw, COL_GATE_B, batch, seq, t_row)

        mu = rwkv_mu[l]
        wup = jnp.zeros((LANES, W_BR), F32).at[:DECAY_LORA].set(rwkv_w_up[l]).astype(BF16)
        aup = jnp.zeros((LANES, W_BR), F32).at[DECAY_LORA:].set(rwkv_a_up[l]).astype(BF16)
        rw = [row(mu[:W_BR]), row(mu[W_BR:2 * W_BR]), row(mu[2 * W_BR:3 * W_BR]), row(mu[3 * W_BR:]),
              row(rwkv_w0[l]), row(rwkv_a0[l]), wup, aup, row(rwkv_k_k[l]), row(rwkv_k_a[l]),
              row(rwkv_r_k[l])]
        if l > 0:
            vdown = jnp.zeros((W_BR, LANES), F32).at[:, :MV_LORA].set(rwkv_v_down[l - 1]).astype(BF16)
            vup = jnp.zeros((LANES, W_BR), F32).at[:MV_LORA].set(rwkv_v_up[l - 1]).astype(BF16)
            rw += [row(rwkv_v0[l - 1]), vdown, vup]
        rw += [btri, bones]
        outs = _rwkv_prep(pw, pn, vfirst, rw, batch, seq, t_rwkv, first_layer=(l == 0))
        if l == 0:
            vfirst = outs[10]
        o_c = _rwkv_chunk(outs[:10], row(rwkv_lnx_g[l]), row(rwkv_lnx_b[l]), batch, seq, _tile(seq, RWKV_TC))

        x2 = _merge(o_a, o_b, o_c, pw, x2, w_pa[l].astype(BF16), w_pb[l].astype(BF16),
                    w_pc[l].astype(BF16), w_out[l].astype(BF16), batch, seq, t_row)
    return x2.reshape(batch, seq, d)
```

```python
import functools
import math

import jax
import jax.numpy as jnp
import numpy as np
from jax import lax
from jax.experimental import pallas as pl
from jax.experimental.pallas import tpu as pltpu

F32 = jnp.float32
BF16 = jnp.bfloat16

LANES = 128
H = 8
HD = 64
NOPE = 64
ROPE_D = 32
Q_LORA = 256
KV_LORA = 128
D_MODEL = 1024
W_BR = H * HD
DECAY_LORA = 64
AAA_LORA = 64
MV_LORA = 32
ROPE_THETA = 10000.0
RWKV_DECAY_SCALE = 0.606531
GN_EPS = 64e-5
EPS = 1e-6
NEG_INF = -1e30
LOG2E = math.log2(math.e)
CHUNK = 64
ATTN_QT = 4
ONES_ROWS = 16

COL_MERGE = 0
COL_GATE_A = 3072
COL_FQ = 3584
COL_FK = 4096
COL_FV = 4608
COL_GATE_B = 5120
COL_SR = 5632
COL_SK = 6144
COL_SV = 6656
COL_GATE_C = 7168
NCOLS_WIDE = 7680
INPROJ_TN = 1280
COL_CQ = 0
COL_CKV = 256
COL_KR = 384
COL_FF = 512
COL_WA = 640
NCOLS_NARROW = 768

VMEM_LIMIT = 56 * 1024 * 1024


def _cparams(sem):
    return pltpu.CompilerParams(dimension_semantics=sem, vmem_limit_bytes=VMEM_LIMIT)


def _sigmoid(x):
    return 1.0 / (1.0 + jnp.exp(-x))


def _dot(a, b):
    return jnp.dot(a, b, preferred_element_type=F32)


def _split3(x):
    hi = x.astype(BF16)
    r1 = x - hi.astype(F32)
    mid = r1.astype(BF16)
    lo = (r1 - mid.astype(F32)).astype(BF16)
    return hi, mid, lo


def _dot_exact_lhs(m_bf16, x):
    hi, mid, lo = _split3(x)
    return _dot(m_bf16, hi) + _dot(m_bf16, mid) + _dot(m_bf16, lo)


def _pair_seg_sum(x):
    lane = lax.broadcasted_iota(jnp.int32, x.shape, 1)
    first = lane < HD
    s0 = jnp.sum(jnp.where(first, x, 0.0), axis=-1, keepdims=True)
    s1 = jnp.sum(jnp.where(first, 0.0, x), axis=-1, keepdims=True)
    return jnp.where(first, s0, s1)


def _inproj_kernel(x_ref, g_ref, w_ref, o_ref, h_ref):
    @pl.when(pl.program_id(1) == 0)
    def _():
        x = x_ref[...]
        ms = jnp.mean(x * x, axis=-1, keepdims=True)
        h_ref[...] = (x * lax.rsqrt(ms + EPS) * g_ref[...]).astype(BF16)

    o_ref[...] = _dot(h_ref[...], w_ref[...]).astype(o_ref.dtype)


def _inproj(x2, g, w, tm, tn, out_dtype):
    n, ncols = x2.shape[0], w.shape[1]
    return pl.pallas_call(
        _inproj_kernel,
        grid=(n // tm, ncols // tn),
        in_specs=[
            pl.BlockSpec((tm, D_MODEL), lambda i, j: (i, 0)),
            pl.BlockSpec((1, D_MODEL), lambda i, j: (0, 0)),
            pl.BlockSpec((D_MODEL, tn), lambda i, j: (0, j)),
        ],
        out_specs=pl.BlockSpec((tm, tn), lambda i, j: (i, j)),
        out_shape=jax.ShapeDtypeStruct((n, ncols), out_dtype),
        scratch_shapes=[pltpu.VMEM((tm, D_MODEL), BF16)],
        compiler_params=_cparams(("parallel", "arbitrary")),
        name="inproj",
    )(x2, g, w)


def _rms(x, g, n):
    ms = jnp.sum(x * x, axis=-1, keepdims=True) / n
    return x * lax.rsqrt(ms + EPS) * g


def _rope(x, c, s1, s2):
    return x * c + pltpu.roll(x, LANES - ROPE_D // 2, 1) * s1 + pltpu.roll(x, ROPE_D // 2, 1) * s2


def _mla_prep_kernel(cq_ref, ckv_ref, kr_ref, c_ref, s1_ref, s2_ref, qag_ref, wuq_ref, kvag_ref, wk_ref, wv_ref,
                     qg_ref, kng_ref, krg_ref, q_out, k_out, vt_out):
    c, s1, s2 = c_ref[...], s1_ref[...], s2_ref[...]
    cqn = _rms(cq_ref[...], qag_ref[...], Q_LORA).astype(BF16)
    ckvn = _rms(ckv_ref[...], kvag_ref[...], KV_LORA).astype(BF16)
    q_all = _dot(cqn, wuq_ref[...])
    k_all = _dot(ckvn, wk_ref[...])
    vt_out[0, 0] = _dot(ckvn, wv_ref[...]).T.astype(BF16)
    k_rope = _rope(_rms(kr_ref[...], krg_ref[...], ROPE_D), c, s1, s2)
    qg, kng = qg_ref[...], kng_ref[...]
    for h in range(H):
        sl = slice(h * LANES, (h + 1) * LANES)
        q_out[:, sl] = _rope(_rms(q_all[:, sl], qg, NOPE + ROPE_D), c, s1, s2).astype(BF16)
        k_out[:, sl] = (_rms(k_all[:, sl], kng, NOPE) + k_rope).astype(BF16)


def _mla_prep(pn, tabs, wts, seq, t):
    n = pn.shape[0]
    ns = seq // t
    row = lambda i: (i, 0)
    const = lambda i: (0, 0)
    tab_spec = pl.BlockSpec((t, LANES), lambda i: (i % ns, 0))
    return pl.pallas_call(
        _mla_prep_kernel,
        grid=(n // t,),
        in_specs=[
            pl.BlockSpec((t, Q_LORA), lambda i: (i, COL_CQ // Q_LORA)),
            pl.BlockSpec((t, LANES), lambda i: (i, COL_CKV // LANES)),
            pl.BlockSpec((t, LANES), lambda i: (i, COL_KR // LANES)),
            tab_spec, tab_spec, tab_spec,
            pl.BlockSpec((1, Q_LORA), const),
            pl.BlockSpec((Q_LORA, H * LANES), const),
            pl.BlockSpec((1, KV_LORA), const),
            pl.BlockSpec((KV_LORA, H * LANES), const),
            pl.BlockSpec((KV_LORA, W_BR), const),
            pl.BlockSpec((1, LANES), const),
            pl.BlockSpec((1, LANES), const),
            pl.BlockSpec((1, LANES), const),
        ],
        out_specs=[
            pl.BlockSpec((t, H * LANES), row),
            pl.BlockSpec((t, H * LANES), row),
            pl.BlockSpec((1, 1, W_BR, t), lambda i: (i // ns, i % ns, 0, 0)),
        ],
        out_shape=[
            jax.ShapeDtypeStruct((n, H * LANES), BF16),
            jax.ShapeDtypeStruct((n, H * LANES), BF16),
            jax.ShapeDtypeStruct((n // seq, ns, W_BR, t), BF16),
        ],
        compiler_params=_cparams(("parallel",)),
        name="mla_prep",
    )(pn, pn, pn, *tabs, *wts)


def _fox_prep_kernel(fq_ref, fk_ref, fv_ref, ff_ref, bf_ref, qg_ref, kg_ref, tri_ref, selq_ref, selk_ref,
                     oneq_ref, onek_ref, q_out, k_out, vt_out, carry_ref):
    t = fq_ref.shape[0]

    @pl.when(pl.program_id(1) == 0)
    def _():
        carry_ref[...] = jnp.zeros_like(carry_ref)

    z = ff_ref[...] + bf_ref[...]
    logf = jnp.minimum(z, 0.0) - jnp.log1p(jnp.exp(-jnp.abs(z)))
    cum = carry_ref[...] + _dot_exact_lhs(tri_ref[...], logf)
    carry_ref[...] = cum[t - 1:t, :]
    hi, mid, lo = _split3(cum * LOG2E)
    aug_q = _dot(hi, selq_ref[0]) + _dot(mid, selq_ref[1]) + _dot(lo, selq_ref[2]) + oneq_ref[...]
    aug_k = _dot(hi, selk_ref[0]) + _dot(mid, selk_ref[1]) + _dot(lo, selk_ref[2]) + onek_ref[...]
    vt_out[0, 0] = fv_ref[...].astype(F32).T.astype(BF16)
    lane = lax.broadcasted_iota(jnp.int32, (t, LANES), 1)
    first = lane < HD
    for src_ref, g_ref, aug, dst in ((fq_ref, qg_ref, aug_q, q_out), (fk_ref, kg_ref, aug_k, k_out)):
        for p in range(H // 2):
            sl = slice(p * LANES, (p + 1) * LANES)
            x = src_ref[:, sl].astype(F32)
            ms = _pair_seg_sum(x * x) / HD
            xn = x * lax.rsqrt(ms + EPS) * g_ref[:, sl]
            he, ho = 2 * p, 2 * p + 1
            dst[:, he * LANES:(he + 1) * LANES] = jnp.where(first, xn, aug[:, he * LANES:(he + 1) * LANES]).astype(BF16)
            dst[:, ho * LANES:(ho + 1) * LANES] = jnp.where(
                first, pltpu.roll(xn, HD, 1), aug[:, ho * LANES:(ho + 1) * LANES]).astype(BF16)


def _fox_prep(pw, pn, wts, batch, seq, t):
    n = pw.shape[0]
    ns = seq // t
    row = lambda b, i: (b * ns + i, 0)
    const2 = lambda b, i: (0, 0)
    const3 = lambda b, i: (0, 0, 0)
    return pl.pallas_call(
        _fox_prep_kernel,
        grid=(batch, ns),
        in_specs=[
            pl.BlockSpec((t, W_BR), lambda b, i: (b * ns + i, COL_FQ // W_BR)),
            pl.BlockSpec((t, W_BR), lambda b, i: (b * ns + i, COL_FK // W_BR)),
            pl.BlockSpec((t, W_BR), lambda b, i: (b * ns + i, COL_FV // W_BR)),
            pl.BlockSpec((t, LANES), lambda b, i: (b * ns + i, COL_FF // LANES)),
            pl.BlockSpec((1, LANES), const2),
            pl.BlockSpec((1, W_BR), const2),
            pl.BlockSpec((1, W_BR), const2),
            pl.BlockSpec((t, t), const2),
            pl.BlockSpec((3, LANES, H * LANES), const3),
            pl.BlockSpec((3, LANES, H * LANES), const3),
            pl.BlockSpec((1, H * LANES), const2),
            pl.BlockSpec((1, H * LANES), const2),
        ],
        out_specs=[
            pl.BlockSpec((t, H * LANES), row),
            pl.BlockSpec((t, H * LANES), row),
            pl.BlockSpec((1, 1, W_BR, t), lambda b, i: (b, i, 0, 0)),
        ],
        out_shape=[
            jax.ShapeDtypeStruct((n, H * LANES), BF16),
            jax.ShapeDtypeStruct((n, H * LANES), BF16),
            jax.ShapeDtypeStruct((batch, ns, W_BR, t), BF16),
        ],
        scratch_shapes=[pltpu.VMEM((1, LANES), F32)],
        compiler_params=_cparams(("parallel", "arbitrary")),
        name="fox_prep",
    )(pw, pw, pw, pn, *wts)


def _attn_kernel(q_ref, k_ref, vt_ref, g_ref, o_ref, m_ref, acc_ref, sa_ref, sb_ref, mxa_ref, mxb_ref, *, tk):
    tq = ATTN_QT * tk
    gi = pl.program_id(2)
    nt = (((1,), (1,)), ((), ()))
    kv_pos = lax.broadcasted_iota(jnp.int32, (tk, tq), 0)
    q_pos = lax.broadcasted_iota(jnp.int32, (tk, tq), 1)
    causal = kv_pos <= q_pos
    both = slice(0, tq)
    ones = jnp.ones((ONES_ROWS, tk), BF16)
    m_ref[...] = jnp.full(m_ref.shape, NEG_INF, F32)
    acc_ref[...] = jnp.zeros(acc_ref.shape, F32)

    def scores(c, s_ref, mx_ref, cols):
        start = pl.multiple_of(c * tk, tk)
        for j in range(2):
            q = q_ref[cols, j * LANES:(j + 1) * LANES]
            k = k_ref[pl.ds(start, tk), j * LANES:(j + 1) * LANES]
            s = lax.dot_general(k, q, nt, preferred_element_type=F32)
            s_ref[j, :, cols] = s
            mx_ref[j, :, cols] = jnp.max(s, axis=0, keepdims=True)

    def consume(c, s_ref, mx_ref, cols, mask):
        for j in range(2):
            vt = jnp.concatenate([vt_ref[0, c, j * HD:(j + 1) * HD, :], ones], axis=0)
            s = s_ref[j, :, cols]
            if mask is not None:
                s = jnp.where(mask, s, NEG_INF)
                m_cur = jnp.max(s, axis=0, keepdims=True)
            else:
                m_cur = mx_ref[j, :, cols]
            m_old = m_ref[j, :, cols]
            m_new = jnp.maximum(m_old, m_cur)
            alpha = jnp.exp2(m_old - m_new)
            p = jnp.exp2(s - m_new)
            acc_ref[j, :, cols] = alpha * acc_ref[j, :, cols] + _dot(vt, p.astype(BF16))
            m_ref[j, :, cols] = m_new

    scores(0, sa_ref, mxa_ref, both)

    def body(i, carry):
        c = 2 * i
        scores(c + 1, sb_ref, mxb_ref, both)
        consume(c, sa_ref, mxa_ref, both, None)
        scores(c + 2, sa_ref, mxa_ref, both)
        consume(c + 1, sb_ref, mxb_ref, both, None)
        return carry

    lax.fori_loop(0, gi * (ATTN_QT // 2), body, 0)
    bufs = ((sa_ref, mxa_ref), (sb_ref, mxb_ref))
    for d in range(ATTN_QT):
        c = ATTN_QT * gi + d
        if d + 1 < ATTN_QT:
            scores(c + 1, *bufs[(d + 1) % 2], slice((d + 1) * tk, tq))
        consume(c, *bufs[d % 2], slice(d * tk, tq), causal[:, :tq - d * tk])

    o_t = jnp.concatenate([acc_ref[j, :HD, :] / acc_ref[j, HD:HD + 1, :] for j in range(2)], axis=0)
    g = g_ref[...].astype(F32)
    o_ref[...] = (o_t.T * (g * _sigmoid(g))).astype(BF16)


def _attention(q, k, vt, pw, gate_col, batch, seq, tk):
    n = q.shape[0]
    tq = ATTN_QT * tk
    nq, nk = seq // tq, seq // tk
    return pl.pallas_call(
        functools.partial(_attn_kernel, tk=tk),
        grid=(batch, H // 2, nq),
        in_specs=[
            pl.BlockSpec((tq, 2 * LANES), lambda b, hp, i: (b * nq + i, hp)),
            pl.BlockSpec((seq, 2 * LANES), lambda b, hp, i: (b, hp)),
            pl.BlockSpec((1, nk, LANES, tk), lambda b, hp, i: (b, 0, hp, 0)),
            pl.BlockSpec((tq, LANES), lambda b, hp, i: (b * nq + i, gate_col // LANES + hp)),
        ],
        out_specs=pl.BlockSpec((tq, LANES), lambda b, hp, i: (b * nq + i, hp)),
        out_shape=jax.ShapeDtypeStruct((n, W_BR), BF16),
        scratch_shapes=[
            pltpu.VMEM((2, 1, tq), F32),
            pltpu.VMEM((2, HD + ONES_ROWS, tq), F32),
            pltpu.VMEM((2, tk, tq), F32),
            pltpu.VMEM((2, tk, tq), F32),
            pltpu.VMEM((2, 1, tq), F32),
            pltpu.VMEM((2, 1, tq), F32),
        ],
        compiler_params=_cparams(("parallel", "parallel", "arbitrary")),
        name="attention",
    )(q, k, vt, pw)


def _rwkv_prep_kernel(*refs, first_layer):
    if first_layer:
        (sr_ref, sk_ref, sv_ref, swa_ref, gc_ref, mur_ref, muk_ref, muv_ref, muwa_ref, w0_ref, a0_ref, wup_ref,
         aup_ref, kk_ref, ka_ref, rk_ref, bt_ref, bo_ref,
         at_o, rt_o, bt_o, kt_o, v_o, bh_o, kh_o, wc_o, bonus_o, sg_o, vfirst_o,
         cr_ref, ck_ref, cv_ref, cwa_ref) = refs
    else:
        (sr_ref, sk_ref, sv_ref, swa_ref, gc_ref, vf_ref, mur_ref, muk_ref, muv_ref, muwa_ref, w0_ref, a0_ref,
         wup_ref, aup_ref, kk_ref, ka_ref, rk_ref, v0_ref, vdown_ref, vup_ref, bt_ref, bo_ref,
         at_o, rt_o, bt_o, kt_o, v_o, bh_o, kh_o, wc_o, bonus_o, sg_o,
         cr_ref, ck_ref, cv_ref, cwa_ref) = refs
    t = sr_ref.shape[0]

    @pl.when(pl.program_id(1) == 0)
    def _():
        for c_ref in (cr_ref, ck_ref, cv_ref, cwa_ref):
            c_ref[...] = jnp.zeros_like(c_ref)

    def shift(x_ref, c_ref, mu_ref):
        x = x_ref[...].astype(F32)
        rowid = lax.broadcasted_iota(jnp.int32, x.shape, 0)
        prev = jnp.where(rowid == 0, c_ref[...], pltpu.roll(x, 1, 0))
        c_ref[...] = x[t - 1:t, :]
        return x + mu_ref[...] * (prev - x)

    r = shift(sr_ref, cr_ref, mur_ref)
    k = shift(sk_ref, ck_ref, muk_ref)
    v = shift(sv_ref, cv_ref, muv_ref)
    wa = shift(swa_ref, cwa_ref, muwa_ref)

    logw = -RWKV_DECAY_SCALE * _sigmoid(w0_ref[...] + _dot(jnp.tanh(wa).astype(BF16), wup_ref[...]))
    a = _sigmoid(a0_ref[...] + _dot(wa.astype(BF16), aup_ref[...]))
    if first_layer:
        vfirst_o[...] = v
    else:
        low = _dot(v.astype(BF16), vdown_ref[...]).astype(BF16)
        nu = _sigmoid(v0_ref[...] + _dot(low, vup_ref[...]))
        v = v + (vf_ref[...] - v) * nu

    kk = k * kk_ref[...]
    k_mod = k * (1.0 + (a - 1.0) * ka_ref[...])
    rk = r * k_mod * rk_ref[...]
    kk_n, bonus = [], []
    for p in range(H // 2):
        sl = slice(p * LANES, (p + 1) * LANES)
        nrm = jnp.sqrt(_pair_seg_sum(kk[:, sl] * kk[:, sl]))
        kk_n.append(kk[:, sl] / jnp.maximum(nrm, 1e-12))
        bonus.append(_pair_seg_sum(rk[:, sl]) * v[:, sl])
    kk = jnp.concatenate(kk_n, axis=-1)
    bonus = jnp.concatenate(bonus, axis=-1)
    a_vec = -kk
    b_vec = kk * a

    lcum = _dot_exact_lhs(bt_ref[...], logw)
    ltot = _dot_exact_lhs(bo_ref[...], logw)
    e_minus = jnp.exp(-lcum)
    e_rem = jnp.exp(ltot - lcum)
    g = gc_ref[...].astype(F32)
    for o_ref, val in ((at_o, a_vec * jnp.exp(lcum - logw)), (rt_o, r * jnp.exp(lcum)), (bt_o, b_vec * e_minus),
                       (kt_o, k_mod * e_minus), (v_o, v), (bh_o, b_vec * e_rem), (kh_o, k_mod * e_rem),
                       (wc_o, jnp.exp(ltot)), (bonus_o, bonus), (sg_o, g * _sigmoid(g))):
        o_ref[...] = val.astype(o_ref.dtype)


def _rwkv_prep(pw, pn, vfirst, wts, batch, seq, t, first_layer):
    n = pw.shape[0]
    ns = seq // t
    const2 = lambda b, i: (0, 0)
    pcol = lambda col, w: pl.BlockSpec((t, w), lambda b, i: (b * ns + i, col // w))
    vec = pl.BlockSpec((1, W_BR), const2)
    in_specs = [pcol(COL_SR, W_BR), pcol(COL_SK, W_BR), pcol(COL_SV, W_BR), pcol(COL_WA, LANES),
                pcol(COL_GATE_C, W_BR)]
    args = [pw, pw, pw, pn, pw]
    if not first_layer:
        in_specs.append(pl.BlockSpec((t, W_BR), lambda b, i: (b * ns + i, 0)))
        args.append(vfirst)
    in_specs += [vec, vec, vec, pl.BlockSpec((1, LANES), const2), vec, vec,
                 pl.BlockSpec((LANES, W_BR), const2), pl.BlockSpec((LANES, W_BR), const2), vec, vec, vec]
    if not first_layer:
        in_specs += [vec, pl.BlockSpec((W_BR, LANES), const2), pl.BlockSpec((LANES, W_BR), const2)]
    in_specs += [pl.BlockSpec((t, t), const2), pl.BlockSpec((t, t), const2)]
    args += list(wts)
    row_spec = pl.BlockSpec((t, W_BR), lambda b, i: (b * ns + i, 0))
    out_dtypes = (BF16, F32, BF16, BF16, BF16, BF16, BF16, F32, F32, F32) + ((F32,) if first_layer else ())
    out_specs = [row_spec] * len(out_dtypes)
    out_shape = [jax.ShapeDtypeStruct((n, W_BR), dt) for dt in out_dtypes]
    return pl.pallas_call(
        functools.partial(_rwkv_prep_kernel, first_layer=first_layer),
        grid=(batch, ns),
        in_specs=in_specs,
        out_specs=out_specs,
        out_shape=out_shape,
        scratch_shapes=[pltpu.VMEM((1, W_BR), F32), pltpu.VMEM((1, W_BR), F32), pltpu.VMEM((1, W_BR), F32),
                        pltpu.VMEM((1, LANES), F32)],
        compiler_params=_cparams(("parallel", "arbitrary")),
        name="rwkv_prep",
    )(*args)


def _mm(a, b, dims):
    return lax.dot_general(a.astype(BF16), b.astype(BF16), dims, preferred_element_type=F32)


_NN = (((1,), (0,)), ((), ()))
_NT = (((1,), (1,)), ((), ()))
_TN = (((0,), (0,)), ((), ()))


def _rwkv_chunk_kernel(at_ref, rt_ref, bt_ref, kt_ref, v_ref, bh_ref, kh_ref, wc_ref, bonus_ref, sg_ref,
                       lg_ref, lb_ref, o_ref, s_all_ref, *, nchunk):
    @pl.when(pl.program_id(1) == 0)
    def _():
        s_all_ref[...] = jnp.zeros_like(s_all_ref)

    mm = _mm
    first = lax.broadcasted_iota(jnp.int32, (CHUNK, LANES), 1) < HD
    r2 = lax.broadcasted_iota(jnp.int32, (LANES, LANES), 0)
    c2 = lax.broadcasted_iota(jnp.int32, (LANES, LANES), 1)
    same_head = (r2 < HD) == (c2 < HD)
    eye2 = r2 == c2
    t_row, t_col = r2 % CHUNK, c2 % CHUNK
    strict2 = jnp.logical_and(same_head, t_col < t_row)
    incl2 = jnp.logical_and(same_head, t_col <= t_row)
    each = lambda f, *cols: [f(*xs) for xs in zip(*cols)]
    pairs = [(c, p) for c in range(nchunk) for p in range(H // 2)]
    npair = range(len(pairs))
    ld = lambda ref: [ref[c * CHUNK:(c + 1) * CHUNK, p * LANES:(p + 1) * LANES] for c, p in pairs]
    at, rt, bt, kt, v, bh, kh = ld(at_ref), ld(rt_ref), ld(bt_ref), ld(kt_ref), ld(v_ref), ld(bh_ref), ld(kh_ref)
    zero = jnp.zeros((CHUNK, LANES), BF16)
    stack = lambda x: jnp.concatenate([jnp.where(first, x, zero), jnp.where(first, zero, x)], axis=0)
    fold = lambda x: x[:CHUNK] + x[CHUNK:]
    a_s, b_s, k_s, v_s = each(stack, at), each(stack, bt), each(stack, kt), each(stack, v)
    x1 = each(lambda a, r: jnp.concatenate([a, stack(r.astype(BF16))], axis=0), a_s, rt)
    gb = each(lambda x, b: mm(x, b, _NT), x1, b_s)
    gk = each(lambda x, k: mm(x, k, _NT), x1, k_s)
    a_ab = [jnp.where(strict2, g[:LANES], 0.0) for g in gb]
    a_rb = [jnp.where(incl2, g[LANES:], 0.0) for g in gb]
    a_ak = [jnp.where(strict2, g[:LANES], 0.0) for g in gk]
    a_rk = [jnp.where(incl2, g[LANES:], 0.0) for g in gk]
    tinv = [jnp.where(eye2, 1.0, a) for a in a_ab]
    pw = a_ab
    for _ in range(int(math.log2(CHUNK)) - 1):
        pw = each(lambda p: mm(p, p, _NN), pw)
        tinv = each(lambda t, p: t + mm(t, p, _NN), tinv, pw)
    av = each(lambda a, x: mm(a, x, _NN), a_ak, v_s)
    tx = each(lambda t, x, a: mm(t, jnp.concatenate([x.astype(BF16), a], axis=1), _NN), tinv, av, a_s)
    ry = each(lambda a, x: mm(a, x, _NN), a_rb, tx)
    yk = each(lambda a, x: mm(a, x, _NN), a_rk, v_s)
    u = [fold(t[:, :LANES]) for t in tx]
    a_til = [fold(t[:, LANES:]) for t in tx]
    y_in = [fold(r[:, :LANES] + k) for r, k in zip(ry, yk)]
    r_hat = [r + fold(x[:, LANES:]) for r, x in zip(rt, ry)]
    m_new = [jnp.where(same_head, mm(a_til[i], bh[i], _TN), 0.0) for i in npair]
    n_new = [jnp.where(same_head, mm(jnp.concatenate([u[i].astype(BF16), v[i]], axis=0),
                                     jnp.concatenate([bh[i], kh[i]], axis=0), _TN), 0.0) for i in npair]
    lg, lb = lg_ref[...], lb_ref[...]
    for c in range(nchunk):
        idx = [i for i, (ci, _) in enumerate(pairs) if ci == c]
        s0 = [s_all_ref[p] for p in range(H // 2)]
        y = [y_in[i] + mm(r_hat[i], s0[p], _NT) for p, i in enumerate(idx)]
        for p, i in enumerate(idx):
            wc = wc_ref[c * CHUNK:c * CHUNK + 1, p * LANES:(p + 1) * LANES]
            s_all_ref[p] = mm(s0[p], jnp.where(eye2, wc, 0.0) + m_new[i], _NN) + n_new[i]
        for p, i in enumerate(idx):
            rows, lanes = slice(c * CHUNK, (c + 1) * CHUNK), slice(p * LANES, (p + 1) * LANES)
            mu = _pair_seg_sum(y[p]) / HD
            d = y[p] - mu
            var = _pair_seg_sum(d * d) / HD
            yn = d * lax.rsqrt(var + GN_EPS) * lg[:, lanes] + lb[:, lanes]
            o_ref[rows, lanes] = ((yn + bonus_ref[rows, lanes]) * sg_ref[rows, lanes]).astype(o_ref.dtype)


def _rwkv_chunk(arrs, lnx_g, lnx_b, batch, seq, tc):
    n = batch * seq
    ns = seq // tc
    spec = pl.BlockSpec((tc, W_BR), lambda b, i: (b * ns + i, 0))
    gspec = pl.BlockSpec((1, W_BR), lambda b, i: (0, 0))
    return pl.pallas_call(
        functools.partial(_rwkv_chunk_kernel, nchunk=tc // CHUNK),
        grid=(batch, ns),
        in_specs=[spec] * 10 + [gspec, gspec],
        out_specs=spec,
        out_shape=jax.ShapeDtypeStruct((n, W_BR), BF16),
        scratch_shapes=[pltpu.VMEM((H // 2, LANES, LANES), F32)],
        compiler_params=_cparams(("parallel", "arbitrary")),
        name="rwkv_chunk",
    )(*arrs, lnx_g, lnx_b)


def _merge_kernel(oa_ref, ob_ref, oc_ref, ga_ref, gb_ref, gc_ref, x_ref, wpa_ref, wpb_ref, wpc_ref, wout_ref,
                  o_ref):
    pa = _dot(oa_ref[...], wpa_ref[...])
    pb = _dot(ob_ref[...], wpb_ref[...])
    pc = _dot(oc_ref[...], wpc_ref[...])
    sig = lambda ref: _sigmoid(ref[...].astype(F32))
    merged = sig(ga_ref) * pa + sig(gb_ref) * pb + sig(gc_ref) * pc
    o_ref[...] = x_ref[...] + _dot(merged.astype(BF16), wout_ref[...])


def _merge(oa, ob, oc, p, x2, wpa, wpb, wpc, wout, batch, seq, t):
    n = x2.shape[0]
    ns = seq // t
    row = lambda b, i: (b * ns + i, 0)
    const2 = lambda b, i: (0, 0)
    gate = lambda g: pl.BlockSpec((t, D_MODEL), lambda b, i: (b * ns + i, COL_MERGE // D_MODEL + g))
    return pl.pallas_call(
        _merge_kernel,
        grid=(batch, ns),
        in_specs=[
            pl.BlockSpec((t, W_BR), row),
            pl.BlockSpec((t, W_BR), row),
            pl.BlockSpec((t, W_BR), row),
            gate(0), gate(1), gate(2),
            pl.BlockSpec((t, D_MODEL), row),
            pl.BlockSpec((W_BR, D_MODEL), const2),
            pl.BlockSpec((W_BR, D_MODEL), const2),
            pl.BlockSpec((W_BR, D_MODEL), const2),
            pl.BlockSpec((D_MODEL, D_MODEL), const2),
        ],
        out_specs=pl.BlockSpec((t, D_MODEL), row),
        out_shape=jax.ShapeDtypeStruct((n, D_MODEL), F32),
        compiler_params=_cparams(("parallel", "parallel")),
        name="merge",
    )(oa, ob, oc, p, p, p, x2, wpa, wpb, wpc, wout)


def _regroup_w_in(w):
    d = w.shape[0]
    sizes = (Q_LORA, KV_LORA, ROPE_D, W_BR, W_BR, W_BR, W_BR, H, W_BR,
             3 * W_BR + DECAY_LORA + AAA_LORA, W_BR, 3 * D_MODEL)
    offs = np.concatenate([[0], np.cumsum(sizes)])
    (c_q, c_kv, k_r, gate_a, fq, fk, fv, ff, gate_b, shift, gate_c, merge) = [
        w[:, offs[i]:offs[i + 1]] for i in range(len(sizes))]
    z = lambda n: jnp.zeros((d, n), w.dtype)
    wide = jnp.concatenate([
        merge, gate_a, fq, fk, fv, gate_b,
        shift[:, :W_BR], shift[:, W_BR:2 * W_BR], shift[:, 2 * W_BR:3 * W_BR], gate_c,
    ], axis=1)
    narrow = jnp.concatenate([
        c_q, c_kv,
        z(NOPE), k_r, z(LANES - NOPE - ROPE_D),
        ff, z(LANES - H),
        shift[:, 3 * W_BR:],
    ], axis=1)
    return wide.astype(BF16), narrow.astype(BF16)


def _rope_tables(seq):
    half = ROPE_D // 2
    inv = ROPE_THETA ** (-jnp.arange(0, ROPE_D, 2, dtype=F32) / ROPE_D)
    ang = jnp.arange(seq, dtype=F32)[:, None] * inv[None, :]
    cos, sin = jnp.cos(ang), jnp.sin(ang)
    z = lambda n: jnp.zeros((seq, n), F32)
    c = jnp.concatenate([jnp.ones((seq, NOPE), F32), cos, cos, z(LANES - NOPE - ROPE_D)], axis=1)
    s1 = jnp.concatenate([z(NOPE), -sin, z(LANES - NOPE - half)], axis=1)
    s2 = jnp.concatenate([z(NOPE + half), sin, z(LANES - NOPE - ROPE_D)], axis=1)
    return c, s1, s2


def _pad_lanes(v, lo, total=LANES):
    return jnp.zeros((1, total), F32).at[0, lo:lo + v.shape[0]].set(v)


def _fox_selectors():
    selq = np.zeros((3, LANES, H * LANES), np.float32)
    selk = np.zeros((3, LANES, H * LANES), np.float32)
    oneq = np.zeros((1, H * LANES), np.float32)
    onek = np.zeros((1, H * LANES), np.float32)
    for h in range(H):
        for j in range(3):
            selq[j, h, h * LANES + HD + j] = 1.0
            selk[j, h, h * LANES + HD + 3 + j] = -1.0
            oneq[0, h * LANES + HD + 3 + j] = 1.0
            onek[0, h * LANES + HD + j] = 1.0
    return (jnp.asarray(selq, BF16), jnp.asarray(selk, BF16), jnp.asarray(oneq), jnp.asarray(onek))


def _tri(t):
    r = np.arange(t)
    return jnp.asarray((r[None, :] <= r[:, None]).astype(np.float32), BF16)


def _chunk_tri(t):
    r = np.arange(t)
    same = (r[None, :] // CHUNK) == (r[:, None] // CHUNK)
    lower = r[None, :] <= r[:, None]
    return (jnp.asarray((same & lower).astype(np.float32), BF16), jnp.asarray(same.astype(np.float32), BF16))


def _tile(seq, pref):
    return pref if seq % pref == 0 else seq


RWKV_TC = 256


def kernel(x, norm_g, w_in, mla_qa_g, mla_w_uq, mla_kva_g, mla_w_ukv, mla_q_g, mla_knope_g, mla_krope_g, fox_b_f, fox_q_g, fox_k_g, rwkv_mu, rwkv_w0, rwkv_w_up, rwkv_a0, rwkv_a_up, rwkv_k_k, rwkv_k_a, rwkv_r_k, rwkv_lnx_g, rwkv_lnx_b, rwkv_v0, rwkv_v_down, rwkv_v_up, w_pa, w_pb, w_pc, w_out):
    batch, seq, d = x.shape
    depth = w_in.shape[0]
    n = batch * seq
    t_row = _tile(seq, 512)
    t_in = _tile(n, 1024)
    t_rwkv = _tile(seq, 256)
    x2 = x.reshape(n, d)

    tabs = _rope_tables(seq)
    selq, selk, oneq, onek = _fox_selectors()
    tri = _tri(t_row)
    btri, bones = _chunk_tri(t_rwkv)
    row = lambda v: v.reshape(1, -1).astype(F32)

    vfirst = None
    for l in range(depth):
        w_wide, w_narrow = _regroup_w_in(w_in[l])
        pw = _inproj(x2, row(norm_g[l]), w_wide, t_in, INPROJ_TN, BF16)
        pn = _inproj(x2, row(norm_g[l]), w_narrow, t_in, NCOLS_NARROW, F32)

        wuq = mla_w_uq[l].reshape(Q_LORA, H, NOPE + ROPE_D)
        wuq = jnp.pad(wuq, ((0, 0), (0, 0), (0, LANES - NOPE - ROPE_D))).reshape(Q_LORA, H * LANES).astype(BF16)
        wukv = mla_w_ukv[l].reshape(KV_LORA, H, NOPE + HD)
        wk = jnp.pad(wukv[:, :, :NOPE], ((0, 0), (0, 0), (0, LANES - NOPE))).reshape(KV_LORA, H * LANES).astype(BF16)
        wv = wukv[:, :, NOPE:].reshape(KV_LORA, W_BR).astype(BF16)
        mla_scale = float(NOPE + ROPE_D) ** -0.5 * LOG2E
        mla_wts = (row(mla_qa_g[l]), wuq, row(mla_kva_g[l]), wk, wv,
                   _pad_lanes(mla_q_g[l] * mla_scale, 0), _pad_lanes(mla_knope_g[l], 0),
                   _pad_lanes(mla_krope_g[l], NOPE))
        qa, ka, va = _mla_prep(pn, tabs, mla_wts, seq, t_row)
        o_a = _attention(qa, ka, va, pw, COL_GATE_A, batch, seq, t_row)

        fox_wts = (_pad_lanes(fox_b_f[l], 0), row(jnp.tile(fox_q_g[l] * (float(HD) ** -0.5 * LOG2E), H)),
                   row(jnp.tile(fox_k_g[l], H)), tri, selq, selk, oneq, onek)
        qb, kb, vb = _fox_prep(pw, pn, fox_wts, batch, seq, t_row)
        o_b = _attention(qb, kb, vb, pw, COL_GATE_B, batch, seq, t_row)

        mu = rwkv_mu[l]
        wup = jnp.zeros((LANES, W_BR), F32).at[:DECAY_LORA].set(rwkv_w_up[l]).astype(BF16)
        aup = jnp.zeros((LANES, W_BR), F32).at[DECAY_LORA:].set(rwkv_a_up[l]).astype(BF16)
        rw = [row(mu[:W_BR]), row(mu[W_BR:2 * W_BR]), row(mu[2 * W_BR:3 * W_BR]), row(mu[3 * W_BR:]),
              row(rwkv_w0[l]), row(rwkv_a0[l]), wup, aup, row(rwkv_k_k[l]), row(rwkv_k_a[l]),
              row(rwkv_r_k[l])]
        if l > 0:
            vdown = jnp.zeros((W_BR, LANES), F32).at[:, :MV_LORA].set(rwkv_v_down[l - 1]).astype(BF16)
            vup = jnp.zeros((LANES, W_BR), F32).at[:MV_LORA].set(rwkv_v_up[l - 1]).astype(BF16)
            rw += [row(rwkv_v0[l - 1]), vdown, vup]
        rw += [btri, bones]
        outs = _rwkv_prep(pw, pn, vfirst, rw, batch, seq, t_rwkv, first_layer=(l == 0))
        if l == 0:
            vfirst = outs[10]
        o_c = _rwkv_chunk(outs[:10], row(rwkv_lnx_g[l]), row(rwkv_lnx_b[l]), batch, seq, _tile(seq, RWKV_TC))

        x2 = _merge(o_a, o_b, o_c, pw, x2, w_pa[l].astype(BF16), w_pb[l].astype(BF16),
                    w_pc[l].astype(BF16), w_out[l].astype(BF16), batch, seq, t_row)
    return x2.reshape(batch, seq, d)
```

```python
import functools
import math

import jax
import jax.numpy as jnp
import numpy as np
from jax import lax
from jax.experimental import pallas as pl
from jax.experimental.pallas import tpu as pltpu

F32 = jnp.float32
BF16 = jnp.bfloat16

LANES = 128
H = 8
HD = 64
NOPE = 64
ROPE_D = 32
Q_LORA = 256
KV_LORA = 128
D_MODEL = 1024
W_BR = H * HD
DECAY_LORA = 64
AAA_LORA = 64
MV_LORA = 32
ROPE_THETA = 10000.0
RWKV_DECAY_SCALE = 0.606531
GN_EPS = 64e-5
EPS = 1e-6
NEG_INF = -1e30
LOG2E = math.log2(math.e)
CHUNK = 64
ATTN_QT = 4
ONES_ROWS = 16

COL_MERGE = 0
COL_GATE_A = 3072
COL_FQ = 3584
COL_FK = 4096
COL_FV = 4608
COL_GATE_B = 5120
COL_SR = 5632
COL_SK = 6144
COL_SV = 6656
COL_GATE_C = 7168
NCOLS_WIDE = 7680
INPROJ_TN = 2560
COL_CQ = 0
COL_CKV = 256
COL_KR = 384
COL_FF = 512
COL_WA = 640
NCOLS_NARROW = 768

VMEM_LIMIT = 56 * 1024 * 1024


def _cparams(sem):
    return pltpu.CompilerParams(dimension_semantics=sem, vmem_limit_bytes=VMEM_LIMIT)


def _sigmoid(x):
    return 1.0 / (1.0 + jnp.exp(-x))


def _dot(a, b):
    return jnp.dot(a, b, preferred_element_type=F32)


def _split3(x):
    hi = x.astype(BF16)
    r1 = x - hi.astype(F32)
    mid = r1.astype(BF16)
    lo = (r1 - mid.astype(F32)).astype(BF16)
    return hi, mid, lo


def _dot_exact_lhs(m_bf16, x):
    hi, mid, lo = _split3(x)
    return _dot(m_bf16, hi) + _dot(m_bf16, mid) + _dot(m_bf16, lo)


def _pair_seg_sum(x):
    lane = lax.broadcasted_iota(jnp.int32, x.shape, 1)
    first = lane < HD
    s0 = jnp.sum(jnp.where(first, x, 0.0), axis=-1, keepdims=True)
    s1 = jnp.sum(jnp.where(first, 0.0, x), axis=-1, keepdims=True)
    return jnp.where(first, s0, s1)


def _inproj_kernel(x_ref, g_ref, w_ref, o_ref, h_ref):
    @pl.when(pl.program_id(1) == 0)
    def _():
        x = x_ref[...]
        ms = jnp.mean(x * x, axis=-1, keepdims=True)
        h_ref[...] = (x * lax.rsqrt(ms + EPS) * g_ref[...]).astype(BF16)

    o_ref[...] = _dot(h_ref[...], w_ref[...]).astype(o_ref.dtype)


def _inproj(x2, g, w, tm, tn, out_dtype):
    n, ncols = x2.shape[0], w.shape[1]
    return pl.pallas_call(
        _inproj_kernel,
        grid=(n // tm, ncols // tn),
        in_specs=[
            pl.BlockSpec((tm, D_MODEL), lambda i, j: (i, 0)),
            pl.BlockSpec((1, D_MODEL), lambda i, j: (0, 0)),
            pl.BlockSpec((D_MODEL, tn), lambda i, j: (0, j)),
        ],
        out_specs=pl.BlockSpec((tm, tn), lambda i, j: (i, j)),
        out_shape=jax.ShapeDtypeStruct((n, ncols), out_dtype),
        scratch_shapes=[pltpu.VMEM((tm, D_MODEL), BF16)],
        compiler_params=_cparams(("parallel", "arbitrary")),
        name="inproj",
    )(x2, g, w)


def _rms(x, g, n):
    ms = jnp.sum(x * x, axis=-1, keepdims=True) / n
    return x * lax.rsqrt(ms + EPS) * g


def _rope(x, c, s1, s2):
    return x * c + pltpu.roll(x, LANES - ROPE_D // 2, 1) * s1 + pltpu.roll(x, ROPE_D // 2, 1) * s2


def _mla_prep_kernel(cq_ref, ckv_ref, kr_ref, c_ref, s1_ref, s2_ref, qag_ref, wuq_ref, kvag_ref, wk_ref, wv_ref,
                     qg_ref, kng_ref, krg_ref, q_out, k_out, vt_out):
    c, s1, s2 = c_ref[...], s1_ref[...], s2_ref[...]
    cqn = _rms(cq_ref[...], qag_ref[...], Q_LORA).astype(BF16)
    ckvn = _rms(ckv_ref[...], kvag_ref[...], KV_LORA).astype(BF16)
    q_all = _dot(cqn, wuq_ref[...])
    k_all = _dot(ckvn, wk_ref[...])
    vt_out[0, 0] = lax.dot_general(wv_ref[...], ckvn, _NT, preferred_element_type=F32).astype(BF16)
    k_rope = _rope(_rms(kr_ref[...], krg_ref[...], ROPE_D), c, s1, s2)
    qg, kng = qg_ref[...], kng_ref[...]
    for h in range(H):
        sl = slice(h * LANES, (h + 1) * LANES)
        q_out[:, sl] = _rope(_rms(q_all[:, sl], qg, NOPE + ROPE_D), c, s1, s2).astype(BF16)
        k_out[:, sl] = (_rms(k_all[:, sl], kng, NOPE) + k_rope).astype(BF16)


def _mla_prep(pn, tabs, wts, seq, t):
    n = pn.shape[0]
    ns = seq // t
    row = lambda i: (i, 0)
    const = lambda i: (0, 0)
    tab_spec = pl.BlockSpec((t, LANES), lambda i: (i % ns, 0))
    return pl.pallas_call(
        _mla_prep_kernel,
        grid=(n // t,),
        in_specs=[
            pl.BlockSpec((t, Q_LORA), lambda i: (i, COL_CQ // Q_LORA)),
            pl.BlockSpec((t, LANES), lambda i: (i, COL_CKV // LANES)),
            pl.BlockSpec((t, LANES), lambda i: (i, COL_KR // LANES)),
            tab_spec, tab_spec, tab_spec,
            pl.BlockSpec((1, Q_LORA), const),
            pl.BlockSpec((Q_LORA, H * LANES), const),
            pl.BlockSpec((1, KV_LORA), const),
            pl.BlockSpec((KV_LORA, H * LANES), const),
            pl.BlockSpec((W_BR, KV_LORA), const),
            pl.BlockSpec((1, LANES), const),
            pl.BlockSpec((1, LANES), const),
            pl.BlockSpec((1, LANES), const),
        ],
        out_specs=[
            pl.BlockSpec((t, H * LANES), row),
            pl.BlockSpec((t, H * LANES), row),
            pl.BlockSpec((1, 1, W_BR, t), lambda i: (i // ns, i % ns, 0, 0)),
        ],
        out_shape=[
            jax.ShapeDtypeStruct((n, H * LANES), BF16),
            jax.ShapeDtypeStruct((n, H * LANES), BF16),
            jax.ShapeDtypeStruct((n // seq, ns, W_BR, t), BF16),
        ],
        compiler_params=_cparams(("parallel",)),
        name="mla_prep",
    )(pn, pn, pn, *tabs, *wts)


def _fox_prep_kernel(fq_ref, fk_ref, fv_ref, ff_ref, bf_ref, qg_ref, kg_ref, tri_ref, selq_ref, selk_ref,
                     oneq_ref, onek_ref, eye_ref, q_out, k_out, vt_out, carry_ref):
    t = fq_ref.shape[0]

    @pl.when(pl.program_id(1) == 0)
    def _():
        carry_ref[...] = jnp.zeros_like(carry_ref)

    lane = lax.broadcasted_iota(jnp.int32, (t, LANES), 1)
    first = lane < HD
    z = ff_ref[...] + bf_ref[...]
    logf = jnp.minimum(z, 0.0) - jnp.log1p(jnp.exp(-jnp.abs(z)))
    logf = jnp.where(lane < H, logf, 0.0)
    cum = carry_ref[...] + _dot_exact_lhs(tri_ref[...], logf)
    carry_ref[...] = cum[t - 1:t, :]
    hi, mid, lo = _split3(cum * LOG2E)
    terms = (hi.astype(F32) + pltpu.roll(mid.astype(F32), H, 1) + pltpu.roll(lo.astype(F32), 2 * H, 1)).astype(BF16)
    aug_q = _dot(terms, selq_ref[...]) + oneq_ref[...]
    aug_k = _dot(terms, selk_ref[...]) + onek_ref[...]
    vt_out[0, 0] = lax.dot_general(eye_ref[...], fv_ref[...], _NT, preferred_element_type=F32).astype(BF16)
    for src_ref, g_ref, aug, dst in ((fq_ref, qg_ref, aug_q, q_out), (fk_ref, kg_ref, aug_k, k_out)):
        for p in range(H // 2):
            sl = slice(p * LANES, (p + 1) * LANES)
            x = src_ref[:, sl].astype(F32)
            ms = _pair_seg_sum(x * x) / HD
            xn = x * lax.rsqrt(ms + EPS) * g_ref[:, sl]
            he, ho = 2 * p, 2 * p + 1
            dst[:, he * LANES:(he + 1) * LANES] = jnp.where(first, xn, aug[:, he * LANES:(he + 1) * LANES]).astype(BF16)
            dst[:, ho * LANES:(ho + 1) * LANES] = jnp.where(
                first, pltpu.roll(xn, HD, 1), aug[:, ho * LANES:(ho + 1) * LANES]).astype(BF16)


def _fox_prep(pw, pn, wts, batch, seq, t):
    n = pw.shape[0]
    ns = seq // t
    row = lambda b, i: (b * ns + i, 0)
    const2 = lambda b, i: (0, 0)
    const3 = lambda b, i: (0, 0, 0)
    return pl.pallas_call(
        _fox_prep_kernel,
        grid=(batch, ns),
        in_specs=[
            pl.BlockSpec((t, W_BR), lambda b, i: (b * ns + i, COL_FQ // W_BR)),
            pl.BlockSpec((t, W_BR), lambda b, i: (b * ns + i, COL_FK // W_BR)),
            pl.BlockSpec((t, W_BR), lambda b, i: (b * ns + i, COL_FV // W_BR)),
            pl.BlockSpec((t, LANES), lambda b, i: (b * ns + i, COL_FF // LANES)),
            pl.BlockSpec((1, LANES), const2),
            pl.BlockSpec((1, W_BR), const2),
            pl.BlockSpec((1, W_BR), const2),
            pl.BlockSpec((t, t), const2),
            pl.BlockSpec((LANES, H * LANES), const2),
            pl.BlockSpec((LANES, H * LANES), const2),
            pl.BlockSpec((1, H * LANES), const2),
            pl.BlockSpec((1, H * LANES), const2),
            pl.BlockSpec((W_BR, W_BR), const2),
        ],
        out_specs=[
            pl.BlockSpec((t, H * LANES), row),
            pl.BlockSpec((t, H * LANES), row),
            pl.BlockSpec((1, 1, W_BR, t), lambda b, i: (b, i, 0, 0)),
        ],
        out_shape=[
            jax.ShapeDtypeStruct((n, H * LANES), BF16),
            jax.ShapeDtypeStruct((n, H * LANES), BF16),
            jax.ShapeDtypeStruct((batch, ns, W_BR, t), BF16),
        ],
        scratch_shapes=[pltpu.VMEM((1, LANES), F32)],
        compiler_params=_cparams(("parallel", "arbitrary")),
        name="fox_prep",
    )(pw, pw, pw, pn, *wts)


def _attn_kernel(q_ref, k_ref, vt_ref, g_ref, o_ref, m_ref, acc_ref, sa_ref, sb_ref, mxa_ref, mxb_ref, *, tk):
    tq = ATTN_QT * tk
    gi = pl.program_id(2)
    nt = (((1,), (1,)), ((), ()))
    kv_pos = lax.broadcasted_iota(jnp.int32, (tk, tq), 0)
    q_pos = lax.broadcasted_iota(jnp.int32, (tk, tq), 1)
    causal = kv_pos <= q_pos
    both = slice(0, tq)
    ones = jnp.ones((ONES_ROWS, tk), BF16)
    m_ref[...] = jnp.full(m_ref.shape, NEG_INF, F32)
    acc_ref[...] = jnp.zeros(acc_ref.shape, F32)

    def scores(c, s_ref, mx_ref, cols):
        start = pl.multiple_of(c * tk, tk)
        for j in range(2):
            q = q_ref[cols, j * LANES:(j + 1) * LANES]
            k = k_ref[pl.ds(start, tk), j * LANES:(j + 1) * LANES]
            s = lax.dot_general(k, q, nt, preferred_element_type=F32)
            s_ref[j, :, cols] = s
            mx_ref[j, :, cols] = jnp.max(s, axis=0, keepdims=True)

    def consume(c, s_ref, mx_ref, cols, mask):
        for j in range(2):
            vt = jnp.concatenate([vt_ref[0, c, j * HD:(j + 1) * HD, :], ones], axis=0)
            s = s_ref[j, :, cols]
            if mask is not None:
                s = jnp.where(mask, s, NEG_INF)
                m_cur = jnp.max(s, axis=0, keepdims=True)
            else:
                m_cur = mx_ref[j, :, cols]
            m_old = m_ref[j, :, cols]
            m_new = jnp.maximum(m_old, m_cur)
            alpha = jnp.exp2(m_old - m_new)
            p = jnp.exp2(s - m_new)
            acc_ref[j, :, cols] = alpha * acc_ref[j, :, cols] + _dot(vt, p.astype(BF16))
            m_ref[j, :, cols] = m_new

    scores(0, sa_ref, mxa_ref, both)

    def body(i, carry):
        c = 2 * i
        scores(c + 1, sb_ref, mxb_ref, both)
        consume(c, sa_ref, mxa_ref, both, None)
        scores(c + 2, sa_ref, mxa_ref, both)
        consume(c + 1, sb_ref, mxb_ref, both, None)
        return carry

    lax.fori_loop(0, gi * (ATTN_QT // 2), body, 0)
    bufs = ((sa_ref, mxa_ref), (sb_ref, mxb_ref))
    for d in range(ATTN_QT):
        c = ATTN_QT * gi + d
        if d + 1 < ATTN_QT:
            scores(c + 1, *bufs[(d + 1) % 2], slice((d + 1) * tk, tq))
        consume(c, *bufs[d % 2], slice(d * tk, tq), causal[:, :tq - d * tk])

    o_t = jnp.concatenate([acc_ref[j, :HD, :] / acc_ref[j, HD:HD + 1, :] for j in range(2)], axis=0)
    g = g_ref[...].astype(F32)
    o_ref[...] = (o_t.T * (g * _sigmoid(g))).astype(BF16)


def _attention(q, k, vt, pw, gate_col, batch, seq, tk):
    n = q.shape[0]
    tq = ATTN_QT * tk
    nq, nk = seq // tq, seq // tk
    return pl.pallas_call(
        functools.partial(_attn_kernel, tk=tk),
        grid=(batch, H // 2, nq),
        in_specs=[
            pl.BlockSpec((tq, 2 * LANES), lambda b, hp, i: (b * nq + i, hp)),
            pl.BlockSpec((seq, 2 * LANES), lambda b, hp, i: (b, hp)),
            pl.BlockSpec((1, nk, LANES, tk), lambda b, hp, i: (b, 0, hp, 0)),
            pl.BlockSpec((tq, LANES), lambda b, hp, i: (b * nq + i, gate_col // LANES + hp)),
        ],
        out_specs=pl.BlockSpec((tq, LANES), lambda b, hp, i: (b * nq + i, hp)),
        out_shape=jax.ShapeDtypeStruct((n, W_BR), BF16),
        scratch_shapes=[
            pltpu.VMEM((2, 1, tq), F32),
            pltpu.VMEM((2, HD + ONES_ROWS, tq), F32),
            pltpu.VMEM((2, tk, tq), F32),
            pltpu.VMEM((2, tk, tq), F32),
            pltpu.VMEM((2, 1, tq), F32),
            pltpu.VMEM((2, 1, tq), F32),
        ],
        compiler_params=_cparams(("parallel", "parallel", "arbitrary")),
        name="attention",
    )(q, k, vt, pw)


def _rwkv_prep_kernel(*refs, first_layer):
    if first_layer:
        (sr_ref, sk_ref, sv_ref, swa_ref, gc_ref, mur_ref, muk_ref, muv_ref, muwa_ref, w0_ref, a0_ref, wup_ref,
         aup_ref, kk_ref, ka_ref, rk_ref, bt_ref, bo_ref,
         at_o, rt_o, bt_o, kt_o, v_o, bh_o, kh_o, wc_o, bonus_o, sg_o, vfirst_o,
         cr_ref, ck_ref, cv_ref, cwa_ref) = refs
    else:
        (sr_ref, sk_ref, sv_ref, swa_ref, gc_ref, vf_ref, mur_ref, muk_ref, muv_ref, muwa_ref, w0_ref, a0_ref,
         wup_ref, aup_ref, kk_ref, ka_ref, rk_ref, v0_ref, vdown_ref, vup_ref, bt_ref, bo_ref,
         at_o, rt_o, bt_o, kt_o, v_o, bh_o, kh_o, wc_o, bonus_o, sg_o,
         cr_ref, ck_ref, cv_ref, cwa_ref) = refs
    t = sr_ref.shape[0]

    @pl.when(pl.program_id(1) == 0)
    def _():
        for c_ref in (cr_ref, ck_ref, cv_ref, cwa_ref):
            c_ref[...] = jnp.zeros_like(c_ref)

    def shift(x_ref, c_ref, mu_ref):
        x = x_ref[...].astype(F32)
        rowid = lax.broadcasted_iota(jnp.int32, x.shape, 0)
        prev = jnp.where(rowid == 0, c_ref[...], pltpu.roll(x, 1, 0))
        c_ref[...] = x[t - 1:t, :]
        return x + mu_ref[...] * (prev - x)

    r = shift(sr_ref, cr_ref, mur_ref)
    k = shift(sk_ref, ck_ref, muk_ref)
    v = shift(sv_ref, cv_ref, muv_ref)
    wa = shift(swa_ref, cwa_ref, muwa_ref)

    logw = -RWKV_DECAY_SCALE * _sigmoid(w0_ref[...] + _dot(jnp.tanh(wa).astype(BF16), wup_ref[...]))
    a = _sigmoid(a0_ref[...] + _dot(wa.astype(BF16), aup_ref[...]))
    if first_layer:
        vfirst_o[...] = v
    else:
        low = _dot(v.astype(BF16), vdown_ref[...]).astype(BF16)
        nu = _sigmoid(v0_ref[...] + _dot(low, vup_ref[...]))
        v = v + (vf_ref[...] - v) * nu

    kk = k * kk_ref[...]
    k_mod = k * (1.0 + (a - 1.0) * ka_ref[...])
    rk = r * k_mod * rk_ref[...]
    kk_n, bonus = [], []
    for p in range(H // 2):
        sl = slice(p * LANES, (p + 1) * LANES)
        nrm = jnp.sqrt(_pair_seg_sum(kk[:, sl] * kk[:, sl]))
        kk_n.append(kk[:, sl] / jnp.maximum(nrm, 1e-12))
        bonus.append(_pair_seg_sum(rk[:, sl]) * v[:, sl])
    kk = jnp.concatenate(kk_n, axis=-1)
    bonus = jnp.concatenate(bonus, axis=-1)
    a_vec = -kk
    b_vec = kk * a

    lcum = _dot_exact_lhs(bt_ref[...], logw)
    ltot = _dot_exact_lhs(bo_ref[...], logw)
    e_minus = jnp.exp(-lcum)
    e_rem = jnp.exp(ltot - lcum)
    g = gc_ref[...].astype(F32)
    for o_ref, val in ((at_o, a_vec * jnp.exp(lcum - logw)), (rt_o, r * jnp.exp(lcum)), (bt_o, b_vec * e_minus),
                       (kt_o, k_mod * e_minus), (v_o, v), (bh_o, b_vec * e_rem), (kh_o, k_mod * e_rem),
                       (wc_o, jnp.exp(ltot)), (bonus_o, bonus), (sg_o, g * _sigmoid(g))):
        o_ref[...] = val.astype(o_ref.dtype)


def _rwkv_prep(pw, pn, vfirst, wts, batch, seq, t, first_layer):
    n = pw.shape[0]
    ns = seq // t
    const2 = lambda b, i: (0, 0)
    pcol = lambda col, w: pl.BlockSpec((t, w), lambda b, i: (b * ns + i, col // w))
    vec = pl.BlockSpec((1, W_BR), const2)
    in_specs = [pcol(COL_SR, W_BR), pcol(COL_SK, W_BR), pcol(COL_SV, W_BR), pcol(COL_WA, LANES),
                pcol(COL_GATE_C, W_BR)]
    args = [pw, pw, pw, pn, pw]
    if not first_layer:
        in_specs.append(pl.BlockSpec((t, W_BR), lambda b, i: (b * ns + i, 0)))
        args.append(vfirst)
    in_specs += [vec, vec, vec, pl.BlockSpec((1, LANES), const2), vec, vec,
                 pl.BlockSpec((LANES, W_BR), const2), pl.BlockSpec((LANES, W_BR), const2), vec, vec, vec]
    if not first_layer:
        in_specs += [vec, pl.BlockSpec((W_BR, LANES), const2), pl.BlockSpec((LANES, W_BR), const2)]
    in_specs += [pl.BlockSpec((t, t), const2), pl.BlockSpec((t, t), const2)]
    args += list(wts)
    row_spec = pl.BlockSpec((t, W_BR), lambda b, i: (b * ns + i, 0))
    out_dtypes = (BF16, F32, BF16, BF16, BF16, BF16, BF16, F32, F32, F32) + ((F32,) if first_layer else ())
    out_specs = [row_spec] * len(out_dtypes)
    out_shape = [jax.ShapeDtypeStruct((n, W_BR), dt) for dt in out_dtypes]
    return pl.pallas_call(
        functools.partial(_rwkv_prep_kernel, first_layer=first_layer),
        grid=(batch, ns),
        in_specs=in_specs,
        out_specs=out_specs,
        out_shape=out_shape,
        scratch_shapes=[pltpu.VMEM((1, W_BR), F32), pltpu.VMEM((1, W_BR), F32), pltpu.VMEM((1, W_BR), F32),
                        pltpu.VMEM((1, LANES), F32)],
        compiler_params=_cparams(("parallel", "arbitrary")),
        name="rwkv_prep",
    )(*args)


def _mm(a, b, dims):
    return lax.dot_general(a.astype(BF16), b.astype(BF16), dims, preferred_element_type=F32)


_NN = (((1,), (0,)), ((), ()))
_NT = (((1,), (1,)), ((), ()))
_TN = (((0,), (0,)), ((), ()))


def _rwkv_chunk_kernel(at_ref, rt_ref, bt_ref, kt_ref, v_ref, bh_ref, kh_ref, wc_ref, bonus_ref, sg_ref,
                       lg_ref, lb_ref, o_ref, s_all_ref, *, nchunk):
    @pl.when(pl.program_id(1) == 0)
    def _():
        s_all_ref[...] = jnp.zeros_like(s_all_ref)

    mm = _mm
    first = lax.broadcasted_iota(jnp.int32, (CHUNK, LANES), 1) < HD
    r2 = lax.broadcasted_iota(jnp.int32, (LANES, LANES), 0)
    c2 = lax.broadcasted_iota(jnp.int32, (LANES, LANES), 1)
    same_head = (r2 < HD) == (c2 < HD)
    eye2 = r2 == c2
    t_row, t_col = r2 % CHUNK, c2 % CHUNK
    strict2 = jnp.logical_and(same_head, t_col < t_row)
    incl2 = jnp.logical_and(same_head, t_col <= t_row)
    each = lambda f, *cols: [f(*xs) for xs in zip(*cols)]
    pairs = [(c, p) for c in range(nchunk) for p in range(H // 2)]
    npair = range(len(pairs))
    ld = lambda ref: [ref[c * CHUNK:(c + 1) * CHUNK, p * LANES:(p + 1) * LANES] for c, p in pairs]
    at, rt, bt, kt, v, bh, kh = ld(at_ref), ld(rt_ref), ld(bt_ref), ld(kt_ref), ld(v_ref), ld(bh_ref), ld(kh_ref)
    zero = jnp.zeros((CHUNK, LANES), BF16)
    stack = lambda x: jnp.concatenate([jnp.where(first, x, zero), jnp.where(first, zero, x)], axis=0)
    fold = lambda x: x[:CHUNK] + x[CHUNK:]
    a_s, b_s, k_s, v_s = each(stack, at), each(stack, bt), each(stack, kt), each(stack, v)
    x1 = each(lambda a, r: jnp.concatenate([a, stack(r.astype(BF16))], axis=0), a_s, rt)
    gb = each(lambda x, b: mm(x, b, _NT), x1, b_s)
    gk = each(lambda x, k: mm(x, k, _NT), x1, k_s)
    a_ab = [jnp.where(strict2, g[:LANES], 0.0) for g in gb]
    a_rb = [jnp.where(incl2, g[LANES:], 0.0) for g in gb]
    a_ak = [jnp.where(strict2, g[:LANES], 0.0) for g in gk]
    a_rk = [jnp.where(incl2, g[LANES:], 0.0) for g in gk]
    tinv = [jnp.where(eye2, 1.0, a) for a in a_ab]
    pw = a_ab
    for _ in range(int(math.log2(CHUNK)) - 1):
        pw = each(lambda p: mm(p, p, _NN), pw)
        tinv = each(lambda t, p: t + mm(t, p, _NN), tinv, pw)
    av = each(lambda a, x: mm(a, x, _NN), a_ak, v_s)
    tx = each(lambda t, x, a: mm(t, jnp.concatenate([x.astype(BF16), a], axis=1), _NN), tinv, av, a_s)
    ry = each(lambda a, x: mm(a, x, _NN), a_rb, tx)
    yk = each(lambda a, x: mm(a, x, _NN), a_rk, v_s)
    u = [fold(t[:, :LANES]) for t in tx]
    a_til = [fold(t[:, LANES:]) for t in tx]
    y_in = [fold(r[:, :LANES] + k) for r, k in zip(ry, yk)]
    r_hat = [r + fold(x[:, LANES:]) for r, x in zip(rt, ry)]
    m_new = [jnp.where(same_head, mm(a_til[i], bh[i], _TN), 0.0) for i in npair]
    n_new = [jnp.where(same_head, mm(jnp.concatenate([u[i].astype(BF16), v[i]], axis=0),
                                     jnp.concatenate([bh[i], kh[i]], axis=0), _TN), 0.0) for i in npair]
    lg, lb = lg_ref[...], lb_ref[...]
    for c in range(nchunk):
        idx = [i for i, (ci, _) in enumerate(pairs) if ci == c]
        s0 = [s_all_ref[p] for p in range(H // 2)]
        y = [y_in[i] + mm(r_hat[i], s0[p], _NT) for p, i in enumerate(idx)]
        for p, i in enumerate(idx):
            wc = wc_ref[c * CHUNK:c * CHUNK + 1, p * LANES:(p + 1) * LANES]
            s_all_ref[p] = mm(s0[p], jnp.where(eye2, wc, 0.0) + m_new[i], _NN) + n_new[i]
        for p, i in enumerate(idx):
            rows, lanes = slice(c * CHUNK, (c + 1) * CHUNK), slice(p * LANES, (p + 1) * LANES)
            mu = _pair_seg_sum(y[p]) / HD
            d = y[p] - mu
            var = _pair_seg_sum(d * d) / HD
            yn = d * lax.rsqrt(var + GN_EPS) * lg[:, lanes] + lb[:, lanes]
            o_ref[rows, lanes] = ((yn + bonus_ref[rows, lanes]) * sg_ref[rows, lanes]).astype(o_ref.dtype)


def _rwkv_chunk(arrs, lnx_g, lnx_b, batch, seq, tc):
    n = batch * seq
    ns = seq // tc
    spec = pl.BlockSpec((tc, W_BR), lambda b, i: (b * ns + i, 0))
    gspec = pl.BlockSpec((1, W_BR), lambda b, i: (0, 0))
    return pl.pallas_call(
        functools.partial(_rwkv_chunk_kernel, nchunk=tc // CHUNK),
        grid=(batch, ns),
        in_specs=[spec] * 10 + [gspec, gspec],
        out_specs=spec,
        out_shape=jax.ShapeDtypeStruct((n, W_BR), BF16),
        scratch_shapes=[pltpu.VMEM((H // 2, LANES, LANES), F32)],
        compiler_params=_cparams(("parallel", "arbitrary")),
        name="rwkv_chunk",
    )(*arrs, lnx_g, lnx_b)


def _merge_kernel(oa_ref, ob_ref, oc_ref, ga_ref, gb_ref, gc_ref, x_ref, wpa_ref, wpb_ref, wpc_ref, wout_ref,
                  o_ref):
    pa = _dot(oa_ref[...], wpa_ref[...])
    pb = _dot(ob_ref[...], wpb_ref[...])
    pc = _dot(oc_ref[...], wpc_ref[...])
    sig = lambda ref: _sigmoid(ref[...].astype(F32))
    merged = sig(ga_ref) * pa + sig(gb_ref) * pb + sig(gc_ref) * pc
    o_ref[...] = x_ref[...] + _dot(merged.astype(BF16), wout_ref[...])


def _merge(oa, ob, oc, p, x2, wpa, wpb, wpc, wout, batch, seq, t):
    n = x2.shape[0]
    ns = seq // t
    row = lambda b, i: (b * ns + i, 0)
    const2 = lambda b, i: (0, 0)
    gate = lambda g: pl.BlockSpec((t, D_MODEL), lambda b, i: (b * ns + i, COL_MERGE // D_MODEL + g))
    return pl.pallas_call(
        _merge_kernel,
        grid=(batch, ns),
        in_specs=[
            pl.BlockSpec((t, W_BR), row),
            pl.BlockSpec((t, W_BR), row),
            pl.BlockSpec((t, W_BR), row),
            gate(0), gate(1), gate(2),
            pl.BlockSpec((t, D_MODEL), row),
            pl.BlockSpec((W_BR, D_MODEL), const2),
            pl.BlockSpec((W_BR, D_MODEL), const2),
            pl.BlockSpec((W_BR, D_MODEL), const2),
            pl.BlockSpec((D_MODEL, D_MODEL), const2),
        ],
        out_specs=pl.BlockSpec((t, D_MODEL), row),
        out_shape=jax.ShapeDtypeStruct((n, D_MODEL), F32),
        compiler_params=_cparams(("parallel", "parallel")),
        name="merge",
    )(oa, ob, oc, p, p, p, x2, wpa, wpb, wpc, wout)


def _regroup_w_in(w_in, l):
    d = w_in.shape[1]
    sizes = (Q_LORA, KV_LORA, ROPE_D, W_BR, W_BR, W_BR, W_BR, H, W_BR,
             3 * W_BR + DECAY_LORA + AAA_LORA, W_BR, 3 * D_MODEL)
    offs = [int(o) for o in np.concatenate([[0], np.cumsum(sizes)])]
    (c_q, c_kv, k_r, gate_a, fq, fk, fv, ff, gate_b, shift, gate_c, merge) = [
        w_in[l, :, offs[i]:offs[i + 1]].astype(BF16) for i in range(len(sizes))]
    z = lambda n: jnp.zeros((d, n), BF16)
    wide = jnp.concatenate([
        merge, gate_a, fq, fk, fv, gate_b,
        shift[:, :W_BR], shift[:, W_BR:2 * W_BR], shift[:, 2 * W_BR:3 * W_BR], gate_c,
    ], axis=1)
    narrow = jnp.concatenate([
        c_q, c_kv,
        z(NOPE), k_r, z(LANES - NOPE - ROPE_D),
        ff, z(LANES - H),
        shift[:, 3 * W_BR:],
    ], axis=1)
    return wide, narrow


def _rope_tables(seq):
    half = ROPE_D // 2
    inv = ROPE_THETA ** (-jnp.arange(0, ROPE_D, 2, dtype=F32) / ROPE_D)
    ang = jnp.arange(seq, dtype=F32)[:, None] * inv[None, :]
    cos, sin = jnp.cos(ang), jnp.sin(ang)
    z = lambda n: jnp.zeros((seq, n), F32)
    c = jnp.concatenate([jnp.ones((seq, NOPE), F32), cos, cos, z(LANES - NOPE - ROPE_D)], axis=1)
    s1 = jnp.concatenate([z(NOPE), -sin, z(LANES - NOPE - half)], axis=1)
    s2 = jnp.concatenate([z(NOPE + half), sin, z(LANES - NOPE - ROPE_D)], axis=1)
    return c, s1, s2


def _pad_lanes(v, lo, total=LANES):
    return jnp.zeros((1, total), F32).at[0, lo:lo + v.shape[0]].set(v)


def _fox_selectors():
    selq = np.zeros((LANES, H * LANES), np.float32)
    selk = np.zeros((LANES, H * LANES), np.float32)
    oneq = np.zeros((1, H * LANES), np.float32)
    onek = np.zeros((1, H * LANES), np.float32)
    for h in range(H):
        for j in range(3):
            selq[j * H + h, h * LANES + HD + j] = 1.0
            selk[j * H + h, h * LANES + HD + 3 + j] = -1.0
            oneq[0, h * LANES + HD + 3 + j] = 1.0
            onek[0, h * LANES + HD + j] = 1.0
    return (jnp.asarray(selq, BF16), jnp.asarray(selk, BF16), jnp.asarray(oneq), jnp.asarray(onek))


def _tri(t):
    r = np.arange(t)
    return jnp.asarray((r[None, :] <= r[:, None]).astype(np.float32), BF16)


def _chunk_tri(t):
    r = np.arange(t)
    same = (r[None, :] // CHUNK) == (r[:, None] // CHUNK)
    lower = r[None, :] <= r[:, None]
    return (jnp.asarray((same & lower).astype(np.float32), BF16), jnp.asarray(same.astype(np.float32), BF16))


def _tile(seq, pref):
    return pref if seq % pref == 0 else seq


RWKV_TC = 256


def kernel(x, norm_g, w_in, mla_qa_g, mla_w_uq, mla_kva_g, mla_w_ukv, mla_q_g, mla_knope_g, mla_krope_g, fox_b_f, fox_q_g, fox_k_g, rwkv_mu, rwkv_w0, rwkv_w_up, rwkv_a0, rwkv_a_up, rwkv_k_k, rwkv_k_a, rwkv_r_k, rwkv_lnx_g, rwkv_lnx_b, rwkv_v0, rwkv_v_down, rwkv_v_up, w_pa, w_pb, w_pc, w_out):
    batch, seq, d = x.shape
    depth = w_in.shape[0]
    n = batch * seq
    t_row = _tile(seq, 512)
    t_in = _tile(n, 1024)
    t_rwkv = _tile(seq, 256)
    x2 = x.reshape(n, d)

    tabs = _rope_tables(seq)
    selq, selk, oneq, onek = _fox_selectors()
    tri = _tri(t_row)
    eye_br = jnp.asarray(np.eye(W_BR, dtype=np.float32), BF16)
    btri, bones = _chunk_tri(t_rwkv)
    row = lambda v: v.reshape(1, -1).astype(F32)

    vfirst = None
    for l in range(depth):
        w_wide, w_narrow = _regroup_w_in(w_in, l)
        pw = _inproj(x2, row(norm_g[l]), w_wide, t_in, INPROJ_TN, BF16)
        pn = _inproj(x2, row(norm_g[l]), w_narrow, t_in, NCOLS_NARROW, F32)

        wuq = mla_w_uq[l].reshape(Q_LORA, H, NOPE + ROPE_D)
        wuq = jnp.pad(wuq, ((0, 0), (0, 0), (0, LANES - NOPE - ROPE_D))).reshape(Q_LORA, H * LANES).astype(BF16)
        wukv = mla_w_ukv[l].reshape(KV_LORA, H, NOPE + HD)
        wk = jnp.pad(wukv[:, :, :NOPE], ((0, 0), (0, 0), (0, LANES - NOPE))).reshape(KV_LORA, H * LANES).astype(BF16)
        wv = wukv[:, :, NOPE:].reshape(KV_LORA, W_BR).T.astype(BF16)
        mla_scale = float(NOPE + ROPE_D) ** -0.5 * LOG2E
        mla_wts = (row(mla_qa_g[l]), wuq, row(mla_kva_g[l]), wk, wv,
                   _pad_lanes(mla_q_g[l] * mla_scale, 0), _pad_lanes(mla_knope_g[l], 0),
                   _pad_lanes(mla_krope_g[l], NOPE))
        qa, ka, va = _mla_prep(pn, tabs, mla_wts, seq, t_row)
        o_a = _attention(qa, ka, va, pw, COL_GATE_A, batch, seq, t_row)

        fox_wts = (_pad_lanes(fox_b_f[l], 0), row(jnp.tile(fox_q_g[l] * (float(HD) ** -0.5 * LOG2E), H)),
                   row(jnp.tile(fox_k_g[l], H)), tri, selq, selk, oneq, onek, eye_br)
        qb, kb, vb = _fox_prep(pw, pn, fox_wts, batch, seq, t_row)
        o_b = _attention(qb, kb, vb, pw, COL_GATE_B, batch, seq, t_row)

        mu = rwkv_mu[l]
        wup = jnp.zeros((LANES, W_BR), F32).at[:DECAY_LORA].set(rwkv_w_up[l]).astype(BF16)
        aup = jnp.zeros((LANES, W_BR), F32).at[DECAY_LORA:].set(rwkv_a_up[l]).astype(BF16)
        rw = [row(mu[:W_BR]), row(mu[W_BR:2 * W_BR]), row(mu[2 * W_BR:3 * W_BR]), row(mu[3 * W_BR:]),
              row(rwkv_w0[l]), row(rwkv_a0[l]), wup, aup, row(rwkv_k_k[l]), row(rwkv_k_a[l]),
              row(rwkv_r_k[l])]
        if l > 0:
            vdown = jnp.zeros((W_BR, LANES), F32).at[:, :MV_LORA].set(rwkv_v_down[l - 1]).astype(BF16)
            vup = jnp.zeros((LANES, W_BR), F32).at[:MV_LORA].set(rwkv_v_up[l - 1]).astype(BF16)
            rw += [row(rwkv_v0[l - 1]), vdown, vup]
        rw += [btri, bones]
        outs = _rwkv_prep(pw, pn, vfirst, rw, batch, seq, t_rwkv, first_layer=(l == 0))
        if l == 0:
            vfirst = outs[10]
        o_c = _rwkv_chunk(outs[:10], row(rwkv_lnx_g[l]), row(rwkv_lnx_b[l]), batch, seq, _tile(seq, RWKV_TC))

        x2 = _merge(o_a, o_b, o_c, pw, x2, w_pa[l].astype(BF16), w_pb[l].astype(BF16),
                    w_pc[l].astype(BF16), w_out[l].astype(BF16), batch, seq, t_row)
    return x2.reshape(batch, seq, d)
```

```python
import functools
import math

import jax
import jax.numpy as jnp
import numpy as np
from jax import lax
from jax.experimental import pallas as pl
from jax.experimental.pallas import tpu as pltpu

F32 = jnp.float32
BF16 = jnp.bfloat16

LANES = 128
H = 8
HD = 64
NOPE = 64
ROPE_D = 32
Q_LORA = 256
KV_LORA = 128
D_MODEL = 1024
W_BR = H * HD
DECAY_LORA = 64
AAA_LORA = 64
MV_LORA = 32
ROPE_THETA = 10000.0
RWKV_DECAY_SCALE = 0.606531
GN_EPS = 64e-5
EPS = 1e-6
NEG_INF = -1e30
LOG2E = math.log2(math.e)
CHUNK = 64
ATTN_QT = 4
ONES_ROWS = 16

COL_MERGE = 0
COL_GATE_A = 3072
COL_FQ = 3584
COL_FK = 4096
COL_FV = 4608
COL_GATE_B = 5120
COL_SR = 5632
COL_SK = 6144
COL_SV = 6656
COL_GATE_C = 7168
NCOLS_WIDE = 7680
INPROJ_TN = 2560
COL_CQ = 0
COL_CKV = 256
COL_KR = 384
COL_FF = 512
COL_WA = 640
NCOLS_NARROW = 768

VMEM_LIMIT = 56 * 1024 * 1024


def _cparams(sem):
    return pltpu.CompilerParams(dimension_semantics=sem, vmem_limit_bytes=VMEM_LIMIT)


def _sigmoid(x):
    return 1.0 / (1.0 + jnp.exp(-x))


def _dot(a, b):
    return jnp.dot(a, b, preferred_element_type=F32)


def _split3(x):
    hi = x.astype(BF16)
    r1 = x - hi.astype(F32)
    mid = r1.astype(BF16)
    lo = (r1 - mid.astype(F32)).astype(BF16)
    return hi, mid, lo


def _dot_exact_lhs(m_bf16, x):
    hi, mid, lo = _split3(x)
    return _dot(m_bf16, hi) + _dot(m_bf16, mid) + _dot(m_bf16, lo)


def _pair_seg_sum(x):
    lane = lax.broadcasted_iota(jnp.int32, x.shape, 1)
    first = lane < HD
    s0 = jnp.sum(jnp.where(first, x, 0.0), axis=-1, keepdims=True)
    s1 = jnp.sum(jnp.where(first, 0.0, x), axis=-1, keepdims=True)
    return jnp.where(first, s0, s1)


def _inproj_kernel(x_ref, g_ref, w_ref, o_ref, h_ref):
    @pl.when(pl.program_id(1) == 0)
    def _():
        x = x_ref[...]
        ms = jnp.mean(x * x, axis=-1, keepdims=True)
        h_ref[...] = (x * lax.rsqrt(ms + EPS) * g_ref[...]).astype(BF16)

    o_ref[...] = _dot(h_ref[...], w_ref[...]).astype(o_ref.dtype)


def _inproj(x2, g, w, tm, tn, out_dtype):
    n, ncols = x2.shape[0], w.shape[1]
    return pl.pallas_call(
        _inproj_kernel,
        grid=(n // tm, ncols // tn),
        in_specs=[
            pl.BlockSpec((tm, D_MODEL), lambda i, j: (i, 0)),
            pl.BlockSpec((1, D_MODEL), lambda i, j: (0, 0)),
            pl.BlockSpec((D_MODEL, tn), lambda i, j: (0, j)),
        ],
        out_specs=pl.BlockSpec((tm, tn), lambda i, j: (i, j)),
        out_shape=jax.ShapeDtypeStruct((n, ncols), out_dtype),
        scratch_shapes=[pltpu.VMEM((tm, D_MODEL), BF16)],
        compiler_params=_cparams(("parallel", "arbitrary")),
        name="inproj",
    )(x2, g, w)


def _rms(x, g, n):
    ms = jnp.sum(x * x, axis=-1, keepdims=True) / n
    return x * lax.rsqrt(ms + EPS) * g


def _rope(x, c, s1, s2):
    return x * c + pltpu.roll(x, LANES - ROPE_D // 2, 1) * s1 + pltpu.roll(x, ROPE_D // 2, 1) * s2


def _mla_prep_kernel(cq_ref, ckv_ref, kr_ref, c_ref, s1_ref, s2_ref, qag_ref, wuq_ref, kvag_ref, wk_ref, wv_ref,
                     qg_ref, kng_ref, krg_ref, q_out, k_out, vt_out):
    c, s1, s2 = c_ref[...], s1_ref[...], s2_ref[...]
    cqn = _rms(cq_ref[...], qag_ref[...], Q_LORA).astype(BF16)
    ckvn = _rms(ckv_ref[...], kvag_ref[...], KV_LORA).astype(BF16)
    q_all = _dot(cqn, wuq_ref[...])
    k_all = _dot(ckvn, wk_ref[...])
    vt_out[0, 0] = lax.dot_general(wv_ref[...], ckvn, _NT, preferred_element_type=F32).astype(BF16)
    k_rope = _rope(_rms(kr_ref[...], krg_ref[...], ROPE_D), c, s1, s2)
    qg, kng = qg_ref[...], kng_ref[...]
    for h in range(H):
        sl = slice(h * LANES, (h + 1) * LANES)
        q_out[:, sl] = _rope(_rms(q_all[:, sl], qg, NOPE + ROPE_D), c, s1, s2).astype(BF16)
        k_out[:, sl] = (_rms(k_all[:, sl], kng, NOPE) + k_rope).astype(BF16)


def _mla_prep(pn, tabs, wts, seq, t):
    n = pn.shape[0]
    ns = seq // t
    row = lambda i: (i, 0)
    const = lambda i: (0, 0)
    tab_spec = pl.BlockSpec((t, LANES), lambda i: (i % ns, 0))
    return pl.pallas_call(
        _mla_prep_kernel,
        grid=(n // t,),
        in_specs=[
            pl.BlockSpec((t, Q_LORA), lambda i: (i, COL_CQ // Q_LORA)),
            pl.BlockSpec((t, LANES), lambda i: (i, COL_CKV // LANES)),
            pl.BlockSpec((t, LANES), lambda i: (i, COL_KR // LANES)),
            tab_spec, tab_spec, tab_spec,
            pl.BlockSpec((1, Q_LORA), const),
            pl.BlockSpec((Q_LORA, H * LANES), const),
            pl.BlockSpec((1, KV_LORA), const),
            pl.BlockSpec((KV_LORA, H * LANES), const),
            pl.BlockSpec((W_BR, KV_LORA), const),
            pl.BlockSpec((1, LANES), const),
            pl.BlockSpec((1, LANES), const),
            pl.BlockSpec((1, LANES), const),
        ],
        out_specs=[
            pl.BlockSpec((t, H * LANES), row),
            pl.BlockSpec((t, H * LANES), row),
            pl.BlockSpec((1, 1, W_BR, t), lambda i: (i // ns, i % ns, 0, 0)),
        ],
        out_shape=[
            jax.ShapeDtypeStruct((n, H * LANES), BF16),
            jax.ShapeDtypeStruct((n, H * LANES), BF16),
            jax.ShapeDtypeStruct((n // seq, ns, W_BR, t), BF16),
        ],
        compiler_params=_cparams(("parallel",)),
        name="mla_prep",
    )(pn, pn, pn, *tabs, *wts)


def _fox_prep_kernel(fq_ref, fk_ref, fv_ref, ff_ref, bf_ref, qg_ref, kg_ref, tri_ref, selq_ref, selk_ref,
                     oneq_ref, onek_ref, q_out, k_out, vt_out, carry_ref):
    t = fq_ref.shape[0]

    @pl.when(pl.program_id(1) == 0)
    def _():
        carry_ref[...] = jnp.zeros_like(carry_ref)

    z = ff_ref[...] + bf_ref[...]
    logf = jnp.minimum(z, 0.0) - jnp.log1p(jnp.exp(-jnp.abs(z)))
    cum = carry_ref[...] + _dot_exact_lhs(tri_ref[...], logf)
    carry_ref[...] = cum[t - 1:t, :]
    hi, mid, lo = _split3(cum * LOG2E)
    aug_q = _dot(hi, selq_ref[0]) + _dot(mid, selq_ref[1]) + _dot(lo, selq_ref[2]) + oneq_ref[...]
    aug_k = _dot(hi, selk_ref[0]) + _dot(mid, selk_ref[1]) + _dot(lo, selk_ref[2]) + onek_ref[...]
    vt_out[0, 0] = fv_ref[...].astype(F32).T.astype(BF16)
    lane = lax.broadcasted_iota(jnp.int32, (t, LANES), 1)
    first = lane < HD
    for src_ref, g_ref, aug, dst in ((fq_ref, qg_ref, aug_q, q_out), (fk_ref, kg_ref, aug_k, k_out)):
        for p in range(H // 2):
            sl = slice(p * LANES, (p + 1) * LANES)
            x = src_ref[:, sl].astype(F32)
            ms = _pair_seg_sum(x * x) / HD
            xn = x * lax.rsqrt(ms + EPS) * g_ref[:, sl]
            he, ho = 2 * p, 2 * p + 1
            dst[:, he * LANES:(he + 1) * LANES] = jnp.where(first, xn, aug[:, he * LANES:(he + 1) * LANES]).astype(BF16)
            dst[:, ho * LANES:(ho + 1) * LANES] = jnp.where(
                first, pltpu.roll(xn, HD, 1), aug[:, ho * LANES:(ho + 1) * LANES]).astype(BF16)


def _fox_prep(pw, pn, wts, batch, seq, t):
    n = pw.shape[0]
    ns = seq // t
    row = lambda b, i: (b * ns + i, 0)
    const2 = lambda b, i: (0, 0)
    const3 = lambda b, i: (0, 0, 0)
    return pl.pallas_call(
        _fox_prep_kernel,
        grid=(batch, ns),
        in_specs=[
            pl.BlockSpec((t, W_BR), lambda b, i: (b * ns + i, COL_FQ // W_BR)),
            pl.BlockSpec((t, W_BR), lambda b, i: (b * ns + i, COL_FK // W_BR)),
            pl.BlockSpec((t, W_BR), lambda b, i: (b * ns + i, COL_FV // W_BR)),
            pl.BlockSpec((t, LANES), lambda b, i: (b * ns + i, COL_FF // LANES)),
            pl.BlockSpec((1, LANES), const2),
            pl.BlockSpec((1, W_BR), const2),
            pl.BlockSpec((1, W_BR), const2),
            pl.BlockSpec((t, t), const2),
            pl.BlockSpec((3, LANES, H * LANES), const3),
            pl.BlockSpec((3, LANES, H * LANES), const3),
            pl.BlockSpec((1, H * LANES), const2),
            pl.BlockSpec((1, H * LANES), const2),
        ],
        out_specs=[
            pl.BlockSpec((t, H * LANES), row),
            pl.BlockSpec((t, H * LANES), row),
            pl.BlockSpec((1, 1, W_BR, t), lambda b, i: (b, i, 0, 0)),
        ],
        out_shape=[
            jax.ShapeDtypeStruct((n, H * LANES), BF16),
            jax.ShapeDtypeStruct((n, H * LANES), BF16),
            jax.ShapeDtypeStruct((batch, ns, W_BR, t), BF16),
        ],
        scratch_shapes=[pltpu.VMEM((1, LANES), F32)],
        compiler_params=_cparams(("parallel", "arbitrary")),
        name="fox_prep",
    )(pw, pw, pw, pn, *wts)


def _attn_kernel(q_ref, k_ref, vt_ref, g_ref, o_ref, m_ref, acc_ref, sa_ref, sb_ref, mxa_ref, mxb_ref, *, tk):
    tq = ATTN_QT * tk
    gi = pl.program_id(2)
    nt = (((1,), (1,)), ((), ()))
    kv_pos = lax.broadcasted_iota(jnp.int32, (tk, tq), 0)
    q_pos = lax.broadcasted_iota(jnp.int32, (tk, tq), 1)
    causal = kv_pos <= q_pos
    both = slice(0, tq)
    ones = jnp.ones((ONES_ROWS, tk), BF16)
    m_ref[...] = jnp.full(m_ref.shape, NEG_INF, F32)
    acc_ref[...] = jnp.zeros(acc_ref.shape, F32)

    def scores(c, s_ref, mx_ref, cols):
        start = pl.multiple_of(c * tk, tk)
        for j in range(2):
            q = q_ref[cols, j * LANES:(j + 1) * LANES]
            k = k_ref[pl.ds(start, tk), j * LANES:(j + 1) * LANES]
            s = lax.dot_general(k, q, nt, preferred_element_type=F32)
            s_ref[j, :, cols] = s
            mx_ref[j, :, cols] = jnp.max(s, axis=0, keepdims=True)

    def consume(c, s_ref, mx_ref, cols, mask):
        for j in range(2):
            vt = jnp.concatenate([vt_ref[0, c, j * HD:(j + 1) * HD, :], ones], axis=0)
            s = s_ref[j, :, cols]
            if mask is not None:
                s = jnp.where(mask, s, NEG_INF)
                m_cur = jnp.max(s, axis=0, keepdims=True)
            else:
                m_cur = mx_ref[j, :, cols]
            m_old = m_ref[j, :, cols]
            m_new = jnp.maximum(m_old, m_cur)
            alpha = jnp.exp2(m_old - m_new)
            p = jnp.exp2(s - m_new)
            acc_ref[j, :, cols] = alpha * acc_ref[j, :, cols] + _dot(vt, p.astype(BF16))
            m_ref[j, :, cols] = m_new

    scores(0, sa_ref, mxa_ref, both)

    def body(i, carry):
        c = 2 * i
        scores(c + 1, sb_ref, mxb_ref, both)
        consume(c, sa_ref, mxa_ref, both, None)
        scores(c + 2, sa_ref, mxa_ref, both)
        consume(c + 1, sb_ref, mxb_ref, both, None)
        return carry

    lax.fori_loop(0, gi * (ATTN_QT // 2), body, 0)
    bufs = ((sa_ref, mxa_ref), (sb_ref, mxb_ref))
    for d in range(ATTN_QT):
        c = ATTN_QT * gi + d
        if d + 1 < ATTN_QT:
            scores(c + 1, *bufs[(d + 1) % 2], slice((d + 1) * tk, tq))
        consume(c, *bufs[d % 2], slice(d * tk, tq), causal[:, :tq - d * tk])

    o_t = jnp.concatenate([acc_ref[j, :HD, :] / acc_ref[j, HD:HD + 1, :] for j in range(2)], axis=0)
    g = g_ref[...].astype(F32)
    o_ref[...] = (o_t.T * (g * _sigmoid(g))).astype(BF16)


def _attention(q, k, vt, pw, gate_col, batch, seq, tk):
    n = q.shape[0]
    tq = ATTN_QT * tk
    nq, nk = seq // tq, seq // tk
    return pl.pallas_call(
        functools.partial(_attn_kernel, tk=tk),
        grid=(batch, H // 2, nq),
        in_specs=[
            pl.BlockSpec((tq, 2 * LANES), lambda b, hp, i: (b * nq + i, hp)),
            pl.BlockSpec((seq, 2 * LANES), lambda b, hp, i: (b, hp)),
            pl.BlockSpec((1, nk, LANES, tk), lambda b, hp, i: (b, 0, hp, 0)),
            pl.BlockSpec((tq, LANES), lambda b, hp, i: (b * nq + i, gate_col // LANES + hp)),
        ],
        out_specs=pl.BlockSpec((tq, LANES), lambda b, hp, i: (b * nq + i, hp)),
        out_shape=jax.ShapeDtypeStruct((n, W_BR), BF16),
        scratch_shapes=[
            pltpu.VMEM((2, 1, tq), F32),
            pltpu.VMEM((2, HD + ONES_ROWS, tq), F32),
            pltpu.VMEM((2, tk, tq), F32),
            pltpu.VMEM((2, tk, tq), F32),
            pltpu.VMEM((2, 1, tq), F32),
            pltpu.VMEM((2, 1, tq), F32),
        ],
        compiler_params=_cparams(("parallel", "parallel", "arbitrary")),
        name="attention",
    )(q, k, vt, pw)


def _rwkv_prep_kernel(*refs, first_layer):
    if first_layer:
        (sr_ref, sk_ref, sv_ref, swa_ref, gc_ref, mur_ref, muk_ref, muv_ref, muwa_ref, w0_ref, a0_ref, wup_ref,
         aup_ref, kk_ref, ka_ref, rk_ref, bt_ref, bo_ref,
         at_o, rt_o, bt_o, kt_o, v_o, bh_o, kh_o, wc_o, bonus_o, sg_o, vfirst_o,
         cr_ref, ck_ref, cv_ref, cwa_ref) = refs
    else:
        (sr_ref, sk_ref, sv_ref, swa_ref, gc_ref, vf_ref, mur_ref, muk_ref, muv_ref, muwa_ref, w0_ref, a0_ref,
         wup_ref, aup_ref, kk_ref, ka_ref, rk_ref, v0_ref, vdown_ref, vup_ref, bt_ref, bo_ref,
         at_o, rt_o, bt_o, kt_o, v_o, bh_o, kh_o, wc_o, bonus_o, sg_o,
         cr_ref, ck_ref, cv_ref, cwa_ref) = refs
    t = sr_ref.shape[0]

    @pl.when(pl.program_id(1) == 0)
    def _():
        for c_ref in (cr_ref, ck_ref, cv_ref, cwa_ref):
            c_ref[...] = jnp.zeros_like(c_ref)

    def shift(x_ref, c_ref, mu_ref):
        x = x_ref[...].astype(F32)
        rowid = lax.broadcasted_iota(jnp.int32, x.shape, 0)
        prev = jnp.where(rowid == 0, c_ref[...], pltpu.roll(x, 1, 0))
        c_ref[...] = x[t - 1:t, :]
        return x + mu_ref[...] * (prev - x)

    r = shift(sr_ref, cr_ref, mur_ref)
    k = shift(sk_ref, ck_ref, muk_ref)
    v = shift(sv_ref, cv_ref, muv_ref)
    wa = shift(swa_ref, cwa_ref, muwa_ref)

    logw = -RWKV_DECAY_SCALE * _sigmoid(w0_ref[...] + _dot(jnp.tanh(wa).astype(BF16), wup_ref[...]))
    a = _sigmoid(a0_ref[...] + _dot(wa.astype(BF16), aup_ref[...]))
    if first_layer:
        vfirst_o[...] = v
    else:
        low = _dot(v.astype(BF16), vdown_ref[...]).astype(BF16)
        nu = _sigmoid(v0_ref[...] + _dot(low, vup_ref[...]))
        v = v + (vf_ref[...] - v) * nu

    kk = k * kk_ref[...]
    k_mod = k * (1.0 + (a - 1.0) * ka_ref[...])
    rk = r * k_mod * rk_ref[...]
    kk_n, bonus = [], []
    for p in range(H // 2):
        sl = slice(p * LANES, (p + 1) * LANES)
        nrm = jnp.sqrt(_pair_seg_sum(kk[:, sl] * kk[:, sl]))
        kk_n.append(kk[:, sl] / jnp.maximum(nrm, 1e-12))
        bonus.append(_pair_seg_sum(rk[:, sl]) * v[:, sl])
    kk = jnp.concatenate(kk_n, axis=-1)
    bonus = jnp.concatenate(bonus, axis=-1)
    a_vec = -kk
    b_vec = kk * a

    lcum = _dot_exact_lhs(bt_ref[...], logw)
    ltot = _dot_exact_lhs(bo_ref[...], logw)
    e_minus = jnp.exp(-lcum)
    e_rem = jnp.exp(ltot - lcum)
    g = gc_ref[...].astype(F32)
    for o_ref, val in ((at_o, a_vec * jnp.exp(lcum - logw)), (rt_o, r * jnp.exp(lcum)), (bt_o, b_vec * e_minus),
                       (kt_o, k_mod * e_minus), (v_o, v), (bh_o, b_vec * e_rem), (kh_o, k_mod * e_rem),
                       (wc_o, jnp.exp(ltot)), (bonus_o, bonus), (sg_o, g * _sigmoid(g))):
        o_ref[...] = val.astype(o_ref.dtype)


def _rwkv_prep(pw, pn, vfirst, wts, batch, seq, t, first_layer):
    n = pw.shape[0]
    ns = seq // t
    const2 = lambda b, i: (0, 0)
    pcol = lambda col, w: pl.BlockSpec((t, w), lambda b, i: (b * ns + i, col // w))
    vec = pl.BlockSpec((1, W_BR), const2)
    in_specs = [pcol(COL_SR, W_BR), pcol(COL_SK, W_BR), pcol(COL_SV, W_BR), pcol(COL_WA, LANES),
                pcol(COL_GATE_C, W_BR)]
    args = [pw, pw, pw, pn, pw]
    if not first_layer:
        in_specs.append(pl.BlockSpec((t, W_BR), lambda b, i: (b * ns + i, 0)))
        args.append(vfirst)
    in_specs += [vec, vec, vec, pl.BlockSpec((1, LANES), const2), vec, vec,
                 pl.BlockSpec((LANES, W_BR), const2), pl.BlockSpec((LANES, W_BR), const2), vec, vec, vec]
    if not first_layer:
        in_specs += [vec, pl.BlockSpec((W_BR, LANES), const2), pl.BlockSpec((LANES, W_BR), const2)]
    in_specs += [pl.BlockSpec((t, t), const2), pl.BlockSpec((t, t), const2)]
    args += list(wts)
    row_spec = pl.BlockSpec((t, W_BR), lambda b, i: (b * ns + i, 0))
    out_dtypes = (BF16, F32, BF16, BF16, BF16, BF16, BF16, F32, F32, F32) + ((F32,) if first_layer else ())
    out_specs = [row_spec] * len(out_dtypes)
    out_shape = [jax.ShapeDtypeStruct((n, W_BR), dt) for dt in out_dtypes]
    return pl.pallas_call(
        functools.partial(_rwkv_prep_kernel, first_layer=first_layer),
        grid=(batch, ns),
        in_specs=in_specs,
        out_specs=out_specs,
        out_shape=out_shape,
        scratch_shapes=[pltpu.VMEM((1, W_BR), F32), pltpu.VMEM((1, W_BR), F32), pltpu.VMEM((1, W_BR), F32),
                        pltpu.VMEM((1, LANES), F32)],
        compiler_params=_cparams(("parallel", "arbitrary")),
        name="rwkv_prep",
    )(*args)


def _mm(a, b, dims):
    return lax.dot_general(a.astype(BF16), b.astype(BF16), dims, preferred_element_type=F32)


_NN = (((1,), (0,)), ((), ()))
_NT = (((1,), (1,)), ((), ()))
_TN = (((0,), (0,)), ((), ()))


def _rwkv_chunk_kernel(at_ref, rt_ref, bt_ref, kt_ref, v_ref, bh_ref, kh_ref, wc_ref, bonus_ref, sg_ref,
                       lg_ref, lb_ref, o_ref, s_all_ref, *, nchunk):
    @pl.when(pl.program_id(1) == 0)
    def _():
        s_all_ref[...] = jnp.zeros_like(s_all_ref)

    mm = _mm
    first = lax.broadcasted_iota(jnp.int32, (CHUNK, LANES), 1) < HD
    r2 = lax.broadcasted_iota(jnp.int32, (LANES, LANES), 0)
    c2 = lax.broadcasted_iota(jnp.int32, (LANES, LANES), 1)
    same_head = (r2 < HD) == (c2 < HD)
    eye2 = r2 == c2
    t_row, t_col = r2 % CHUNK, c2 % CHUNK
    strict2 = jnp.logical_and(same_head, t_col < t_row)
    incl2 = jnp.logical_and(same_head, t_col <= t_row)
    each = lambda f, *cols: [f(*xs) for xs in zip(*cols)]
    pairs = [(c, p) for c in range(nchunk) for p in range(H // 2)]
    npair = range(len(pairs))
    ld = lambda ref: [ref[c * CHUNK:(c + 1) * CHUNK, p * LANES:(p + 1) * LANES] for c, p in pairs]
    at, rt, bt, kt, v, bh, kh = ld(at_ref), ld(rt_ref), ld(bt_ref), ld(kt_ref), ld(v_ref), ld(bh_ref), ld(kh_ref)
    zero = jnp.zeros((CHUNK, LANES), BF16)
    stack = lambda x: jnp.concatenate([jnp.where(first, x, zero), jnp.where(first, zero, x)], axis=0)
    fold = lambda x: x[:CHUNK] + x[CHUNK:]
    a_s, b_s, k_s, v_s = each(stack, at), each(stack, bt), each(stack, kt), each(stack, v)
    x1 = each(lambda a, r: jnp.concatenate([a, stack(r.astype(BF16))], axis=0), a_s, rt)
    gb = each(lambda x, b: mm(x, b, _NT), x1, b_s)
    gk = each(lambda x, k: mm(x, k, _NT), x1, k_s)
    a_ab = [jnp.where(strict2, g[:LANES], 0.0) for g in gb]
    a_rb = [jnp.where(incl2, g[LANES:], 0.0) for g in gb]
    a_ak = [jnp.where(strict2, g[:LANES], 0.0) for g in gk]
    a_rk = [jnp.where(incl2, g[LANES:], 0.0) for g in gk]
    tinv = [jnp.where(eye2, 1.0, a) for a in a_ab]
    pw = a_ab
    for _ in range(int(math.log2(CHUNK)) - 1):
        pw = each(lambda p: mm(p, p, _NN), pw)
        tinv = each(lambda t, p: t + mm(t, p, _NN), tinv, pw)
    av = each(lambda a, x: mm(a, x, _NN), a_ak, v_s)
    tx = each(lambda t, x, a: mm(t, jnp.concatenate([x.astype(BF16), a], axis=1), _NN), tinv, av, a_s)
    ry = each(lambda a, x: mm(a, x, _NN), a_rb, tx)
    yk = each(lambda a, x: mm(a, x, _NN), a_rk, v_s)
    u = [fold(t[:, :LANES]) for t in tx]
    a_til = [fold(t[:, LANES:]) for t in tx]
    y_in = [fold(r[:, :LANES] + k) for r, k in zip(ry, yk)]
    r_hat = [r + fold(x[:, LANES:]) for r, x in zip(rt, ry)]
    m_new = [jnp.where(same_head, mm(a_til[i], bh[i], _TN), 0.0) for i in npair]
    n_new = [jnp.where(same_head, mm(jnp.concatenate([u[i].astype(BF16), v[i]], axis=0),
                                     jnp.concatenate([bh[i], kh[i]], axis=0), _TN), 0.0) for i in npair]
    lg, lb = lg_ref[...], lb_ref[...]
    for c in range(nchunk):
        idx = [i for i, (ci, _) in enumerate(pairs) if ci == c]
        s0 = [s_all_ref[p] for p in range(H // 2)]
        y = [y_in[i] + mm(r_hat[i], s0[p], _NT) for p, i in enumerate(idx)]
        for p, i in enumerate(idx):
            wc = wc_ref[c * CHUNK:c * CHUNK + 1, p * LANES:(p + 1) * LANES]
            s_all_ref[p] = mm(s0[p], jnp.where(eye2, wc, 0.0) + m_new[i], _NN) + n_new[i]
        for p, i in enumerate(idx):
            rows, lanes = slice(c * CHUNK, (c + 1) * CHUNK), slice(p * LANES, (p + 1) * LANES)
            mu = _pair_seg_sum(y[p]) / HD
            d = y[p] - mu
            var = _pair_seg_sum(d * d) / HD
            yn = d * lax.rsqrt(var + GN_EPS) * lg[:, lanes] + lb[:, lanes]
            o_ref[rows, lanes] = ((yn + bonus_ref[rows, lanes]) * sg_ref[rows, lanes]).astype(o_ref.dtype)


def _rwkv_chunk(arrs, lnx_g, lnx_b, batch, seq, tc):
    n = batch * seq
    ns = seq // tc
    spec = pl.BlockSpec((tc, W_BR), lambda b, i: (b * ns + i, 0))
    gspec = pl.BlockSpec((1, W_BR), lambda b, i: (0, 0))
    return pl.pallas_call(
        functools.partial(_rwkv_chunk_kernel, nchunk=tc // CHUNK),
        grid=(batch, ns),
        in_specs=[spec] * 10 + [gspec, gspec],
        out_specs=spec,
        out_shape=jax.ShapeDtypeStruct((n, W_BR), BF16),
        scratch_shapes=[pltpu.VMEM((H // 2, LANES, LANES), F32)],
        compiler_params=_cparams(("parallel", "arbitrary")),
        name="rwkv_chunk",
    )(*arrs, lnx_g, lnx_b)


def _merge_kernel(oa_ref, ob_ref, oc_ref, ga_ref, gb_ref, gc_ref, x_ref, wpa_ref, wpb_ref, wpc_ref, wout_ref,
                  o_ref):
    pa = _dot(oa_ref[...], wpa_ref[...])
    pb = _dot(ob_ref[...], wpb_ref[...])
    pc = _dot(oc_ref[...], wpc_ref[...])
    sig = lambda ref: _sigmoid(ref[...].astype(F32))
    merged = sig(ga_ref) * pa + sig(gb_ref) * pb + sig(gc_ref) * pc
    o_ref[...] = x_ref[...] + _dot(merged.astype(BF16), wout_ref[...])


def _merge(oa, ob, oc, p, x2, wpa, wpb, wpc, wout, batch, seq, t):
    n = x2.shape[0]
    ns = seq // t
    row = lambda b, i: (b * ns + i, 0)
    const2 = lambda b, i: (0, 0)
    gate = lambda g: pl.BlockSpec((t, D_MODEL), lambda b, i: (b * ns + i, COL_MERGE // D_MODEL + g))
    return pl.pallas_call(
        _merge_kernel,
        grid=(batch, ns),
        in_specs=[
            pl.BlockSpec((t, W_BR), row),
            pl.BlockSpec((t, W_BR), row),
            pl.BlockSpec((t, W_BR), row),
            gate(0), gate(1), gate(2),
            pl.BlockSpec((t, D_MODEL), row),
            pl.BlockSpec((W_BR, D_MODEL), const2),
            pl.BlockSpec((W_BR, D_MODEL), const2),
            pl.BlockSpec((W_BR, D_MODEL), const2),
            pl.BlockSpec((D_MODEL, D_MODEL), const2),
        ],
        out_specs=pl.BlockSpec((t, D_MODEL), row),
        out_shape=jax.ShapeDtypeStruct((n, D_MODEL), F32),
        compiler_params=_cparams(("parallel", "parallel")),
        name="merge",
    )(oa, ob, oc, p, p, p, x2, wpa, wpb, wpc, wout)


def _regroup_w_in(w_in, l):
    d = w_in.shape[1]
    sizes = (Q_LORA, KV_LORA, ROPE_D, W_BR, W_BR, W_BR, W_BR, H, W_BR,
             3 * W_BR + DECAY_LORA + AAA_LORA, W_BR, 3 * D_MODEL)
    offs = [int(o) for o in np.concatenate([[0], np.cumsum(sizes)])]
    (c_q, c_kv, k_r, gate_a, fq, fk, fv, ff, gate_b, shift, gate_c, merge) = [
        w_in[l, :, offs[i]:offs[i + 1]].astype(BF16) for i in range(len(sizes))]
    z = lambda n: jnp.zeros((d, n), BF16)
    wide = jnp.concatenate([
        merge, gate_a, fq, fk, fv, gate_b,
        shift[:, :W_BR], shift[:, W_BR:2 * W_BR], shift[:, 2 * W_BR:3 * W_BR], gate_c,
    ], axis=1)
    narrow = jnp.concatenate([
        c_q, c_kv,
        z(NOPE), k_r, z(LANES - NOPE - ROPE_D),
        ff, z(LANES - H),
        shift[:, 3 * W_BR:],
    ], axis=1)
    return wide, narrow


def _rope_tables(seq):
    half = ROPE_D // 2
    inv = ROPE_THETA ** (-jnp.arange(0, ROPE_D, 2, dtype=F32) / ROPE_D)
    ang = jnp.arange(seq, dtype=F32)[:, None] * inv[None, :]
    cos, sin = jnp.cos(ang), jnp.sin(ang)
    z = lambda n: jnp.zeros((seq, n), F32)
    c = jnp.concatenate([jnp.ones((seq, NOPE), F32), cos, cos, z(LANES - NOPE - ROPE_D)], axis=1)
    s1 = jnp.concatenate([z(NOPE), -sin, z(LANES - NOPE - half)], axis=1)
    s2 = jnp.concatenate([z(NOPE + half), sin, z(LANES - NOPE - ROPE_D)], axis=1)
    return c, s1, s2


def _pad_lanes(v, lo, total=LANES):
    return jnp.zeros((1, total), F32).at[0, lo:lo + v.shape[0]].set(v)


def _fox_selectors():
    selq = np.zeros((3, LANES, H * LANES), np.float32)
    selk = np.zeros((3, LANES, H * LANES), np.float32)
    oneq = np.zeros((1, H * LANES), np.float32)
    onek = np.zeros((1, H * LANES), np.float32)
    for h in range(H):
        for j in range(3):
            selq[j, h, h * LANES + HD + j] = 1.0
            selk[j, h, h * LANES + HD + 3 + j] = -1.0
            oneq[0, h * LANES + HD + 3 + j] = 1.0
            onek[0, h * LANES + HD + j] = 1.0
    return (jnp.asarray(selq, BF16), jnp.asarray(selk, BF16), jnp.asarray(oneq), jnp.asarray(onek))


def _tri(t):
    r = np.arange(t)
    return jnp.asarray((r[None, :] <= r[:, None]).astype(np.float32), BF16)


def _chunk_tri(t):
    r = np.arange(t)
    same = (r[None, :] // CHUNK) == (r[:, None] // CHUNK)
    lower = r[None, :] <= r[:, None]
    return (jnp.asarray((same & lower).astype(np.float32), BF16), jnp.asarray(same.astype(np.float32), BF16))


def _tile(seq, pref):
    return pref if seq % pref == 0 else seq


RWKV_TC = 256


def kernel(x, norm_g, w_in, mla_qa_g, mla_w_uq, mla_kva_g, mla_w_ukv, mla_q_g, mla_knope_g, mla_krope_g, fox_b_f, fox_q_g, fox_k_g, rwkv_mu, rwkv_w0, rwkv_w_up, rwkv_a0, rwkv_a_up, rwkv_k_k, rwkv_k_a, rwkv_r_k, rwkv_lnx_g, rwkv_lnx_b, rwkv_v0, rwkv_v_down, rwkv_v_up, w_pa, w_pb, w_pc, w_out):
    batch, seq, d = x.shape
    depth = w_in.shape[0]
    n = batch * seq
    t_row = _tile(seq, 512)
    t_in = _tile(n, 1024)
    t_rwkv = _tile(seq, 256)
    x2 = x.reshape(n, d)

    tabs = _rope_tables(seq)
    selq, selk, oneq, onek = _fox_selectors()
    tri = _tri(t_row)
    btri, bones = _chunk_tri(t_rwkv)
    row = lambda v: v.reshape(1, -1).astype(F32)

    vfirst = None
    for l in range(depth):
        w_wide, w_narrow = _regroup_w_in(w_in, l)
        pw = _inproj(x2, row(norm_g[l]), w_wide, t_in, INPROJ_TN, BF16)
        pn = _inproj(x2, row(norm_g[l]), w_narrow, t_in, NCOLS_NARROW, F32)

        wuq = mla_w_uq[l].reshape(Q_LORA, H, NOPE + ROPE_D)
        wuq = jnp.pad(wuq, ((0, 0), (0, 0), (0, LANES - NOPE - ROPE_D))).reshape(Q_LORA, H * LANES).astype(BF16)
        wukv = mla_w_ukv[l].reshape(KV_LORA, H, NOPE + HD)
        wk = jnp.pad(wukv[:, :, :NOPE], ((0, 0), (0, 0), (0, LANES - NOPE))).reshape(KV_LORA, H * LANES).astype(BF16)
        wv = wukv[:, :, NOPE:].reshape(KV_LORA, W_BR).T.astype(BF16)
        mla_scale = float(NOPE + ROPE_D) ** -0.5 * LOG2E
        mla_wts = (row(mla_qa_g[l]), wuq, row(mla_kva_g[l]), wk, wv,
                   _pad_lanes(mla_q_g[l] * mla_scale, 0), _pad_lanes(mla_knope_g[l], 0),
                   _pad_lanes(mla_krope_g[l], NOPE))
        qa, ka, va = _mla_prep(pn, tabs, mla_wts, seq, t_row)
        o_a = _attention(qa, ka, va, pw, COL_GATE_A, batch, seq, t_row)

        fox_wts = (_pad_lanes(fox_b_f[l], 0), row(jnp.tile(fox_q_g[l] * (float(HD) ** -0.5 * LOG2E), H)),
                   row(jnp.tile(fox_k_g[l], H)), tri, selq, selk, oneq, onek)
        qb, kb, vb = _fox_prep(pw, pn, fox_wts, batch, seq, t_row)
        o_b = _attention(qb, kb, vb, pw, COL_GATE_B, batch, seq, t_row)

        mu = rwkv_mu[l]
        wup = jnp.zeros((LANES, W_BR), F32).at[:DECAY_LORA].set(rwkv_w_up[l]).astype(BF16)
        aup = jnp.zeros((LANES, W_BR), F32).at[DECAY_LORA:].set(rwkv_a_up[l]).astype(BF16)
        rw = [row(mu[:W_BR]), row(mu[W_BR:2 * W_BR]), row(mu[2 * W_BR:3 * W_BR]), row(mu[3 * W_BR:]),
              row(rwkv_w0[l]), row(rwkv_a0[l]), wup, aup, row(rwkv_k_k[l]), row(rwkv_k_a[l]),
              row(rwkv_r_k[l])]
        if l > 0:
            vdown = jnp.zeros((W_BR, LANES), F32).at[:, :MV_LORA].set(rwkv_v_down[l - 1]).astype(BF16)
            vup = jnp.zeros((LANES, W_BR), F32).at[:MV_LORA].set(rwkv_v_up[l - 1]).astype(BF16)
            rw += [row(rwkv_v0[l - 1]), vdown, vup]
        rw += [btri, bones]
        outs = _rwkv_prep(pw, pn, vfirst, rw, batch, seq, t_rwkv, first_layer=(l == 0))
        if l == 0:
            vfirst = outs[10]
        o_c = _rwkv_chunk(outs[:10], row(rwkv_lnx_g[l]), row(rwkv_lnx_b[l]), batch, seq, _tile(seq, RWKV_TC))

        x2 = _merge(o_a, o_b, o_c, pw, x2, w_pa[l].astype(BF16), w_pb[l].astype(BF16),
                    w_pc[l].astype(BF16), w_out[l].astype(BF16), batch, seq, t_row)
    return x2.reshape(batch, seq, d)
```

```python
import functools
import math

import jax
import jax.numpy as jnp
import numpy as np
from jax import lax
from jax.experimental import pallas as pl
from jax.experimental.pallas import tpu as pltpu

F32 = jnp.float32
BF16 = jnp.bfloat16

LANES = 128
H = 8
HD = 64
NOPE = 64
ROPE_D = 32
Q_LORA = 256
KV_LORA = 128
D_MODEL = 1024
W_BR = H * HD
DECAY_LORA = 64
AAA_LORA = 64
MV_LORA = 32
ROPE_THETA = 10000.0
RWKV_DECAY_SCALE = 0.606531
GN_EPS = 64e-5
EPS = 1e-6
NEG_INF = -1e30
LOG2E = math.log2(math.e)
CHUNK = 64
ATTN_QT = 4
ONES_ROWS = 16

COL_MERGE = 0
COL_GATE_A = 3072
COL_FQ = 3584
COL_FK = 4096
COL_FV = 4608
COL_GATE_B = 5120
COL_SR = 5632
COL_SK = 6144
COL_SV = 6656
COL_GATE_C = 7168
NCOLS_WIDE = 7680
INPROJ_TN = 2560
COL_CQ = 0
COL_CKV = 256
COL_KR = 384
COL_FF = 512
COL_WA = 640
NCOLS_NARROW = 768

VMEM_LIMIT = 56 * 1024 * 1024


def _cparams(sem):
    return pltpu.CompilerParams(dimension_semantics=sem, vmem_limit_bytes=VMEM_LIMIT)


def _sigmoid(x):
    return 0.5 * jnp.tanh(0.5 * x) + 0.5


def _dot(a, b):
    return jnp.dot(a, b, preferred_element_type=F32)


def _split3(x):
    hi = x.astype(BF16)
    r1 = x - hi.astype(F32)
    mid = r1.astype(BF16)
    lo = (r1 - mid.astype(F32)).astype(BF16)
    return hi, mid, lo


def _dot_exact_lhs(m_bf16, x):
    hi, mid, lo = _split3(x)
    return _dot(m_bf16, hi) + _dot(m_bf16, mid) + _dot(m_bf16, lo)


def _pair_seg_sum(x):
    lane = lax.broadcasted_iota(jnp.int32, x.shape, 1)
    first = lane < HD
    s0 = jnp.sum(jnp.where(first, x, 0.0), axis=-1, keepdims=True)
    s1 = jnp.sum(jnp.where(first, 0.0, x), axis=-1, keepdims=True)
    return jnp.where(first, s0, s1)


def _inproj_kernel(x_ref, g_ref, w_ref, o_ref, h_ref):
    @pl.when(pl.program_id(1) == 0)
    def _():
        x = x_ref[...]
        ms = jnp.mean(x * x, axis=-1, keepdims=True)
        h_ref[...] = (x * lax.rsqrt(ms + EPS) * g_ref[...]).astype(BF16)

    o_ref[...] = _dot(h_ref[...], w_ref[...]).astype(o_ref.dtype)


def _inproj(x2, g, w, tm, tn, out_dtype):
    n, ncols = x2.shape[0], w.shape[1]
    return pl.pallas_call(
        _inproj_kernel,
        grid=(n // tm, ncols // tn),
        in_specs=[
            pl.BlockSpec((tm, D_MODEL), lambda i, j: (i, 0)),
            pl.BlockSpec((1, D_MODEL), lambda i, j: (0, 0)),
            pl.BlockSpec((D_MODEL, tn), lambda i, j: (0, j)),
        ],
        out_specs=pl.BlockSpec((tm, tn), lambda i, j: (i, j)),
        out_shape=jax.ShapeDtypeStruct((n, ncols), out_dtype),
        scratch_shapes=[pltpu.VMEM((tm, D_MODEL), BF16)],
        compiler_params=_cparams(("parallel", "arbitrary")),
        name="inproj",
    )(x2, g, w)


def _rms(x, g, n):
    ms = jnp.sum(x * x, axis=-1, keepdims=True) / n
    return x * lax.rsqrt(ms + EPS) * g


def _rope(x, c, s1, s2):
    return x * c + pltpu.roll(x, LANES - ROPE_D // 2, 1) * s1 + pltpu.roll(x, ROPE_D // 2, 1) * s2


def _mla_prep_kernel(cq_ref, ckv_ref, kr_ref, c_ref, s1_ref, s2_ref, qag_ref, wuq_ref, kvag_ref, wk_ref, wv_ref,
                     qg_ref, kng_ref, krg_ref, q_out, k_out, vt_out):
    c, s1, s2 = c_ref[...], s1_ref[...], s2_ref[...]
    cqn = _rms(cq_ref[...], qag_ref[...], Q_LORA).astype(BF16)
    ckvn = _rms(ckv_ref[...], kvag_ref[...], KV_LORA).astype(BF16)
    q_all = _dot(cqn, wuq_ref[...])
    k_all = _dot(ckvn, wk_ref[...])
    vt_out[0, 0] = lax.dot_general(wv_ref[...], ckvn, _NT, preferred_element_type=F32).astype(BF16)
    k_rope = _rope(_rms(kr_ref[...], krg_ref[...], ROPE_D), c, s1, s2)
    qg, kng = qg_ref[...], kng_ref[...]
    for h in range(H):
        sl = slice(h * LANES, (h + 1) * LANES)
        q_out[:, sl] = _rope(_rms(q_all[:, sl], qg, NOPE + ROPE_D), c, s1, s2).astype(BF16)
        k_out[:, sl] = (_rms(k_all[:, sl], kng, NOPE) + k_rope).astype(BF16)


def _mla_prep(pn, tabs, wts, seq, t):
    n = pn.shape[0]
    ns = seq // t
    row = lambda i: (i, 0)
    const = lambda i: (0, 0)
    tab_spec = pl.BlockSpec((t, LANES), lambda i: (i % ns, 0))
    return pl.pallas_call(
        _mla_prep_kernel,
        grid=(n // t,),
        in_specs=[
            pl.BlockSpec((t, Q_LORA), lambda i: (i, COL_CQ // Q_LORA)),
            pl.BlockSpec((t, LANES), lambda i: (i, COL_CKV // LANES)),
            pl.BlockSpec((t, LANES), lambda i: (i, COL_KR // LANES)),
            tab_spec, tab_spec, tab_spec,
            pl.BlockSpec((1, Q_LORA), const),
            pl.BlockSpec((Q_LORA, H * LANES), const),
            pl.BlockSpec((1, KV_LORA), const),
            pl.BlockSpec((KV_LORA, H * LANES), const),
            pl.BlockSpec((W_BR, KV_LORA), const),
            pl.BlockSpec((1, LANES), const),
            pl.BlockSpec((1, LANES), const),
            pl.BlockSpec((1, LANES), const),
        ],
        out_specs=[
            pl.BlockSpec((t, H * LANES), row),
            pl.BlockSpec((t, H * LANES), row),
            pl.BlockSpec((1, 1, W_BR, t), lambda i: (i // ns, i % ns, 0, 0)),
        ],
        out_shape=[
            jax.ShapeDtypeStruct((n, H * LANES), BF16),
            jax.ShapeDtypeStruct((n, H * LANES), BF16),
            jax.ShapeDtypeStruct((n // seq, ns, W_BR, t), BF16),
        ],
        compiler_params=_cparams(("parallel",)),
        name="mla_prep",
    )(pn, pn, pn, *tabs, *wts)


def _fox_prep_kernel(fq_ref, fk_ref, fv_ref, ff_ref, bf_ref, qg_ref, kg_ref, tri_ref, selq_ref, selk_ref,
                     oneq_ref, onek_ref, q_out, k_out, vt_out, carry_ref):
    t = fq_ref.shape[0]

    @pl.when(pl.program_id(1) == 0)
    def _():
        carry_ref[...] = jnp.zeros_like(carry_ref)

    z = ff_ref[...] + bf_ref[...]
    logf = jnp.minimum(z, 0.0) - jnp.log1p(jnp.exp(-jnp.abs(z)))
    cum = carry_ref[...] + _dot_exact_lhs(tri_ref[...], logf)
    carry_ref[...] = cum[t - 1:t, :]
    hi, mid, lo = _split3(cum * LOG2E)
    aug_q = _dot(hi, selq_ref[0]) + _dot(mid, selq_ref[1]) + _dot(lo, selq_ref[2]) + oneq_ref[...]
    aug_k = _dot(hi, selk_ref[0]) + _dot(mid, selk_ref[1]) + _dot(lo, selk_ref[2]) + onek_ref[...]
    vt_out[0, 0] = fv_ref[...].astype(F32).T.astype(BF16)
    lane = lax.broadcasted_iota(jnp.int32, (t, LANES), 1)
    first = lane < HD
    for src_ref, g_ref, aug, dst in ((fq_ref, qg_ref, aug_q, q_out), (fk_ref, kg_ref, aug_k, k_out)):
        for p in range(H // 2):
            sl = slice(p * LANES, (p + 1) * LANES)
            x = src_ref[:, sl].astype(F32)
            ms = _pair_seg_sum(x * x) / HD
            xn = x * lax.rsqrt(ms + EPS) * g_ref[:, sl]
            he, ho = 2 * p, 2 * p + 1
            dst[:, he * LANES:(he + 1) * LANES] = jnp.where(first, xn, aug[:, he * LANES:(he + 1) * LANES]).astype(BF16)
            dst[:, ho * LANES:(ho + 1) * LANES] = jnp.where(
                first, pltpu.roll(xn, HD, 1), aug[:, ho * LANES:(ho + 1) * LANES]).astype(BF16)


def _fox_prep(pw, pn, wts, batch, seq, t):
    n = pw.shape[0]
    ns = seq // t
    row = lambda b, i: (b * ns + i, 0)
    const2 = lambda b, i: (0, 0)
    const3 = lambda b, i: (0, 0, 0)
    return pl.pallas_call(
        _fox_prep_kernel,
        grid=(batch, ns),
        in_specs=[
            pl.BlockSpec((t, W_BR), lambda b, i: (b * ns + i, COL_FQ // W_BR)),
            pl.BlockSpec((t, W_BR), lambda b, i: (b * ns + i, COL_FK // W_BR)),
            pl.BlockSpec((t, W_BR), lambda b, i: (b * ns + i, COL_FV // W_BR)),
            pl.BlockSpec((t, LANES), lambda b, i: (b * ns + i, COL_FF // LANES)),
            pl.BlockSpec((1, LANES), const2),
            pl.BlockSpec((1, W_BR), const2),
            pl.BlockSpec((1, W_BR), const2),
            pl.BlockSpec((t, t), const2),
            pl.BlockSpec((3, LANES, H * LANES), const3),
            pl.BlockSpec((3, LANES, H * LANES), const3),
            pl.BlockSpec((1, H * LANES), const2),
            pl.BlockSpec((1, H * LANES), const2),
        ],
        out_specs=[
            pl.BlockSpec((t, H * LANES), row),
            pl.BlockSpec((t, H * LANES), row),
            pl.BlockSpec((1, 1, W_BR, t), lambda b, i: (b, i, 0, 0)),
        ],
        out_shape=[
            jax.ShapeDtypeStruct((n, H * LANES), BF16),
            jax.ShapeDtypeStruct((n, H * LANES), BF16),
            jax.ShapeDtypeStruct((batch, ns, W_BR, t), BF16),
        ],
        scratch_shapes=[pltpu.VMEM((1, LANES), F32)],
        compiler_params=_cparams(("parallel", "arbitrary")),
        name="fox_prep",
    )(pw, pw, pw, pn, *wts)


def _attn_kernel(q_ref, k_ref, vt_ref, g_ref, o_ref, m_ref, acc_ref, sa_ref, sb_ref, mxa_ref, mxb_ref, *, tk):
    tq = ATTN_QT * tk
    gi = pl.program_id(2)
    nt = (((1,), (1,)), ((), ()))
    kv_pos = lax.broadcasted_iota(jnp.int32, (tk, tq), 0)
    q_pos = lax.broadcasted_iota(jnp.int32, (tk, tq), 1)
    causal = kv_pos <= q_pos
    both = slice(0, tq)
    ones = jnp.ones((ONES_ROWS, tk), BF16)
    m_ref[...] = jnp.full(m_ref.shape, NEG_INF, F32)
    acc_ref[...] = jnp.zeros(acc_ref.shape, F32)

    def scores(c, s_ref, mx_ref, cols):
        start = pl.multiple_of(c * tk, tk)
        for j in range(2):
            q = q_ref[cols, j * LANES:(j + 1) * LANES]
            k = k_ref[pl.ds(start, tk), j * LANES:(j + 1) * LANES]
            s = lax.dot_general(k, q, nt, preferred_element_type=F32)
            s_ref[j, :, cols] = s
            mx_ref[j, :, cols] = jnp.max(s, axis=0, keepdims=True)

    def consume(c, s_ref, mx_ref, cols, mask):
        for j in range(2):
            vt = jnp.concatenate([vt_ref[0, c, j * HD:(j + 1) * HD, :], ones], axis=0)
            s = s_ref[j, :, cols]
            if mask is not None:
                s = jnp.where(mask, s, NEG_INF)
                m_cur = jnp.max(s, axis=0, keepdims=True)
            else:
                m_cur = mx_ref[j, :, cols]
            m_old = m_ref[j, :, cols]
            m_new = jnp.maximum(m_old, m_cur)
            alpha = jnp.exp2(m_old - m_new)
            p = jnp.exp2(s - m_new)
            acc_ref[j, :, cols] = alpha * acc_ref[j, :, cols] + _dot(vt, p.astype(BF16))
            m_ref[j, :, cols] = m_new

    scores(0, sa_ref, mxa_ref, both)

    def body(i, carry):
        c = 2 * i
        scores(c + 1, sb_ref, mxb_ref, both)
        consume(c, sa_ref, mxa_ref, both, None)
        scores(c + 2, sa_ref, mxa_ref, both)
        consume(c + 1, sb_ref, mxb_ref, both, None)
        return carry

    lax.fori_loop(0, gi * (ATTN_QT // 2), body, 0)
    bufs = ((sa_ref, mxa_ref), (sb_ref, mxb_ref))
    for d in range(ATTN_QT):
        c = ATTN_QT * gi + d
        if d + 1 < ATTN_QT:
            scores(c + 1, *bufs[(d + 1) % 2], slice((d + 1) * tk, tq))
        consume(c, *bufs[d % 2], slice(d * tk, tq), causal[:, :tq - d * tk])

    o_t = jnp.concatenate([acc_ref[j, :HD, :] / acc_ref[j, HD:HD + 1, :] for j in range(2)], axis=0)
    g = g_ref[...].astype(F32)
    o_ref[...] = (o_t.T * (g * _sigmoid(g))).astype(BF16)


def _attention(q, k, vt, pw, gate_col, batch, seq, tk):
    n = q.shape[0]
    tq = ATTN_QT * tk
    nq, nk = seq // tq, seq // tk
    return pl.pallas_call(
        functools.partial(_attn_kernel, tk=tk),
        grid=(batch, H // 2, nq),
        in_specs=[
            pl.BlockSpec((tq, 2 * LANES), lambda b, hp, i: (b * nq + i, hp)),
            pl.BlockSpec((seq, 2 * LANES), lambda b, hp, i: (b, hp)),
            pl.BlockSpec((1, nk, LANES, tk), lambda b, hp, i: (b, 0, hp, 0)),
            pl.BlockSpec((tq, LANES), lambda b, hp, i: (b * nq + i, gate_col // LANES + hp)),
        ],
        out_specs=pl.BlockSpec((tq, LANES), lambda b, hp, i: (b * nq + i, hp)),
        out_shape=jax.ShapeDtypeStruct((n, W_BR), BF16),
        scratch_shapes=[
            pltpu.VMEM((2, 1, tq), F32),
            pltpu.VMEM((2, HD + ONES_ROWS, tq), F32),
            pltpu.VMEM((2, tk, tq), F32),
            pltpu.VMEM((2, tk, tq), F32),
            pltpu.VMEM((2, 1, tq), F32),
            pltpu.VMEM((2, 1, tq), F32),
        ],
        compiler_params=_cparams(("parallel", "parallel", "arbitrary")),
        name="attention",
    )(q, k, vt, pw)


def _rwkv_prep_kernel(*refs, first_layer):
    if first_layer:
        (sr_ref, sk_ref, sv_ref, swa_ref, gc_ref, mur_ref, muk_ref, muv_ref, muwa_ref, w0_ref, a0_ref, wup_ref,
         aup_ref, kk_ref, ka_ref, rk_ref, bt_ref, bo_ref,
         at_o, rt_o, bt_o, kt_o, v_o, bh_o, kh_o, wc_o, bonus_o, sg_o, vfirst_o,
         cr_ref, ck_ref, cv_ref, cwa_ref) = refs
    else:
        (sr_ref, sk_ref, sv_ref, swa_ref, gc_ref, vf_ref, mur_ref, muk_ref, muv_ref, muwa_ref, w0_ref, a0_ref,
         wup_ref, aup_ref, kk_ref, ka_ref, rk_ref, v0_ref, vdown_ref, vup_ref, bt_ref, bo_ref,
         at_o, rt_o, bt_o, kt_o, v_o, bh_o, kh_o, wc_o, bonus_o, sg_o,
         cr_ref, ck_ref, cv_ref, cwa_ref) = refs
    t = sr_ref.shape[0]

    @pl.when(pl.program_id(1) == 0)
    def _():
        for c_ref in (cr_ref, ck_ref, cv_ref, cwa_ref):
            c_ref[...] = jnp.zeros_like(c_ref)

    def shift(x_ref, c_ref, mu_ref):
        x = x_ref[...].astype(F32)
        rowid = lax.broadcasted_iota(jnp.int32, x.shape, 0)
        prev = jnp.where(rowid == 0, c_ref[...], pltpu.roll(x, 1, 0))
        c_ref[...] = x[t - 1:t, :]
        return x + mu_ref[...] * (prev - x)

    r = shift(sr_ref, cr_ref, mur_ref)
    k = shift(sk_ref, ck_ref, muk_ref)
    v = shift(sv_ref, cv_ref, muv_ref)
    wa = shift(swa_ref, cwa_ref, muwa_ref)

    logw = -RWKV_DECAY_SCALE * _sigmoid(w0_ref[...] + _dot(jnp.tanh(wa).astype(BF16), wup_ref[...]))
    a = _sigmoid(a0_ref[...] + _dot(wa.astype(BF16), aup_ref[...]))
    if first_layer:
        vfirst_o[...] = v
    else:
        low = _dot(v.astype(BF16), vdown_ref[...]).astype(BF16)
        nu = _sigmoid(v0_ref[...] + _dot(low, vup_ref[...]))
        v = v + (vf_ref[...] - v) * nu

    kk = k * kk_ref[...]
    k_mod = k * (1.0 + (a - 1.0) * ka_ref[...])
    rk = r * k_mod * rk_ref[...]
    kk_n, bonus = [], []
    for p in range(H // 2):
        sl = slice(p * LANES, (p + 1) * LANES)
        nrm = jnp.sqrt(_pair_seg_sum(kk[:, sl] * kk[:, sl]))
        kk_n.append(kk[:, sl] / jnp.maximum(nrm, 1e-12))
        bonus.append(_pair_seg_sum(rk[:, sl]) * v[:, sl])
    kk = jnp.concatenate(kk_n, axis=-1)
    bonus = jnp.concatenate(bonus, axis=-1)
    a_vec = -kk
    b_vec = kk * a

    terms = _split3(logw)
    lcum = sum(_dot(bt_ref[...], x) for x in terms)
    ltot = sum(_dot(bo_ref[...], x) for x in terms)
    e_minus = jnp.exp(-lcum)
    e_rem = jnp.exp(ltot - lcum)
    g = gc_ref[...].astype(F32)
    for o_ref, val in ((at_o, a_vec * jnp.exp(lcum - logw)), (rt_o, r * jnp.exp(lcum)), (bt_o, b_vec * e_minus),
                       (kt_o, k_mod * e_minus), (v_o, v), (bh_o, b_vec * e_rem), (kh_o, k_mod * e_rem),
                       (wc_o, jnp.exp(ltot)), (bonus_o, bonus), (sg_o, g * _sigmoid(g))):
        o_ref[...] = val.astype(o_ref.dtype)


def _rwkv_prep(pw, pn, vfirst, wts, batch, seq, t, first_layer):
    n = pw.shape[0]
    ns = seq // t
    const2 = lambda b, i: (0, 0)
    pcol = lambda col, w: pl.BlockSpec((t, w), lambda b, i: (b * ns + i, col // w))
    vec = pl.BlockSpec((1, W_BR), const2)
    in_specs = [pcol(COL_SR, W_BR), pcol(COL_SK, W_BR), pcol(COL_SV, W_BR), pcol(COL_WA, LANES),
                pcol(COL_GATE_C, W_BR)]
    args = [pw, pw, pw, pn, pw]
    if not first_layer:
        in_specs.append(pl.BlockSpec((t, W_BR), lambda b, i: (b * ns + i, 0)))
        args.append(vfirst)
    in_specs += [vec, vec, vec, pl.BlockSpec((1, LANES), const2), vec, vec,
                 pl.BlockSpec((LANES, W_BR), const2), pl.BlockSpec((LANES, W_BR), const2), vec, vec, vec]
    if not first_layer:
        in_specs += [vec, pl.BlockSpec((W_BR, LANES), const2), pl.BlockSpec((LANES, W_BR), const2)]
    in_specs += [pl.BlockSpec((t, t), const2)] * 2
    args += list(wts)
    row_spec = pl.BlockSpec((t, W_BR), lambda b, i: (b * ns + i, 0))
    out_dtypes = (BF16, F32, BF16, BF16, BF16, BF16, BF16, F32, F32, F32) + ((F32,) if first_layer else ())
    out_specs = [row_spec] * len(out_dtypes)
    out_shape = [jax.ShapeDtypeStruct((n, W_BR), dt) for dt in out_dtypes]
    return pl.pallas_call(
        functools.partial(_rwkv_prep_kernel, first_layer=first_layer),
        grid=(batch, ns),
        in_specs=in_specs,
        out_specs=out_specs,
        out_shape=out_shape,
        scratch_shapes=[pltpu.VMEM((1, W_BR), F32), pltpu.VMEM((1, W_BR), F32), pltpu.VMEM((1, W_BR), F32),
                        pltpu.VMEM((1, LANES), F32)],
        compiler_params=_cparams(("parallel", "arbitrary")),
        name="rwkv_prep",
    )(*args)


def _mm(a, b, dims):
    return lax.dot_general(a.astype(BF16), b.astype(BF16), dims, preferred_element_type=F32)


_NN = (((1,), (0,)), ((), ()))
_NT = (((1,), (1,)), ((), ()))
_TN = (((0,), (0,)), ((), ()))


def _rwkv_chunk_kernel(at_ref, rt_ref, bt_ref, kt_ref, v_ref, bh_ref, kh_ref, wc_ref, bonus_ref, sg_ref,
                       lg_ref, lb_ref, o_ref, s_all_ref, *, nchunk):
    @pl.when(pl.program_id(1) == 0)
    def _():
        s_all_ref[...] = jnp.zeros_like(s_all_ref)

    mm = _mm
    first = lax.broadcasted_iota(jnp.int32, (CHUNK, LANES), 1) < HD
    r2 = lax.broadcasted_iota(jnp.int32, (LANES, LANES), 0)
    c2 = lax.broadcasted_iota(jnp.int32, (LANES, LANES), 1)
    same_head = (r2 < HD) == (c2 < HD)
    eye2 = r2 == c2
    t_row, t_col = r2 % CHUNK, c2 % CHUNK
    strict2 = jnp.logical_and(same_head, t_col < t_row)
    incl2 = jnp.logical_and(same_head, t_col <= t_row)
    each = lambda f, *cols: [f(*xs) for xs in zip(*cols)]
    pairs = [(c, p) for c in range(nchunk) for p in range(H // 2)]
    npair = range(len(pairs))
    ld = lambda ref: [ref[c * CHUNK:(c + 1) * CHUNK, p * LANES:(p + 1) * LANES] for c, p in pairs]
    at, rt, bt, kt, v, bh, kh = ld(at_ref), ld(rt_ref), ld(bt_ref), ld(kt_ref), ld(v_ref), ld(bh_ref), ld(kh_ref)
    zero = jnp.zeros((CHUNK, LANES), BF16)
    stack = lambda x: jnp.concatenate([jnp.where(first, x, zero), jnp.where(first, zero, x)], axis=0)
    fold = lambda x: x[:CHUNK] + x[CHUNK:]
    a_s, b_s, k_s, v_s = each(stack, at), each(stack, bt), each(stack, kt), each(stack, v)
    x1 = each(lambda a, r: jnp.concatenate([a, stack(r.astype(BF16))], axis=0), a_s, rt)
    gb = each(lambda x, b: mm(x, b, _NT), x1, b_s)
    gk = each(lambda x, k: mm(x, k, _NT), x1, k_s)
    a_ab = [jnp.where(strict2, g[:LANES], 0.0) for g in gb]
    a_rb = [jnp.where(incl2, g[LANES:], 0.0) for g in gb]
    a_ak = [jnp.where(strict2, g[:LANES], 0.0) for g in gk]
    a_rk = [jnp.where(incl2, g[LANES:], 0.0) for g in gk]
    tinv = [jnp.where(eye2, 1.0, a) for a in a_ab]
    pw = a_ab
    for _ in range(int(math.log2(CHUNK)) - 1):
        pw = each(lambda p: mm(p, p, _NN), pw)
        tinv = each(lambda t, p: t + mm(t, p, _NN), tinv, pw)
    av = each(lambda a, x: mm(a, x, _NN), a_ak, v_s)
    tx = each(lambda t, x, a: mm(t, jnp.concatenate([x.astype(BF16), a], axis=1), _NN), tinv, av, a_s)
    ry = each(lambda a, x: mm(a, x, _NN), a_rb, tx)
    yk = each(lambda a, x: mm(a, x, _NN), a_rk, v_s)
    u = [fold(t[:, :LANES]) for t in tx]
    a_til = [fold(t[:, LANES:]) for t in tx]
    y_in = [fold(r[:, :LANES] + k) for r, k in zip(ry, yk)]
    r_hat = [r + fold(x[:, LANES:]) for r, x in zip(rt, ry)]
    m_new = [jnp.where(same_head, mm(a_til[i], bh[i], _TN), 0.0) for i in npair]
    n_new = [jnp.where(same_head, mm(jnp.concatenate([u[i].astype(BF16), v[i]], axis=0),
                                     jnp.concatenate([bh[i], kh[i]], axis=0), _TN), 0.0) for i in npair]
    lg, lb = lg_ref[...], lb_ref[...]
    for c in range(nchunk):
        idx = [i for i, (ci, _) in enumerate(pairs) if ci == c]
        s0 = [s_all_ref[p] for p in range(H // 2)]
        y = [y_in[i] + mm(r_hat[i], s0[p], _NT) for p, i in enumerate(idx)]
        for p, i in enumerate(idx):
            wc = wc_ref[c * CHUNK:c * CHUNK + 1, p * LANES:(p + 1) * LANES]
            s_all_ref[p] = mm(s0[p], jnp.where(eye2, wc, 0.0) + m_new[i], _NN) + n_new[i]
        for p, i in enumerate(idx):
            rows, lanes = slice(c * CHUNK, (c + 1) * CHUNK), slice(p * LANES, (p + 1) * LANES)
            mu = _pair_seg_sum(y[p]) / HD
            d = y[p] - mu
            var = _pair_seg_sum(d * d) / HD
            yn = d * lax.rsqrt(var + GN_EPS) * lg[:, lanes] + lb[:, lanes]
            o_ref[rows, lanes] = ((yn + bonus_ref[rows, lanes]) * sg_ref[rows, lanes]).astype(o_ref.dtype)


def _rwkv_chunk(arrs, lnx_g, lnx_b, batch, seq, tc):
    n = batch * seq
    ns = seq // tc
    spec = pl.BlockSpec((tc, W_BR), lambda b, i: (b * ns + i, 0))
    gspec = pl.BlockSpec((1, W_BR), lambda b, i: (0, 0))
    return pl.pallas_call(
        functools.partial(_rwkv_chunk_kernel, nchunk=tc // CHUNK),
        grid=(batch, ns),
        in_specs=[spec] * 10 + [gspec, gspec],
        out_specs=spec,
        out_shape=jax.ShapeDtypeStruct((n, W_BR), BF16),
        scratch_shapes=[pltpu.VMEM((H // 2, LANES, LANES), F32)],
        compiler_params=_cparams(("parallel", "arbitrary")),
        name="rwkv_chunk",
    )(*arrs, lnx_g, lnx_b)


def _merge_kernel(oa_ref, ob_ref, oc_ref, ga_ref, gb_ref, gc_ref, x_ref, wpa_ref, wpb_ref, wpc_ref, wout_ref,
                  o_ref):
    pa = _dot(oa_ref[...], wpa_ref[...])
    pb = _dot(ob_ref[...], wpb_ref[...])
    pc = _dot(oc_ref[...], wpc_ref[...])
    sig = lambda ref: _sigmoid(ref[...].astype(F32))
    merged = sig(ga_ref) * pa + sig(gb_ref) * pb + sig(gc_ref) * pc
    o_ref[...] = x_ref[...] + _dot(merged.astype(BF16), wout_ref[...])


def _merge(oa, ob, oc, p, x2, wpa, wpb, wpc, wout, batch, seq, t):
    n = x2.shape[0]
    ns = seq // t
    row = lambda b, i: (b * ns + i, 0)
    const2 = lambda b, i: (0, 0)
    gate = lambda g: pl.BlockSpec((t, D_MODEL), lambda b, i: (b * ns + i, COL_MERGE // D_MODEL + g))
    return pl.pallas_call(
        _merge_kernel,
        grid=(batch, ns),
        in_specs=[
            pl.BlockSpec((t, W_BR), row),
            pl.BlockSpec((t, W_BR), row),
            pl.BlockSpec((t, W_BR), row),
            gate(0), gate(1), gate(2),
            pl.BlockSpec((t, D_MODEL), row),
            pl.BlockSpec((W_BR, D_MODEL), const2),
            pl.BlockSpec((W_BR, D_MODEL), const2),
            pl.BlockSpec((W_BR, D_MODEL), const2),
            pl.BlockSpec((D_MODEL, D_MODEL), const2),
        ],
        out_specs=pl.BlockSpec((t, D_MODEL), row),
        out_shape=jax.ShapeDtypeStruct((n, D_MODEL), F32),
        compiler_params=_cparams(("parallel", "parallel")),
        name="merge",
    )(oa, ob, oc, p, p, p, x2, wpa, wpb, wpc, wout)


def _regroup_w_in(w_in, l):
    d = w_in.shape[1]
    sizes = (Q_LORA, KV_LORA, ROPE_D, W_BR, W_BR, W_BR, W_BR, H, W_BR,
             3 * W_BR + DECAY_LORA + AAA_LORA, W_BR, 3 * D_MODEL)
    offs = [int(o) for o in np.concatenate([[0], np.cumsum(sizes)])]
    (c_q, c_kv, k_r, gate_a, fq, fk, fv, ff, gate_b, shift, gate_c, merge) = [
        w_in[l, :, offs[i]:offs[i + 1]].astype(BF16) for i in range(len(sizes))]
    z = lambda n: jnp.zeros((d, n), BF16)
    wide = jnp.concatenate([
        merge, gate_a, fq, fk, fv, gate_b,
        shift[:, :W_BR], shift[:, W_BR:2 * W_BR], shift[:, 2 * W_BR:3 * W_BR], gate_c,
    ], axis=1)
    narrow = jnp.concatenate([
        c_q, c_kv,
        z(NOPE), k_r, z(LANES - NOPE - ROPE_D),
        ff, z(LANES - H),
        shift[:, 3 * W_BR:],
    ], axis=1)
    return wide, narrow


def _rope_tables(seq):
    half = ROPE_D // 2
    inv = ROPE_THETA ** (-jnp.arange(0, ROPE_D, 2, dtype=F32) / ROPE_D)
    ang = jnp.arange(seq, dtype=F32)[:, None] * inv[None, :]
    cos, sin = jnp.cos(ang), jnp.sin(ang)
    z = lambda n: jnp.zeros((seq, n), F32)
    c = jnp.concatenate([jnp.ones((seq, NOPE), F32), cos, cos, z(LANES - NOPE - ROPE_D)], axis=1)
    s1 = jnp.concatenate([z(NOPE), -sin, z(LANES - NOPE - half)], axis=1)
    s2 = jnp.concatenate([z(NOPE + half), sin, z(LANES - NOPE - ROPE_D)], axis=1)
    return c, s1, s2


def _pad_lanes(v, lo, total=LANES):
    return jnp.zeros((1, total), F32).at[0, lo:lo + v.shape[0]].set(v)


def _fox_selectors():
    selq = np.zeros((3, LANES, H * LANES), np.float32)
    selk = np.zeros((3, LANES, H * LANES), np.float32)
    oneq = np.zeros((1, H * LANES), np.float32)
    onek = np.zeros((1, H * LANES), np.float32)
    for h in range(H):
        for j in range(3):
            selq[j, h, h * LANES + HD + j] = 1.0
            selk[j, h, h * LANES + HD + 3 + j] = -1.0
            oneq[0, h * LANES + HD + 3 + j] = 1.0
            onek[0, h * LANES + HD + j] = 1.0
    return (jnp.asarray(selq, BF16), jnp.asarray(selk, BF16), jnp.asarray(oneq), jnp.asarray(onek))


def _tri(t):
    r = np.arange(t)
    return jnp.asarray((r[None, :] <= r[:, None]).astype(np.float32), BF16)


def _chunk_tri(t):
    r = np.arange(t)
    same = (r[None, :] // CHUNK) == (r[:, None] // CHUNK)
    lower = r[None, :] <= r[:, None]
    return (jnp.asarray((same & lower).astype(np.float32), BF16), jnp.asarray(same.astype(np.float32), BF16))


def _tile(seq, pref):
    return pref if seq % pref == 0 else seq


RWKV_TC = 256


def kernel(x, norm_g, w_in, mla_qa_g, mla_w_uq, mla_kva_g, mla_w_ukv, mla_q_g, mla_knope_g, mla_krope_g, fox_b_f, fox_q_g, fox_k_g, rwkv_mu, rwkv_w0, rwkv_w_up, rwkv_a0, rwkv_a_up, rwkv_k_k, rwkv_k_a, rwkv_r_k, rwkv_lnx_g, rwkv_lnx_b, rwkv_v0, rwkv_v_down, rwkv_v_up, w_pa, w_pb, w_pc, w_out):
    batch, seq, d = x.shape
    depth = w_in.shape[0]
    n = batch * seq
    t_row = _tile(seq, 512)
    t_in = _tile(n, 1024)
    t_rwkv = _tile(seq, 256)
    x2 = x.reshape(n, d)

    tabs = _rope_tables(seq)
    selq, selk, oneq, onek = _fox_selectors()
    tri = _tri(t_row)
    btri, bones = _chunk_tri(t_rwkv)
    row = lambda v: v.reshape(1, -1).astype(F32)

    vfirst = None
    for l in range(depth):
        w_wide, w_narrow = _regroup_w_in(w_in, l)
        pw = _inproj(x2, row(norm_g[l]), w_wide, t_in, INPROJ_TN, BF16)
        pn = _inproj(x2, row(norm_g[l]), w_narrow, t_in, NCOLS_NARROW, F32)

        wuq = mla_w_uq[l].reshape(Q_LORA, H, NOPE + ROPE_D)
        wuq = jnp.pad(wuq, ((0, 0), (0, 0), (0, LANES - NOPE - ROPE_D))).reshape(Q_LORA, H * LANES).astype(BF16)
        wukv = mla_w_ukv[l].reshape(KV_LORA, H, NOPE + HD)
        wk = jnp.pad(wukv[:, :, :NOPE], ((0, 0), (0, 0), (0, LANES - NOPE))).reshape(KV_LORA, H * LANES).astype(BF16)
        wv = wukv[:, :, NOPE:].reshape(KV_LORA, W_BR).T.astype(BF16)
        mla_scale = float(NOPE + ROPE_D) ** -0.5 * LOG2E
        mla_wts = (row(mla_qa_g[l]), wuq, row(mla_kva_g[l]), wk, wv,
                   _pad_lanes(mla_q_g[l] * mla_scale, 0), _pad_lanes(mla_knope_g[l], 0),
                   _pad_lanes(mla_krope_g[l], NOPE))
        qa, ka, va = _mla_prep(pn, tabs, mla_wts, seq, t_row)
        o_a = _attention(qa, ka, va, pw, COL_GATE_A, batch, seq, t_row)

        fox_wts = (_pad_lanes(fox_b_f[l], 0), row(jnp.tile(fox_q_g[l] * (float(HD) ** -0.5 * LOG2E), H)),
                   row(jnp.tile(fox_k_g[l], H)), tri, selq, selk, oneq, onek)
        qb, kb, vb = _fox_prep(pw, pn, fox_wts, batch, seq, t_row)
        o_b = _attention(qb, kb, vb, pw, COL_GATE_B, batch, seq, t_row)

        mu = rwkv_mu[l]
        wup = jnp.zeros((LANES, W_BR), F32).at[:DECAY_LORA].set(rwkv_w_up[l]).astype(BF16)
        aup = jnp.zeros((LANES, W_BR), F32).at[DECAY_LORA:].set(rwkv_a_up[l]).astype(BF16)
        rw = [row(mu[:W_BR]), row(mu[W_BR:2 * W_BR]), row(mu[2 * W_BR:3 * W_BR]), row(mu[3 * W_BR:]),
              row(rwkv_w0[l]), row(rwkv_a0[l]), wup, aup, row(rwkv_k_k[l]), row(rwkv_k_a[l]),
              row(rwkv_r_k[l])]
        if l > 0:
            vdown = jnp.zeros((W_BR, LANES), F32).at[:, :MV_LORA].set(rwkv_v_down[l - 1]).astype(BF16)
            vup = jnp.zeros((LANES, W_BR), F32).at[:MV_LORA].set(rwkv_v_up[l - 1]).astype(BF16)
            rw += [row(rwkv_v0[l - 1]), vdown, vup]
        rw += [btri, bones]
        outs = _rwkv_prep(pw, pn, vfirst, rw, batch, seq, t_rwkv, first_layer=(l == 0))
        if l == 0:
            vfirst = outs[10]
        o_c = _rwkv_chunk(outs[:10], row(rwkv_lnx_g[l]), row(rwkv_lnx_b[l]), batch, seq, _tile(seq, RWKV_TC))

        x2 = _merge(o_a, o_b, o_c, pw, x2, w_pa[l].astype(BF16), w_pb[l].astype(BF16),
                    w_pc[l].astype(BF16), w_out[l].astype(BF16), batch, seq, t_row)
    return x2.reshape(batch, seq, d)
```

```python
import functools
import math

import jax
import jax.numpy as jnp
import numpy as np
from jax import lax
from jax.experimental import pallas as pl
from jax.experimental.pallas import tpu as pltpu

F32 = jnp.float32
BF16 = jnp.bfloat16

LANES = 128
H = 8
HD = 64
NOPE = 64
ROPE_D = 32
Q_LORA = 256
KV_LORA = 128
D_MODEL = 1024
W_BR = H * HD
DECAY_LORA = 64
AAA_LORA = 64
MV_LORA = 32
ROPE_THETA = 10000.0
RWKV_DECAY_SCALE = 0.606531
GN_EPS = 64e-5
EPS = 1e-6
NEG_INF = -1e30
LOG2E = math.log2(math.e)
CHUNK = 64
ATTN_QT = 4
ONES_ROWS = 16

COL_MERGE = 0
COL_GATE_A = 3072
COL_FQ = 3584
COL_FK = 4096
COL_FV = 4608
COL_GATE_B = 5120
COL_SR = 5632
COL_SK = 6144
COL_SV = 6656
COL_GATE_C = 7168
NCOLS_WIDE = 7680
INPROJ_TN = 2560
COL_CQ = 0
COL_CKV = 256
COL_KR = 384
COL_FF = 512
COL_WA = 640
NCOLS_NARROW = 768

VMEM_LIMIT = 56 * 1024 * 1024


def _cparams(sem):
    return pltpu.CompilerParams(dimension_semantics=sem, vmem_limit_bytes=VMEM_LIMIT)


def _sigmoid(x):
    return 0.5 * jnp.tanh(0.5 * x) + 0.5


def _dot(a, b):
    return jnp.dot(a, b, preferred_element_type=F32)


def _split3(x):
    hi = x.astype(BF16)
    r1 = x - hi.astype(F32)
    mid = r1.astype(BF16)
    lo = (r1 - mid.astype(F32)).astype(BF16)
    return hi, mid, lo


def _dot_exact_lhs(m_bf16, x):
    hi, mid, lo = _split3(x)
    return _dot(m_bf16, hi) + _dot(m_bf16, mid) + _dot(m_bf16, lo)


def _pair_seg_sum(x):
    lane = lax.broadcasted_iota(jnp.int32, x.shape, 1)
    first = lane < HD
    s0 = jnp.sum(jnp.where(first, x, 0.0), axis=-1, keepdims=True)
    s1 = jnp.sum(jnp.where(first, 0.0, x), axis=-1, keepdims=True)
    return jnp.where(first, s0, s1)


def _inproj_kernel(x_ref, g_ref, w_ref, o_ref, h_ref):
    @pl.when(pl.program_id(1) == 0)
    def _():
        x = x_ref[...]
        ms = jnp.mean(x * x, axis=-1, keepdims=True)
        h_ref[...] = (x * lax.rsqrt(ms + EPS) * g_ref[...]).astype(BF16)

    o_ref[...] = _dot(h_ref[...], w_ref[...]).astype(o_ref.dtype)


def _inproj(x2, g, w, tm, tn, out_dtype):
    n, ncols = x2.shape[0], w.shape[1]
    return pl.pallas_call(
        _inproj_kernel,
        grid=(n // tm, ncols // tn),
        in_specs=[
            pl.BlockSpec((tm, D_MODEL), lambda i, j: (i, 0)),
            pl.BlockSpec((1, D_MODEL), lambda i, j: (0, 0)),
            pl.BlockSpec((D_MODEL, tn), lambda i, j: (0, j)),
        ],
        out_specs=pl.BlockSpec((tm, tn), lambda i, j: (i, j)),
        out_shape=jax.ShapeDtypeStruct((n, ncols), out_dtype),
        scratch_shapes=[pltpu.VMEM((tm, D_MODEL), BF16)],
        compiler_params=_cparams(("parallel", "arbitrary")),
        name="inproj",
    )(x2, g, w)


def _rms(x, g, n):
    ms = jnp.sum(x * x, axis=-1, keepdims=True) / n
    return x * lax.rsqrt(ms + EPS) * g


def _rope(x, c, s1, s2):
    return x * c + pltpu.roll(x, LANES - ROPE_D // 2, 1) * s1 + pltpu.roll(x, ROPE_D // 2, 1) * s2


def _mla_prep_kernel(cq_ref, ckv_ref, kr_ref, c_ref, s1_ref, s2_ref, qag_ref, wuq_ref, kvag_ref, wk_ref, wv_ref,
                     qg_ref, kng_ref, krg_ref, q_out, k_out, vt_out):
    c, s1, s2 = c_ref[...], s1_ref[...], s2_ref[...]
    cqn = _rms(cq_ref[...], qag_ref[...], Q_LORA).astype(BF16)
    ckvn = _rms(ckv_ref[...], kvag_ref[...], KV_LORA).astype(BF16)
    q_all = _dot(cqn, wuq_ref[...])
    k_all = _dot(ckvn, wk_ref[...])
    vt_out[0, 0] = lax.dot_general(wv_ref[...], ckvn, _NT, preferred_element_type=F32).astype(BF16)
    k_rope = _rope(_rms(kr_ref[...], krg_ref[...], ROPE_D), c, s1, s2)
    qg, kng = qg_ref[...], kng_ref[...]
    for h in range(H):
        sl = slice(h * LANES, (h + 1) * LANES)
        q_out[:, sl] = _rope(_rms(q_all[:, sl], qg, NOPE + ROPE_D), c, s1, s2).astype(BF16)
        k_out[:, sl] = (_rms(k_all[:, sl], kng, NOPE) + k_rope).astype(BF16)


def _mla_prep(pn, tabs, wts, seq, t):
    n = pn.shape[0]
    ns = seq // t
    row = lambda i: (i, 0)
    const = lambda i: (0, 0)
    tab_spec = pl.BlockSpec((t, LANES), lambda i: (i % ns, 0))
    return pl.pallas_call(
        _mla_prep_kernel,
        grid=(n // t,),
        in_specs=[
            pl.BlockSpec((t, Q_LORA), lambda i: (i, COL_CQ // Q_LORA)),
            pl.BlockSpec((t, LANES), lambda i: (i, COL_CKV // LANES)),
            pl.BlockSpec((t, LANES), lambda i: (i, COL_KR // LANES)),
            tab_spec, tab_spec, tab_spec,
            pl.BlockSpec((1, Q_LORA), const),
            pl.BlockSpec((Q_LORA, H * LANES), const),
            pl.BlockSpec((1, KV_LORA), const),
            pl.BlockSpec((KV_LORA, H * LANES), const),
            pl.BlockSpec((W_BR, KV_LORA), const),
            pl.BlockSpec((1, LANES), const),
            pl.BlockSpec((1, LANES), const),
            pl.BlockSpec((1, LANES), const),
        ],
        out_specs=[
            pl.BlockSpec((t, H * LANES), row),
            pl.BlockSpec((t, H * LANES), row),
            pl.BlockSpec((1, 1, W_BR, t), lambda i: (i // ns, i % ns, 0, 0)),
        ],
        out_shape=[
            jax.ShapeDtypeStruct((n, H * LANES), BF16),
            jax.ShapeDtypeStruct((n, H * LANES), BF16),
            jax.ShapeDtypeStruct((n // seq, ns, W_BR, t), BF16),
        ],
        compiler_params=_cparams(("parallel",)),
        name="mla_prep",
    )(pn, pn, pn, *tabs, *wts)


def _fox_prep_kernel(fq_ref, fk_ref, fv_ref, ff_ref, bf_ref, qg_ref, kg_ref, tri_ref, selq_ref, selk_ref,
                     oneq_ref, onek_ref, q_out, k_out, vt_out, carry_ref):
    t = fq_ref.shape[0]

    @pl.when(pl.program_id(1) == 0)
    def _():
        carry_ref[...] = jnp.zeros_like(carry_ref)

    z = ff_ref[...] + bf_ref[...]
    logf = jnp.minimum(z, 0.0) - jnp.log1p(jnp.exp(-jnp.abs(z)))
    cum = carry_ref[...] + _dot_exact_lhs(tri_ref[...], logf)
    carry_ref[...] = cum[t - 1:t, :]
    hi, mid, lo = _split3(cum * LOG2E)
    aug_q = _dot(hi, selq_ref[0]) + _dot(mid, selq_ref[1]) + _dot(lo, selq_ref[2]) + oneq_ref[...]
    aug_k = _dot(hi, selk_ref[0]) + _dot(mid, selk_ref[1]) + _dot(lo, selk_ref[2]) + onek_ref[...]
    vt_out[0, 0] = fv_ref[...].astype(F32).T.astype(BF16)
    lane = lax.broadcasted_iota(jnp.int32, (t, LANES), 1)
    first = lane < HD
    for src_ref, g_ref, aug, dst in ((fq_ref, qg_ref, aug_q, q_out), (fk_ref, kg_ref, aug_k, k_out)):
        for p in range(H // 2):
            sl = slice(p * LANES, (p + 1) * LANES)
            x = src_ref[:, sl].astype(F32)
            ms = _pair_seg_sum(x * x) / HD
            xn = x * lax.rsqrt(ms + EPS) * g_ref[:, sl]
            he, ho = 2 * p, 2 * p + 1
            dst[:, he * LANES:(he + 1) * LANES] = jnp.where(first, xn, aug[:, he * LANES:(he + 1) * LANES]).astype(BF16)
            dst[:, ho * LANES:(ho + 1) * LANES] = jnp.where(
                first, pltpu.roll(xn, HD, 1), aug[:, ho * LANES:(ho + 1) * LANES]).astype(BF16)


def _fox_prep(pw, pn, wts, batch, seq, t):
    n = pw.shape[0]
    ns = seq // t
    row = lambda b, i: (b * ns + i, 0)
    const2 = lambda b, i: (0, 0)
    const3 = lambda b, i: (0, 0, 0)
    return pl.pallas_call(
        _fox_prep_kernel,
        grid=(batch, ns),
        in_specs=[
            pl.BlockSpec((t, W_BR), lambda b, i: (b * ns + i, COL_FQ // W_BR)),
            pl.BlockSpec((t, W_BR), lambda b, i: (b * ns + i, COL_FK // W_BR)),
            pl.BlockSpec((t, W_BR), lambda b, i: (b * ns + i, COL_FV // W_BR)),
            pl.BlockSpec((t, LANES), lambda b, i: (b * ns + i, COL_FF // LANES)),
            pl.BlockSpec((1, LANES), const2),
            pl.BlockSpec((1, W_BR), const2),
            pl.BlockSpec((1, W_BR), const2),
            pl.BlockSpec((t, t), const2),
            pl.BlockSpec((3, LANES, H * LANES), const3),
            pl.BlockSpec((3, LANES, H * LANES), const3),
            pl.BlockSpec((1, H * LANES), const2),
            pl.BlockSpec((1, H * LANES), const2),
        ],
        out_specs=[
            pl.BlockSpec((t, H * LANES), row),
            pl.BlockSpec((t, H * LANES), row),
            pl.BlockSpec((1, 1, W_BR, t), lambda b, i: (b, i, 0, 0)),
        ],
        out_shape=[
            jax.ShapeDtypeStruct((n, H * LANES), BF16),
            jax.ShapeDtypeStruct((n, H * LANES), BF16),
            jax.ShapeDtypeStruct((batch, ns, W_BR, t), BF16),
        ],
        scratch_shapes=[pltpu.VMEM((1, LANES), F32)],
        compiler_params=_cparams(("parallel", "arbitrary")),
        name="fox_prep",
    )(pw, pw, pw, pn, *wts)


def _attn_kernel(q_ref, k_ref, vt_ref, g_ref, o_ref, m_ref, acc_ref, sa_ref, sb_ref, mxa_ref, mxb_ref, *, tk):
    tq = ATTN_QT * tk
    gi = pl.program_id(2)
    nt = (((1,), (1,)), ((), ()))
    kv_pos = lax.broadcasted_iota(jnp.int32, (tk, tq), 0)
    q_pos = lax.broadcasted_iota(jnp.int32, (tk, tq), 1)
    causal = kv_pos <= q_pos
    both = slice(0, tq)
    ones = jnp.ones((ONES_ROWS, tk), BF16)
    m_ref[...] = jnp.full(m_ref.shape, NEG_INF, F32)
    acc_ref[...] = jnp.zeros(acc_ref.shape, F32)

    def scores(c, s_ref, mx_ref, cols):
        start = pl.multiple_of(c * tk, tk)
        for j in range(2):
            q = q_ref[cols, j * LANES:(j + 1) * LANES]
            k = k_ref[pl.ds(start, tk), j * LANES:(j + 1) * LANES]
            s = lax.dot_general(k, q, nt, preferred_element_type=F32)
            s_ref[j, :, cols] = s
            mx_ref[j, :, cols] = jnp.max(s, axis=0, keepdims=True)

    def consume(c, s_ref, mx_ref, cols, mask):
        for j in range(2):
            vt = jnp.concatenate([vt_ref[0, c, j * HD:(j + 1) * HD, :], ones], axis=0)
            s = s_ref[j, :, cols]
            if mask is not None:
                s = jnp.where(mask, s, NEG_INF)
                m_cur = jnp.max(s, axis=0, keepdims=True)
            else:
                m_cur = mx_ref[j, :, cols]
            m_old = m_ref[j, :, cols]
            m_new = jnp.maximum(m_old, m_cur)
            alpha = jnp.exp2(m_old - m_new)
            p = jnp.exp2(s - m_new)
            acc_ref[j, :, cols] = alpha * acc_ref[j, :, cols] + _dot(vt, p.astype(BF16))
            m_ref[j, :, cols] = m_new

    scores(0, sa_ref, mxa_ref, both)

    def body(i, carry):
        c = 2 * i
        scores(c + 1, sb_ref, mxb_ref, both)
        consume(c, sa_ref, mxa_ref, both, None)
        scores(c + 2, sa_ref, mxa_ref, both)
        consume(c + 1, sb_ref, mxb_ref, both, None)
        return carry

    lax.fori_loop(0, gi * (ATTN_QT // 2), body, 0)
    bufs = ((sa_ref, mxa_ref), (sb_ref, mxb_ref))
    for d in range(ATTN_QT):
        c = ATTN_QT * gi + d
        if d + 1 < ATTN_QT:
            scores(c + 1, *bufs[(d + 1) % 2], slice((d + 1) * tk, tq))
        consume(c, *bufs[d % 2], slice(d * tk, tq), causal[:, :tq - d * tk])

    o_t = jnp.concatenate([acc_ref[j, :HD, :] / acc_ref[j, HD:HD + 1, :] for j in range(2)], axis=0)
    g = g_ref[...].astype(F32)
    o_ref[...] = (o_t.T * (g * _sigmoid(g))).astype(BF16)


def _attention(q, k, vt, pw, gate_col, batch, seq, tk):
    n = q.shape[0]
    tq = ATTN_QT * tk
    nq, nk = seq // tq, seq // tk
    return pl.pallas_call(
        functools.partial(_attn_kernel, tk=tk),
        grid=(batch, H // 2, nq),
        in_specs=[
            pl.BlockSpec((tq, 2 * LANES), lambda b, hp, i: (b * nq + i, hp)),
            pl.BlockSpec((seq, 2 * LANES), lambda b, hp, i: (b, hp)),
            pl.BlockSpec((1, nk, LANES, tk), lambda b, hp, i: (b, 0, hp, 0)),
            pl.BlockSpec((tq, LANES), lambda b, hp, i: (b * nq + i, gate_col // LANES + hp)),
        ],
        out_specs=pl.BlockSpec((tq, LANES), lambda b, hp, i: (b * nq + i, hp)),
        out_shape=jax.ShapeDtypeStruct((n, W_BR), BF16),
        scratch_shapes=[
            pltpu.VMEM((2, 1, tq), F32),
            pltpu.VMEM((2, HD + ONES_ROWS, tq), F32),
            pltpu.VMEM((2, tk, tq), F32),
            pltpu.VMEM((2, tk, tq), F32),
            pltpu.VMEM((2, 1, tq), F32),
            pltpu.VMEM((2, 1, tq), F32),
        ],
        compiler_params=_cparams(("parallel", "parallel", "arbitrary")),
        name="attention",
    )(q, k, vt, pw)


def _rwkv_prep_kernel(*refs, first_layer):
    if first_layer:
        (sr_ref, sk_ref, sv_ref, swa_ref, gc_ref, mur_ref, muk_ref, muv_ref, muwa_ref, w0_ref, a0_ref, wup_ref,
         aup_ref, kk_ref, ka_ref, rk_ref, bt_ref, bo_ref,
         at_o, rt_o, bt_o, kt_o, v_o, bh_o, kh_o, wc_o, bonus_o, sg_o, vfirst_o,
         cr_ref, ck_ref, cv_ref, cwa_ref) = refs
    else:
        (sr_ref, sk_ref, sv_ref, swa_ref, gc_ref, vf_ref, mur_ref, muk_ref, muv_ref, muwa_ref, w0_ref, a0_ref,
         wup_ref, aup_ref, kk_ref, ka_ref, rk_ref, v0_ref, vdown_ref, vup_ref, bt_ref, bo_ref,
         at_o, rt_o, bt_o, kt_o, v_o, bh_o, kh_o, wc_o, bonus_o, sg_o,
         cr_ref, ck_ref, cv_ref, cwa_ref) = refs
    t = sr_ref.shape[0]

    def shift(x_ref, c_ref, mu_ref):
        x = x_ref[...].astype(F32)
        rowid = lax.broadcasted_iota(jnp.int32, x.shape, 0)
        prev = jnp.where(rowid == 0, c_ref[...], pltpu.roll(x, 1, 0))
        c_ref[...] = x[t - 1:t, :]
        return x + mu_ref[...] * (prev - x)

    r = shift(sr_ref, cr_ref, mur_ref)
    k = shift(sk_ref, ck_ref, muk_ref)
    v = shift(sv_ref, cv_ref, muv_ref)
    wa = shift(swa_ref, cwa_ref, muwa_ref)

    logw = -RWKV_DECAY_SCALE * _sigmoid(w0_ref[...] + _dot(jnp.tanh(wa).astype(BF16), wup_ref[...]))
    a = _sigmoid(a0_ref[...] + _dot(wa.astype(BF16), aup_ref[...]))
    if first_layer:
        vfirst_o[...] = v
    else:
        low = _dot(v.astype(BF16), vdown_ref[...]).astype(BF16)
        nu = _sigmoid(v0_ref[...] + _dot(low, vup_ref[...]))
        v = v + (vf_ref[...] - v) * nu

    kk = k * kk_ref[...]
    k_mod = k * (1.0 + (a - 1.0) * ka_ref[...])
    rk = r * k_mod * rk_ref[...]
    kk_n, bonus = [], []
    for p in range(H // 2):
        sl = slice(p * LANES, (p + 1) * LANES)
        nrm = jnp.sqrt(_pair_seg_sum(kk[:, sl] * kk[:, sl]))
        kk_n.append(kk[:, sl] / jnp.maximum(nrm, 1e-12))
        bonus.append(_pair_seg_sum(rk[:, sl]) * v[:, sl])
    kk = jnp.concatenate(kk_n, axis=-1)
    bonus = jnp.concatenate(bonus, axis=-1)
    a_vec = -kk
    b_vec = kk * a

    terms = _split3(logw)
    lcum = sum(_dot(bt_ref[...], x) for x in terms)
    ltot = sum(_dot(bo_ref[...], x) for x in terms)
    e_minus = jnp.exp(-lcum)
    e_rem = jnp.exp(ltot - lcum)
    g = gc_ref[...].astype(F32)
    for o_ref, val in ((at_o, a_vec * jnp.exp(lcum - logw)), (rt_o, r * jnp.exp(lcum)), (bt_o, b_vec * e_minus),
                       (kt_o, k_mod * e_minus), (v_o, v), (bh_o, b_vec * e_rem), (kh_o, k_mod * e_rem),
                       (wc_o, jnp.exp(ltot)), (bonus_o, bonus), (sg_o, g * _sigmoid(g))):
        o_ref[...] = val.astype(o_ref.dtype)


def _mm(a, b, dims):
    return lax.dot_general(a.astype(BF16), b.astype(BF16), dims, preferred_element_type=F32)


_NN = (((1,), (0,)), ((), ()))
_NT = (((1,), (1,)), ((), ()))
_TN = (((0,), (0,)), ((), ()))


def _rwkv_chunk_kernel(at_ref, rt_ref, bt_ref, kt_ref, v_ref, bh_ref, kh_ref, wc_ref, bonus_ref, sg_ref,
                       lg_ref, lb_ref, o_ref, s_all_ref, *, nchunk):
    mm = _mm
    first = lax.broadcasted_iota(jnp.int32, (CHUNK, LANES), 1) < HD
    r2 = lax.broadcasted_iota(jnp.int32, (LANES, LANES), 0)
    c2 = lax.broadcasted_iota(jnp.int32, (LANES, LANES), 1)
    same_head = (r2 < HD) == (c2 < HD)
    eye2 = r2 == c2
    t_row, t_col = r2 % CHUNK, c2 % CHUNK
    strict2 = jnp.logical_and(same_head, t_col < t_row)
    incl2 = jnp.logical_and(same_head, t_col <= t_row)
    each = lambda f, *cols: [f(*xs) for xs in zip(*cols)]
    pairs = [(c, p) for c in range(nchunk) for p in range(H // 2)]
    npair = range(len(pairs))
    ld = lambda ref: [ref[c * CHUNK:(c + 1) * CHUNK, p * LANES:(p + 1) * LANES] for c, p in pairs]
    at, rt, bt, kt, v, bh, kh = ld(at_ref), ld(rt_ref), ld(bt_ref), ld(kt_ref), ld(v_ref), ld(bh_ref), ld(kh_ref)
    zero = jnp.zeros((CHUNK, LANES), BF16)
    stack = lambda x: jnp.concatenate([jnp.where(first, x, zero), jnp.where(first, zero, x)], axis=0)
    fold = lambda x: x[:CHUNK] + x[CHUNK:]
    a_s, b_s, k_s, v_s = each(stack, at), each(stack, bt), each(stack, kt), each(stack, v)
    x1 = each(lambda a, r: jnp.concatenate([a, stack(r.astype(BF16))], axis=0), a_s, rt)
    gb = each(lambda x, b: mm(x, b, _NT), x1, b_s)
    gk = each(lambda x, k: mm(x, k, _NT), x1, k_s)
    a_ab = [jnp.where(strict2, g[:LANES], 0.0) for g in gb]
    a_rb = [jnp.where(incl2, g[LANES:], 0.0) for g in gb]
    a_ak = [jnp.where(strict2, g[:LANES], 0.0) for g in gk]
    a_rk = [jnp.where(incl2, g[LANES:], 0.0) for g in gk]
    tinv = [jnp.where(eye2, 1.0, a) for a in a_ab]
    pw = a_ab
    for _ in range(int(math.log2(CHUNK)) - 1):
        pw = each(lambda p: mm(p, p, _NN), pw)
        tinv = each(lambda t, p: t + mm(t, p, _NN), tinv, pw)
    av = each(lambda a, x: mm(a, x, _NN), a_ak, v_s)
    tx = each(lambda t, x, a: mm(t, jnp.concatenate([x.astype(BF16), a], axis=1), _NN), tinv, av, a_s)
    ry = each(lambda a, x: mm(a, x, _NN), a_rb, tx)
    yk = each(lambda a, x: mm(a, x, _NN), a_rk, v_s)
    u = [fold(t[:, :LANES]) for t in tx]
    a_til = [fold(t[:, LANES:]) for t in tx]
    y_in = [fold(r[:, :LANES] + k) for r, k in zip(ry, yk)]
    r_hat = [r + fold(x[:, LANES:]) for r, x in zip(rt, ry)]
    m_new = [jnp.where(same_head, mm(a_til[i], bh[i], _TN), 0.0) for i in npair]
    n_new = [jnp.where(same_head, mm(jnp.concatenate([u[i].astype(BF16), v[i]], axis=0),
                                     jnp.concatenate([bh[i], kh[i]], axis=0), _TN), 0.0) for i in npair]
    lg, lb = lg_ref[...], lb_ref[...]
    for c in range(nchunk):
        idx = [i for i, (ci, _) in enumerate(pairs) if ci == c]
        s0 = [s_all_ref[p] for p in range(H // 2)]
        y = [y_in[i] + mm(r_hat[i], s0[p], _NT) for p, i in enumerate(idx)]
        for p, i in enumerate(idx):
            wc = wc_ref[c * CHUNK:c * CHUNK + 1, p * LANES:(p + 1) * LANES]
            s_all_ref[p] = mm(s0[p], jnp.where(eye2, wc, 0.0) + m_new[i], _NN) + n_new[i]
        for p, i in enumerate(idx):
            rows, lanes = slice(c * CHUNK, (c + 1) * CHUNK), slice(p * LANES, (p + 1) * LANES)
            mu = _pair_seg_sum(y[p]) / HD
            d = y[p] - mu
            var = _pair_seg_sum(d * d) / HD
            yn = d * lax.rsqrt(var + GN_EPS) * lg[:, lanes] + lb[:, lanes]
            o_ref[rows, lanes] = ((yn + bonus_ref[rows, lanes]) * sg_ref[rows, lanes]).astype(o_ref.dtype)


_RWKV_STAGE_DTYPES = (BF16, F32, BF16, BF16, BF16, BF16, BF16, F32, F32, F32)


def _rwkv_kernel(*refs, first_layer, nchunk, ns):
    n_in = 18 if first_layer else 22
    ins = refs[:n_in]
    lg_ref, lb_ref, o_ref = refs[n_in:n_in + 3]
    pos = n_in + 3
    vfirst = refs[pos:pos + 1] if first_layer else ()
    pos += len(vfirst)
    stage_a, stage_b = refs[pos:pos + 10], refs[pos + 10:pos + 20]
    carries = refs[pos + 20:pos + 24]
    s_all_ref = refs[pos + 24]
    i = pl.program_id(1)

    @pl.when(i == 0)
    def _():
        for r in stage_b + carries + (s_all_ref,):
            r[...] = jnp.zeros(r.shape, r.dtype)

    def run(done, todo):
        _rwkv_prep_kernel(*ins, *todo, *vfirst, *carries, first_layer=first_layer)
        _rwkv_chunk_kernel(*done, lg_ref, lb_ref, o_ref, s_all_ref, nchunk=nchunk)

    @pl.when(i % 2 == 0)
    def _():
        run(stage_b, stage_a)

    @pl.when(i % 2 == 1)
    def _():
        run(stage_a, stage_b)


def _rwkv(pw, pn, vfirst, wts, lnx_g, lnx_b, batch, seq, t, first_layer):
    n = pw.shape[0]
    ns = seq // t
    const2 = lambda b, i: (0, 0)
    cur = lambda b, i: b * ns + jnp.minimum(i, ns - 1)
    pcol = lambda col, w: pl.BlockSpec((t, w), lambda b, i: (cur(b, i), col // w))
    vec = pl.BlockSpec((1, W_BR), const2)
    in_specs = [pcol(COL_SR, W_BR), pcol(COL_SK, W_BR), pcol(COL_SV, W_BR), pcol(COL_WA, LANES),
                pcol(COL_GATE_C, W_BR)]
    args = [pw, pw, pw, pn, pw]
    if not first_layer:
        in_specs.append(pl.BlockSpec((t, W_BR), lambda b, i: (cur(b, i), 0)))
        args.append(vfirst)
    in_specs += [vec, vec, vec, pl.BlockSpec((1, LANES), const2), vec, vec,
                 pl.BlockSpec((LANES, W_BR), const2), pl.BlockSpec((LANES, W_BR), const2), vec, vec, vec]
    if not first_layer:
        in_specs += [vec, pl.BlockSpec((W_BR, LANES), const2), pl.BlockSpec((LANES, W_BR), const2)]
    in_specs += [pl.BlockSpec((t, t), const2)] * 2 + [vec, vec]
    args += list(wts) + [lnx_g, lnx_b]
    out_specs = [pl.BlockSpec((t, W_BR), lambda b, i: (b * ns + jnp.maximum(i - 1, 0), 0))]
    out_shape = [jax.ShapeDtypeStruct((n, W_BR), BF16)]
    if first_layer:
        out_specs.append(pl.BlockSpec((t, W_BR), lambda b, i: (jnp.where(i < ns, b * ns + i, batch * ns + b), 0)))
        out_shape.append(jax.ShapeDtypeStruct((n + batch * t, W_BR), F32))
    return pl.pallas_call(
        functools.partial(_rwkv_kernel, first_layer=first_layer, nchunk=t // CHUNK, ns=ns),
        grid=(batch, ns + 1),
        in_specs=in_specs,
        out_specs=out_specs,
        out_shape=out_shape,
        scratch_shapes=[pltpu.VMEM((t, W_BR), dt) for dt in _RWKV_STAGE_DTYPES * 2]
        + [pltpu.VMEM((1, W_BR), F32), pltpu.VMEM((1, W_BR), F32), pltpu.VMEM((1, W_BR), F32),
           pltpu.VMEM((1, LANES), F32), pltpu.VMEM((H // 2, LANES, LANES), F32)],
        compiler_params=_cparams(("parallel", "arbitrary")),
        name="rwkv",
    )(*args)


def _merge_kernel(oa_ref, ob_ref, oc_ref, ga_ref, gb_ref, gc_ref, x_ref, wpa_ref, wpb_ref, wpc_ref, wout_ref,
                  o_ref):
    pa = _dot(oa_ref[...], wpa_ref[...])
    pb = _dot(ob_ref[...], wpb_ref[...])
    pc = _dot(oc_ref[...], wpc_ref[...])
    sig = lambda ref: _sigmoid(ref[...].astype(F32))
    merged = sig(ga_ref) * pa + sig(gb_ref) * pb + sig(gc_ref) * pc
    o_ref[...] = x_ref[...] + _dot(merged.astype(BF16), wout_ref[...])


def _merge(oa, ob, oc, p, x2, wpa, wpb, wpc, wout, batch, seq, t):
    n = x2.shape[0]
    ns = seq // t
    row = lambda b, i: (b * ns + i, 0)
    const2 = lambda b, i: (0, 0)
    gate = lambda g: pl.BlockSpec((t, D_MODEL), lambda b, i: (b * ns + i, COL_MERGE // D_MODEL + g))
    return pl.pallas_call(
        _merge_kernel,
        grid=(batch, ns),
        in_specs=[
            pl.BlockSpec((t, W_BR), row),
            pl.BlockSpec((t, W_BR), row),
            pl.BlockSpec((t, W_BR), row),
            gate(0), gate(1), gate(2),
            pl.BlockSpec((t, D_MODEL), row),
            pl.BlockSpec((W_BR, D_MODEL), const2),
            pl.BlockSpec((W_BR, D_MODEL), const2),
            pl.BlockSpec((W_BR, D_MODEL), const2),
            pl.BlockSpec((D_MODEL, D_MODEL), const2),
        ],
        out_specs=pl.BlockSpec((t, D_MODEL), row),
        out_shape=jax.ShapeDtypeStruct((n, D_MODEL), F32),
        compiler_params=_cparams(("parallel", "parallel")),
        name="merge",
    )(oa, ob, oc, p, p, p, x2, wpa, wpb, wpc, wout)


def _regroup_w_in(w_in, l):
    d = w_in.shape[1]
    sizes = (Q_LORA, KV_LORA, ROPE_D, W_BR, W_BR, W_BR, W_BR, H, W_BR,
             3 * W_BR + DECAY_LORA + AAA_LORA, W_BR, 3 * D_MODEL)
    offs = [int(o) for o in np.concatenate([[0], np.cumsum(sizes)])]
    (c_q, c_kv, k_r, gate_a, fq, fk, fv, ff, gate_b, shift, gate_c, merge) = [
        w_in[l, :, offs[i]:offs[i + 1]].astype(BF16) for i in range(len(sizes))]
    z = lambda n: jnp.zeros((d, n), BF16)
    wide = jnp.concatenate([
        merge, gate_a, fq, fk, fv, gate_b,
        shift[:, :W_BR], shift[:, W_BR:2 * W_BR], shift[:, 2 * W_BR:3 * W_BR], gate_c,
    ], axis=1)
    narrow = jnp.concatenate([
        c_q, c_kv,
        z(NOPE), k_r, z(LANES - NOPE - ROPE_D),
        ff, z(LANES - H),
        shift[:, 3 * W_BR:],
    ], axis=1)
    return wide, narrow


def _rope_tables(seq):
    half = ROPE_D // 2
    inv = ROPE_THETA ** (-jnp.arange(0, ROPE_D, 2, dtype=F32) / ROPE_D)
    ang = jnp.arange(seq, dtype=F32)[:, None] * inv[None, :]
    cos, sin = jnp.cos(ang), jnp.sin(ang)
    z = lambda n: jnp.zeros((seq, n), F32)
    c = jnp.concatenate([jnp.ones((seq, NOPE), F32), cos, cos, z(LANES - NOPE - ROPE_D)], axis=1)
    s1 = jnp.concatenate([z(NOPE), -sin, z(LANES - NOPE - half)], axis=1)
    s2 = jnp.concatenate([z(NOPE + half), sin, z(LANES - NOPE - ROPE_D)], axis=1)
    return c, s1, s2


def _pad_lanes(v, lo, total=LANES):
    return jnp.zeros((1, total), F32).at[0, lo:lo + v.shape[0]].set(v)


def _fox_selectors():
    selq = np.zeros((3, LANES, H * LANES), np.float32)
    selk = np.zeros((3, LANES, H * LANES), np.float32)
    oneq = np.zeros((1, H * LANES), np.float32)
    onek = np.zeros((1, H * LANES), np.float32)
    for h in range(H):
        for j in range(3):
            selq[j, h, h * LANES + HD + j] = 1.0
            selk[j, h, h * LANES + HD + 3 + j] = -1.0
            oneq[0, h * LANES + HD + 3 + j] = 1.0
            onek[0, h * LANES + HD + j] = 1.0
    return (jnp.asarray(selq, BF16), jnp.asarray(selk, BF16), jnp.asarray(oneq), jnp.asarray(onek))


def _tri(t):
    r = np.arange(t)
    return jnp.asarray((r[None, :] <= r[:, None]).astype(np.float32), BF16)


def _chunk_tri(t):
    r = np.arange(t)
    same = (r[None, :] // CHUNK) == (r[:, None] // CHUNK)
    lower = r[None, :] <= r[:, None]
    return (jnp.asarray((same & lower).astype(np.float32), BF16), jnp.asarray(same.astype(np.float32), BF16))


def _tile(seq, pref):
    return pref if seq % pref == 0 else seq


def kernel(x, norm_g, w_in, mla_qa_g, mla_w_uq, mla_kva_g, mla_w_ukv, mla_q_g, mla_knope_g, mla_krope_g, fox_b_f, fox_q_g, fox_k_g, rwkv_mu, rwkv_w0, rwkv_w_up, rwkv_a0, rwkv_a_up, rwkv_k_k, rwkv_k_a, rwkv_r_k, rwkv_lnx_g, rwkv_lnx_b, rwkv_v0, rwkv_v_down, rwkv_v_up, w_pa, w_pb, w_pc, w_out):
    batch, seq, d = x.shape
    depth = w_in.shape[0]
    n = batch * seq
    t_row = _tile(seq, 512)
    t_in = _tile(n, 1024)
    t_rwkv = _tile(seq, 256)
    assert seq % (ATTN_QT * t_row) == 0 and seq % t_rwkv == 0 and t_rwkv % CHUNK == 0, seq
    x2 = x.reshape(n, d)

    tabs = _rope_tables(seq)
    selq, selk, oneq, onek = _fox_selectors()
    tri = _tri(t_row)
    btri, bones = _chunk_tri(t_rwkv)
    row = lambda v: v.reshape(1, -1).astype(F32)

    vfirst = None
    for l in range(depth):
        w_wide, w_narrow = _regroup_w_in(w_in, l)
        pw = _inproj(x2, row(norm_g[l]), w_wide, t_in, INPROJ_TN, BF16)
        pn = _inproj(x2, row(norm_g[l]), w_narrow, t_in, NCOLS_NARROW, F32)

        wuq = mla_w_uq[l].reshape(Q_LORA, H, NOPE + ROPE_D)
        wuq = jnp.pad(wuq, ((0, 0), (0, 0), (0, LANES - NOPE - ROPE_D))).reshape(Q_LORA, H * LANES).astype(BF16)
        wukv = mla_w_ukv[l].reshape(KV_LORA, H, NOPE + HD)
        wk = jnp.pad(wukv[:, :, :NOPE], ((0, 0), (0, 0), (0, LANES - NOPE))).reshape(KV_LORA, H * LANES).astype(BF16)
        wv = wukv[:, :, NOPE:].reshape(KV_LORA, W_BR).T.astype(BF16)
        mla_scale = float(NOPE + ROPE_D) ** -0.5 * LOG2E
        mla_wts = (row(mla_qa_g[l]), wuq, row(mla_kva_g[l]), wk, wv,
                   _pad_lanes(mla_q_g[l] * mla_scale, 0), _pad_lanes(mla_knope_g[l], 0),
                   _pad_lanes(mla_krope_g[l], NOPE))
        qa, ka, va = _mla_prep(pn, tabs, mla_wts, seq, t_row)
        o_a = _attention(qa, ka, va, pw, COL_GATE_A, batch, seq, t_row)

        fox_wts = (_pad_lanes(fox_b_f[l], 0), row(jnp.tile(fox_q_g[l] * (float(HD) ** -0.5 * LOG2E), H)),
                   row(jnp.tile(fox_k_g[l], H)), tri, selq, selk, oneq, onek)
        qb, kb, vb = _fox_prep(pw, pn, fox_wts, batch, seq, t_row)
        o_b = _attention(qb, kb, vb, pw, COL_GATE_B, batch, seq, t_row)

        mu = rwkv_mu[l]
        wup = jnp.zeros((LANES, W_BR), F32).at[:DECAY_LORA].set(rwkv_w_up[l]).astype(BF16)
        aup = jnp.zeros((LANES, W_BR), F32).at[DECAY_LORA:].set(rwkv_a_up[l]).astype(BF16)
        rw = [row(mu[:W_BR]), row(mu[W_BR:2 * W_BR]), row(mu[2 * W_BR:3 * W_BR]), row(mu[3 * W_BR:]),
              row(rwkv_w0[l]), row(rwkv_a0[l]), wup, aup, row(rwkv_k_k[l]), row(rwkv_k_a[l]),
              row(rwkv_r_k[l])]
        if l > 0:
            vdown = jnp.zeros((W_BR, LANES), F32).at[:, :MV_LORA].set(rwkv_v_down[l - 1]).astype(BF16)
            vup = jnp.zeros((LANES, W_BR), F32).at[:MV_LORA].set(rwkv_v_up[l - 1]).astype(BF16)
            rw += [row(rwkv_v0[l - 1]), vdown, vup]
        rw += [btri, bones]
        outs = _rwkv(pw, pn, vfirst, rw, row(rwkv_lnx_g[l]), row(rwkv_lnx_b[l]), batch, seq, t_rwkv,
                     first_layer=(l == 0))
        o_c = outs[0]
        if l == 0:
            vfirst = outs[1]

        x2 = _merge(o_a, o_b, o_c, pw, x2, w_pa[l].astype(BF16), w_pb[l].astype(BF16),
                    w_pc[l].astype(BF16), w_out[l].astype(BF16), batch, seq, t_row)
    return x2.reshape(batch, seq, d)
```

```python
import functools
import math

import jax
import jax.numpy as jnp
import numpy as np
from jax import lax
from jax.experimental import pallas as pl
from jax.experimental.pallas import tpu as pltpu

F32 = jnp.float32
BF16 = jnp.bfloat16

LANES = 128
H = 8
HD = 64
NOPE = 64
ROPE_D = 32
Q_LORA = 256
KV_LORA = 128
D_MODEL = 1024
W_BR = H * HD
DECAY_LORA = 64
AAA_LORA = 64
MV_LORA = 32
ROPE_THETA = 10000.0
RWKV_DECAY_SCALE = 0.606531
GN_EPS = 64e-5
EPS = 1e-6
NEG_INF = -1e30
LOG2E = math.log2(math.e)
CHUNK = 64
ATTN_QT = 4
ONES_ROWS = 16

COL_MERGE = 0
COL_GATE_A = 3072
COL_FQ = 3584
COL_FK = 4096
COL_FV = 4608
COL_GATE_B = 5120
COL_SR = 5632
COL_SK = 6144
COL_SV = 6656
COL_GATE_C = 7168
NCOLS_WIDE = 7680
INPROJ_TN = 2560
COL_CQ = 0
COL_CKV = 256
COL_KR = 384
COL_FF = 512
COL_WA = 640
NCOLS_NARROW = 768

VMEM_LIMIT = 56 * 1024 * 1024


def _cparams(sem):
    return pltpu.CompilerParams(dimension_semantics=sem, vmem_limit_bytes=VMEM_LIMIT)


def _sigmoid(x):
    return 0.5 * jnp.tanh(0.5 * x) + 0.5


def _dot(a, b):
    return jnp.dot(a, b, preferred_element_type=F32)


def _split3(x):
    hi = x.astype(BF16)
    r1 = x - hi.astype(F32)
    mid = r1.astype(BF16)
    lo = (r1 - mid.astype(F32)).astype(BF16)
    return hi, mid, lo


def _dot_exact_lhs(m_bf16, x):
    hi, mid, lo = _split3(x)
    return _dot(m_bf16, hi) + _dot(m_bf16, mid) + _dot(m_bf16, lo)


def _pair_seg_sum(x):
    lane = lax.broadcasted_iota(jnp.int32, x.shape, 1)
    first = lane < HD
    s0 = jnp.sum(jnp.where(first, x, 0.0), axis=-1, keepdims=True)
    s1 = jnp.sum(jnp.where(first, 0.0, x), axis=-1, keepdims=True)
    return jnp.where(first, s0, s1)


def _inproj_kernel(x_ref, g_ref, w_ref, wn_ref, o_ref, on_ref, h_ref):
    @pl.when(pl.program_id(1) == 0)
    def _():
        x = x_ref[...]
        ms = jnp.mean(x * x, axis=-1, keepdims=True)
        h = (x * lax.rsqrt(ms + EPS) * g_ref[...]).astype(BF16)
        h_ref[...] = h
        on_ref[...] = _dot(h, wn_ref[...])

    o_ref[...] = _dot(h_ref[...], w_ref[...]).astype(o_ref.dtype)


def _inproj(x2, g, w_wide, w_narrow, tm, tn):
    n = x2.shape[0]
    return pl.pallas_call(
        _inproj_kernel,
        grid=(n // tm, NCOLS_WIDE // tn),
        in_specs=[
            pl.BlockSpec((tm, D_MODEL), lambda i, j: (i, 0)),
            pl.BlockSpec((1, D_MODEL), lambda i, j: (0, 0)),
            pl.BlockSpec((D_MODEL, tn), lambda i, j: (0, j)),
            pl.BlockSpec((D_MODEL, NCOLS_NARROW), lambda i, j: (0, 0)),
        ],
        out_specs=[pl.BlockSpec((tm, tn), lambda i, j: (i, j)),
                   pl.BlockSpec((tm, NCOLS_NARROW), lambda i, j: (i, 0))],
        out_shape=[jax.ShapeDtypeStruct((n, NCOLS_WIDE), BF16), jax.ShapeDtypeStruct((n, NCOLS_NARROW), F32)],
        scratch_shapes=[pltpu.VMEM((tm, D_MODEL), BF16)],
        compiler_params=_cparams(("parallel", "arbitrary")),
        name="inproj",
    )(x2, g, w_wide, w_narrow)


def _rms(x, g, n, keep=None):
    sq = x * x
    if keep is not None:
        sq = jnp.where(keep, sq, 0.0)
    ms = jnp.sum(sq, axis=-1, keepdims=True) / n
    return x * lax.rsqrt(ms + EPS) * g


def _rope(x, c, s):
    return x * c + pltpu.roll(x, LANES - ROPE_D // 2, 1) * s


def _mla_prep_kernel(cq_ref, ckv_ref, kr_ref, c_ref, s_ref, qag_ref, wuq_ref, kvag_ref, wk_ref, wv_ref,
                     qg_ref, kng_ref, krg_ref, q_out, k_out, vt_out):
    c, s = c_ref[...], s_ref[...]
    keep = lax.broadcasted_iota(jnp.int32, c.shape, 1) < NOPE + ROPE_D
    cqn = _rms(cq_ref[...], qag_ref[...], Q_LORA).astype(BF16)
    ckvn = _rms(ckv_ref[...], kvag_ref[...], KV_LORA).astype(BF16)
    q_all = _dot(cqn, wuq_ref[...])
    k_all = _dot(ckvn, wk_ref[...])
    vt_out[0, 0] = lax.dot_general(wv_ref[...], ckvn, _NT, preferred_element_type=F32).astype(BF16)
    k_rope = _rope(_rms(kr_ref[...], krg_ref[...], ROPE_D, keep), c, s)
    qg, kng = qg_ref[...], kng_ref[...]
    for h in range(H):
        sl = slice(h * LANES, (h + 1) * LANES)
        q_out[:, sl] = _rope(_rms(q_all[:, sl], qg, NOPE + ROPE_D, keep), c, s).astype(BF16)
        k_out[:, sl] = (_rms(k_all[:, sl], kng, NOPE) + k_rope).astype(BF16)


def _mla_prep(pn, tabs, wts, seq, t):
    n = pn.shape[0]
    ns = seq // t
    row = lambda i: (i, 0)
    const = lambda i: (0, 0)
    tab_spec = pl.BlockSpec((t, LANES), lambda i: (i % ns, 0))
    return pl.pallas_call(
        _mla_prep_kernel,
        grid=(n // t,),
        in_specs=[
            pl.BlockSpec((t, Q_LORA), lambda i: (i, COL_CQ // Q_LORA)),
            pl.BlockSpec((t, LANES), lambda i: (i, COL_CKV // LANES)),
            pl.BlockSpec((t, LANES), lambda i: (i, COL_KR // LANES)),
            tab_spec, tab_spec,
            pl.BlockSpec((1, Q_LORA), const),
            pl.BlockSpec((Q_LORA, H * LANES), const),
            pl.BlockSpec((1, KV_LORA), const),
            pl.BlockSpec((KV_LORA, H * LANES), const),
            pl.BlockSpec((W_BR, KV_LORA), const),
            pl.BlockSpec((1, LANES), const),
            pl.BlockSpec((1, LANES), const),
            pl.BlockSpec((1, LANES), const),
        ],
        out_specs=[
            pl.BlockSpec((t, H * LANES), row),
            pl.BlockSpec((t, H * LANES), row),
            pl.BlockSpec((1, 1, W_BR, t), lambda i: (i // ns, i % ns, 0, 0)),
        ],
        out_shape=[
            jax.ShapeDtypeStruct((n, H * LANES), BF16),
            jax.ShapeDtypeStruct((n, H * LANES), BF16),
            jax.ShapeDtypeStruct((n // seq, ns, W_BR, t), BF16),
        ],
        compiler_params=_cparams(("parallel",)),
        name="mla_prep",
    )(pn, pn, pn, *tabs, *wts)


def _fox_prep_kernel(fq_ref, fk_ref, fv_ref, ff_ref, bf_ref, qg_ref, kg_ref, tri_ref, selq_ref, selk_ref,
                     oneq_ref, onek_ref, q_out, k_out, vt_out, carry_ref):
    t = fq_ref.shape[0]

    @pl.when(pl.program_id(1) == 0)
    def _():
        carry_ref[...] = jnp.zeros_like(carry_ref)

    z = ff_ref[...] + bf_ref[...]
    logf = jnp.minimum(z, 0.0) - jnp.log1p(jnp.exp(-jnp.abs(z)))
    cum = carry_ref[...] + _dot_exact_lhs(tri_ref[...], logf)
    carry_ref[...] = cum[t - 1:t, :]
    hi, mid, lo = _split3(cum * LOG2E)
    aug_q = _dot(hi, selq_ref[0]) + _dot(mid, selq_ref[1]) + _dot(lo, selq_ref[2]) + oneq_ref[...]
    aug_k = _dot(hi, selk_ref[0]) + _dot(mid, selk_ref[1]) + _dot(lo, selk_ref[2]) + onek_ref[...]
    vt_out[0, 0] = fv_ref[...].astype(F32).T.astype(BF16)
    lane = lax.broadcasted_iota(jnp.int32, (t, LANES), 1)
    first = lane < HD
    for src_ref, g_ref, aug, dst in ((fq_ref, qg_ref, aug_q, q_out), (fk_ref, kg_ref, aug_k, k_out)):
        for p in range(H // 2):
            sl = slice(p * LANES, (p + 1) * LANES)
            x = src_ref[:, sl].astype(F32)
            ms = _pair_seg_sum(x * x) / HD
            xn = x * lax.rsqrt(ms + EPS) * g_ref[:, sl]
            he, ho = 2 * p, 2 * p + 1
            dst[:, he * LANES:(he + 1) * LANES] = jnp.where(first, xn, aug[:, he * LANES:(he + 1) * LANES]).astype(BF16)
            dst[:, ho * LANES:(ho + 1) * LANES] = jnp.where(
                first, pltpu.roll(xn, HD, 1), aug[:, ho * LANES:(ho + 1) * LANES]).astype(BF16)


def _fox_prep(pw, pn, wts, batch, seq, t):
    n = pw.shape[0]
    ns = seq // t
    row = lambda b, i: (b * ns + i, 0)
    const2 = lambda b, i: (0, 0)
    const3 = lambda b, i: (0, 0, 0)
    return pl.pallas_call(
        _fox_prep_kernel,
        grid=(batch, ns),
        in_specs=[
            pl.BlockSpec((t, W_BR), lambda b, i: (b * ns + i, COL_FQ // W_BR)),
            pl.BlockSpec((t, W_BR), lambda b, i: (b * ns + i, COL_FK // W_BR)),
            pl.BlockSpec((t, W_BR), lambda b, i: (b * ns + i, COL_FV // W_BR)),
            pl.BlockSpec((t, LANES), lambda b, i: (b * ns + i, COL_FF // LANES)),
            pl.BlockSpec((1, LANES), const2),
            pl.BlockSpec((1, W_BR), const2),
            pl.BlockSpec((1, W_BR), const2),
            pl.BlockSpec((t, t), const2),
            pl.BlockSpec((3, LANES, H * LANES), const3),
            pl.BlockSpec((3, LANES, H * LANES), const3),
            pl.BlockSpec((1, H * LANES), const2),
            pl.BlockSpec((1, H * LANES), const2),
        ],
        out_specs=[
            pl.BlockSpec((t, H * LANES), row),
            pl.BlockSpec((t, H * LANES), row),
            pl.BlockSpec((1, 1, W_BR, t), lambda b, i: (b, i, 0, 0)),
        ],
        out_shape=[
            jax.ShapeDtypeStruct((n, H * LANES), BF16),
            jax.ShapeDtypeStruct((n, H * LANES), BF16),
            jax.ShapeDtypeStruct((batch, ns, W_BR, t), BF16),
        ],
        scratch_shapes=[pltpu.VMEM((1, LANES), F32)],
        compiler_params=_cparams(("parallel", "arbitrary")),
        name="fox_prep",
    )(pw, pw, pw, pn, *wts)


def _attn_kernel(q_ref, k_ref, vt_ref, g_ref, o_ref, m_ref, acc_ref, sa_ref, sb_ref, mxa_ref, mxb_ref, *, tk):
    tq = ATTN_QT * tk
    gi = pl.program_id(2)
    nt = (((1,), (1,)), ((), ()))
    kv_pos = lax.broadcasted_iota(jnp.int32, (tk, tq), 0)
    q_pos = lax.broadcasted_iota(jnp.int32, (tk, tq), 1)
    causal = kv_pos <= q_pos
    both = slice(0, tq)
    ones = jnp.ones((ONES_ROWS, tk), BF16)
    m_ref[...] = jnp.full(m_ref.shape, NEG_INF, F32)
    acc_ref[...] = jnp.zeros(acc_ref.shape, F32)

    def scores(c, s_ref, mx_ref, cols):
        start = pl.multiple_of(c * tk, tk)
        for j in range(2):
            q = q_ref[cols, j * LANES:(j + 1) * LANES]
            k = k_ref[pl.ds(start, tk), j * LANES:(j + 1) * LANES]
            s = lax.dot_general(k, q, nt, preferred_element_type=F32)
            s_ref[j, :, cols] = s
            mx_ref[j, :, cols] = jnp.max(s, axis=0, keepdims=True)

    def consume(c, s_ref, mx_ref, cols, mask):
        for j in range(2):
            vt = jnp.concatenate([vt_ref[0, c, j * HD:(j + 1) * HD, :], ones], axis=0)
            s = s_ref[j, :, cols]
            if mask is not None:
                s = jnp.where(mask, s, NEG_INF)
                m_cur = jnp.max(s, axis=0, keepdims=True)
            else:
                m_cur = mx_ref[j, :, cols]
            m_old = m_ref[j, :, cols]
            m_new = jnp.maximum(m_old, m_cur)
            alpha = jnp.exp2(m_old - m_new)
            p = jnp.exp2(s - m_new)
            acc_ref[j, :, cols] = alpha * acc_ref[j, :, cols] + _dot(vt, p.astype(BF16))
            m_ref[j, :, cols] = m_new

    scores(0, sa_ref, mxa_ref, both)

    def body(i, carry):
        c = 2 * i
        scores(c + 1, sb_ref, mxb_ref, both)
        consume(c, sa_ref, mxa_ref, both, None)
        scores(c + 2, sa_ref, mxa_ref, both)
        consume(c + 1, sb_ref, mxb_ref, both, None)
        return carry

    lax.fori_loop(0, gi * (ATTN_QT // 2), body, 0)
    bufs = ((sa_ref, mxa_ref), (sb_ref, mxb_ref))
    for d in range(ATTN_QT):
        c = ATTN_QT * gi + d
        if d + 1 < ATTN_QT:
            scores(c + 1, *bufs[(d + 1) % 2], slice((d + 1) * tk, tq))
        consume(c, *bufs[d % 2], slice(d * tk, tq), causal[:, :tq - d * tk])

    o_t = jnp.concatenate([acc_ref[j, :HD, :] / acc_ref[j, HD:HD + 1, :] for j in range(2)], axis=0)
    g = g_ref[...].astype(F32)
    o_ref[...] = (o_t.T * (g * _sigmoid(g))).astype(BF16)


def _attention(q, k, vt, pw, gate_col, batch, seq, tk):
    n = q.shape[0]
    tq = ATTN_QT * tk
    nq, nk = seq // tq, seq // tk
    return pl.pallas_call(
        functools.partial(_attn_kernel, tk=tk),
        grid=(batch, H // 2, nq),
        in_specs=[
            pl.BlockSpec((tq, 2 * LANES), lambda b, hp, i: (b * nq + i, hp)),
            pl.BlockSpec((seq, 2 * LANES), lambda b, hp, i: (b, hp)),
            pl.BlockSpec((1, nk, LANES, tk), lambda b, hp, i: (b, 0, hp, 0)),
            pl.BlockSpec((tq, LANES), lambda b, hp, i: (b * nq + i, gate_col // LANES + hp)),
        ],
        out_specs=pl.BlockSpec((tq, LANES), lambda b, hp, i: (b * nq + i, hp)),
        out_shape=jax.ShapeDtypeStruct((n, W_BR), BF16),
        scratch_shapes=[
            pltpu.VMEM((2, 1, tq), F32),
            pltpu.VMEM((2, HD + ONES_ROWS, tq), F32),
            pltpu.VMEM((2, tk, tq), F32),
            pltpu.VMEM((2, tk, tq), F32),
            pltpu.VMEM((2, 1, tq), F32),
            pltpu.VMEM((2, 1, tq), F32),
        ],
        compiler_params=_cparams(("parallel", "parallel", "arbitrary")),
        name="attention",
    )(q, k, vt, pw)


def _rwkv_prep_kernel(*refs, first_layer):
    if first_layer:
        (sr_ref, sk_ref, sv_ref, swa_ref, gc_ref, mur_ref, muk_ref, muv_ref, muwa_ref, w0_ref, a0_ref, wup_ref,
         aup_ref, kk_ref, ka_ref, rk_ref, bt_ref, bo_ref,
         at_o, rt_o, bt_o, kt_o, v_o, bh_o, kh_o, wc_o, bonus_o, sg_o, vfirst_o,
         cr_ref, ck_ref, cv_ref, cwa_ref) = refs
    else:
        (sr_ref, sk_ref, sv_ref, swa_ref, gc_ref, vf_ref, mur_ref, muk_ref, muv_ref, muwa_ref, w0_ref, a0_ref,
         wup_ref, aup_ref, kk_ref, ka_ref, rk_ref, v0_ref, vdown_ref, vup_ref, bt_ref, bo_ref,
         at_o, rt_o, bt_o, kt_o, v_o, bh_o, kh_o, wc_o, bonus_o, sg_o,
         cr_ref, ck_ref, cv_ref, cwa_ref) = refs
    t = sr_ref.shape[0]

    def shift(x_ref, c_ref, mu_ref):
        x = x_ref[...].astype(F32)
        rowid = lax.broadcasted_iota(jnp.int32, x.shape, 0)
        prev = jnp.where(rowid == 0, c_ref[...], pltpu.roll(x, 1, 0))
        c_ref[...] = x[t - 1:t, :]
        return x + mu_ref[...] * (prev - x)

    r = shift(sr_ref, cr_ref, mur_ref)
    k = shift(sk_ref, ck_ref, muk_ref)
    v = shift(sv_ref, cv_ref, muv_ref)
    wa = shift(swa_ref, cwa_ref, muwa_ref)

    logw = -RWKV_DECAY_SCALE * _sigmoid(w0_ref[...] + _dot(jnp.tanh(wa).astype(BF16), wup_ref[...]))
    a = _sigmoid(a0_ref[...] + _dot(wa.astype(BF16), aup_ref[...]))
    if first_layer:
        vfirst_o[...] = v
    else:
        low = _dot(v.astype(BF16), vdown_ref[...]).astype(BF16)
        nu = _sigmoid(v0_ref[...] + _dot(low, vup_ref[...]))
        v = v + (vf_ref[...] - v) * nu

    kk = k * kk_ref[...]
    k_mod = k * (1.0 + (a - 1.0) * ka_ref[...])
    rk = r * k_mod * rk_ref[...]
    kk_n, bonus = [], []
    for p in range(H // 2):
        sl = slice(p * LANES, (p + 1) * LANES)
        nrm = jnp.sqrt(_pair_seg_sum(kk[:, sl] * kk[:, sl]))
        kk_n.append(kk[:, sl] / jnp.maximum(nrm, 1e-12))
        bonus.append(_pair_seg_sum(rk[:, sl]) * v[:, sl])
    kk = jnp.concatenate(kk_n, axis=-1)
    bonus = jnp.concatenate(bonus, axis=-1)
    a_vec = -kk
    b_vec = kk * a

    terms = _split3(logw)
    lcum = sum(_dot(bt_ref[...], x) for x in terms)
    ltot = sum(_dot(bo_ref[...], x) for x in terms)
    e_minus = jnp.exp(-lcum)
    e_rem = jnp.exp(ltot - lcum)
    g = gc_ref[...].astype(F32)
    for o_ref, val in ((at_o, a_vec * jnp.exp(lcum - logw)), (rt_o, r * jnp.exp(lcum)), (bt_o, b_vec * e_minus),
                       (kt_o, k_mod * e_minus), (v_o, v), (bh_o, b_vec * e_rem), (kh_o, k_mod * e_rem),
                       (wc_o, jnp.exp(ltot)), (bonus_o, bonus), (sg_o, g * _sigmoid(g))):
        o_ref[...] = val.astype(o_ref.dtype)


def _mm(a, b, dims):
    return lax.dot_general(a.astype(BF16), b.astype(BF16), dims, preferred_element_type=F32)


_NN = (((1,), (0,)), ((), ()))
_NT = (((1,), (1,)), ((), ()))
_TN = (((0,), (0,)), ((), ()))


def _rwkv_chunk_kernel(at_ref, rt_ref, bt_ref, kt_ref, v_ref, bh_ref, kh_ref, wc_ref, bonus_ref, sg_ref,
                       lg_ref, lb_ref, o_ref, s_all_ref, *, nchunk):
    mm = _mm
    first = lax.broadcasted_iota(jnp.int32, (CHUNK, LANES), 1) < HD
    r2 = lax.broadcasted_iota(jnp.int32, (LANES, LANES), 0)
    c2 = lax.broadcasted_iota(jnp.int32, (LANES, LANES), 1)
    same_head = (r2 < HD) == (c2 < HD)
    eye2 = r2 == c2
    t_row, t_col = r2 % CHUNK, c2 % CHUNK
    strict2 = jnp.logical_and(same_head, t_col < t_row)
    incl2 = jnp.logical_and(same_head, t_col <= t_row)
    each = lambda f, *cols: [f(*xs) for xs in zip(*cols)]
    pairs = [(c, p) for c in range(nchunk) for p in range(H // 2)]
    npair = range(len(pairs))
    ld = lambda ref: [ref[c * CHUNK:(c + 1) * CHUNK, p * LANES:(p + 1) * LANES] for c, p in pairs]
    at, rt, bt, kt, v, bh, kh = ld(at_ref), ld(rt_ref), ld(bt_ref), ld(kt_ref), ld(v_ref), ld(bh_ref), ld(kh_ref)
    zero = jnp.zeros((CHUNK, LANES), BF16)
    stack = lambda x: jnp.concatenate([jnp.where(first, x, zero), jnp.where(first, zero, x)], axis=0)
    fold = lambda x: x[:CHUNK] + x[CHUNK:]
    a_s, b_s, k_s, v_s = each(stack, at), each(stack, bt), each(stack, kt), each(stack, v)
    x1 = each(lambda a, r: jnp.concatenate([a, stack(r.astype(BF16))], axis=0), a_s, rt)
    gb = each(lambda x, b: mm(x, b, _NT), x1, b_s)
    gk = each(lambda x, k: mm(x, k, _NT), x1, k_s)
    a_ab = [jnp.where(strict2, g[:LANES], 0.0) for g in gb]
    a_rb = [jnp.where(incl2, g[LANES:], 0.0) for g in gb]
    a_ak = [jnp.where(strict2, g[:LANES], 0.0) for g in gk]
    a_rk = [jnp.where(incl2, g[LANES:], 0.0) for g in gk]
    tinv = [jnp.where(eye2, 1.0, a) for a in a_ab]
    pw = a_ab
    for _ in range(int(math.log2(CHUNK)) - 1):
        pw = each(lambda p: mm(p, p, _NN), pw)
        tinv = each(lambda t, p: t + mm(t, p, _NN), tinv, pw)
    av = each(lambda a, x: mm(a, x, _NN), a_ak, v_s)
    tx = each(lambda t, x, a: mm(t, jnp.concatenate([x.astype(BF16), a], axis=1), _NN), tinv, av, a_s)
    ry = each(lambda a, x: mm(a, x, _NN), a_rb, tx)
    yk = each(lambda a, x: mm(a, x, _NN), a_rk, v_s)
    u = [fold(t[:, :LANES]) for t in tx]
    a_til = [fold(t[:, LANES:]) for t in tx]
    y_in = [fold(r[:, :LANES] + k) for r, k in zip(ry, yk)]
    r_hat = [r + fold(x[:, LANES:]) for r, x in zip(rt, ry)]
    m_new = [jnp.where(same_head, mm(a_til[i], bh[i], _TN), 0.0) for i in npair]
    n_new = [jnp.where(same_head, mm(jnp.concatenate([u[i].astype(BF16), v[i]], axis=0),
                                     jnp.concatenate([bh[i], kh[i]], axis=0), _TN), 0.0) for i in npair]
    lg, lb = lg_ref[...], lb_ref[...]
    for c in range(nchunk):
        idx = [i for i, (ci, _) in enumerate(pairs) if ci == c]
        s0 = [s_all_ref[p] for p in range(H // 2)]
        y = [y_in[i] + mm(r_hat[i], s0[p], _NT) for p, i in enumerate(idx)]
        for p, i in enumerate(idx):
            wc = wc_ref[c * CHUNK:c * CHUNK + 1, p * LANES:(p + 1) * LANES]
            s_all_ref[p] = mm(s0[p], jnp.where(eye2, wc, 0.0) + m_new[i], _NN) + n_new[i]
        for p, i in enumerate(idx):
            rows, lanes = slice(c * CHUNK, (c + 1) * CHUNK), slice(p * LANES, (p + 1) * LANES)
            mu = _pair_seg_sum(y[p]) / HD
            d = y[p] - mu
            var = _pair_seg_sum(d * d) / HD
            yn = d * lax.rsqrt(var + GN_EPS) * lg[:, lanes] + lb[:, lanes]
            o_ref[rows, lanes] = ((yn + bonus_ref[rows, lanes]) * sg_ref[rows, lanes]).astype(o_ref.dtype)


_RWKV_STAGE_DTYPES = (BF16, F32, BF16, BF16, BF16, BF16, BF16, F32, F32, F32)


def _rwkv_kernel(*refs, first_layer, nchunk, ns):
    n_in = 18 if first_layer else 22
    ins = refs[:n_in]
    lg_ref, lb_ref, o_ref = refs[n_in:n_in + 3]
    pos = n_in + 3
    vfirst = refs[pos:pos + 1] if first_layer else ()
    pos += len(vfirst)
    stage_a, stage_b = refs[pos:pos + 10], refs[pos + 10:pos + 20]
    carries = refs[pos + 20:pos + 24]
    s_all_ref = refs[pos + 24]
    i = pl.program_id(1)

    @pl.when(i == 0)
    def _():
        for r in stage_b + carries + (s_all_ref,):
            r[...] = jnp.zeros(r.shape, r.dtype)

    def run(done, todo):
        _rwkv_prep_kernel(*ins, *todo, *vfirst, *carries, first_layer=first_layer)
        _rwkv_chunk_kernel(*done, lg_ref, lb_ref, o_ref, s_all_ref, nchunk=nchunk)

    @pl.when(i % 2 == 0)
    def _():
        run(stage_b, stage_a)

    @pl.when(i % 2 == 1)
    def _():
        run(stage_a, stage_b)


def _rwkv(pw, pn, vfirst, wts, lnx_g, lnx_b, batch, seq, t, first_layer):
    n = pw.shape[0]
    ns = seq // t
    const2 = lambda b, i: (0, 0)
    cur = lambda b, i: b * ns + jnp.minimum(i, ns - 1)
    pcol = lambda col, w: pl.BlockSpec((t, w), lambda b, i: (cur(b, i), col // w))
    vec = pl.BlockSpec((1, W_BR), const2)
    in_specs = [pcol(COL_SR, W_BR), pcol(COL_SK, W_BR), pcol(COL_SV, W_BR), pcol(COL_WA, LANES),
                pcol(COL_GATE_C, W_BR)]
    args = [pw, pw, pw, pn, pw]
    if not first_layer:
        in_specs.append(pl.BlockSpec((t, W_BR), lambda b, i: (cur(b, i), 0)))
        args.append(vfirst)
    in_specs += [vec, vec, vec, pl.BlockSpec((1, LANES), const2), vec, vec,
                 pl.BlockSpec((LANES, W_BR), const2), pl.BlockSpec((LANES, W_BR), const2), vec, vec, vec]
    if not first_layer:
        in_specs += [vec, pl.BlockSpec((W_BR, LANES), const2), pl.BlockSpec((LANES, W_BR), const2)]
    in_specs += [pl.BlockSpec((t, t), const2)] * 2 + [vec, vec]
    args += list(wts) + [lnx_g, lnx_b]
    out_specs = [pl.BlockSpec((t, W_BR), lambda b, i: (b * ns + jnp.maximum(i - 1, 0), 0))]
    out_shape = [jax.ShapeDtypeStruct((n, W_BR), BF16)]
    if first_layer:
        out_specs.append(pl.BlockSpec((t, W_BR), lambda b, i: (jnp.where(i < ns, b * ns + i, batch * ns + b), 0)))
        out_shape.append(jax.ShapeDtypeStruct((n + batch * t, W_BR), F32))
    return pl.pallas_call(
        functools.partial(_rwkv_kernel, first_layer=first_layer, nchunk=t // CHUNK, ns=ns),
        grid=(batch, ns + 1),
        in_specs=in_specs,
        out_specs=out_specs,
        out_shape=out_shape,
        scratch_shapes=[pltpu.VMEM((t, W_BR), dt) for dt in _RWKV_STAGE_DTYPES * 2]
        + [pltpu.VMEM((1, W_BR), F32), pltpu.VMEM((1, W_BR), F32), pltpu.VMEM((1, W_BR), F32),
           pltpu.VMEM((1, LANES), F32), pltpu.VMEM((H // 2, LANES, LANES), F32)],
        compiler_params=_cparams(("parallel", "arbitrary")),
        name="rwkv",
    )(*args)


def _merge_kernel(oa_ref, ob_ref, oc_ref, ga_ref, gb_ref, gc_ref, x_ref, wpa_ref, wpb_ref, wpc_ref, wout_ref,
                  o_ref):
    pa = _dot(oa_ref[...], wpa_ref[...])
    pb = _dot(ob_ref[...], wpb_ref[...])
    pc = _dot(oc_ref[...], wpc_ref[...])
    sig = lambda ref: _sigmoid(ref[...].astype(F32))
    merged = sig(ga_ref) * pa + sig(gb_ref) * pb + sig(gc_ref) * pc
    o_ref[...] = x_ref[...] + _dot(merged.astype(BF16), wout_ref[...])


def _merge(oa, ob, oc, p, x2, wpa, wpb, wpc, wout, batch, seq, t):
    n = x2.shape[0]
    ns = seq // t
    row = lambda b, i: (b * ns + i, 0)
    const2 = lambda b, i: (0, 0)
    gate = lambda g: pl.BlockSpec((t, D_MODEL), lambda b, i: (b * ns + i, COL_MERGE // D_MODEL + g))
    return pl.pallas_call(
        _merge_kernel,
        grid=(batch, ns),
        in_specs=[
            pl.BlockSpec((t, W_BR), row),
            pl.BlockSpec((t, W_BR), row),
            pl.BlockSpec((t, W_BR), row),
            gate(0), gate(1), gate(2),
            pl.BlockSpec((t, D_MODEL), row),
            pl.BlockSpec((W_BR, D_MODEL), const2),
            pl.BlockSpec((W_BR, D_MODEL), const2),
            pl.BlockSpec((W_BR, D_MODEL), const2),
            pl.BlockSpec((D_MODEL, D_MODEL), const2),
        ],
        out_specs=pl.BlockSpec((t, D_MODEL), row),
        out_shape=jax.ShapeDtypeStruct((n, D_MODEL), F32),
        compiler_params=_cparams(("parallel", "parallel")),
        name="merge",
    )(oa, ob, oc, p, p, p, x2, wpa, wpb, wpc, wout)


def _regroup_w_in(w_in, l):
    d = w_in.shape[1]
    sizes = (Q_LORA, KV_LORA, ROPE_D, W_BR, W_BR, W_BR, W_BR, H, W_BR,
             3 * W_BR + DECAY_LORA + AAA_LORA, W_BR, 3 * D_MODEL)
    offs = [int(o) for o in np.concatenate([[0], np.cumsum(sizes)])]
    (c_q, c_kv, k_r, gate_a, fq, fk, fv, ff, gate_b, shift, gate_c, merge) = [
        w_in[l, :, offs[i]:offs[i + 1]].astype(BF16) for i in range(len(sizes))]
    z = lambda n: jnp.zeros((d, n), BF16)
    wide = jnp.concatenate([
        merge, gate_a, fq, fk, fv, gate_b,
        shift[:, :W_BR], shift[:, W_BR:2 * W_BR], shift[:, 2 * W_BR:3 * W_BR], gate_c,
    ], axis=1)
    narrow = jnp.concatenate([
        c_q, c_kv,
        z(NOPE), k_r, k_r[:, :ROPE_D // 2], z(LANES - NOPE - ROPE_D - ROPE_D // 2),
        ff, z(LANES - H),
        shift[:, 3 * W_BR:],
    ], axis=1)
    return wide, narrow


def _rope_tables(seq):
    half = ROPE_D // 2
    inv = ROPE_THETA ** (-jnp.arange(0, ROPE_D, 2, dtype=F32) / ROPE_D)
    ang = jnp.arange(seq, dtype=F32)[:, None] * inv[None, :]
    cos, sin = jnp.cos(ang), jnp.sin(ang)
    z = lambda n: jnp.zeros((seq, n), F32)
    c = jnp.concatenate([jnp.ones((seq, NOPE), F32), cos, cos, z(LANES - NOPE - ROPE_D)], axis=1)
    s = jnp.concatenate([z(NOPE), -sin, sin, z(LANES - NOPE - ROPE_D)], axis=1)
    return c, s


def _pad_lanes(v, lo, total=LANES):
    return jnp.zeros((1, total), F32).at[0, lo:lo + v.shape[0]].set(v)


def _fox_selectors():
    selq = np.zeros((3, LANES, H * LANES), np.float32)
    selk = np.zeros((3, LANES, H * LANES), np.float32)
    oneq = np.zeros((1, H * LANES), np.float32)
    onek = np.zeros((1, H * LANES), np.float32)
    for h in range(H):
        for j in range(3):
            selq[j, h, h * LANES + HD + j] = 1.0
            selk[j, h, h * LANES + HD + 3 + j] = -1.0
            oneq[0, h * LANES + HD + 3 + j] = 1.0
            onek[0, h * LANES + HD + j] = 1.0
    return (jnp.asarray(selq, BF16), jnp.asarray(selk, BF16), jnp.asarray(oneq), jnp.asarray(onek))


def _tri(t):
    r = np.arange(t)
    return jnp.asarray((r[None, :] <= r[:, None]).astype(np.float32), BF16)


def _chunk_tri(t):
    r = np.arange(t)
    same = (r[None, :] // CHUNK) == (r[:, None] // CHUNK)
    lower = r[None, :] <= r[:, None]
    return (jnp.asarray((same & lower).astype(np.float32), BF16), jnp.asarray(same.astype(np.float32), BF16))


def _tile(seq, pref):
    return pref if seq % pref == 0 else seq


def kernel(x, norm_g, w_in, mla_qa_g, mla_w_uq, mla_kva_g, mla_w_ukv, mla_q_g, mla_knope_g, mla_krope_g, fox_b_f, fox_q_g, fox_k_g, rwkv_mu, rwkv_w0, rwkv_w_up, rwkv_a0, rwkv_a_up, rwkv_k_k, rwkv_k_a, rwkv_r_k, rwkv_lnx_g, rwkv_lnx_b, rwkv_v0, rwkv_v_down, rwkv_v_up, w_pa, w_pb, w_pc, w_out):
    batch, seq, d = x.shape
    depth = w_in.shape[0]
    n = batch * seq
    t_row = _tile(seq, 512)
    t_in = _tile(n, 1024)
    t_rwkv = _tile(seq, 256)
    assert seq % (ATTN_QT * t_row) == 0 and seq % t_rwkv == 0 and t_rwkv % CHUNK == 0, seq
    x2 = x.reshape(n, d)

    tabs = _rope_tables(seq)
    selq, selk, oneq, onek = _fox_selectors()
    tri = _tri(t_row)
    btri, bones = _chunk_tri(t_rwkv)
    row = lambda v: v.reshape(1, -1).astype(F32)

    half = ROPE_D // 2
    spare = LANES - NOPE - ROPE_D - half
    with_copy = lambda g, lo: jnp.concatenate([g, g[lo:lo + half]])
    vfirst = None
    for l in range(depth):
        w_wide, w_narrow = _regroup_w_in(w_in, l)
        pw, pn = _inproj(x2, row(norm_g[l]), w_wide, w_narrow, t_in, INPROJ_TN)

        wuq = mla_w_uq[l].reshape(Q_LORA, H, NOPE + ROPE_D)
        wuq = jnp.concatenate([wuq, wuq[:, :, NOPE:NOPE + half], jnp.zeros((Q_LORA, H, spare), F32)], axis=2)
        wuq = wuq.reshape(Q_LORA, H * LANES).astype(BF16)
        wukv = mla_w_ukv[l].reshape(KV_LORA, H, NOPE + HD)
        wk = jnp.pad(wukv[:, :, :NOPE], ((0, 0), (0, 0), (0, LANES - NOPE))).reshape(KV_LORA, H * LANES).astype(BF16)
        wv = wukv[:, :, NOPE:].reshape(KV_LORA, W_BR).T.astype(BF16)
        mla_scale = float(NOPE + ROPE_D) ** -0.5 * LOG2E
        mla_wts = (row(mla_qa_g[l]), wuq, row(mla_kva_g[l]), wk, wv,
                   _pad_lanes(with_copy(mla_q_g[l], NOPE) * mla_scale, 0), _pad_lanes(mla_knope_g[l], 0),
                   _pad_lanes(with_copy(mla_krope_g[l], 0), NOPE))
        qa, ka, va = _mla_prep(pn, tabs, mla_wts, seq, t_row)
        o_a = _attention(qa, ka, va, pw, COL_GATE_A, batch, seq, t_row)

        fox_wts = (_pad_lanes(fox_b_f[l], 0), row(jnp.tile(fox_q_g[l] * (float(HD) ** -0.5 * LOG2E), H)),
                   row(jnp.tile(fox_k_g[l], H)), tri, selq, selk, oneq, onek)
        qb, kb, vb = _fox_prep(pw, pn, fox_wts, batch, seq, t_row)
        o_b = _attention(qb, kb, vb, pw, COL_GATE_B, batch, seq, t_row)

        mu = rwkv_mu[l]
        wup = jnp.zeros((LANES, W_BR), F32).at[:DECAY_LORA].set(rwkv_w_up[l]).astype(BF16)
        aup = jnp.zeros((LANES, W_BR), F32).at[DECAY_LORA:].set(rwkv_a_up[l]).astype(BF16)
        rw = [row(mu[:W_BR]), row(mu[W_BR:2 * W_BR]), row(mu[2 * W_BR:3 * W_BR]), row(mu[3 * W_BR:]),
              row(rwkv_w0[l]), row(rwkv_a0[l]), wup, aup, row(rwkv_k_k[l]), row(rwkv_k_a[l]),
              row(rwkv_r_k[l])]
        if l > 0:
            vdown = jnp.zeros((W_BR, LANES), F32).at[:, :MV_LORA].set(rwkv_v_down[l - 1]).astype(BF16)
            vup = jnp.zeros((LANES, W_BR), F32).at[:MV_LORA].set(rwkv_v_up[l - 1]).astype(BF16)
            rw += [row(rwkv_v0[l - 1]), vdown, vup]
        rw += [btri, bones]
        outs = _rwkv(pw, pn, vfirst, rw, row(rwkv_lnx_g[l]), row(rwkv_lnx_b[l]), batch, seq, t_rwkv,
                     first_layer=(l == 0))
        o_c = outs[0]
        if l == 0:
            vfirst = outs[1]

        x2 = _merge(o_a, o_b, o_c, pw, x2, w_pa[l].astype(BF16), w_pb[l].astype(BF16),
                    w_pc[l].astype(BF16), w_out[l].astype(BF16), batch, seq, t_row)
    return x2.reshape(batch, seq, d)
```

```python
import functools
import math

import jax
import jax.numpy as jnp
import numpy as np
from jax import lax
from jax.experimental import pallas as pl
from jax.experimental.pallas import tpu as pltpu

F32 = jnp.float32
BF16 = jnp.bfloat16

LANES = 128
H = 8
HD = 64
NOPE = 64
ROPE_D = 32
Q_LORA = 256
KV_LORA = 128
D_MODEL = 1024
W_BR = H * HD
DECAY_LORA = 64
AAA_LORA = 64
MV_LORA = 32
ROPE_THETA = 10000.0
RWKV_DECAY_SCALE = 0.606531
GN_EPS = 64e-5
EPS = 1e-6
NEG_INF = -1e30
LOG2E = math.log2(math.e)
CHUNK = 64
ATTN_QT = 4
ONES_ROWS = 16

COL_MERGE = 0
COL_GATE_A = 3072
COL_FQ = 3584
COL_FK = 4096
COL_FV = 4608
COL_GATE_B = 5120
COL_SR = 5632
COL_SK = 6144
COL_SV = 6656
COL_GATE_C = 7168
NCOLS_WIDE = 7680
INPROJ_TN = 2560
COL_CQ = 0
COL_CKV = 256
COL_KR = 384
COL_FF = 512
COL_WA = 640
NCOLS_NARROW = 768

VMEM_LIMIT = 56 * 1024 * 1024


def _cparams(sem):
    return pltpu.CompilerParams(dimension_semantics=sem, vmem_limit_bytes=VMEM_LIMIT)


def _sigmoid(x):
    return 0.5 * jnp.tanh(0.5 * x) + 0.5


def _dot(a, b):
    return jnp.dot(a, b, preferred_element_type=F32)


def _split3(x):
    hi = x.astype(BF16)
    r1 = x - hi.astype(F32)
    mid = r1.astype(BF16)
    lo = (r1 - mid.astype(F32)).astype(BF16)
    return hi, mid, lo


def _dot_exact_lhs(m_bf16, x):
    hi, mid, lo = _split3(x)
    return _dot(m_bf16, hi) + _dot(m_bf16, mid) + _dot(m_bf16, lo)


def _pair_seg_sum(x):
    lane = lax.broadcasted_iota(jnp.int32, x.shape, 1)
    first = lane < HD
    s0 = jnp.sum(jnp.where(first, x, 0.0), axis=-1, keepdims=True)
    s1 = jnp.sum(jnp.where(first, 0.0, x), axis=-1, keepdims=True)
    return jnp.where(first, s0, s1)


def _inproj_kernel(x_ref, g_ref, w_ref, wn_ref, o_ref, on_ref, h_ref):
    @pl.when(pl.program_id(1) == 0)
    def _():
        x = x_ref[...]
        ms = jnp.mean(x * x, axis=-1, keepdims=True)
        h = (x * lax.rsqrt(ms + EPS) * g_ref[...]).astype(BF16)
        h_ref[...] = h
        on_ref[...] = _dot(h, wn_ref[...])

    o_ref[...] = _dot(h_ref[...], w_ref[...]).astype(o_ref.dtype)


def _inproj(x2, g, w_wide, w_narrow, tm, tn):
    n = x2.shape[0]
    return pl.pallas_call(
        _inproj_kernel,
        grid=(n // tm, NCOLS_WIDE // tn),
        in_specs=[
            pl.BlockSpec((tm, D_MODEL), lambda i, j: (i, 0)),
            pl.BlockSpec((1, D_MODEL), lambda i, j: (0, 0)),
            pl.BlockSpec((D_MODEL, tn), lambda i, j: (0, j)),
            pl.BlockSpec((D_MODEL, NCOLS_NARROW), lambda i, j: (0, 0)),
        ],
        out_specs=[pl.BlockSpec((tm, tn), lambda i, j: (i, j)),
                   pl.BlockSpec((tm, NCOLS_NARROW), lambda i, j: (i, 0))],
        out_shape=[jax.ShapeDtypeStruct((n, NCOLS_WIDE), BF16), jax.ShapeDtypeStruct((n, NCOLS_NARROW), F32)],
        scratch_shapes=[pltpu.VMEM((tm, D_MODEL), BF16)],
        compiler_params=_cparams(("parallel", "arbitrary")),
        name="inproj",
    )(x2, g, w_wide, w_narrow)


def _rms(x, g, n, keep=None):
    sq = x * x
    if keep is not None:
        sq = jnp.where(keep, sq, 0.0)
    ms = jnp.sum(sq, axis=-1, keepdims=True) / n
    return x * lax.rsqrt(ms + EPS) * g


def _rope(x, c, s):
    return x * c + pltpu.roll(x, LANES - ROPE_D // 2, 1) * s


def _mla_prep_kernel(cq_ref, ckv_ref, kr_ref, c_ref, s_ref, qag_ref, wuq_ref, kvag_ref, wk_ref, wv_ref,
                     qg_ref, kng_ref, krg_ref, q_out, k_out, vt_out):
    c, s = c_ref[...], s_ref[...]
    keep = lax.broadcasted_iota(jnp.int32, c.shape, 1) < NOPE + ROPE_D
    cqn = _rms(cq_ref[...], qag_ref[...], Q_LORA).astype(BF16)
    ckvn = _rms(ckv_ref[...], kvag_ref[...], KV_LORA).astype(BF16)
    q_all = _dot(cqn, wuq_ref[...])
    k_all = _dot(ckvn, wk_ref[...])
    vt_out[0, 0] = lax.dot_general(wv_ref[...], ckvn, _NT, preferred_element_type=F32).astype(BF16)
    k_rope = _rope(_rms(kr_ref[...], krg_ref[...], ROPE_D, keep), c, s)
    qg, kng = qg_ref[...], kng_ref[...]
    for h in range(H):
        sl = slice(h * LANES, (h + 1) * LANES)
        q_out[:, sl] = _rope(_rms(q_all[:, sl], qg, NOPE + ROPE_D, keep), c, s).astype(BF16)
        k_out[:, sl] = (_rms(k_all[:, sl], kng, NOPE) + k_rope).astype(BF16)


def _mla_prep(pn, tabs, wts, seq, t):
    n = pn.shape[0]
    ns = seq // t
    row = lambda i: (i, 0)
    const = lambda i: (0, 0)
    tab_spec = pl.BlockSpec((t, LANES), lambda i: (i % ns, 0))
    return pl.pallas_call(
        _mla_prep_kernel,
        grid=(n // t,),
        in_specs=[
            pl.BlockSpec((t, Q_LORA), lambda i: (i, COL_CQ // Q_LORA)),
            pl.BlockSpec((t, LANES), lambda i: (i, COL_CKV // LANES)),
            pl.BlockSpec((t, LANES), lambda i: (i, COL_KR // LANES)),
            tab_spec, tab_spec,
            pl.BlockSpec((1, Q_LORA), const),
            pl.BlockSpec((Q_LORA, H * LANES), const),
            pl.BlockSpec((1, KV_LORA), const),
            pl.BlockSpec((KV_LORA, H * LANES), const),
            pl.BlockSpec((W_BR, KV_LORA), const),
            pl.BlockSpec((1, LANES), const),
            pl.BlockSpec((1, LANES), const),
            pl.BlockSpec((1, LANES), const),
        ],
        out_specs=[
            pl.BlockSpec((t, H * LANES), row),
            pl.BlockSpec((t, H * LANES), row),
            pl.BlockSpec((1, 1, W_BR, t), lambda i: (i // ns, i % ns, 0, 0)),
        ],
        out_shape=[
            jax.ShapeDtypeStruct((n, H * LANES), BF16),
            jax.ShapeDtypeStruct((n, H * LANES), BF16),
            jax.ShapeDtypeStruct((n // seq, ns, W_BR, t), BF16),
        ],
        compiler_params=_cparams(("parallel",)),
        name="mla_prep",
    )(pn, pn, pn, *tabs, *wts)


def _fox_prep_kernel(fq_ref, fk_ref, fv_ref, ff_ref, bf_ref, qg_ref, kg_ref, tri_ref, selq_ref, selk_ref,
                     oneq_ref, onek_ref, q_out, k_out, vt_out, carry_ref):
    t = fq_ref.shape[0]

    @pl.when(pl.program_id(1) == 0)
    def _():
        carry_ref[...] = jnp.zeros_like(carry_ref)

    z = ff_ref[...] + bf_ref[...]
    logf = jnp.minimum(z, 0.0) - jnp.log1p(jnp.exp(-jnp.abs(z)))
    cum = carry_ref[...] + _dot_exact_lhs(tri_ref[...], logf)
    carry_ref[...] = cum[t - 1:t, :]
    hi, mid, lo = _split3(cum * LOG2E)
    aug_q = _dot(hi, selq_ref[0]) + _dot(mid, selq_ref[1]) + _dot(lo, selq_ref[2]) + oneq_ref[...]
    aug_k = _dot(hi, selk_ref[0]) + _dot(mid, selk_ref[1]) + _dot(lo, selk_ref[2]) + onek_ref[...]
    vt_out[0, 0] = fv_ref[...].astype(F32).T.astype(BF16)
    lane = lax.broadcasted_iota(jnp.int32, (t, LANES), 1)
    first = lane < HD
    for src_ref, g_ref, aug, dst in ((fq_ref, qg_ref, aug_q, q_out), (fk_ref, kg_ref, aug_k, k_out)):
        for p in range(H // 2):
            sl = slice(p * LANES, (p + 1) * LANES)
            x = src_ref[:, sl].astype(F32)
            ms = _pair_seg_sum(x * x) / HD
            xn = x * lax.rsqrt(ms + EPS) * g_ref[:, sl]
            he, ho = 2 * p, 2 * p + 1
            dst[:, he * LANES:(he + 1) * LANES] = jnp.where(first, xn, aug[:, he * LANES:(he + 1) * LANES]).astype(BF16)
            dst[:, ho * LANES:(ho + 1) * LANES] = jnp.where(
                first, pltpu.roll(xn, HD, 1), aug[:, ho * LANES:(ho + 1) * LANES]).astype(BF16)


def _fox_prep(pw, pn, wts, batch, seq, t):
    n = pw.shape[0]
    ns = seq // t
    row = lambda b, i: (b * ns + i, 0)
    const2 = lambda b, i: (0, 0)
    const3 = lambda b, i: (0, 0, 0)
    return pl.pallas_call(
        _fox_prep_kernel,
        grid=(batch, ns),
        in_specs=[
            pl.BlockSpec((t, W_BR), lambda b, i: (b * ns + i, COL_FQ // W_BR)),
            pl.BlockSpec((t, W_BR), lambda b, i: (b * ns + i, COL_FK // W_BR)),
            pl.BlockSpec((t, W_BR), lambda b, i: (b * ns + i, COL_FV // W_BR)),
            pl.BlockSpec((t, LANES), lambda b, i: (b * ns + i, COL_FF // LANES)),
            pl.BlockSpec((1, LANES), const2),
            pl.BlockSpec((1, W_BR), const2),
            pl.BlockSpec((1, W_BR), const2),
            pl.BlockSpec((t, t), const2),
            pl.BlockSpec((3, LANES, H * LANES), const3),
            pl.BlockSpec((3, LANES, H * LANES), const3),
            pl.BlockSpec((1, H * LANES), const2),
            pl.BlockSpec((1, H * LANES), const2),
        ],
        out_specs=[
            pl.BlockSpec((t, H * LANES), row),
            pl.BlockSpec((t, H * LANES), row),
            pl.BlockSpec((1, 1, W_BR, t), lambda b, i: (b, i, 0, 0)),
        ],
        out_shape=[
            jax.ShapeDtypeStruct((n, H * LANES), BF16),
            jax.ShapeDtypeStruct((n, H * LANES), BF16),
            jax.ShapeDtypeStruct((batch, ns, W_BR, t), BF16),
        ],
        scratch_shapes=[pltpu.VMEM((1, LANES), F32)],
        compiler_params=_cparams(("parallel", "arbitrary")),
        name="fox_prep",
    )(pw, pw, pw, pn, *wts)


def _attn_kernel(q_ref, k_ref, vt_ref, g_ref, o_ref, m_ref, acc_ref, sa_ref, sb_ref, mxa_ref, mxb_ref, *, tk):
    tq = ATTN_QT * tk
    gi = pl.program_id(2)
    nt = (((1,), (1,)), ((), ()))
    kv_pos = lax.broadcasted_iota(jnp.int32, (tk, tk), 0)
    q_pos = lax.broadcasted_iota(jnp.int32, (tk, tk), 1)
    causal = kv_pos <= q_pos
    ones = jnp.ones((ONES_ROWS, tk), BF16)
    m_ref[...] = jnp.full(m_ref.shape, NEG_INF, F32)
    acc_ref[...] = jnp.zeros(acc_ref.shape, F32)
    bufs = ((sa_ref, mxa_ref), (sb_ref, mxb_ref))

    def scores(c, par, t):
        s_ref, mx_ref = bufs[par]
        start = pl.multiple_of(c * tk, tk)
        cols = slice(t * tk, (t + 1) * tk)
        for j in range(2):
            q = q_ref[cols, j * LANES:(j + 1) * LANES]
            k = k_ref[pl.ds(start, tk), j * LANES:(j + 1) * LANES]
            s = lax.dot_general(k, q, nt, preferred_element_type=F32)
            s_ref[j, :, cols] = s
            mx_ref[j, :, cols] = jnp.max(s, axis=0, keepdims=True)

    def consume(c, par, t, masked):
        s_ref, mx_ref = bufs[par]
        cols = slice(t * tk, (t + 1) * tk)
        for j in range(2):
            vt = jnp.concatenate([vt_ref[0, c, j * HD:(j + 1) * HD, :], ones], axis=0)
            s = s_ref[j, :, cols]
            if masked:
                s = jnp.where(causal, s, NEG_INF)
                m_cur = jnp.max(s, axis=0, keepdims=True)
            else:
                m_cur = mx_ref[j, :, cols]
            m_old = m_ref[j, :, cols]
            m_new = jnp.maximum(m_old, m_cur)
            alpha = jnp.exp2(m_old - m_new)
            p = jnp.exp2(s - m_new)
            acc_ref[j, :, cols] = alpha * acc_ref[j, :, cols] + _dot(vt, p.astype(BF16))
            m_ref[j, :, cols] = m_new

    tiles = range(ATTN_QT)
    for par in range(2):
        for t in tiles:
            scores(par, par, t)

    def body(i, carry):
        for par in range(2):
            c = 2 * i + par
            for t in tiles:
                consume(c, par, t, False)
                scores(c + 2, par, t)
        return carry

    lax.fori_loop(0, gi * (ATTN_QT // 2), body, 0)
    c0 = ATTN_QT * gi
    for d in tiles:
        for t in range(d, ATTN_QT):
            consume(c0 + d, d % 2, t, t == d)
            if d + 2 < ATTN_QT and t >= d + 2:
                scores(c0 + d + 2, d % 2, t)

    o_t = jnp.concatenate([acc_ref[j, :HD, :] / acc_ref[j, HD:HD + 1, :] for j in range(2)], axis=0)
    g = g_ref[...].astype(F32)
    o_ref[...] = (o_t.T * (g * _sigmoid(g))).astype(BF16)


def _attention(q, k, vt, pw, gate_col, batch, seq, tk):
    n = q.shape[0]
    tq = ATTN_QT * tk
    nq, nk = seq // tq, seq // tk
    return pl.pallas_call(
        functools.partial(_attn_kernel, tk=tk),
        grid=(batch, H // 2, nq),
        in_specs=[
            pl.BlockSpec((tq, 2 * LANES), lambda b, hp, i: (b * nq + i, hp)),
            pl.BlockSpec((seq, 2 * LANES), lambda b, hp, i: (b, hp)),
            pl.BlockSpec((1, nk, LANES, tk), lambda b, hp, i: (b, 0, hp, 0)),
            pl.BlockSpec((tq, LANES), lambda b, hp, i: (b * nq + i, gate_col // LANES + hp)),
        ],
        out_specs=pl.BlockSpec((tq, LANES), lambda b, hp, i: (b * nq + i, hp)),
        out_shape=jax.ShapeDtypeStruct((n, W_BR), BF16),
        scratch_shapes=[
            pltpu.VMEM((2, 1, tq), F32),
            pltpu.VMEM((2, HD + ONES_ROWS, tq), F32),
            pltpu.VMEM((2, tk, tq), F32),
            pltpu.VMEM((2, tk, tq), F32),
            pltpu.VMEM((2, 1, tq), F32),
            pltpu.VMEM((2, 1, tq), F32),
        ],
        compiler_params=_cparams(("parallel", "parallel", "arbitrary")),
        name="attention",
    )(q, k, vt, pw)


def _rwkv_prep_kernel(*refs, first_layer):
    if first_layer:
        (sr_ref, sk_ref, sv_ref, swa_ref, gc_ref, mur_ref, muk_ref, muv_ref, muwa_ref, w0_ref, a0_ref, wup_ref,
         aup_ref, kk_ref, ka_ref, rk_ref, bt_ref, bo_ref,
         at_o, rt_o, bt_o, kt_o, v_o, bh_o, kh_o, wc_o, bonus_o, sg_o, vfirst_o,
         cr_ref, ck_ref, cv_ref, cwa_ref) = refs
    else:
        (sr_ref, sk_ref, sv_ref, swa_ref, gc_ref, vf_ref, mur_ref, muk_ref, muv_ref, muwa_ref, w0_ref, a0_ref,
         wup_ref, aup_ref, kk_ref, ka_ref, rk_ref, v0_ref, vdown_ref, vup_ref, bt_ref, bo_ref,
         at_o, rt_o, bt_o, kt_o, v_o, bh_o, kh_o, wc_o, bonus_o, sg_o,
         cr_ref, ck_ref, cv_ref, cwa_ref) = refs
    t = sr_ref.shape[0]

    def shift(x_ref, c_ref, mu_ref):
        x = x_ref[...].astype(F32)
        rowid = lax.broadcasted_iota(jnp.int32, x.shape, 0)
        prev = jnp.where(rowid == 0, c_ref[...], pltpu.roll(x, 1, 0))
        c_ref[...] = x[t - 1:t, :]
        return x + mu_ref[...] * (prev - x)

    r = shift(sr_ref, cr_ref, mur_ref)
    k = shift(sk_ref, ck_ref, muk_ref)
    v = shift(sv_ref, cv_ref, muv_ref)
    wa = shift(swa_ref, cwa_ref, muwa_ref)

    logw = -RWKV_DECAY_SCALE * _sigmoid(w0_ref[...] + _dot(jnp.tanh(wa).astype(BF16), wup_ref[...]))
    a = _sigmoid(a0_ref[...] + _dot(wa.astype(BF16), aup_ref[...]))
    if first_layer:
        vfirst_o[...] = v
    else:
        low = _dot(v.astype(BF16), vdown_ref[...]).astype(BF16)
        nu = _sigmoid(v0_ref[...] + _dot(low, vup_ref[...]))
        v = v + (vf_ref[...] - v) * nu

    kk = k * kk_ref[...]
    k_mod = k * (1.0 + (a - 1.0) * ka_ref[...])
    rk = r * k_mod * rk_ref[...]
    kk_n, bonus = [], []
    for p in range(H // 2):
        sl = slice(p * LANES, (p + 1) * LANES)
        nrm = jnp.sqrt(_pair_seg_sum(kk[:, sl] * kk[:, sl]))
        kk_n.append(kk[:, sl] / jnp.maximum(nrm, 1e-12))
        bonus.append(_pair_seg_sum(rk[:, sl]) * v[:, sl])
    kk = jnp.concatenate(kk_n, axis=-1)
    bonus = jnp.concatenate(bonus, axis=-1)
    a_vec = -kk
    b_vec = kk * a

    terms = _split3(logw)
    lcum = sum(_dot(bt_ref[...], x) for x in terms)
    ltot = sum(_dot(bo_ref[...], x) for x in terms)
    e_minus = jnp.exp(-lcum)
    e_rem = jnp.exp(ltot - lcum)
    g = gc_ref[...].astype(F32)
    for o_ref, val in ((at_o, a_vec * jnp.exp(lcum - logw)), (rt_o, r * jnp.exp(lcum)), (bt_o, b_vec * e_minus),
                       (kt_o, k_mod * e_minus), (v_o, v), (bh_o, b_vec * e_rem), (kh_o, k_mod * e_rem),
                       (wc_o, jnp.exp(ltot)), (bonus_o, bonus), (sg_o, g * _sigmoid(g))):
        o_ref[...] = val.astype(o_ref.dtype)


def _mm(a, b, dims):
    return lax.dot_general(a.astype(BF16), b.astype(BF16), dims, preferred_element_type=F32)


_NN = (((1,), (0,)), ((), ()))
_NT = (((1,), (1,)), ((), ()))
_TN = (((0,), (0,)), ((), ()))


def _rwkv_chunk_kernel(at_ref, rt_ref, bt_ref, kt_ref, v_ref, bh_ref, kh_ref, wc_ref, bonus_ref, sg_ref,
                       lg_ref, lb_ref, o_ref, s_all_ref, *, nchunk):
    mm = _mm
    first = lax.broadcasted_iota(jnp.int32, (CHUNK, LANES), 1) < HD
    r2 = lax.broadcasted_iota(jnp.int32, (LANES, LANES), 0)
    c2 = lax.broadcasted_iota(jnp.int32, (LANES, LANES), 1)
    same_head = (r2 < HD) == (c2 < HD)
    eye2 = r2 == c2
    t_row, t_col = r2 % CHUNK, c2 % CHUNK
    strict2 = jnp.logical_and(same_head, t_col < t_row)
    incl2 = jnp.logical_and(same_head, t_col <= t_row)
    each = lambda f, *cols: [f(*xs) for xs in zip(*cols)]
    pairs = [(c, p) for c in range(nchunk) for p in range(H // 2)]
    npair = range(len(pairs))
    ld = lambda ref: [ref[c * CHUNK:(c + 1) * CHUNK, p * LANES:(p + 1) * LANES] for c, p in pairs]
    at, rt, bt, kt, v, bh, kh = ld(at_ref), ld(rt_ref), ld(bt_ref), ld(kt_ref), ld(v_ref), ld(bh_ref), ld(kh_ref)
    zero = jnp.zeros((CHUNK, LANES), BF16)
    stack = lambda x: jnp.concatenate([jnp.where(first, x, zero), jnp.where(first, zero, x)], axis=0)
    fold = lambda x: x[:CHUNK] + x[CHUNK:]
    a_s, b_s, k_s, v_s = each(stack, at), each(stack, bt), each(stack, kt), each(stack, v)
    x1 = each(lambda a, r: jnp.concatenate([a, stack(r.astype(BF16))], axis=0), a_s, rt)
    gb = each(lambda x, b: mm(x, b, _NT), x1, b_s)
    gk = each(lambda x, k: mm(x, k, _NT), x1, k_s)
    a_ab = [jnp.where(strict2, g[:LANES], 0.0) for g in gb]
    a_rb = [jnp.where(incl2, g[LANES:], 0.0) for g in gb]
    a_ak = [jnp.where(strict2, g[:LANES], 0.0) for g in gk]
    a_rk = [jnp.where(incl2, g[LANES:], 0.0) for g in gk]
    tinv = [jnp.where(eye2, 1.0, a) for a in a_ab]
    pw = a_ab
    for _ in range(int(math.log2(CHUNK)) - 1):
        pw = each(lambda p: mm(p, p, _NN), pw)
        tinv = each(lambda t, p: t + mm(t, p, _NN), tinv, pw)
    av = each(lambda a, x: mm(a, x, _NN), a_ak, v_s)
    tx = each(lambda t, x, a: mm(t, jnp.concatenate([x.astype(BF16), a], axis=1), _NN), tinv, av, a_s)
    ry = each(lambda a, x: mm(a, x, _NN), a_rb, tx)
    yk = each(lambda a, x: mm(a, x, _NN), a_rk, v_s)
    u = [fold(t[:, :LANES]) for t in tx]
    a_til = [fold(t[:, LANES:]) for t in tx]
    y_in = [fold(r[:, :LANES] + k) for r, k in zip(ry, yk)]
    r_hat = [r + fold(x[:, LANES:]) for r, x in zip(rt, ry)]
    m_new = [jnp.where(same_head, mm(a_til[i], bh[i], _TN), 0.0) for i in npair]
    n_new = [jnp.where(same_head, mm(jnp.concatenate([u[i].astype(BF16), v[i]], axis=0),
                                     jnp.concatenate([bh[i], kh[i]], axis=0), _TN), 0.0) for i in npair]
    lg, lb = lg_ref[...], lb_ref[...]
    for c in range(nchunk):
        idx = [i for i, (ci, _) in enumerate(pairs) if ci == c]
        s0 = [s_all_ref[p] for p in range(H // 2)]
        y = [y_in[i] + mm(r_hat[i], s0[p], _NT) for p, i in enumerate(idx)]
        for p, i in enumerate(idx):
            wc = wc_ref[c * CHUNK:c * CHUNK + 1, p * LANES:(p + 1) * LANES]
            s_all_ref[p] = mm(s0[p], jnp.where(eye2, wc, 0.0) + m_new[i], _NN) + n_new[i]
        for p, i in enumerate(idx):
            rows, lanes = slice(c * CHUNK, (c + 1) * CHUNK), slice(p * LANES, (p + 1) * LANES)
            mu = _pair_seg_sum(y[p]) / HD
            d = y[p] - mu
            var = _pair_seg_sum(d * d) / HD
            yn = d * lax.rsqrt(var + GN_EPS) * lg[:, lanes] + lb[:, lanes]
            o_ref[rows, lanes] = ((yn + bonus_ref[rows, lanes]) * sg_ref[rows, lanes]).astype(o_ref.dtype)


_RWKV_STAGE_DTYPES = (BF16, F32, BF16, BF16, BF16, BF16, BF16, F32, F32, F32)


def _rwkv_kernel(*refs, first_layer, nchunk, ns):
    n_in = 18 if first_layer else 22
    ins = refs[:n_in]
    lg_ref, lb_ref, o_ref = refs[n_in:n_in + 3]
    pos = n_in + 3
    vfirst = refs[pos:pos + 1] if first_layer else ()
    pos += len(vfirst)
    stage_a, stage_b = refs[pos:pos + 10], refs[pos + 10:pos + 20]
    carries = refs[pos + 20:pos + 24]
    s_all_ref = refs[pos + 24]
    i = pl.program_id(1)

    @pl.when(i == 0)
    def _():
        for r in stage_b + carries + (s_all_ref,):
            r[...] = jnp.zeros(r.shape, r.dtype)

    def run(done, todo):
        _rwkv_prep_kernel(*ins, *todo, *vfirst, *carries, first_layer=first_layer)
        _rwkv_chunk_kernel(*done, lg_ref, lb_ref, o_ref, s_all_ref, nchunk=nchunk)

    @pl.when(i % 2 == 0)
    def _():
        run(stage_b, stage_a)

    @pl.when(i % 2 == 1)
    def _():
        run(stage_a, stage_b)


def _rwkv(pw, pn, vfirst, wts, lnx_g, lnx_b, batch, seq, t, first_layer):
    n = pw.shape[0]
    ns = seq // t
    const2 = lambda b, i: (0, 0)
    cur = lambda b, i: b * ns + jnp.minimum(i, ns - 1)
    pcol = lambda col, w: pl.BlockSpec((t, w), lambda b, i: (cur(b, i), col // w))
    vec = pl.BlockSpec((1, W_BR), const2)
    in_specs = [pcol(COL_SR, W_BR), pcol(COL_SK, W_BR), pcol(COL_SV, W_BR), pcol(COL_WA, LANES),
                pcol(COL_GATE_C, W_BR)]
    args = [pw, pw, pw, pn, pw]
    if not first_layer:
        in_specs.append(pl.BlockSpec((t, W_BR), lambda b, i: (cur(b, i), 0)))
        args.append(vfirst)
    in_specs += [vec, vec, vec, pl.BlockSpec((1, LANES), const2), vec, vec,
                 pl.BlockSpec((LANES, W_BR), const2), pl.BlockSpec((LANES, W_BR), const2), vec, vec, vec]
    if not first_layer:
        in_specs += [vec, pl.BlockSpec((W_BR, LANES), const2), pl.BlockSpec((LANES, W_BR), const2)]
    in_specs += [pl.BlockSpec((t, t), const2)] * 2 + [vec, vec]
    args += list(wts) + [lnx_g, lnx_b]
    out_specs = [pl.BlockSpec((t, W_BR), lambda b, i: (b * ns + jnp.maximum(i - 1, 0), 0))]
    out_shape = [jax.ShapeDtypeStruct((n, W_BR), BF16)]
    if first_layer:
        out_specs.append(pl.BlockSpec((t, W_BR), lambda b, i: (jnp.where(i < ns, b * ns + i, batch * ns + b), 0)))
        out_shape.append(jax.ShapeDtypeStruct((n + batch * t, W_BR), F32))
    return pl.pallas_call(
        functools.partial(_rwkv_kernel, first_layer=first_layer, nchunk=t // CHUNK, ns=ns),
        grid=(batch, ns + 1),
        in_specs=in_specs,
        out_specs=out_specs,
        out_shape=out_shape,
        scratch_shapes=[pltpu.VMEM((t, W_BR), dt) for dt in _RWKV_STAGE_DTYPES * 2]
        + [pltpu.VMEM((1, W_BR), F32), pltpu.VMEM((1, W_BR), F32), pltpu.VMEM((1, W_BR), F32),
           pltpu.VMEM((1, LANES), F32), pltpu.VMEM((H // 2, LANES, LANES), F32)],
        compiler_params=_cparams(("parallel", "arbitrary")),
        name="rwkv",
    )(*args)


def _merge_kernel(oa_ref, ob_ref, oc_ref, ga_ref, gb_ref, gc_ref, x_ref, wpa_ref, wpb_ref, wpc_ref, wout_ref,
                  o_ref):
    pa = _dot(oa_ref[...], wpa_ref[...])
    pb = _dot(ob_ref[...], wpb_ref[...])
    pc = _dot(oc_ref[...], wpc_ref[...])
    sig = lambda ref: _sigmoid(ref[...].astype(F32))
    merged = sig(ga_ref) * pa + sig(gb_ref) * pb + sig(gc_ref) * pc
    o_ref[...] = x_ref[...] + _dot(merged.astype(BF16), wout_ref[...])


def _merge(oa, ob, oc, p, x2, wpa, wpb, wpc, wout, batch, seq, t):
    n = x2.shape[0]
    ns = seq // t
    row = lambda b, i: (b * ns + i, 0)
    const2 = lambda b, i: (0, 0)
    gate = lambda g: pl.BlockSpec((t, D_MODEL), lambda b, i: (b * ns + i, COL_MERGE // D_MODEL + g))
    return pl.pallas_call(
        _merge_kernel,
        grid=(batch, ns),
        in_specs=[
            pl.BlockSpec((t, W_BR), row),
            pl.BlockSpec((t, W_BR), row),
            pl.BlockSpec((t, W_BR), row),
            gate(0), gate(1), gate(2),
            pl.BlockSpec((t, D_MODEL), row),
            pl.BlockSpec((W_BR, D_MODEL), const2),
            pl.BlockSpec((W_BR, D_MODEL), const2),
            pl.BlockSpec((W_BR, D_MODEL), const2),
            pl.BlockSpec((D_MODEL, D_MODEL), const2),
        ],
        out_specs=pl.BlockSpec((t, D_MODEL), row),
        out_shape=jax.ShapeDtypeStruct((n, D_MODEL), F32),
        compiler_params=_cparams(("parallel", "parallel")),
        name="merge",
    )(oa, ob, oc, p, p, p, x2, wpa, wpb, wpc, wout)


def _regroup_w_in(w_in, l):
    d = w_in.shape[1]
    sizes = (Q_LORA, KV_LORA, ROPE_D, W_BR, W_BR, W_BR, W_BR, H, W_BR,
             3 * W_BR + DECAY_LORA + AAA_LORA, W_BR, 3 * D_MODEL)
    offs = [int(o) for o in np.concatenate([[0], np.cumsum(sizes)])]
    (c_q, c_kv, k_r, gate_a, fq, fk, fv, ff, gate_b, shift, gate_c, merge) = [
        w_in[l, :, offs[i]:offs[i + 1]].astype(BF16) for i in range(len(sizes))]
    z = lambda n: jnp.zeros((d, n), BF16)
    wide = jnp.concatenate([
        merge, gate_a, fq, fk, fv, gate_b,
        shift[:, :W_BR], shift[:, W_BR:2 * W_BR], shift[:, 2 * W_BR:3 * W_BR], gate_c,
    ], axis=1)
    narrow = jnp.concatenate([
        c_q, c_kv,
        z(NOPE), k_r, k_r[:, :ROPE_D // 2], z(LANES - NOPE - ROPE_D - ROPE_D // 2),
        ff, z(LANES - H),
        shift[:, 3 * W_BR:],
    ], axis=1)
    return wide, narrow


def _rope_tables(seq):
    half = ROPE_D // 2
    inv = ROPE_THETA ** (-jnp.arange(0, ROPE_D, 2, dtype=F32) / ROPE_D)
    ang = jnp.arange(seq, dtype=F32)[:, None] * inv[None, :]
    cos, sin = jnp.cos(ang), jnp.sin(ang)
    z = lambda n: jnp.zeros((seq, n), F32)
    c = jnp.concatenate([jnp.ones((seq, NOPE), F32), cos, cos, z(LANES - NOPE - ROPE_D)], axis=1)
    s = jnp.concatenate([z(NOPE), -sin, sin, z(LANES - NOPE - ROPE_D)], axis=1)
    return c, s


def _pad_lanes(v, lo, total=LANES):
    return jnp.zeros((1, total), F32).at[0, lo:lo + v.shape[0]].set(v)


def _fox_selectors():
    selq = np.zeros((3, LANES, H * LANES), np.float32)
    selk = np.zeros((3, LANES, H * LANES), np.float32)
    oneq = np.zeros((1, H * LANES), np.float32)
    onek = np.zeros((1, H * LANES), np.float32)
    for h in range(H):
        for j in range(3):
            selq[j, h, h * LANES + HD + j] = 1.0
            selk[j, h, h * LANES + HD + 3 + j] = -1.0
            oneq[0, h * LANES + HD + 3 + j] = 1.0
            onek[0, h * LANES + HD + j] = 1.0
    return (jnp.asarray(selq, BF16), jnp.asarray(selk, BF16), jnp.asarray(oneq), jnp.asarray(onek))


def _tri(t):
    r = np.arange(t)
    return jnp.asarray((r[None, :] <= r[:, None]).astype(np.float32), BF16)


def _chunk_tri(t):
    r = np.arange(t)
    same = (r[None, :] // CHUNK) == (r[:, None] // CHUNK)
    lower = r[None, :] <= r[:, None]
    return (jnp.asarray((same & lower).astype(np.float32), BF16), jnp.asarray(same.astype(np.float32), BF16))


def _tile(seq, pref):
    return pref if seq % pref == 0 else seq


def kernel(x, norm_g, w_in, mla_qa_g, mla_w_uq, mla_kva_g, mla_w_ukv, mla_q_g, mla_knope_g, mla_krope_g, fox_b_f, fox_q_g, fox_k_g, rwkv_mu, rwkv_w0, rwkv_w_up, rwkv_a0, rwkv_a_up, rwkv_k_k, rwkv_k_a, rwkv_r_k, rwkv_lnx_g, rwkv_lnx_b, rwkv_v0, rwkv_v_down, rwkv_v_up, w_pa, w_pb, w_pc, w_out):
    batch, seq, d = x.shape
    depth = w_in.shape[0]
    n = batch * seq
    t_row = _tile(seq, 512)
    t_in = _tile(n, 1024)
    t_rwkv = _tile(seq, 256)
    assert seq % (ATTN_QT * t_row) == 0 and seq % t_rwkv == 0 and t_rwkv % CHUNK == 0, seq
    x2 = x.reshape(n, d)

    tabs = _rope_tables(seq)
    selq, selk, oneq, onek = _fox_selectors()
    tri = _tri(t_row)
    btri, bones = _chunk_tri(t_rwkv)
    row = lambda v: v.reshape(1, -1).astype(F32)

    half = ROPE_D // 2
    spare = LANES - NOPE - ROPE_D - half
    with_copy = lambda g, lo: jnp.concatenate([g, g[lo:lo + half]])
    vfirst = None
    for l in range(depth):
        w_wide, w_narrow = _regroup_w_in(w_in, l)
        pw, pn = _inproj(x2, row(norm_g[l]), w_wide, w_narrow, t_in, INPROJ_TN)

        wuq = mla_w_uq[l].reshape(Q_LORA, H, NOPE + ROPE_D)
        wuq = jnp.concatenate([wuq, wuq[:, :, NOPE:NOPE + half], jnp.zeros((Q_LORA, H, spare), F32)], axis=2)
        wuq = wuq.reshape(Q_LORA, H * LANES).astype(BF16)
        wukv = mla_w_ukv[l].reshape(KV_LORA, H, NOPE + HD)
        wk = jnp.pad(wukv[:, :, :NOPE], ((0, 0), (0, 0), (0, LANES - NOPE))).reshape(KV_LORA, H * LANES).astype(BF16)
        wv = wukv[:, :, NOPE:].reshape(KV_LORA, W_BR).T.astype(BF16)
        mla_scale = float(NOPE + ROPE_D) ** -0.5 * LOG2E
        mla_wts = (row(mla_qa_g[l]), wuq, row(mla_kva_g[l]), wk, wv,
                   _pad_lanes(with_copy(mla_q_g[l], NOPE) * mla_scale, 0), _pad_lanes(mla_knope_g[l], 0),
                   _pad_lanes(with_copy(mla_krope_g[l], 0), NOPE))
        qa, ka, va = _mla_prep(pn, tabs, mla_wts, seq, t_row)
        o_a = _attention(qa, ka, va, pw, COL_GATE_A, batch, seq, t_row)

        fox_wts = (_pad_lanes(fox_b_f[l], 0), row(jnp.tile(fox_q_g[l] * (float(HD) ** -0.5 * LOG2E), H)),
                   row(jnp.tile(fox_k_g[l], H)), tri, selq, selk, oneq, onek)
        qb, kb, vb = _fox_prep(pw, pn, fox_wts, batch, seq, t_row)
        o_b = _attention(qb, kb, vb, pw, COL_GATE_B, batch, seq, t_row)

        mu = rwkv_mu[l]
        wup = jnp.zeros((LANES, W_BR), F32).at[:DECAY_LORA].set(rwkv_w_up[l]).astype(BF16)
        aup = jnp.zeros((LANES, W_BR), F32).at[DECAY_LORA:].set(rwkv_a_up[l]).astype(BF16)
        rw = [row(mu[:W_BR]), row(mu[W_BR:2 * W_BR]), row(mu[2 * W_BR:3 * W_BR]), row(mu[3 * W_BR:]),
              row(rwkv_w0[l]), row(rwkv_a0[l]), wup, aup, row(rwkv_k_k[l]), row(rwkv_k_a[l]),
              row(rwkv_r_k[l])]
        if l > 0:
            vdown = jnp.zeros((W_BR, LANES), F32).at[:, :MV_LORA].set(rwkv_v_down[l - 1]).astype(BF16)
            vup = jnp.zeros((LANES, W_BR), F32).at[:MV_LORA].set(rwkv_v_up[l - 1]).astype(BF16)
            rw += [row(rwkv_v0[l - 1]), vdown, vup]
        rw += [btri, bones]
        outs = _rwkv(pw, pn, vfirst, rw, row(rwkv_lnx_g[l]), row(rwkv_lnx_b[l]), batch, seq, t_rwkv,
                     first_layer=(l == 0))
        o_c = outs[0]
        if l == 0:
            vfirst = outs[1]

        x2 = _merge(o_a, o_b, o_c, pw, x2, w_pa[l].astype(BF16), w_pb[l].astype(BF16),
                    w_pc[l].astype(BF16), w_out[l].astype(BF16), batch, seq, t_row)
    return x2.reshape(batch, seq, d)
```

```python
import functools
import math

import jax
import jax.numpy as jnp
import numpy as np
from jax import lax
from jax.experimental import pallas as pl
from jax.experimental.pallas import tpu as pltpu

F32 = jnp.float32
BF16 = jnp.bfloat16

LANES = 128
H = 8
HD = 64
NOPE = 64
ROPE_D = 32
Q_LORA = 256
KV_LORA = 128
D_MODEL = 1024
W_BR = H * HD
DECAY_LORA = 64
AAA_LORA = 64
MV_LORA = 32
ROPE_THETA = 10000.0
RWKV_DECAY_SCALE = 0.606531
GN_EPS = 64e-5
EPS = 1e-6
NEG_INF = -1e30
LOG2E = math.log2(math.e)
CHUNK = 64
ROW_TILE = 512
INPROJ_TM = 1024
RWKV_TILE = 256
ATTN_QT = 4
ONES_ROWS = 16

COL_MERGE = 0
COL_GATE_A = 3072
COL_FQ = 3584
COL_FK = 4096
COL_FV = 4608
COL_GATE_B = 5120
COL_SR = 5632
COL_SK = 6144
COL_SV = 6656
COL_GATE_C = 7168
NCOLS_WIDE = 7680
INPROJ_TN = 2560
COL_CQ = 0
COL_CKV = 256
COL_KR = 384
COL_FF = 512
COL_WA = 640
NCOLS_NARROW = 768

VMEM_LIMIT = 56 * 1024 * 1024


def _cparams(sem):
    return pltpu.CompilerParams(dimension_semantics=sem, vmem_limit_bytes=VMEM_LIMIT)


def _sigmoid(x):
    return 0.5 * jnp.tanh(0.5 * x) + 0.5


def _dot(a, b):
    return jnp.dot(a, b, preferred_element_type=F32)


def _split3(x):
    hi = x.astype(BF16)
    r1 = x - hi.astype(F32)
    mid = r1.astype(BF16)
    lo = (r1 - mid.astype(F32)).astype(BF16)
    return hi, mid, lo


def _dot_exact_lhs(m_bf16, x):
    hi, mid, lo = _split3(x)
    return _dot(m_bf16, hi) + _dot(m_bf16, mid) + _dot(m_bf16, lo)


def _pair_seg_sum(x):
    lane = lax.broadcasted_iota(jnp.int32, x.shape, 1)
    first = lane < HD
    s0 = jnp.sum(jnp.where(first, x, 0.0), axis=-1, keepdims=True)
    s1 = jnp.sum(jnp.where(first, 0.0, x), axis=-1, keepdims=True)
    return jnp.where(first, s0, s1)


def _inproj_kernel(x_ref, g_ref, w_ref, wn_ref, o_ref, on_ref, h_ref):
    @pl.when(pl.program_id(1) == 0)
    def _():
        x = x_ref[...]
        ms = jnp.mean(x * x, axis=-1, keepdims=True)
        h = (x * lax.rsqrt(ms + EPS) * g_ref[...]).astype(BF16)
        h_ref[...] = h
        on_ref[...] = _dot(h, wn_ref[...])

    o_ref[...] = _dot(h_ref[...], w_ref[...]).astype(o_ref.dtype)


def _inproj(x2, g, w_wide, w_narrow, tm, tn):
    n = x2.shape[0]
    return pl.pallas_call(
        _inproj_kernel,
        grid=(n // tm, NCOLS_WIDE // tn),
        in_specs=[
            pl.BlockSpec((tm, D_MODEL), lambda i, j: (i, 0)),
            pl.BlockSpec((1, D_MODEL), lambda i, j: (0, 0)),
            pl.BlockSpec((D_MODEL, tn), lambda i, j: (0, j)),
            pl.BlockSpec((D_MODEL, NCOLS_NARROW), lambda i, j: (0, 0)),
        ],
        out_specs=[pl.BlockSpec((tm, tn), lambda i, j: (i, j)),
                   pl.BlockSpec((tm, NCOLS_NARROW), lambda i, j: (i, 0))],
        out_shape=[jax.ShapeDtypeStruct((n, NCOLS_WIDE), BF16), jax.ShapeDtypeStruct((n, NCOLS_NARROW), F32)],
        scratch_shapes=[pltpu.VMEM((tm, D_MODEL), BF16)],
        compiler_params=_cparams(("parallel", "arbitrary")),
        name="inproj",
    )(x2, g, w_wide, w_narrow)


def _rms(x, g, n, keep=None):
    sq = x * x
    if keep is not None:
        sq = jnp.where(keep, sq, 0.0)
    ms = jnp.sum(sq, axis=-1, keepdims=True) / n
    return x * lax.rsqrt(ms + EPS) * g


def _rope(x, c, s):
    return x * c + pltpu.roll(x, LANES - ROPE_D // 2, 1) * s


def _mla_prep_kernel(cq_ref, ckv_ref, kr_ref, c_ref, s_ref, qag_ref, wuq_ref, kvag_ref, wk_ref, wv_ref,
                     qg_ref, kng_ref, krg_ref, q_out, k_out, vt_out):
    c, s = c_ref[...], s_ref[...]
    keep = lax.broadcasted_iota(jnp.int32, c.shape, 1) < NOPE + ROPE_D
    cqn = _rms(cq_ref[...], qag_ref[...], Q_LORA).astype(BF16)
    ckvn = _rms(ckv_ref[...], kvag_ref[...], KV_LORA).astype(BF16)
    q_all = _dot(cqn, wuq_ref[...])
    k_all = _dot(ckvn, wk_ref[...])
    vt_out[0, 0] = lax.dot_general(wv_ref[...], ckvn, _NT, preferred_element_type=F32).astype(BF16)
    k_rope = _rope(_rms(kr_ref[...], krg_ref[...], ROPE_D, keep), c, s)
    qg, kng = qg_ref[...], kng_ref[...]
    for h in range(H):
        sl = slice(h * LANES, (h + 1) * LANES)
        q_out[:, sl] = _rope(_rms(q_all[:, sl], qg, NOPE + ROPE_D, keep), c, s).astype(BF16)
        k_out[:, sl] = (_rms(k_all[:, sl], kng, NOPE) + k_rope).astype(BF16)


def _mla_prep(pn, tabs, wts, seq, t):
    n = pn.shape[0]
    ns = seq // t
    row = lambda i: (i, 0)
    const = lambda i: (0, 0)
    tab_spec = pl.BlockSpec((t, LANES), lambda i: (i % ns, 0))
    return pl.pallas_call(
        _mla_prep_kernel,
        grid=(n // t,),
        in_specs=[
            pl.BlockSpec((t, Q_LORA), lambda i: (i, COL_CQ // Q_LORA)),
            pl.BlockSpec((t, LANES), lambda i: (i, COL_CKV // LANES)),
            pl.BlockSpec((t, LANES), lambda i: (i, COL_KR // LANES)),
            tab_spec, tab_spec,
            pl.BlockSpec((1, Q_LORA), const),
            pl.BlockSpec((Q_LORA, H * LANES), const),
            pl.BlockSpec((1, KV_LORA), const),
            pl.BlockSpec((KV_LORA, H * LANES), const),
            pl.BlockSpec((W_BR, KV_LORA), const),
            pl.BlockSpec((1, LANES), const),
            pl.BlockSpec((1, LANES), const),
            pl.BlockSpec((1, LANES), const),
        ],
        out_specs=[
            pl.BlockSpec((t, H * LANES), row),
            pl.BlockSpec((t, H * LANES), row),
            pl.BlockSpec((1, 1, W_BR, t), lambda i: (i // ns, i % ns, 0, 0)),
        ],
        out_shape=[
            jax.ShapeDtypeStruct((n, H * LANES), BF16),
            jax.ShapeDtypeStruct((n, H * LANES), BF16),
            jax.ShapeDtypeStruct((n // seq, ns, W_BR, t), BF16),
        ],
        compiler_params=_cparams(("parallel",)),
        name="mla_prep",
    )(pn, pn, pn, *tabs, *wts)


def _fox_prep_kernel(fq_ref, fk_ref, fv_ref, ff_ref, bf_ref, qg_ref, kg_ref, tri_ref, selq_ref, selk_ref,
                     oneq_ref, onek_ref, q_out, k_out, vt_out, carry_ref):
    t = fq_ref.shape[0]

    @pl.when(pl.program_id(1) == 0)
    def _():
        carry_ref[...] = jnp.zeros_like(carry_ref)

    z = ff_ref[...] + bf_ref[...]
    logf = jnp.minimum(z, 0.0) - jnp.log1p(jnp.exp(-jnp.abs(z)))
    cum = carry_ref[...] + _dot_exact_lhs(tri_ref[...], logf)
    carry_ref[...] = cum[t - 1:t, :]
    hi, mid, lo = _split3(cum * LOG2E)
    aug_q = _dot(hi, selq_ref[0]) + _dot(mid, selq_ref[1]) + _dot(lo, selq_ref[2]) + oneq_ref[...]
    aug_k = _dot(hi, selk_ref[0]) + _dot(mid, selk_ref[1]) + _dot(lo, selk_ref[2]) + onek_ref[...]
    vt_out[0, 0] = fv_ref[...].astype(F32).T.astype(BF16)
    lane = lax.broadcasted_iota(jnp.int32, (t, LANES), 1)
    first = lane < HD
    for src_ref, g_ref, aug, dst in ((fq_ref, qg_ref, aug_q, q_out), (fk_ref, kg_ref, aug_k, k_out)):
        for p in range(H // 2):
            sl = slice(p * LANES, (p + 1) * LANES)
            x = src_ref[:, sl].astype(F32)
            ms = _pair_seg_sum(x * x) / HD
            xn = x * lax.rsqrt(ms + EPS) * g_ref[:, sl]
            he, ho = 2 * p, 2 * p + 1
            dst[:, he * LANES:(he + 1) * LANES] = jnp.where(first, xn, aug[:, he * LANES:(he + 1) * LANES]).astype(BF16)
            dst[:, ho * LANES:(ho + 1) * LANES] = jnp.where(
                first, pltpu.roll(xn, HD, 1), aug[:, ho * LANES:(ho + 1) * LANES]).astype(BF16)


def _fox_prep(pw, pn, wts, batch, seq, t):
    n = pw.shape[0]
    ns = seq // t
    row = lambda b, i: (b * ns + i, 0)
    const2 = lambda b, i: (0, 0)
    const3 = lambda b, i: (0, 0, 0)
    return pl.pallas_call(
        _fox_prep_kernel,
        grid=(batch, ns),
        in_specs=[
            pl.BlockSpec((t, W_BR), lambda b, i: (b * ns + i, COL_FQ // W_BR)),
            pl.BlockSpec((t, W_BR), lambda b, i: (b * ns + i, COL_FK // W_BR)),
            pl.BlockSpec((t, W_BR), lambda b, i: (b * ns + i, COL_FV // W_BR)),
            pl.BlockSpec((t, LANES), lambda b, i: (b * ns + i, COL_FF // LANES)),
            pl.BlockSpec((1, LANES), const2),
            pl.BlockSpec((1, W_BR), const2),
            pl.BlockSpec((1, W_BR), const2),
            pl.BlockSpec((t, t), const2),
            pl.BlockSpec((3, LANES, H * LANES), const3),
            pl.BlockSpec((3, LANES, H * LANES), const3),
            pl.BlockSpec((1, H * LANES), const2),
            pl.BlockSpec((1, H * LANES), const2),
        ],
        out_specs=[
            pl.BlockSpec((t, H * LANES), row),
            pl.BlockSpec((t, H * LANES), row),
            pl.BlockSpec((1, 1, W_BR, t), lambda b, i: (b, i, 0, 0)),
        ],
        out_shape=[
            jax.ShapeDtypeStruct((n, H * LANES), BF16),
            jax.ShapeDtypeStruct((n, H * LANES), BF16),
            jax.ShapeDtypeStruct((batch, ns, W_BR, t), BF16),
        ],
        scratch_shapes=[pltpu.VMEM((1, LANES), F32)],
        compiler_params=_cparams(("parallel", "arbitrary")),
        name="fox_prep",
    )(pw, pw, pw, pn, *wts)


def _attn_kernel(q_ref, k_ref, vt_ref, g_ref, o_ref, m_ref, acc_ref, sa_ref, sb_ref, mxa_ref, mxb_ref, *, tk):
    tq = ATTN_QT * tk
    gi = pl.program_id(2)
    nt = (((1,), (1,)), ((), ()))
    kv_pos = lax.broadcasted_iota(jnp.int32, (tk, tq), 0)
    q_pos = lax.broadcasted_iota(jnp.int32, (tk, tq), 1)
    causal = kv_pos <= q_pos
    both = slice(0, tq)
    ones = jnp.ones((ONES_ROWS, tk), BF16)
    m_ref[...] = jnp.full(m_ref.shape, NEG_INF, F32)
    acc_ref[...] = jnp.zeros(acc_ref.shape, F32)

    def scores(c, s_ref, mx_ref, cols):
        start = pl.multiple_of(c * tk, tk)
        for j in range(2):
            q = q_ref[cols, j * LANES:(j + 1) * LANES]
            k = k_ref[pl.ds(start, tk), j * LANES:(j + 1) * LANES]
            s = lax.dot_general(k, q, nt, preferred_element_type=F32)
            s_ref[j, :, cols] = s
            mx_ref[j, :, cols] = jnp.max(s, axis=0, keepdims=True)

    def consume(c, s_ref, mx_ref, cols, mask):
        for j in range(2):
            vt = jnp.concatenate([vt_ref[0, c, j * HD:(j + 1) * HD, :], ones], axis=0)
            s = s_ref[j, :, cols]
            if mask is not None:
                s = jnp.where(mask, s, NEG_INF)
                m_cur = jnp.max(s, axis=0, keepdims=True)
            else:
                m_cur = mx_ref[j, :, cols]
            m_old = m_ref[j, :, cols]
            m_new = jnp.maximum(m_old, m_cur)
            alpha = jnp.exp2(m_old - m_new)
            p = jnp.exp2(s - m_new)
            acc_ref[j, :, cols] = alpha * acc_ref[j, :, cols] + _dot(vt, p.astype(BF16))
            m_ref[j, :, cols] = m_new

    scores(0, sa_ref, mxa_ref, both)

    def body(i, carry):
        c = 2 * i
        scores(c + 1, sb_ref, mxb_ref, both)
        consume(c, sa_ref, mxa_ref, both, None)
        scores(c + 2, sa_ref, mxa_ref, both)
        consume(c + 1, sb_ref, mxb_ref, both, None)
        return carry

    lax.fori_loop(0, gi * (ATTN_QT // 2), body, 0)
    bufs = ((sa_ref, mxa_ref), (sb_ref, mxb_ref))
    for d in range(ATTN_QT):
        c = ATTN_QT * gi + d
        if d + 1 < ATTN_QT:
            scores(c + 1, *bufs[(d + 1) % 2], slice((d + 1) * tk, tq))
        consume(c, *bufs[d % 2], slice(d * tk, tq), causal[:, :tq - d * tk])

    o_t = jnp.concatenate([acc_ref[j, :HD, :] / acc_ref[j, HD:HD + 1, :] for j in range(2)], axis=0)
    g = g_ref[...].astype(F32)
    o_ref[...] = (o_t.T * (g * _sigmoid(g))).astype(BF16)


def _attention(q, k, vt, pw, gate_col, batch, seq, tk):
    n = q.shape[0]
    tq = ATTN_QT * tk
    nq, nk = seq // tq, seq // tk
    return pl.pallas_call(
        functools.partial(_attn_kernel, tk=tk),
        grid=(batch, H // 2, nq),
        in_specs=[
            pl.BlockSpec((tq, 2 * LANES), lambda b, hp, i: (b * nq + i, hp)),
            pl.BlockSpec((seq, 2 * LANES), lambda b, hp, i: (b, hp)),
            pl.BlockSpec((1, nk, LANES, tk), lambda b, hp, i: (b, 0, hp, 0)),
            pl.BlockSpec((tq, LANES), lambda b, hp, i: (b * nq + i, gate_col // LANES + hp)),
        ],
        out_specs=pl.BlockSpec((tq, LANES), lambda b, hp, i: (b * nq + i, hp)),
        out_shape=jax.ShapeDtypeStruct((n, W_BR), BF16),
        scratch_shapes=[
            pltpu.VMEM((2, 1, tq), F32),
            pltpu.VMEM((2, HD + ONES_ROWS, tq), F32),
            pltpu.VMEM((2, tk, tq), F32),
            pltpu.VMEM((2, tk, tq), F32),
            pltpu.VMEM((2, 1, tq), F32),
            pltpu.VMEM((2, 1, tq), F32),
        ],
        compiler_params=_cparams(("parallel", "parallel", "arbitrary")),
        name="attention",
    )(q, k, vt, pw)


def _rwkv_prep_kernel(*refs, first_layer):
    if first_layer:
        (sr_ref, sk_ref, sv_ref, swa_ref, gc_ref, mur_ref, muk_ref, muv_ref, muwa_ref, w0_ref, a0_ref, wup_ref,
         aup_ref, kk_ref, ka_ref, rk_ref, bt_ref, bo_ref,
         at_o, rt_o, bt_o, kt_o, v_o, bh_o, kh_o, wc_o, bonus_o, sg_o, vfirst_o,
         cr_ref, ck_ref, cv_ref, cwa_ref) = refs
    else:
        (sr_ref, sk_ref, sv_ref, swa_ref, gc_ref, vf_ref, mur_ref, muk_ref, muv_ref, muwa_ref, w0_ref, a0_ref,
         wup_ref, aup_ref, kk_ref, ka_ref, rk_ref, v0_ref, vdown_ref, vup_ref, bt_ref, bo_ref,
         at_o, rt_o, bt_o, kt_o, v_o, bh_o, kh_o, wc_o, bonus_o, sg_o,
         cr_ref, ck_ref, cv_ref, cwa_ref) = refs
    t = sr_ref.shape[0]

    def shift(x_ref, c_ref, mu_ref):
        x = x_ref[...].astype(F32)
        rowid = lax.broadcasted_iota(jnp.int32, x.shape, 0)
        prev = jnp.where(rowid == 0, c_ref[...], pltpu.roll(x, 1, 0))
        c_ref[...] = x[t - 1:t, :]
        return x + mu_ref[...] * (prev - x)

    r = shift(sr_ref, cr_ref, mur_ref)
    k = shift(sk_ref, ck_ref, muk_ref)
    v = shift(sv_ref, cv_ref, muv_ref)
    wa = shift(swa_ref, cwa_ref, muwa_ref)

    logw = -RWKV_DECAY_SCALE * _sigmoid(w0_ref[...] + _dot(jnp.tanh(wa).astype(BF16), wup_ref[...]))
    a = _sigmoid(a0_ref[...] + _dot(wa.astype(BF16), aup_ref[...]))
    if first_layer:
        vfirst_o[...] = v
    else:
        low = _dot(v.astype(BF16), vdown_ref[...]).astype(BF16)
        nu = _sigmoid(v0_ref[...] + _dot(low, vup_ref[...]))
        v = v + (vf_ref[...] - v) * nu

    kk = k * kk_ref[...]
    k_mod = k * (1.0 + (a - 1.0) * ka_ref[...])
    rk = r * k_mod * rk_ref[...]
    kk_n, bonus = [], []
    for p in range(H // 2):
        sl = slice(p * LANES, (p + 1) * LANES)
        nrm = jnp.sqrt(_pair_seg_sum(kk[:, sl] * kk[:, sl]))
        kk_n.append(kk[:, sl] / jnp.maximum(nrm, 1e-12))
        bonus.append(_pair_seg_sum(rk[:, sl]) * v[:, sl])
    kk = jnp.concatenate(kk_n, axis=-1)
    bonus = jnp.concatenate(bonus, axis=-1)
    a_vec = -kk
    b_vec = kk * a

    terms = _split3(logw)
    lcum = sum(_dot(bt_ref[...], x) for x in terms)
    ltot = sum(_dot(bo_ref[...], x) for x in terms)
    e_minus = jnp.exp(-lcum)
    e_rem = jnp.exp(ltot - lcum)
    g = gc_ref[...].astype(F32)
    for o_ref, val in ((at_o, a_vec * jnp.exp(lcum - logw)), (rt_o, r * jnp.exp(lcum)), (bt_o, b_vec * e_minus),
                       (kt_o, k_mod * e_minus), (v_o, v), (bh_o, b_vec * e_rem), (kh_o, k_mod * e_rem),
                       (wc_o, jnp.exp(ltot)), (bonus_o, bonus), (sg_o, g * _sigmoid(g))):
        o_ref[...] = val.astype(o_ref.dtype)


def _mm(a, b, dims):
    return lax.dot_general(a.astype(BF16), b.astype(BF16), dims, preferred_element_type=F32)


_NN = (((1,), (0,)), ((), ()))
_NT = (((1,), (1,)), ((), ()))
_TN = (((0,), (0,)), ((), ()))


def _rwkv_chunk_kernel(at_ref, rt_ref, bt_ref, kt_ref, v_ref, bh_ref, kh_ref, wc_ref, bonus_ref, sg_ref,
                       lg_ref, lb_ref, o_ref, s_all_ref, *, nchunk):
    mm = _mm
    first = lax.broadcasted_iota(jnp.int32, (CHUNK, LANES), 1) < HD
    r2 = lax.broadcasted_iota(jnp.int32, (LANES, LANES), 0)
    c2 = lax.broadcasted_iota(jnp.int32, (LANES, LANES), 1)
    same_head = (r2 < HD) == (c2 < HD)
    eye2 = r2 == c2
    t_row, t_col = r2 % CHUNK, c2 % CHUNK
    strict2 = jnp.logical_and(same_head, t_col < t_row)
    incl2 = jnp.logical_and(same_head, t_col <= t_row)
    each = lambda f, *cols: [f(*xs) for xs in zip(*cols)]
    pairs = [(c, p) for c in range(nchunk) for p in range(H // 2)]
    npair = range(len(pairs))
    ld = lambda ref: [ref[c * CHUNK:(c + 1) * CHUNK, p * LANES:(p + 1) * LANES] for c, p in pairs]
    at, rt, bt, kt, v, bh, kh = ld(at_ref), ld(rt_ref), ld(bt_ref), ld(kt_ref), ld(v_ref), ld(bh_ref), ld(kh_ref)
    zero = jnp.zeros((CHUNK, LANES), BF16)
    stack = lambda x: jnp.concatenate([jnp.where(first, x, zero), jnp.where(first, zero, x)], axis=0)
    fold = lambda x: x[:CHUNK] + x[CHUNK:]
    a_s, b_s, k_s, v_s = each(stack, at), each(stack, bt), each(stack, kt), each(stack, v)
    x1 = each(lambda a, r: jnp.concatenate([a, stack(r.astype(BF16))], axis=0), a_s, rt)
    gb = each(lambda x, b: mm(x, b, _NT), x1, b_s)
    gk = each(lambda x, k: mm(x, k, _NT), x1, k_s)
    a_ab = [jnp.where(strict2, g[:LANES], 0.0) for g in gb]
    a_rb = [jnp.where(incl2, g[LANES:], 0.0) for g in gb]
    a_ak = [jnp.where(strict2, g[:LANES], 0.0) for g in gk]
    a_rk = [jnp.where(incl2, g[LANES:], 0.0) for g in gk]
    tinv = [jnp.where(eye2, 1.0, a) for a in a_ab]
    pw = a_ab
    for _ in range(int(math.log2(CHUNK)) - 1):
        pw = each(lambda p: mm(p, p, _NN), pw)
        tinv = each(lambda t, p: t + mm(t, p, _NN), tinv, pw)
    av = each(lambda a, x: mm(a, x, _NN), a_ak, v_s)
    tx = each(lambda t, x, a: mm(t, jnp.concatenate([x.astype(BF16), a], axis=1), _NN), tinv, av, a_s)
    ry = each(lambda a, x: mm(a, x, _NN), a_rb, tx)
    yk = each(lambda a, x: mm(a, x, _NN), a_rk, v_s)
    u = [fold(t[:, :LANES]) for t in tx]
    a_til = [fold(t[:, LANES:]) for t in tx]
    y_in = [fold(r[:, :LANES] + k) for r, k in zip(ry, yk)]
    r_hat = [r + fold(x[:, LANES:]) for r, x in zip(rt, ry)]
    m_new = [jnp.where(same_head, mm(a_til[i], bh[i], _TN), 0.0) for i in npair]
    n_new = [jnp.where(same_head, mm(jnp.concatenate([u[i].astype(BF16), v[i]], axis=0),
                                     jnp.concatenate([bh[i], kh[i]], axis=0), _TN), 0.0) for i in npair]
    lg, lb = lg_ref[...], lb_ref[...]
    for c in range(nchunk):
        idx = [i for i, (ci, _) in enumerate(pairs) if ci == c]
        s0 = [s_all_ref[p] for p in range(H // 2)]
        y = [y_in[i] + mm(r_hat[i], s0[p], _NT) for p, i in enumerate(idx)]
        for p, i in enumerate(idx):
            wc = wc_ref[c * CHUNK:c * CHUNK + 1, p * LANES:(p + 1) * LANES]
            s_all_ref[p] = mm(s0[p], jnp.where(eye2, wc, 0.0) + m_new[i], _NN) + n_new[i]
        for p, i in enumerate(idx):
            rows, lanes = slice(c * CHUNK, (c + 1) * CHUNK), slice(p * LANES, (p + 1) * LANES)
            mu = _pair_seg_sum(y[p]) / HD
            d = y[p] - mu
            var = _pair_seg_sum(d * d) / HD
            yn = d * lax.rsqrt(var + GN_EPS) * lg[:, lanes] + lb[:, lanes]
            o_ref[rows, lanes] = ((yn + bonus_ref[rows, lanes]) * sg_ref[rows, lanes]).astype(o_ref.dtype)


_RWKV_STAGE_DTYPES = (BF16, F32, BF16, BF16, BF16, BF16, BF16, F32, F32, F32)


def _rwkv_kernel(*refs, first_layer, nchunk, ns):
    n_in = 18 if first_layer else 22
    ins = refs[:n_in]
    lg_ref, lb_ref, o_ref = refs[n_in:n_in + 3]
    pos = n_in + 3
    vfirst = refs[pos:pos + 1] if first_layer else ()
    pos += len(vfirst)
    stage_a, stage_b = refs[pos:pos + 10], refs[pos + 10:pos + 20]
    carries = refs[pos + 20:pos + 24]
    s_all_ref = refs[pos + 24]
    i = pl.program_id(1)

    @pl.when(i == 0)
    def _():
        for r in stage_b + carries + (s_all_ref,):
            r[...] = jnp.zeros(r.shape, r.dtype)

    def run(done, todo):
        _rwkv_prep_kernel(*ins, *todo, *vfirst, *carries, first_layer=first_layer)
        _rwkv_chunk_kernel(*done, lg_ref, lb_ref, o_ref, s_all_ref, nchunk=nchunk)

    @pl.when(i % 2 == 0)
    def _():
        run(stage_b, stage_a)

    @pl.when(i % 2 == 1)
    def _():
        run(stage_a, stage_b)


def _rwkv(pw, pn, vfirst, wts, lnx_g, lnx_b, batch, seq, t, first_layer):
    n = pw.shape[0]
    ns = seq // t
    const2 = lambda b, i: (0, 0)
    cur = lambda b, i: b * ns + jnp.minimum(i, ns - 1)
    pcol = lambda col, w: pl.BlockSpec((t, w), lambda b, i: (cur(b, i), col // w))
    vec = pl.BlockSpec((1, W_BR), const2)
    in_specs = [pcol(COL_SR, W_BR), pcol(COL_SK, W_BR), pcol(COL_SV, W_BR), pcol(COL_WA, LANES),
                pcol(COL_GATE_C, W_BR)]
    args = [pw, pw, pw, pn, pw]
    if not first_layer:
        in_specs.append(pl.BlockSpec((t, W_BR), lambda b, i: (cur(b, i), 0)))
        args.append(vfirst)
    in_specs += [vec, vec, vec, pl.BlockSpec((1, LANES), const2), vec, vec,
                 pl.BlockSpec((LANES, W_BR), const2), pl.BlockSpec((LANES, W_BR), const2), vec, vec, vec]
    if not first_layer:
        in_specs += [vec, pl.BlockSpec((W_BR, LANES), const2), pl.BlockSpec((LANES, W_BR), const2)]
    in_specs += [pl.BlockSpec((t, t), const2)] * 2 + [vec, vec]
    args += list(wts) + [lnx_g, lnx_b]
    out_specs = [pl.BlockSpec((t, W_BR), lambda b, i: (b * ns + jnp.maximum(i - 1, 0), 0))]
    out_shape = [jax.ShapeDtypeStruct((n, W_BR), BF16)]
    if first_layer:
        out_specs.append(pl.BlockSpec((t, W_BR), lambda b, i: (jnp.where(i < ns, b * ns + i, batch * ns + b), 0)))
        out_shape.append(jax.ShapeDtypeStruct((n + batch * t, W_BR), F32))
    return pl.pallas_call(
        functools.partial(_rwkv_kernel, first_layer=first_layer, nchunk=t // CHUNK, ns=ns),
        grid=(batch, ns + 1),
        in_specs=in_specs,
        out_specs=out_specs,
        out_shape=out_shape,
        scratch_shapes=[pltpu.VMEM((t, W_BR), dt) for dt in _RWKV_STAGE_DTYPES * 2]
        + [pltpu.VMEM((1, W_BR), F32), pltpu.VMEM((1, W_BR), F32), pltpu.VMEM((1, W_BR), F32),
           pltpu.VMEM((1, LANES), F32), pltpu.VMEM((H // 2, LANES, LANES), F32)],
        compiler_params=_cparams(("parallel", "arbitrary")),
        name="rwkv",
    )(*args)


def _merge_kernel(oa_ref, ob_ref, oc_ref, ga_ref, gb_ref, gc_ref, x_ref, wpa_ref, wpb_ref, wpc_ref, wout_ref,
                  o_ref):
    pa = _dot(oa_ref[...], wpa_ref[...])
    pb = _dot(ob_ref[...], wpb_ref[...])
    pc = _dot(oc_ref[...], wpc_ref[...])
    sig = lambda ref: _sigmoid(ref[...].astype(F32))
    merged = sig(ga_ref) * pa + sig(gb_ref) * pb + sig(gc_ref) * pc
    o_ref[...] = x_ref[...] + _dot(merged.astype(BF16), wout_ref[...])


def _merge(oa, ob, oc, p, x2, wpa, wpb, wpc, wout, batch, seq, t):
    n = x2.shape[0]
    ns = seq // t
    row = lambda b, i: (b * ns + i, 0)
    const2 = lambda b, i: (0, 0)
    gate = lambda g: pl.BlockSpec((t, D_MODEL), lambda b, i: (b * ns + i, COL_MERGE // D_MODEL + g))
    return pl.pallas_call(
        _merge_kernel,
        grid=(batch, ns),
        in_specs=[
            pl.BlockSpec((t, W_BR), row),
            pl.BlockSpec((t, W_BR), row),
            pl.BlockSpec((t, W_BR), row),
            gate(0), gate(1), gate(2),
            pl.BlockSpec((t, D_MODEL), row),
            pl.BlockSpec((W_BR, D_MODEL), const2),
            pl.BlockSpec((W_BR, D_MODEL), const2),
            pl.BlockSpec((W_BR, D_MODEL), const2),
            pl.BlockSpec((D_MODEL, D_MODEL), const2),
        ],
        out_specs=pl.BlockSpec((t, D_MODEL), row),
        out_shape=jax.ShapeDtypeStruct((n, D_MODEL), F32),
        compiler_params=_cparams(("parallel", "parallel")),
        name="merge",
    )(oa, ob, oc, p, p, p, x2, wpa, wpb, wpc, wout)


def _regroup_w_in(w_in, l):
    d = w_in.shape[1]
    sizes = (Q_LORA, KV_LORA, ROPE_D, W_BR, W_BR, W_BR, W_BR, H, W_BR,
             3 * W_BR + DECAY_LORA + AAA_LORA, W_BR, 3 * D_MODEL)
    offs = [int(o) for o in np.concatenate([[0], np.cumsum(sizes)])]
    (c_q, c_kv, k_r, gate_a, fq, fk, fv, ff, gate_b, shift, gate_c, merge) = [
        w_in[l, :, offs[i]:offs[i + 1]].astype(BF16) for i in range(len(sizes))]
    z = lambda n: jnp.zeros((d, n), BF16)
    wide = jnp.concatenate([
        merge, gate_a, fq, fk, fv, gate_b,
        shift[:, :W_BR], shift[:, W_BR:2 * W_BR], shift[:, 2 * W_BR:3 * W_BR], gate_c,
    ], axis=1)
    narrow = jnp.concatenate([
        c_q, c_kv,
        z(NOPE), k_r, k_r[:, :ROPE_D // 2], z(LANES - NOPE - ROPE_D - ROPE_D // 2),
        ff, z(LANES - H),
        shift[:, 3 * W_BR:],
    ], axis=1)
    return wide, narrow


def _rope_tables(seq):
    inv =ROPE_THETA ** (-jnp.arange(0, ROPE_D, 2, dtype=F32) / ROPE_D)
    ang = jnp.arange(seq, dtype=F32)[:, None] * inv[None, :]
    cos, sin = jnp.cos(ang), jnp.sin(ang)
    z = lambda n: jnp.zeros((seq, n), F32)
    c = jnp.concatenate([jnp.ones((seq, NOPE), F32), cos, cos, z(LANES - NOPE - ROPE_D)], axis=1)
    s = jnp.concatenate([z(NOPE), -sin, sin, z(LANES - NOPE - ROPE_D)], axis=1)
    return c, s


def _pad_lanes(v, lo, total=LANES):
    return jnp.zeros((1, total), F32).at[0, lo:lo + v.shape[0]].set(v)


def _fox_selectors():
    selq = np.zeros((3, LANES, H * LANES), np.float32)
    selk = np.zeros((3, LANES, H * LANES), np.float32)
    oneq = np.zeros((1, H * LANES), np.float32)
    onek = np.zeros((1, H * LANES), np.float32)
    for h in range(H):
        for j in range(3):
            selq[j, h, h * LANES + HD + j] = 1.0
            selk[j, h, h * LANES + HD + 3 + j] = -1.0
            oneq[0, h * LANES + HD + 3 + j] = 1.0
            onek[0, h * LANES + HD + j] = 1.0
    return (jnp.asarray(selq, BF16), jnp.asarray(selk, BF16), jnp.asarray(oneq), jnp.asarray(onek))


def _tri(t):
    r = np.arange(t)
    return jnp.asarray((r[None, :] <= r[:, None]).astype(np.float32), BF16)


def _chunk_tri(t):
    r = np.arange(t)
    same = (r[None, :] // CHUNK) == (r[:, None] // CHUNK)
    lower = r[None, :] <= r[:, None]
    return (jnp.asarray((same & lower).astype(np.float32), BF16), jnp.asarray(same.astype(np.float32), BF16))


def _tile(seq, pref):
    return pref if seq % pref == 0 else seq


def kernel(x, norm_g, w_in, mla_qa_g, mla_w_uq, mla_kva_g, mla_w_ukv, mla_q_g, mla_knope_g, mla_krope_g, fox_b_f, fox_q_g, fox_k_g, rwkv_mu, rwkv_w0, rwkv_w_up, rwkv_a0, rwkv_a_up, rwkv_k_k, rwkv_k_a, rwkv_r_k, rwkv_lnx_g, rwkv_lnx_b, rwkv_v0, rwkv_v_down, rwkv_v_up, w_pa, w_pb, w_pc, w_out):
    batch, seq, d = x.shape
    depth = w_in.shape[0]
    n = batch * seq
    t_row = _tile(seq, ROW_TILE)
    t_in = _tile(n, INPROJ_TM)
    t_rwkv = _tile(seq, RWKV_TILE)
    assert seq % (ATTN_QT * t_row) == 0 and seq % t_rwkv == 0 and t_rwkv % CHUNK == 0, seq
    x2 = x.reshape(n, d)

    tabs = _rope_tables(seq)
    selq, selk, oneq, onek = _fox_selectors()
    tri = _tri(t_row)
    btri, bones = _chunk_tri(t_rwkv)
    row = lambda v: v.reshape(1, -1).astype(F32)

    half = ROPE_D // 2
    spare = LANES - NOPE - ROPE_D - half
    with_copy = lambda g, lo: jnp.concatenate([g, g[lo:lo + half]])
    vfirst = None
    for l in range(depth):
        w_wide, w_narrow = _regroup_w_in(w_in, l)
        pw, pn = _inproj(x2, row(norm_g[l]), w_wide, w_narrow, t_in, INPROJ_TN)

        wuq = mla_w_uq[l].reshape(Q_LORA, H, NOPE + ROPE_D)
        wuq = jnp.concatenate([wuq, wuq[:, :, NOPE:NOPE + half], jnp.zeros((Q_LORA, H, spare), F32)], axis=2)
        wuq = wuq.reshape(Q_LORA, H * LANES).astype(BF16)
        wukv = mla_w_ukv[l].reshape(KV_LORA, H, NOPE + HD)
        wk = jnp.pad(wukv[:, :, :NOPE], ((0, 0), (0, 0), (0, LANES - NOPE))).reshape(KV_LORA, H * LANES).astype(BF16)
        wv = wukv[:, :, NOPE:].reshape(KV_LORA, W_BR).T.astype(BF16)
        mla_scale = float(NOPE + ROPE_D) ** -0.5 * LOG2E
        mla_wts = (row(mla_qa_g[l]), wuq, row(mla_kva_g[l]), wk, wv,
                   _pad_lanes(with_copy(mla_q_g[l], NOPE) * mla_scale, 0), _pad_lanes(mla_knope_g[l], 0),
                   _pad_lanes(with_copy(mla_krope_g[l], 0), NOPE))
        qa, ka, va = _mla_prep(pn, tabs, mla_wts, seq, t_row)
        o_a = _attention(qa, ka, va, pw, COL_GATE_A, batch, seq, t_row)

        fox_wts = (_pad_lanes(fox_b_f[l], 0), row(jnp.tile(fox_q_g[l] * (float(HD) ** -0.5 * LOG2E), H)),
                   row(jnp.tile(fox_k_g[l], H)), tri, selq, selk, oneq, onek)
        qb, kb, vb = _fox_prep(pw, pn, fox_wts, batch, seq, t_row)
        o_b = _attention(qb, kb, vb, pw, COL_GATE_B, batch, seq, t_row)

        mu = rwkv_mu[l]
        wup = jnp.zeros((LANES, W_BR), F32).at[:DECAY_LORA].set(rwkv_w_up[l]).astype(BF16)
        aup = jnp.zeros((LANES, W_BR), F32).at[DECAY_LORA:].set(rwkv_a_up[l]).astype(BF16)
        rw = [row(mu[:W_BR]), row(mu[W_BR:2 * W_BR]), row(mu[2 * W_BR:3 * W_BR]), row(mu[3 * W_BR:]),
              row(rwkv_w0[l]), row(rwkv_a0[l]), wup, aup, row(rwkv_k_k[l]), row(rwkv_k_a[l]),
              row(rwkv_r_k[l])]
        if l > 0:
            vdown = jnp.zeros((W_BR, LANES), F32).at[:, :MV_LORA].set(rwkv_v_down[l - 1]).astype(BF16)
            vup = jnp.zeros((LANES, W_BR), F32).at[:MV_LORA].set(rwkv_v_up[l - 1]).astype(BF16)
            rw += [row(rwkv_v0[l - 1]), vdown, vup]
        rw += [btri, bones]
        outs = _rwkv(pw, pn, vfirst, rw, row(rwkv_lnx_g[l]), row(rwkv_lnx_b[l]), batch, seq, t_rwkv,
                     first_layer=(l == 0))
        o_c = outs[0]
        if l == 0:
            vfirst = outs[1]

        x2 = _merge(o_a, o_b, o_c, pw, x2, w_pa[l].astype(BF16), w_pb[l].astype(BF16),
                    w_pc[l].astype(BF16), w_out[l].astype(BF16), batch, seq, t_row)
    return x2.reshape(batch, seq, d)
```

```python
import functools
import math

import jax
import jax.numpy as jnp
import numpy as np
from jax import lax
from jax.experimental import pallas as pl
from jax.experimental.pallas import tpu as pltpu

F32 = jnp.float32
BF16 = jnp.bfloat16

LANES = 128
H = 8
HD = 64
NOPE = 64
ROPE_D = 32
Q_LORA = 256
KV_LORA = 128
D_MODEL = 1024
W_BR = H * HD
DECAY_LORA = 64
AAA_LORA = 64
MV_LORA = 32
ROPE_THETA = 10000.0
RWKV_DECAY_SCALE = 0.606531
GN_EPS = 64e-5
EPS = 1e-6
NEG_INF = -1e30
LOG2E = math.log2(math.e)
CHUNK = 64
ROW_TILE = 512
INPROJ_TM = 1024
RWKV_TILE = 256
ATTN_QT = 4
ONES_ROWS = 16

COL_MERGE = 0
COL_GATE_A = 3072
COL_FQ = 3584
COL_FK = 4096
COL_FV = 4608
COL_GATE_B = 5120
COL_SR = 5632
COL_SK = 6144
COL_SV = 6656
COL_GATE_C = 7168
NCOLS_WIDE = 7680
INPROJ_TN = 2560
COL_CQ = 0
COL_CKV = 256
COL_KR = 384
COL_FF = 512
COL_WA = 640
NCOLS_NARROW = 768

VMEM_LIMIT = 56 * 1024 * 1024


def _cparams(sem):
    return pltpu.CompilerParams(dimension_semantics=sem, vmem_limit_bytes=VMEM_LIMIT)


def _sigmoid(x):
    return 0.5 * jnp.tanh(0.5 * x) + 0.5


def _dot(a, b):
    return jnp.dot(a, b, preferred_element_type=F32)


def _split3(x):
    hi = x.astype(BF16)
    r1 = x - hi.astype(F32)
    mid = r1.astype(BF16)
    lo = (r1 - mid.astype(F32)).astype(BF16)
    return hi, mid, lo


def _dot_exact_lhs(m_bf16, x):
    hi, mid, lo = _split3(x)
    return _dot(m_bf16, hi) + _dot(m_bf16, mid) + _dot(m_bf16, lo)


def _pair_seg_sum(x):
    lane = lax.broadcasted_iota(jnp.int32, x.shape, 1)
    first = lane < HD
    s0 = jnp.sum(jnp.where(first, x, 0.0), axis=-1, keepdims=True)
    s1 = jnp.sum(jnp.where(first, 0.0, x), axis=-1, keepdims=True)
    return jnp.where(first, s0, s1)


def _inproj_kernel(x_ref, g_ref, w_ref, wn_ref, o_ref, on_ref, h_ref):
    @pl.when(pl.program_id(1) == 0)
    def _():
        x = x_ref[...]
        ms = jnp.mean(x * x, axis=-1, keepdims=True)
        h = (x * lax.rsqrt(ms + EPS) * g_ref[...]).astype(BF16)
        h_ref[...] = h
        on_ref[...] = _dot(h, wn_ref[...])

    o_ref[...] = _dot(h_ref[...], w_ref[...]).astype(o_ref.dtype)


def _inproj(x2, g, w_wide, w_narrow, tm, tn):
    n = x2.shape[0]
    return pl.pallas_call(
        _inproj_kernel,
        grid=(n // tm, NCOLS_WIDE // tn),
        in_specs=[
            pl.BlockSpec((tm, D_MODEL), lambda i, j: (i, 0)),
            pl.BlockSpec((1, D_MODEL), lambda i, j: (0, 0)),
            pl.BlockSpec((D_MODEL, tn), lambda i, j: (0, j)),
            pl.BlockSpec((D_MODEL, NCOLS_NARROW), lambda i, j: (0, 0)),
        ],
        out_specs=[pl.BlockSpec((tm, tn), lambda i, j: (i, j)),
                   pl.BlockSpec((tm, NCOLS_NARROW), lambda i, j: (i, 0))],
        out_shape=[jax.ShapeDtypeStruct((n, NCOLS_WIDE), BF16), jax.ShapeDtypeStruct((n, NCOLS_NARROW), F32)],
        scratch_shapes=[pltpu.VMEM((tm, D_MODEL), BF16)],
        compiler_params=_cparams(("parallel", "arbitrary")),
        name="inproj",
    )(x2, g, w_wide, w_narrow)


def _rms(x, g, n, keep=None):
    sq = x * x
    if keep is not None:
        sq = jnp.where(keep, sq, 0.0)
    ms = jnp.sum(sq, axis=-1, keepdims=True) / n
    return x * lax.rsqrt(ms + EPS) * g


def _rope(x, c, s):
    return x * c + pltpu.roll(x, LANES - ROPE_D // 2, 1) * s


def _mla_prep_kernel(cq_ref, ckv_ref, kr_ref, c_ref, s_ref, qag_ref, wuq_ref, kvag_ref, wk_ref, wv_ref,
                     qg_ref, kng_ref, krg_ref, q_out, k_out, vt_out):
    c, s = c_ref[...], s_ref[...]
    keep = lax.broadcasted_iota(jnp.int32, c.shape, 1) < NOPE + ROPE_D
    cqn = _rms(cq_ref[...], qag_ref[...], Q_LORA).astype(BF16)
    ckvn = _rms(ckv_ref[...], kvag_ref[...], KV_LORA).astype(BF16)
    q_all = _dot(cqn, wuq_ref[...])
    k_all = _dot(ckvn, wk_ref[...])
    vt_out[0, 0] = lax.dot_general(wv_ref[...], ckvn, _NT, preferred_element_type=F32).astype(BF16)
    k_rope = _rope(_rms(kr_ref[...], krg_ref[...], ROPE_D, keep), c, s)
    qg, kng = qg_ref[...], kng_ref[...]
    for h in range(H):
        sl = slice(h * LANES, (h + 1) * LANES)
        q_out[:, sl] = _rope(_rms(q_all[:, sl], qg, NOPE + ROPE_D, keep), c, s).astype(BF16)
        k_out[:, sl] = (_rms(k_all[:, sl], kng, NOPE) + k_rope).astype(BF16)


def _mla_prep(pn, tabs, wts, seq, t):
    n = pn.shape[0]
    ns = seq // t
    row = lambda i: (i, 0)
    const = lambda i: (0, 0)
    tab_spec = pl.BlockSpec((t, LANES), lambda i: (i % ns, 0))
    return pl.pallas_call(
        _mla_prep_kernel,
        grid=(n // t,),
        in_specs=[
            pl.BlockSpec((t, Q_LORA), lambda i: (i, COL_CQ // Q_LORA)),
            pl.BlockSpec((t, LANES), lambda i: (i, COL_CKV // LANES)),
            pl.BlockSpec((t, LANES), lambda i: (i, COL_KR // LANES)),
            tab_spec, tab_spec,
            pl.BlockSpec((1, Q_LORA), const),
            pl.BlockSpec((Q_LORA, H * LANES), const),
            pl.BlockSpec((1, KV_LORA), const),
            pl.BlockSpec((KV_LORA, H * LANES), const),
            pl.BlockSpec((W_BR, KV_LORA), const),
            pl.BlockSpec((1, LANES), const),
            pl.BlockSpec((1, LANES), const),
            pl.BlockSpec((1, LANES), const),
        ],
        out_specs=[
            pl.BlockSpec((t, H * LANES), row),
            pl.BlockSpec((t, H * LANES), row),
            pl.BlockSpec((1, 1, W_BR, t), lambda i: (i // ns, i % ns, 0, 0)),
        ],
        out_shape=[
            jax.ShapeDtypeStruct((n, H * LANES), BF16),
            jax.ShapeDtypeStruct((n, H * LANES), BF16),
            jax.ShapeDtypeStruct((n // seq, ns, W_BR, t), BF16),
        ],
        compiler_params=_cparams(("parallel",)),
        name="mla_prep",
    )(pn, pn, pn, *tabs, *wts)


def _fox_prep_kernel(fq_ref, fk_ref, fv_ref, ff_ref, bf_ref, qg_ref, kg_ref, tri_ref, selq_ref, selk_ref,
                     oneq_ref, onek_ref, q_out, k_out, vt_out, carry_ref):
    t = fq_ref.shape[0]

    @pl.when(pl.program_id(1) == 0)
    def _():
        carry_ref[...] = jnp.zeros_like(carry_ref)

    z = ff_ref[...] + bf_ref[...]
    logf = jnp.minimum(z, 0.0) - jnp.log1p(jnp.exp(-jnp.abs(z)))
    cum = carry_ref[...] + _dot_exact_lhs(tri_ref[...], logf)
    carry_ref[...] = cum[t - 1:t, :]
    hi, mid, lo = _split3(cum * LOG2E)
    aug_q = _dot(hi, selq_ref[0]) + _dot(mid, selq_ref[1]) + _dot(lo, selq_ref[2]) + oneq_ref[...]
    aug_k = _dot(hi, selk_ref[0]) + _dot(mid, selk_ref[1]) + _dot(lo, selk_ref[2]) + onek_ref[...]
    vt_out[0, 0] = fv_ref[...].astype(F32).T.astype(BF16)
    lane = lax.broadcasted_iota(jnp.int32, (t, LANES), 1)
    first = lane < HD
    for src_ref, g_ref, aug, dst in ((fq_ref, qg_ref, aug_q, q_out), (fk_ref, kg_ref, aug_k, k_out)):
        for p in range(H // 2):
            sl = slice(p * LANES, (p + 1) * LANES)
            x = src_ref[:, sl].astype(F32)
            ms = _pair_seg_sum(x * x) / HD
            xn = x * lax.rsqrt(ms + EPS) * g_ref[:, sl]
            he, ho = 2 * p, 2 * p + 1
            dst[:, he * LANES:(he + 1) * LANES] = jnp.where(first, xn, aug[:, he * LANES:(he + 1) * LANES]).astype(BF16)
            dst[:, ho * LANES:(ho + 1) * LANES] = jnp.where(
                first, pltpu.roll(xn, HD, 1), aug[:, ho * LANES:(ho + 1) * LANES]).astype(BF16)


def _fox_prep(pw, pn, wts, batch, seq, t):
    n = pw.shape[0]
    ns = seq // t
    row = lambda b, i: (b * ns + i, 0)
    const2 = lambda b, i: (0, 0)
    const3 = lambda b, i: (0, 0, 0)
    return pl.pallas_call(
        _fox_prep_kernel,
        grid=(batch, ns),
        in_specs=[
            pl.BlockSpec((t, W_BR), lambda b, i: (b * ns + i, COL_FQ // W_BR)),
            pl.BlockSpec((t, W_BR), lambda b, i: (b * ns + i, COL_FK // W_BR)),
            pl.BlockSpec((t, W_BR), lambda b, i: (b * ns + i, COL_FV // W_BR)),
            pl.BlockSpec((t, LANES), lambda b, i: (b * ns + i, COL_FF // LANES)),
            pl.BlockSpec((1, LANES), const2),
            pl.BlockSpec((1, W_BR), const2),
            pl.BlockSpec((1, W_BR), const2),
            pl.BlockSpec((t, t), const2),
            pl.BlockSpec((3, LANES, H * LANES), const3),
            pl.BlockSpec((3, LANES, H * LANES), const3),
            pl.BlockSpec((1, H * LANES), const2),
            pl.BlockSpec((1, H * LANES), const2),
        ],
        out_specs=[
            pl.BlockSpec((t, H * LANES), row),
            pl.BlockSpec((t, H * LANES), row),
            pl.BlockSpec((1, 1, W_BR, t), lambda b, i: (b, i, 0, 0)),
        ],
        out_shape=[
            jax.ShapeDtypeStruct((n, H * LANES), BF16),
            jax.ShapeDtypeStruct((n, H * LANES), BF16),
            jax.ShapeDtypeStruct((batch, ns, W_BR, t), BF16),
        ],
        scratch_shapes=[pltpu.VMEM((1, LANES), F32)],
        compiler_params=_cparams(("parallel", "arbitrary")),
        name="fox_prep",
    )(pw, pw, pw, pn, *wts)


def _attn_kernel(q_ref, k_ref, vt_ref, g_ref, o_ref, m_ref, acc_ref, sa_ref, sb_ref, mxa_ref, mxb_ref, *, tk):
    tq = ATTN_QT * tk
    gi = pl.program_id(2)
    nt = (((1,), (1,)), ((), ()))
    kv_pos = lax.broadcasted_iota(jnp.int32, (tk, tq), 0)
    q_pos = lax.broadcasted_iota(jnp.int32, (tk, tq), 1)
    causal = kv_pos <= q_pos
    both = slice(0, tq)
    ones = jnp.ones((ONES_ROWS, tk), BF16)
    m_ref[...] = jnp.full(m_ref.shape, NEG_INF, F32)
    acc_ref[...] = jnp.zeros(acc_ref.shape, F32)

    def scores(c, s_ref, mx_ref, cols):
        start = pl.multiple_of(c * tk, tk)
        for j in range(2):
            q = q_ref[cols, j * LANES:(j + 1) * LANES]
            k = k_ref[pl.ds(start, tk), j * LANES:(j + 1) * LANES]
            s = lax.dot_general(k, q, nt, preferred_element_type=F32)
            s_ref[j, :, cols] = s
            mx_ref[j, :, cols] = jnp.max(s, axis=0, keepdims=True)

    def consume(c, s_ref, mx_ref, cols, mask):
        for j in range(2):
            vt = jnp.concatenate([vt_ref[0, c, j * HD:(j + 1) * HD, :], ones], axis=0)
            s = s_ref[j, :, cols]
            if mask is not None:
                s = jnp.where(mask, s, NEG_INF)
                m_cur = jnp.max(s, axis=0, keepdims=True)
            else:
                m_cur = mx_ref[j, :, cols]
            m_old = m_ref[j, :, cols]
            m_new = jnp.maximum(m_old, m_cur)
            alpha = jnp.exp2(m_old - m_new)
            p = jnp.exp2(s - m_new)
            acc_ref[j, :, cols] = alpha * acc_ref[j, :, cols] + _dot(vt, p.astype(BF16))
            m_ref[j, :, cols] = m_new

    scores(0, sa_ref, mxa_ref, both)

    def body(i, carry):
        c = 2 * i
        scores(c + 1, sb_ref, mxb_ref, both)
        consume(c, sa_ref, mxa_ref, both, None)
        scores(c + 2, sa_ref, mxa_ref, both)
        consume(c + 1, sb_ref, mxb_ref, both, None)
        return carry

    lax.fori_loop(0, gi * (ATTN_QT // 2), body, 0)
    bufs = ((sa_ref, mxa_ref), (sb_ref, mxb_ref))
    for d in range(ATTN_QT):
        c = ATTN_QT * gi + d
        if d + 1 < ATTN_QT:
            scores(c + 1, *bufs[(d + 1) % 2], slice((d + 1) * tk, tq))
        consume(c, *bufs[d % 2], slice(d * tk, tq), causal[:, :tq - d * tk])

    o_t = jnp.concatenate([acc_ref[j, :HD, :] / acc_ref[j, HD:HD + 1, :] for j in range(2)], axis=0)
    g = g_ref[...].astype(F32)
    o_ref[...] = (o_t.T * (g * _sigmoid(g))).astype(BF16)


def _attention(q, k, vt, pw, gate_col, batch, seq, tk):
    n = q.shape[0]
    tq = ATTN_QT * tk
    nq, nk = seq // tq, seq // tk
    return pl.pallas_call(
        functools.partial(_attn_kernel, tk=tk),
        grid=(batch, H // 2, nq),
        in_specs=[
            pl.BlockSpec((tq, 2 * LANES), lambda b, hp, i: (b * nq + i, hp)),
            pl.BlockSpec((seq, 2 * LANES), lambda b, hp, i: (b, hp)),
            pl.BlockSpec((1, nk, LANES, tk), lambda b, hp, i: (b, 0, hp, 0)),
            pl.BlockSpec((tq, LANES), lambda b, hp, i: (b * nq + i, gate_col // LANES + hp)),
        ],
        out_specs=pl.BlockSpec((tq, LANES), lambda b, hp, i: (b * nq + i, hp)),
        out_shape=jax.ShapeDtypeStruct((n, W_BR), BF16),
        scratch_shapes=[
            pltpu.VMEM((2, 1, tq), F32),
            pltpu.VMEM((2, HD + ONES_ROWS, tq), F32),
            pltpu.VMEM((2, tk, tq), F32),
            pltpu.VMEM((2, tk, tq), F32),
            pltpu.VMEM((2, 1, tq), F32),
            pltpu.VMEM((2, 1, tq), F32),
        ],
        compiler_params=_cparams(("parallel", "parallel", "arbitrary")),
        name="attention",
    )(q, k, vt, pw)


def _rwkv_prep_kernel(*refs, first_layer):
    if first_layer:
        (sr_ref, sk_ref, sv_ref, swa_ref, gc_ref, mur_ref, muk_ref, muv_ref, muwa_ref, w0_ref, a0_ref, wup_ref,
         aup_ref, kk_ref, ka_ref, rk_ref, bt_ref, bo_ref,
         at_o, rt_o, bt_o, kt_o, v_o, bh_o, kh_o, wc_o, bonus_o, sg_o, vfirst_o,
         cr_ref, ck_ref, cv_ref, cwa_ref) = refs
    else:
        (sr_ref, sk_ref, sv_ref, swa_ref, gc_ref, vf_ref, mur_ref, muk_ref, muv_ref, muwa_ref, w0_ref, a0_ref,
         wup_ref, aup_ref, kk_ref, ka_ref, rk_ref, v0_ref, vdown_ref, vup_ref, bt_ref, bo_ref,
         at_o, rt_o, bt_o, kt_o, v_o, bh_o, kh_o, wc_o, bonus_o, sg_o,
         cr_ref, ck_ref, cv_ref, cwa_ref) = refs
    t = sr_ref.shape[0]

    def shift(x_ref, c_ref, mu_ref):
        x = x_ref[...].astype(F32)
        rowid = lax.broadcasted_iota(jnp.int32, x.shape, 0)
        prev = jnp.where(rowid == 0, c_ref[...], pltpu.roll(x, 1, 0))
        c_ref[...] = x[t - 1:t, :]
        return x + mu_ref[...] * (prev - x)

    r = shift(sr_ref, cr_ref, mur_ref)
    k = shift(sk_ref, ck_ref, muk_ref)
    v = shift(sv_ref, cv_ref, muv_ref)
    wa = shift(swa_ref, cwa_ref, muwa_ref)

    logw = -(RWKV_DECAY_SCALE * LOG2E) * _sigmoid(w0_ref[...] + _dot(jnp.tanh(wa).astype(BF16), wup_ref[...]))
    a = _sigmoid(a0_ref[...] + _dot(wa.astype(BF16), aup_ref[...]))
    if first_layer:
        vfirst_o[...] = v
    else:
        low = _dot(v.astype(BF16), vdown_ref[...]).astype(BF16)
        nu = _sigmoid(v0_ref[...] + _dot(low, vup_ref[...]))
        v = v + (vf_ref[...] - v) * nu

    kk = k * kk_ref[...]
    k_mod = k * (1.0 + (a - 1.0) * ka_ref[...])
    rk = r * k_mod * rk_ref[...]
    kk_n, bonus = [], []
    for p in range(H // 2):
        sl = slice(p * LANES, (p + 1) * LANES)
        nrm = jnp.sqrt(_pair_seg_sum(kk[:, sl] * kk[:, sl]))
        kk_n.append(kk[:, sl] / jnp.maximum(nrm, 1e-12))
        bonus.append(_pair_seg_sum(rk[:, sl]) * v[:, sl])
    kk = jnp.concatenate(kk_n, axis=-1)
    bonus = jnp.concatenate(bonus, axis=-1)
    a_vec = -kk
    b_vec = kk * a

    terms = _split3(logw)
    lcum = sum(_dot(bt_ref[...], x) for x in terms)
    ltot = sum(_dot(bo_ref[...], x) for x in terms)
    e_minus = jnp.exp2(-lcum)
    e_rem = jnp.exp2(ltot - lcum)
    g = gc_ref[...].astype(F32)
    for o_ref, val in ((at_o, a_vec * jnp.exp2(lcum - logw)), (rt_o, r * jnp.exp2(lcum)), (bt_o, b_vec * e_minus),
                       (kt_o, k_mod * e_minus), (v_o, v), (bh_o, b_vec * e_rem), (kh_o, k_mod * e_rem),
                       (wc_o, jnp.exp2(ltot)), (bonus_o, bonus), (sg_o, g * _sigmoid(g))):
        o_ref[...] = val.astype(o_ref.dtype)


def _mm(a, b, dims):
    return lax.dot_general(a.astype(BF16), b.astype(BF16), dims, preferred_element_type=F32)


_NN = (((1,), (0,)), ((), ()))
_NT = (((1,), (1,)), ((), ()))
_TN = (((0,), (0,)), ((), ()))


def _rwkv_chunk_kernel(at_ref, rt_ref, bt_ref, kt_ref, v_ref, bh_ref, kh_ref, wc_ref, bonus_ref, sg_ref,
                       lg_ref, lb_ref, o_ref, s_all_ref, *, nchunk):
    mm = _mm
    first = lax.broadcasted_iota(jnp.int32, (CHUNK, LANES), 1) < HD
    r2 = lax.broadcasted_iota(jnp.int32, (LANES, LANES), 0)
    c2 = lax.broadcasted_iota(jnp.int32, (LANES, LANES), 1)
    same_head = (r2 < HD) == (c2 < HD)
    eye2 = r2 == c2
    t_row, t_col = r2 % CHUNK, c2 % CHUNK
    strict2 = jnp.logical_and(same_head, t_col < t_row)
    incl2 = jnp.logical_and(same_head, t_col <= t_row)
    each = lambda f, *cols: [f(*xs) for xs in zip(*cols)]
    pairs = [(c, p) for c in range(nchunk) for p in range(H // 2)]
    npair = range(len(pairs))
    ld = lambda ref: [ref[c * CHUNK:(c + 1) * CHUNK, p * LANES:(p + 1) * LANES] for c, p in pairs]
    at, rt, bt, kt, v, bh, kh = ld(at_ref), ld(rt_ref), ld(bt_ref), ld(kt_ref), ld(v_ref), ld(bh_ref), ld(kh_ref)
    zero = jnp.zeros((CHUNK, LANES), BF16)
    stack = lambda x: jnp.concatenate([jnp.where(first, x, zero), jnp.where(first, zero, x)], axis=0)
    fold = lambda x: x[:CHUNK] + x[CHUNK:]
    a_s, b_s, k_s, v_s = each(stack, at), each(stack, bt), each(stack, kt), each(stack, v)
    x1 = each(lambda a, r: jnp.concatenate([a, stack(r.astype(BF16))], axis=0), a_s, rt)
    gb = each(lambda x, b: mm(x, b, _NT), x1, b_s)
    gk = each(lambda x, k: mm(x, k, _NT), x1, k_s)
    a_ab = [jnp.where(strict2, g[:LANES], 0.0) for g in gb]
    a_rb = [jnp.where(incl2, g[LANES:], 0.0) for g in gb]
    a_ak = [jnp.where(strict2, g[:LANES], 0.0) for g in gk]
    a_rk = [jnp.where(incl2, g[LANES:], 0.0) for g in gk]
    tinv = [jnp.where(eye2, 1.0, a) for a in a_ab]
    pw = a_ab
    for _ in range(int(math.log2(CHUNK)) - 1):
        pw = each(lambda p: mm(p, p, _NN), pw)
        tinv = each(lambda t, p: t + mm(t, p, _NN), tinv, pw)
    av = each(lambda a, x: mm(a, x, _NN), a_ak, v_s)
    tx = each(lambda t, x, a: mm(t, jnp.concatenate([x.astype(BF16), a], axis=1), _NN), tinv, av, a_s)
    ry = each(lambda a, x: mm(a, x, _NN), a_rb, tx)
    yk = each(lambda a, x: mm(a, x, _NN), a_rk, v_s)
    u = [fold(t[:, :LANES]) for t in tx]
    a_til = [fold(t[:, LANES:]) for t in tx]
    y_in = [fold(r[:, :LANES] + k) for r, k in zip(ry, yk)]
    r_hat = [r + fold(x[:, LANES:]) for r, x in zip(rt, ry)]
    m_new = [jnp.where(same_head, mm(a_til[i], bh[i], _TN), 0.0) for i in npair]
    n_new = [jnp.where(same_head, mm(jnp.concatenate([u[i].astype(BF16), v[i]], axis=0),
                                     jnp.concatenate([bh[i], kh[i]], axis=0), _TN), 0.0) for i in npair]
    lg, lb = lg_ref[...], lb_ref[...]
    for c in range(nchunk):
        idx = [i for i, (ci, _) in enumerate(pairs) if ci == c]
        s0 = [s_all_ref[p] for p in range(H // 2)]
        y = [y_in[i] + mm(r_hat[i], s0[p], _NT) for p, i in enumerate(idx)]
        for p, i in enumerate(idx):
            wc = wc_ref[c * CHUNK:c * CHUNK + 1, p * LANES:(p + 1) * LANES]
            s_all_ref[p] = mm(s0[p], jnp.where(eye2, wc, 0.0) + m_new[i], _NN) + n_new[i]
        for p, i in enumerate(idx):
            rows, lanes = slice(c * CHUNK, (c + 1) * CHUNK), slice(p * LANES, (p + 1) * LANES)
            mu = _pair_seg_sum(y[p]) / HD
            d = y[p] - mu
            var = _pair_seg_sum(d * d) / HD
            yn = d * lax.rsqrt(var + GN_EPS) * lg[:, lanes] + lb[:, lanes]
            o_ref[rows, lanes] = ((yn + bonus_ref[rows, lanes]) * sg_ref[rows, lanes]).astype(o_ref.dtype)


_RWKV_STAGE_DTYPES = (BF16, F32, BF16, BF16, BF16, BF16, BF16, F32, F32, F32)


def _rwkv_kernel(*refs, first_layer, nchunk, ns):
    n_in = 18 if first_layer else 22
    ins = refs[:n_in]
    lg_ref, lb_ref, o_ref = refs[n_in:n_in + 3]
    pos = n_in + 3
    vfirst = refs[pos:pos + 1] if first_layer else ()
    pos += len(vfirst)
    stage_a, stage_b = refs[pos:pos + 10], refs[pos + 10:pos + 20]
    carries = refs[pos + 20:pos + 24]
    s_all_ref = refs[pos + 24]
    i = pl.program_id(1)

    @pl.when(i == 0)
    def _():
        for r in stage_b + carries + (s_all_ref,):
            r[...] = jnp.zeros(r.shape, r.dtype)

    def run(done, todo):
        _rwkv_prep_kernel(*ins, *todo, *vfirst, *carries, first_layer=first_layer)
        _rwkv_chunk_kernel(*done, lg_ref, lb_ref, o_ref, s_all_ref, nchunk=nchunk)

    @pl.when(i % 2 == 0)
    def _():
        run(stage_b, stage_a)

    @pl.when(i % 2 == 1)
    def _():
        run(stage_a, stage_b)


def _rwkv(pw, pn, vfirst, wts, lnx_g, lnx_b, batch, seq, t, first_layer):
    n = pw.shape[0]
    ns = seq // t
    const2 = lambda b, i: (0, 0)
    cur = lambda b, i: b * ns + jnp.minimum(i, ns - 1)
    pcol = lambda col, w: pl.BlockSpec((t, w), lambda b, i: (cur(b, i), col // w))
    vec = pl.BlockSpec((1, W_BR), const2)
    in_specs = [pcol(COL_SR, W_BR), pcol(COL_SK, W_BR), pcol(COL_SV, W_BR), pcol(COL_WA, LANES),
                pcol(COL_GATE_C, W_BR)]
    args = [pw, pw, pw, pn, pw]
    if not first_layer:
        in_specs.append(pl.BlockSpec((t, W_BR), lambda b, i: (cur(b, i), 0)))
        args.append(vfirst)
    in_specs += [vec, vec, vec, pl.BlockSpec((1, LANES), const2), vec, vec,
                 pl.BlockSpec((LANES, W_BR), const2), pl.BlockSpec((LANES, W_BR), const2), vec, vec, vec]
    if not first_layer:
        in_specs += [vec, pl.BlockSpec((W_BR, LANES), const2), pl.BlockSpec((LANES, W_BR), const2)]
    in_specs += [pl.BlockSpec((t, t), const2)] * 2 + [vec, vec]
    args += list(wts) + [lnx_g, lnx_b]
    out_specs = [pl.BlockSpec((t, W_BR), lambda b, i: (b * ns + jnp.maximum(i - 1, 0), 0))]
    out_shape = [jax.ShapeDtypeStruct((n, W_BR), BF16)]
    if first_layer:
        out_specs.append(pl.BlockSpec((t, W_BR), lambda b, i: (jnp.where(i < ns, b * ns + i, batch * ns + b), 0)))
        out_shape.append(jax.ShapeDtypeStruct((n + batch * t, W_BR), F32))
    return pl.pallas_call(
        functools.partial(_rwkv_kernel, first_layer=first_layer, nchunk=t // CHUNK, ns=ns),
        grid=(batch, ns + 1),
        in_specs=in_specs,
        out_specs=out_specs,
        out_shape=out_shape,
        scratch_shapes=[pltpu.VMEM((t, W_BR), dt) for dt in _RWKV_STAGE_DTYPES * 2]
        + [pltpu.VMEM((1, W_BR), F32), pltpu.VMEM((1, W_BR), F32), pltpu.VMEM((1, W_BR), F32),
           pltpu.VMEM((1, LANES), F32), pltpu.VMEM((H // 2, LANES, LANES), F32)],
        compiler_params=_cparams(("parallel", "arbitrary")),
        name="rwkv",
    )(*args)


def _merge_kernel(oa_ref, ob_ref, oc_ref, ga_ref, gb_ref, gc_ref, x_ref, wpa_ref, wpb_ref, wpc_ref, wout_ref,
                  o_ref):
    pa = _dot(oa_ref[...], wpa_ref[...])
    pb = _dot(ob_ref[...], wpb_ref[...])
    pc = _dot(oc_ref[...], wpc_ref[...])
    sig = lambda ref: _sigmoid(ref[...].astype(F32))
    merged = sig(ga_ref) * pa + sig(gb_ref) * pb + sig(gc_ref) * pc
    o_ref[...] = x_ref[...] + _dot(merged.astype(BF16), wout_ref[...])


def _merge(oa, ob, oc, p, x2, wpa, wpb, wpc, wout, batch, seq, t):
    n = x2.shape[0]
    ns = seq // t
    row = lambda b, i: (b * ns + i, 0)
    const2 = lambda b, i: (0, 0)
    gate = lambda g: pl.BlockSpec((t, D_MODEL), lambda b, i: (b * ns + i, COL_MERGE // D_MODEL + g))
    return pl.pallas_call(
        _merge_kernel,
        grid=(batch, ns),
        in_specs=[
            pl.BlockSpec((t, W_BR), row),
            pl.BlockSpec((t, W_BR), row),
            pl.BlockSpec((t, W_BR), row),
            gate(0), gate(1), gate(2),
            pl.BlockSpec((t, D_MODEL), row),
            pl.BlockSpec((W_BR, D_MODEL), const2),
            pl.BlockSpec((W_BR, D_MODEL), const2),
            pl.BlockSpec((W_BR, D_MODEL), const2),
            pl.BlockSpec((D_MODEL, D_MODEL), const2),
        ],
        out_specs=pl.BlockSpec((t, D_MODEL), row),
        out_shape=jax.ShapeDtypeStruct((n, D_MODEL), F32),
        compiler_params=_cparams(("parallel", "parallel")),
        name="merge",
    )(oa, ob, oc, p, p, p, x2, wpa, wpb, wpc, wout)


def _regroup_w_in(w_in, l):
    d = w_in.shape[1]
    sizes = (Q_LORA, KV_LORA, ROPE_D, W_BR, W_BR, W_BR, W_BR, H, W_BR,
             3 * W_BR + DECAY_LORA + AAA_LORA, W_BR, 3 * D_MODEL)
    offs = [int(o) for o in np.concatenate([[0], np.cumsum(sizes)])]
    (c_q, c_kv, k_r, gate_a, fq, fk, fv, ff, gate_b, shift, gate_c, merge) = [
        w_in[l, :, offs[i]:offs[i + 1]].astype(BF16) for i in range(len(sizes))]
    z = lambda n: jnp.zeros((d, n), BF16)
    wide = jnp.concatenate([
        merge, gate_a, fq, fk, fv, gate_b,
        shift[:, :W_BR], shift[:, W_BR:2 * W_BR], shift[:, 2 * W_BR:3 * W_BR], gate_c,
    ], axis=1)
    narrow = jnp.concatenate([
        c_q, c_kv,
        z(NOPE), k_r, k_r[:, :ROPE_D // 2], z(LANES - NOPE - ROPE_D - ROPE_D // 2),
        ff, z(LANES - H),
        shift[:, 3 * W_BR:],
    ], axis=1)
    return wide, narrow


def _rope_tables(seq):
    inv =ROPE_THETA ** (-jnp.arange(0, ROPE_D, 2, dtype=F32) / ROPE_D)
    ang = jnp.arange(seq, dtype=F32)[:, None] * inv[None, :]
    cos, sin = jnp.cos(ang), jnp.sin(ang)
    z = lambda n: jnp.zeros((seq, n), F32)
    c = jnp.concatenate([jnp.ones((seq, NOPE), F32), cos, cos, z(LANES - NOPE - ROPE_D)], axis=1)
    s = jnp.concatenate([z(NOPE), -sin, sin, z(LANES - NOPE - ROPE_D)], axis=1)
    return c, s


def _pad_lanes(v, lo, total=LANES):
    return jnp.zeros((1, total), F32).at[0, lo:lo + v.shape[0]].set(v)


def _fox_selectors():
    selq = np.zeros((3, LANES, H * LANES), np.float32)
    selk = np.zeros((3, LANES, H * LANES), np.float32)
    oneq = np.zeros((1, H * LANES), np.float32)
    onek = np.zeros((1, H * LANES), np.float32)
    for h in range(H):
        for j in range(3):
            selq[j, h, h * LANES + HD + j] = 1.0
            selk[j, h, h * LANES + HD + 3 + j] = -1.0
            oneq[0, h * LANES + HD + 3 + j] = 1.0
            onek[0, h * LANES + HD + j] = 1.0
    return (jnp.asarray(selq, BF16), jnp.asarray(selk, BF16), jnp.asarray(oneq), jnp.asarray(onek))


def _tri(t):
    r = np.arange(t)
    return jnp.asarray((r[None, :] <= r[:, None]).astype(np.float32), BF16)


def _chunk_tri(t):
    r = np.arange(t)
    same = (r[None, :] // CHUNK) == (r[:, None] // CHUNK)
    lower = r[None, :] <= r[:, None]
    return (jnp.asarray((same & lower).astype(np.float32), BF16), jnp.asarray(same.astype(np.float32), BF16))


def _tile(seq, pref):
    return pref if seq % pref == 0 else seq


def kernel(x, norm_g, w_in, mla_qa_g, mla_w_uq, mla_kva_g, mla_w_ukv, mla_q_g, mla_knope_g, mla_krope_g, fox_b_f, fox_q_g, fox_k_g, rwkv_mu, rwkv_w0, rwkv_w_up, rwkv_a0, rwkv_a_up, rwkv_k_k, rwkv_k_a, rwkv_r_k, rwkv_lnx_g, rwkv_lnx_b, rwkv_v0, rwkv_v_down, rwkv_v_up, w_pa, w_pb, w_pc, w_out):
    batch, seq, d = x.shape
    depth = w_in.shape[0]
    n = batch * seq
    t_row = _tile(seq, ROW_TILE)
    t_in = _tile(n, INPROJ_TM)
    t_rwkv = _tile(seq, RWKV_TILE)
    assert seq % (ATTN_QT * t_row) == 0 and seq % t_rwkv == 0 and t_rwkv % CHUNK == 0, seq
    x2 = x.reshape(n, d)

    tabs = _rope_tables(seq)
    selq, selk, oneq, onek = _fox_selectors()
    tri = _tri(t_row)
    btri, bones = _chunk_tri(t_rwkv)
    row = lambda v: v.reshape(1, -1).astype(F32)

    half = ROPE_D // 2
    spare = LANES - NOPE - ROPE_D - half
    with_copy = lambda g, lo: jnp.concatenate([g, g[lo:lo + half]])
    vfirst = None
    for l in range(depth):
        w_wide, w_narrow = _regroup_w_in(w_in, l)
        pw, pn = _inproj(x2, row(norm_g[l]), w_wide, w_narrow, t_in, INPROJ_TN)

        wuq = mla_w_uq[l].reshape(Q_LORA, H, NOPE + ROPE_D)
        wuq = jnp.concatenate([wuq, wuq[:, :, NOPE:NOPE + half], jnp.zeros((Q_LORA, H, spare), F32)], axis=2)
        wuq = wuq.reshape(Q_LORA, H * LANES).astype(BF16)
        wukv = mla_w_ukv[l].reshape(KV_LORA, H, NOPE + HD)
        wk = jnp.pad(wukv[:, :, :NOPE], ((0, 0), (0, 0), (0, LANES - NOPE))).reshape(KV_LORA, H * LANES).astype(BF16)
        wv = wukv[:, :, NOPE:].reshape(KV_LORA, W_BR).T.astype(BF16)
        mla_scale = float(NOPE + ROPE_D) ** -0.5 * LOG2E
        mla_wts = (row(mla_qa_g[l]), wuq, row(mla_kva_g[l]), wk, wv,
                   _pad_lanes(with_copy(mla_q_g[l], NOPE) * mla_scale, 0), _pad_lanes(mla_knope_g[l], 0),
                   _pad_lanes(with_copy(mla_krope_g[l], 0), NOPE))
        qa, ka, va = _mla_prep(pn, tabs, mla_wts, seq, t_row)
        o_a = _attention(qa, ka, va, pw, COL_GATE_A, batch, seq, t_row)

        fox_wts = (_pad_lanes(fox_b_f[l], 0), row(jnp.tile(fox_q_g[l] * (float(HD) ** -0.5 * LOG2E), H)),
                   row(jnp.tile(fox_k_g[l], H)), tri, selq, selk, oneq, onek)
        qb, kb, vb = _fox_prep(pw, pn, fox_wts, batch, seq, t_row)
        o_b = _attention(qb, kb, vb, pw, COL_GATE_B, batch, seq, t_row)

        mu = rwkv_mu[l]
        wup = jnp.zeros((LANES, W_BR), F32).at[:DECAY_LORA].set(rwkv_w_up[l]).astype(BF16)
        aup = jnp.zeros((LANES, W_BR), F32).at[DECAY_LORA:].set(rwkv_a_up[l]).astype(BF16)
        rw = [row(mu[:W_BR]), row(mu[W_BR:2 * W_BR]), row(mu[2 * W_BR:3 * W_BR]), row(mu[3 * W_BR:]),
              row(rwkv_w0[l]), row(rwkv_a0[l]), wup, aup, row(rwkv_k_k[l]), row(rwkv_k_a[l]),
              row(rwkv_r_k[l])]
        if l > 0:
            vdown = jnp.zeros((W_BR, LANES), F32).at[:, :MV_LORA].set(rwkv_v_down[l - 1]).astype(BF16)
            vup = jnp.zeros((LANES, W_BR), F32).at[:MV_LORA].set(rwkv_v_up[l - 1]).astype(BF16)
            rw += [row(rwkv_v0[l - 1]), vdown, vup]
        rw += [btri, bones]
        outs = _rwkv(pw, pn, vfirst, rw, row(rwkv_lnx_g[l]), row(rwkv_lnx_b[l]), batch, seq, t_rwkv,
                     first_layer=(l == 0))
        o_c = outs[0]
        if l == 0:
            vfirst = outs[1]

        x2 = _merge(o_a, o_b, o_c, pw, x2, w_pa[l].astype(BF16), w_pb[l].astype(BF16),
                    w_pc[l].astype(BF16), w_out[l].astype(BF16), batch, seq, _tile(seq, INPROJ_TM))
    return x2.reshape(batch, seq, d)
```

```python
import functools
import math

import jax
import jax.numpy as jnp
import numpy as np
from jax import lax
from jax.experimental import pallas as pl
from jax.experimental.pallas import tpu as pltpu

F32 = jnp.float32
BF16 = jnp.bfloat16

LANES = 128
H = 8
HD = 64
NOPE = 64
ROPE_D = 32
Q_LORA = 256
KV_LORA = 128
D_MODEL = 1024
W_BR = H * HD
DECAY_LORA = 64
AAA_LORA = 64
MV_LORA = 32
ROPE_THETA = 10000.0
RWKV_DECAY_SCALE = 0.606531
GN_EPS = 64e-5
EPS = 1e-6
NEG_INF = -1e30
LOG2E = math.log2(math.e)
CHUNK = 64
ROW_TILE = 512
INPROJ_TM = 1024
RWKV_TILE = 256
ATTN_QT = 4
ONES_ROWS = 16

COL_MERGE = 0
COL_GATE_A = 3072
COL_FQ = 3584
COL_FK = 4096
COL_FV = 4608
COL_GATE_B = 5120
COL_SR = 5632
COL_SK = 6144
COL_SV = 6656
COL_GATE_C = 7168
NCOLS_WIDE = 7680
INPROJ_TN = 2560
COL_CQ = 0
COL_CKV = 256
COL_KR = 384
COL_FF = 512
COL_WA = 640
NCOLS_NARROW = 768

VMEM_LIMIT = 56 * 1024 * 1024


def _cparams(sem):
    return pltpu.CompilerParams(dimension_semantics=sem, vmem_limit_bytes=VMEM_LIMIT)


def _sigmoid(x):
    return 0.5 * jnp.tanh(0.5 * x) + 0.5


def _dot(a, b):
    return jnp.dot(a, b, preferred_element_type=F32)


def _split3(x):
    hi = x.astype(BF16)
    r1 = x - hi.astype(F32)
    mid = r1.astype(BF16)
    lo = (r1 - mid.astype(F32)).astype(BF16)
    return hi, mid, lo


def _dot_exact_lhs(m_bf16, x):
    hi, mid, lo = _split3(x)
    return _dot(m_bf16, hi) + _dot(m_bf16, mid) + _dot(m_bf16, lo)


def _pair_seg_sum(x):
    lane = lax.broadcasted_iota(jnp.int32, x.shape, 1)
    first = lane < HD
    s0 = jnp.sum(jnp.where(first, x, 0.0), axis=-1, keepdims=True)
    s1 = jnp.sum(jnp.where(first, 0.0, x), axis=-1, keepdims=True)
    return jnp.where(first, s0, s1)


def _inproj_kernel(x_ref, g_ref, w_ref, wn_ref, o_ref, on_ref, h_ref):
    @pl.when(pl.program_id(1) == 0)
    def _():
        x = x_ref[...]
        ms = jnp.mean(x * x, axis=-1, keepdims=True)
        h = (x * lax.rsqrt(ms + EPS) * g_ref[...]).astype(BF16)
        h_ref[...] = h
        on_ref[...] = _dot(h, wn_ref[...])

    o_ref[...] = _dot(h_ref[...], w_ref[...]).astype(o_ref.dtype)


def _inproj(x2, g, w_wide, w_narrow, tm, tn):
    n = x2.shape[0]
    return pl.pallas_call(
        _inproj_kernel,
        grid=(n // tm, NCOLS_WIDE // tn),
        in_specs=[
            pl.BlockSpec((tm, D_MODEL), lambda i, j: (i, 0)),
            pl.BlockSpec((1, D_MODEL), lambda i, j: (0, 0)),
            pl.BlockSpec((D_MODEL, tn), lambda i, j: (0, j)),
            pl.BlockSpec((D_MODEL, NCOLS_NARROW), lambda i, j: (0, 0)),
        ],
        out_specs=[pl.BlockSpec((tm, tn), lambda i, j: (i, j)),
                   pl.BlockSpec((tm, NCOLS_NARROW), lambda i, j: (i, 0))],
        out_shape=[jax.ShapeDtypeStruct((n, NCOLS_WIDE), BF16), jax.ShapeDtypeStruct((n, NCOLS_NARROW), F32)],
        scratch_shapes=[pltpu.VMEM((tm, D_MODEL), BF16)],
        compiler_params=_cparams(("parallel", "arbitrary")),
        name="inproj",
    )(x2, g, w_wide, w_narrow)


def _rms(x, g, n, keep=None):
    sq = x * x
    if keep is not None:
        sq = jnp.where(keep, sq, 0.0)
    ms = jnp.sum(sq, axis=-1, keepdims=True) / n
    return x * lax.rsqrt(ms + EPS) * g


def _rope(x, c, s):
    return x * c + pltpu.roll(x, LANES - ROPE_D // 2, 1) * s


def _mla_prep_kernel(cq_ref, ckv_ref, kr_ref, c_ref, s_ref, qag_ref, wuq_ref, kvag_ref, wk_ref, wv_ref,
                     qg_ref, kng_ref, krg_ref, q_out, k_out, vt_out):
    c, s = c_ref[...], s_ref[...]
    keep = lax.broadcasted_iota(jnp.int32, c.shape, 1) < NOPE + ROPE_D
    cqn = _rms(cq_ref[...], qag_ref[...], Q_LORA).astype(BF16)
    ckvn = _rms(ckv_ref[...], kvag_ref[...], KV_LORA).astype(BF16)
    q_all = _dot(cqn, wuq_ref[...])
    k_all = _dot(ckvn, wk_ref[...])
    vt_out[0, 0] = lax.dot_general(wv_ref[...], ckvn, _NT, preferred_element_type=F32).astype(BF16)
    k_rope = _rope(_rms(kr_ref[...], krg_ref[...], ROPE_D, keep), c, s)
    qg, kng = qg_ref[...], kng_ref[...]
    for h in range(H):
        sl = slice(h * LANES, (h + 1) * LANES)
        q_out[:, sl] = _rope(_rms(q_all[:, sl], qg, NOPE + ROPE_D, keep), c, s).astype(BF16)
        k_out[:, sl] = (_rms(k_all[:, sl], kng, NOPE) + k_rope).astype(BF16)


def _mla_prep(pn, tabs, wts, seq, t):
    n = pn.shape[0]
    ns = seq // t
    row = lambda i: (i, 0)
    const = lambda i: (0, 0)
    tab_spec = pl.BlockSpec((t, LANES), lambda i: (i % ns, 0))
    return pl.pallas_call(
        _mla_prep_kernel,
        grid=(n // t,),
        in_specs=[
            pl.BlockSpec((t, Q_LORA), lambda i: (i, COL_CQ // Q_LORA)),
            pl.BlockSpec((t, LANES), lambda i: (i, COL_CKV // LANES)),
            pl.BlockSpec((t, LANES), lambda i: (i, COL_KR // LANES)),
            tab_spec, tab_spec,
            pl.BlockSpec((1, Q_LORA), const),
            pl.BlockSpec((Q_LORA, H * LANES), const),
            pl.BlockSpec((1, KV_LORA), const),
            pl.BlockSpec((KV_LORA, H * LANES), const),
            pl.BlockSpec((W_BR, KV_LORA), const),
            pl.BlockSpec((1, LANES), const),
            pl.BlockSpec((1, LANES), const),
            pl.BlockSpec((1, LANES), const),
        ],
        out_specs=[
            pl.BlockSpec((t, H * LANES), row),
            pl.BlockSpec((t, H * LANES), row),
            pl.BlockSpec((1, 1, W_BR, t), lambda i: (i // ns, i % ns, 0, 0)),
        ],
        out_shape=[
            jax.ShapeDtypeStruct((n, H * LANES), BF16),
            jax.ShapeDtypeStruct((n, H * LANES), BF16),
            jax.ShapeDtypeStruct((n // seq, ns, W_BR, t), BF16),
        ],
        compiler_params=_cparams(("parallel",)),
        name="mla_prep",
    )(pn, pn, pn, *tabs, *wts)


def _fox_prep_kernel(fq_ref, fk_ref, fv_ref, ff_ref, bf_ref, qg_ref, kg_ref, tri_ref, selq_ref, selk_ref,
                     oneq_ref, onek_ref, q_out, k_out, vt_out, carry_ref):
    t = fq_ref.shape[0]

    @pl.when(pl.program_id(1) == 0)
    def _():
        carry_ref[...] = jnp.zeros_like(carry_ref)

    z = ff_ref[...] + bf_ref[...]
    logf = jnp.minimum(z, 0.0) - jnp.log1p(jnp.exp(-jnp.abs(z)))
    cum = carry_ref[...] + _dot_exact_lhs(tri_ref[...], logf)
    carry_ref[...] = cum[t - 1:t, :]
    hi, mid, lo = _split3(cum * LOG2E)
    terms = jnp.concatenate([hi, mid, lo], axis=1)
    aug_q = _dot(terms, selq_ref[...]) + oneq_ref[...]
    aug_k = _dot(terms, selk_ref[...]) + onek_ref[...]
    vt_out[0, 0] = fv_ref[...].astype(F32).T.astype(BF16)
    lane = lax.broadcasted_iota(jnp.int32, (t, LANES), 1)
    first = lane < HD
    for src_ref, g_ref, aug, dst in ((fq_ref, qg_ref, aug_q, q_out), (fk_ref, kg_ref, aug_k, k_out)):
        for p in range(H // 2):
            sl = slice(p * LANES, (p + 1) * LANES)
            x = src_ref[:, sl].astype(F32)
            ms = _pair_seg_sum(x * x) / HD
            xn = x * lax.rsqrt(ms + EPS) * g_ref[:, sl]
            he, ho = 2 * p, 2 * p + 1
            dst[:, he * LANES:(he + 1) * LANES] = jnp.where(first, xn, aug[:, he * LANES:(he + 1) * LANES]).astype(BF16)
            dst[:, ho * LANES:(ho + 1) * LANES] = jnp.where(
                first, pltpu.roll(xn, HD, 1), aug[:, ho * LANES:(ho + 1) * LANES]).astype(BF16)


def _fox_prep(pw, pn, wts, batch, seq, t):
    n = pw.shape[0]
    ns = seq // t
    row = lambda b, i: (b * ns + i, 0)
    const2 = lambda b, i: (0, 0)
    const3 = lambda b, i: (0, 0, 0)
    return pl.pallas_call(
        _fox_prep_kernel,
        grid=(batch, ns),
        in_specs=[
            pl.BlockSpec((t, W_BR), lambda b, i: (b * ns + i, COL_FQ // W_BR)),
            pl.BlockSpec((t, W_BR), lambda b, i: (b * ns + i, COL_FK // W_BR)),
            pl.BlockSpec((t, W_BR), lambda b, i: (b * ns + i, COL_FV // W_BR)),
            pl.BlockSpec((t, LANES), lambda b, i: (b * ns + i, COL_FF // LANES)),
            pl.BlockSpec((1, LANES), const2),
            pl.BlockSpec((1, W_BR), const2),
            pl.BlockSpec((1, W_BR), const2),
            pl.BlockSpec((t, t), const2),
            pl.BlockSpec((3 * LANES, H * LANES), const2),
            pl.BlockSpec((3 * LANES, H * LANES), const2),
            pl.BlockSpec((1, H * LANES), const2),
            pl.BlockSpec((1, H * LANES), const2),
        ],
        out_specs=[
            pl.BlockSpec((t, H * LANES), row),
            pl.BlockSpec((t, H * LANES), row),
            pl.BlockSpec((1, 1, W_BR, t), lambda b, i: (b, i, 0, 0)),
        ],
        out_shape=[
            jax.ShapeDtypeStruct((n, H * LANES), BF16),
            jax.ShapeDtypeStruct((n, H * LANES), BF16),
            jax.ShapeDtypeStruct((batch, ns, W_BR, t), BF16),
        ],
        scratch_shapes=[pltpu.VMEM((1, LANES), F32)],
        compiler_params=_cparams(("parallel", "arbitrary")),
        name="fox_prep",
    )(pw, pw, pw, pn, *wts)


def _attn_kernel(q_ref, k_ref, vt_ref, g_ref, o_ref, m_ref, acc_ref, sa_ref, sb_ref, mxa_ref, mxb_ref, *, tk):
    tq = ATTN_QT * tk
    gi = pl.program_id(2)
    nt = (((1,), (1,)), ((), ()))
    kv_pos = lax.broadcasted_iota(jnp.int32, (tk, tq), 0)
    q_pos = lax.broadcasted_iota(jnp.int32, (tk, tq), 1)
    causal = kv_pos <= q_pos
    both = slice(0, tq)
    ones = jnp.ones((ONES_ROWS, tk), BF16)
    m_ref[...] = jnp.full(m_ref.shape, NEG_INF, F32)
    acc_ref[...] = jnp.zeros(acc_ref.shape, F32)

    def scores(c, s_ref, mx_ref, cols):
        start = pl.multiple_of(c * tk, tk)
        for j in range(2):
            q = q_ref[cols, j * LANES:(j + 1) * LANES]
            k = k_ref[pl.ds(start, tk), j * LANES:(j + 1) * LANES]
            s = lax.dot_general(k, q, nt, preferred_element_type=F32)
            s_ref[j, :, cols] = s
            mx_ref[j, :, cols] = jnp.max(s, axis=0, keepdims=True)

    def consume(c, s_ref, mx_ref, cols, mask):
        for j in range(2):
            vt = jnp.concatenate([vt_ref[0, c, j * HD:(j + 1) * HD, :], ones], axis=0)
            s = s_ref[j, :, cols]
            if mask is not None:
                s = jnp.where(mask, s, NEG_INF)
                m_cur = jnp.max(s, axis=0, keepdims=True)
            else:
                m_cur = mx_ref[j, :, cols]
            m_old = m_ref[j, :, cols]
            m_new = jnp.maximum(m_old, m_cur)
            alpha = jnp.exp2(m_old - m_new)
            p = jnp.exp2(s - m_new)
            acc_ref[j, :, cols] = alpha * acc_ref[j, :, cols] + _dot(vt, p.astype(BF16))
            m_ref[j, :, cols] = m_new

    scores(0, sa_ref, mxa_ref, both)

    def body(i, carry):
        c = 2 * i
        scores(c + 1, sb_ref, mxb_ref, both)
        consume(c, sa_ref, mxa_ref, both, None)
        scores(c + 2, sa_ref, mxa_ref, both)
        consume(c + 1, sb_ref, mxb_ref, both, None)
        return carry

    lax.fori_loop(0, gi * (ATTN_QT // 2), body, 0)
    bufs = ((sa_ref, mxa_ref), (sb_ref, mxb_ref))
    for d in range(ATTN_QT):
        c = ATTN_QT * gi + d
        if d + 1 < ATTN_QT:
            scores(c + 1, *bufs[(d + 1) % 2], slice((d + 1) * tk, tq))
        consume(c, *bufs[d % 2], slice(d * tk, tq), causal[:, :tq - d * tk])

    o_t = jnp.concatenate([acc_ref[j, :HD, :] / acc_ref[j, HD:HD + 1, :] for j in range(2)], axis=0)
    g = g_ref[...].astype(F32)
    o_ref[...] = (o_t.T * (g * _sigmoid(g))).astype(BF16)


def _attention(q, k, vt, pw, gate_col, batch, seq, tk):
    n = q.shape[0]
    tq = ATTN_QT * tk
    nq, nk = seq // tq, seq // tk
    return pl.pallas_call(
        functools.partial(_attn_kernel, tk=tk),
        grid=(batch, H // 2, nq),
        in_specs=[
            pl.BlockSpec((tq, 2 * LANES), lambda b, hp, i: (b * nq + i, hp)),
            pl.BlockSpec((seq, 2 * LANES), lambda b, hp, i: (b, hp)),
            pl.BlockSpec((1, nk, LANES, tk), lambda b, hp, i: (b, 0, hp, 0)),
            pl.BlockSpec((tq, LANES), lambda b, hp, i: (b * nq + i, gate_col // LANES + hp)),
        ],
        out_specs=pl.BlockSpec((tq, LANES), lambda b, hp, i: (b * nq + i, hp)),
        out_shape=jax.ShapeDtypeStruct((n, W_BR), BF16),
        scratch_shapes=[
            pltpu.VMEM((2, 1, tq), F32),
            pltpu.VMEM((2, HD + ONES_ROWS, tq), F32),
            pltpu.VMEM((2, tk, tq), F32),
            pltpu.VMEM((2, tk, tq), F32),
            pltpu.VMEM((2, 1, tq), F32),
            pltpu.VMEM((2, 1, tq), F32),
        ],
        compiler_params=_cparams(("parallel", "parallel", "arbitrary")),
        name="attention",
    )(q, k, vt, pw)


def _rwkv_prep_kernel(*refs, first_layer):
    if first_layer:
        (sr_ref, sk_ref, sv_ref, swa_ref, gc_ref, mur_ref, muk_ref, muv_ref, muwa_ref, w0_ref, a0_ref, wup_ref,
         aup_ref, kk_ref, ka_ref, rk_ref, bt_ref, bo_ref,
         at_o, rt_o, bt_o, kt_o, v_o, bh_o, kh_o, wc_o, bonus_o, sg_o, vfirst_o,
         cr_ref, ck_ref, cv_ref, cwa_ref) = refs
    else:
        (sr_ref, sk_ref, sv_ref, swa_ref, gc_ref, vf_ref, mur_ref, muk_ref, muv_ref, muwa_ref, w0_ref, a0_ref,
         wup_ref, aup_ref, kk_ref, ka_ref, rk_ref, v0_ref, vdown_ref, vup_ref, bt_ref, bo_ref,
         at_o, rt_o, bt_o, kt_o, v_o, bh_o, kh_o, wc_o, bonus_o, sg_o,
         cr_ref, ck_ref, cv_ref, cwa_ref) = refs
    t = sr_ref.shape[0]

    def shift(x_ref, c_ref, mu_ref):
        x = x_ref[...].astype(F32)
        rowid = lax.broadcasted_iota(jnp.int32, x.shape, 0)
        prev = jnp.where(rowid == 0, c_ref[...], pltpu.roll(x, 1, 0))
        c_ref[...] = x[t - 1:t, :]
        return x + mu_ref[...] * (prev - x)

    r = shift(sr_ref, cr_ref, mur_ref)
    k = shift(sk_ref, ck_ref, muk_ref)
    v = shift(sv_ref, cv_ref, muv_ref)
    wa = shift(swa_ref, cwa_ref, muwa_ref)

    logw = -(RWKV_DECAY_SCALE * LOG2E) * _sigmoid(w0_ref[...] + _dot(jnp.tanh(wa).astype(BF16), wup_ref[...]))
    a = _sigmoid(a0_ref[...] + _dot(wa.astype(BF16), aup_ref[...]))
    if first_layer:
        vfirst_o[...] = v
    else:
        low = _dot(v.astype(BF16), vdown_ref[...]).astype(BF16)
        nu = _sigmoid(v0_ref[...] + _dot(low, vup_ref[...]))
        v = v + (vf_ref[...] - v) * nu

    kk = k * kk_ref[...]
    k_mod = k * (1.0 + (a - 1.0) * ka_ref[...])
    rk = r * k_mod * rk_ref[...]
    kk_n, bonus = [], []
    for p in range(H // 2):
        sl = slice(p * LANES, (p + 1) * LANES)
        nrm = jnp.sqrt(_pair_seg_sum(kk[:, sl] * kk[:, sl]))
        kk_n.append(kk[:, sl] / jnp.maximum(nrm, 1e-12))
        bonus.append(_pair_seg_sum(rk[:, sl]) * v[:, sl])
    kk = jnp.concatenate(kk_n, axis=-1)
    bonus = jnp.concatenate(bonus, axis=-1)
    a_vec = -kk
    b_vec = kk * a

    terms = _split3(logw)
    lcum = sum(_dot(bt_ref[...], x) for x in terms)
    ltot = sum(_dot(bo_ref[...], x) for x in terms)
    e_minus = jnp.exp2(-lcum)
    e_rem = jnp.exp2(ltot - lcum)
    g = gc_ref[...].astype(F32)
    for o_ref, val in ((at_o, a_vec * jnp.exp2(lcum - logw)), (rt_o, r * jnp.exp2(lcum)), (bt_o, b_vec * e_minus),
                       (kt_o, k_mod * e_minus), (v_o, v), (bh_o, b_vec * e_rem), (kh_o, k_mod * e_rem),
                       (wc_o, jnp.exp2(ltot)), (bonus_o, bonus), (sg_o, g * _sigmoid(g))):
        o_ref[...] = val.astype(o_ref.dtype)


def _mm(a, b, dims):
    return lax.dot_general(a.astype(BF16), b.astype(BF16), dims, preferred_element_type=F32)


_NN = (((1,), (0,)), ((), ()))
_NT = (((1,), (1,)), ((), ()))
_TN = (((0,), (0,)), ((), ()))


def _rwkv_chunk_kernel(at_ref, rt_ref, bt_ref, kt_ref, v_ref, bh_ref, kh_ref, wc_ref, bonus_ref, sg_ref,
                       lg_ref, lb_ref, o_ref, s_all_ref, *, nchunk):
    mm = _mm
    first = lax.broadcasted_iota(jnp.int32, (CHUNK, LANES), 1) < HD
    r2 = lax.broadcasted_iota(jnp.int32, (LANES, LANES), 0)
    c2 = lax.broadcasted_iota(jnp.int32, (LANES, LANES), 1)
    same_head = (r2 < HD) == (c2 < HD)
    eye2 = r2 == c2
    t_row, t_col = r2 % CHUNK, c2 % CHUNK
    strict2 = jnp.logical_and(same_head, t_col < t_row)
    incl2 = jnp.logical_and(same_head, t_col <= t_row)
    each = lambda f, *cols: [f(*xs) for xs in zip(*cols)]
    pairs = [(c, p) for c in range(nchunk) for p in range(H // 2)]
    npair = range(len(pairs))
    ld = lambda ref: [ref[c * CHUNK:(c + 1) * CHUNK, p * LANES:(p + 1) * LANES] for c, p in pairs]
    at, rt, bt, kt, v, bh, kh = ld(at_ref), ld(rt_ref), ld(bt_ref), ld(kt_ref), ld(v_ref), ld(bh_ref), ld(kh_ref)
    zero = jnp.zeros((CHUNK, LANES), BF16)
    stack = lambda x: jnp.concatenate([jnp.where(first, x, zero), jnp.where(first, zero, x)], axis=0)
    fold = lambda x: x[:CHUNK] + x[CHUNK:]
    a_s, b_s, k_s, v_s = each(stack, at), each(stack, bt), each(stack, kt), each(stack, v)
    x1 = each(lambda a, r: jnp.concatenate([a, stack(r.astype(BF16))], axis=0), a_s, rt)
    gb = each(lambda x, b: mm(x, b, _NT), x1, b_s)
    gk = each(lambda x, k: mm(x, k, _NT), x1, k_s)
    a_ab = [jnp.where(strict2, g[:LANES], 0.0) for g in gb]
    a_rb = [jnp.where(incl2, g[LANES:], 0.0) for g in gb]
    a_ak = [jnp.where(strict2, g[:LANES], 0.0) for g in gk]
    a_rk = [jnp.where(incl2, g[LANES:], 0.0) for g in gk]
    tinv = [jnp.where(eye2, 1.0, a) for a in a_ab]
    pw = a_ab
    for _ in range(int(math.log2(CHUNK)) - 1):
        pw = each(lambda p: mm(p, p, _NN), pw)
        tinv = each(lambda t, p: t + mm(t, p, _NN), tinv, pw)
    av = each(lambda a, x: mm(a, x, _NN), a_ak, v_s)
    tx = each(lambda t, x, a: mm(t, jnp.concatenate([x.astype(BF16), a], axis=1), _NN), tinv, av, a_s)
    ry = each(lambda a, x: mm(a, x, _NN), a_rb, tx)
    yk = each(lambda a, x: mm(a, x, _NN), a_rk, v_s)
    u = [fold(t[:, :LANES]) for t in tx]
    a_til = [fold(t[:, LANES:]) for t in tx]
    y_in = [fold(r[:, :LANES] + k) for r, k in zip(ry, yk)]
    r_hat = [r + fold(x[:, LANES:]) for r, x in zip(rt, ry)]
    m_new = [jnp.where(same_head, mm(a_til[i], bh[i], _TN), 0.0) for i in npair]
    n_new = [jnp.where(same_head, mm(jnp.concatenate([u[i].astype(BF16), v[i]], axis=0),
                                     jnp.concatenate([bh[i], kh[i]], axis=0), _TN), 0.0) for i in npair]
    lg, lb = lg_ref[...], lb_ref[...]
    for c in range(nchunk):
        idx = [i for i, (ci, _) in enumerate(pairs) if ci == c]
        s0 = [s_all_ref[p] for p in range(H // 2)]
        y = [y_in[i] + mm(r_hat[i], s0[p], _NT) for p, i in enumerate(idx)]
        for p, i in enumerate(idx):
            wc = wc_ref[c * CHUNK:c * CHUNK + 1, p * LANES:(p + 1) * LANES]
            s_all_ref[p] = mm(s0[p], jnp.where(eye2, wc, 0.0) + m_new[i], _NN) + n_new[i]
        for p, i in enumerate(idx):
            rows, lanes = slice(c * CHUNK, (c + 1) * CHUNK), slice(p * LANES, (p + 1) * LANES)
            mu = _pair_seg_sum(y[p]) / HD
            d = y[p] - mu
            var = _pair_seg_sum(d * d) / HD
            yn = d * lax.rsqrt(var + GN_EPS) * lg[:, lanes] + lb[:, lanes]
            o_ref[rows, lanes] = ((yn + bonus_ref[rows, lanes]) * sg_ref[rows, lanes]).astype(o_ref.dtype)


_RWKV_STAGE_DTYPES = (BF16, F32, BF16, BF16, BF16, BF16, BF16, F32, F32, F32)


def _rwkv_kernel(*refs, first_layer, nchunk, ns):
    n_in = 18 if first_layer else 22
    ins = refs[:n_in]
    lg_ref, lb_ref, o_ref = refs[n_in:n_in + 3]
    pos = n_in + 3
    vfirst = refs[pos:pos + 1] if first_layer else ()
    pos += len(vfirst)
    stage_a, stage_b = refs[pos:pos + 10], refs[pos + 10:pos + 20]
    carries = refs[pos + 20:pos + 24]
    s_all_ref = refs[pos + 24]
    i = pl.program_id(1)

    @pl.when(i == 0)
    def _():
        for r in stage_b + carries + (s_all_ref,):
            r[...] = jnp.zeros(r.shape, r.dtype)

    def run(done, todo):
        _rwkv_prep_kernel(*ins, *todo, *vfirst, *carries, first_layer=first_layer)
        _rwkv_chunk_kernel(*done, lg_ref, lb_ref, o_ref, s_all_ref, nchunk=nchunk)

    @pl.when(i % 2 == 0)
    def _():
        run(stage_b, stage_a)

    @pl.when(i % 2 == 1)
    def _():
        run(stage_a, stage_b)


def _rwkv(pw, pn, vfirst, wts, lnx_g, lnx_b, batch, seq, t, first_layer):
    n = pw.shape[0]
    ns = seq // t
    const2 = lambda b, i: (0, 0)
    cur = lambda b, i: b * ns + jnp.minimum(i, ns - 1)
    pcol = lambda col, w: pl.BlockSpec((t, w), lambda b, i: (cur(b, i), col // w))
    vec = pl.BlockSpec((1, W_BR), const2)
    in_specs = [pcol(COL_SR, W_BR), pcol(COL_SK, W_BR), pcol(COL_SV, W_BR), pcol(COL_WA, LANES),
                pcol(COL_GATE_C, W_BR)]
    args = [pw, pw, pw, pn, pw]
    if not first_layer:
        in_specs.append(pl.BlockSpec((t, W_BR), lambda b, i: (cur(b, i), 0)))
        args.append(vfirst)
    in_specs += [vec, vec, vec, pl.BlockSpec((1, LANES), const2), vec, vec,
                 pl.BlockSpec((LANES, W_BR), const2), pl.BlockSpec((LANES, W_BR), const2), vec, vec, vec]
    if not first_layer:
        in_specs += [vec, pl.BlockSpec((W_BR, LANES), const2), pl.BlockSpec((LANES, W_BR), const2)]
    in_specs += [pl.BlockSpec((t, t), const2)] * 2 + [vec, vec]
    args += list(wts) + [lnx_g, lnx_b]
    out_specs = [pl.BlockSpec((t, W_BR), lambda b, i: (b * ns + jnp.maximum(i - 1, 0), 0))]
    out_shape = [jax.ShapeDtypeStruct((n, W_BR), BF16)]
    if first_layer:
        out_specs.append(pl.BlockSpec((t, W_BR), lambda b, i: (jnp.where(i < ns, b * ns + i, batch * ns + b), 0)))
        out_shape.append(jax.ShapeDtypeStruct((n + batch * t, W_BR), F32))
    return pl.pallas_call(
        functools.partial(_rwkv_kernel, first_layer=first_layer, nchunk=t // CHUNK, ns=ns),
        grid=(batch, ns + 1),
        in_specs=in_specs,
        out_specs=out_specs,
        out_shape=out_shape,
        scratch_shapes=[pltpu.VMEM((t, W_BR), dt) for dt in _RWKV_STAGE_DTYPES * 2]
        + [pltpu.VMEM((1, W_BR), F32), pltpu.VMEM((1, W_BR), F32), pltpu.VMEM((1, W_BR), F32),
           pltpu.VMEM((1, LANES), F32), pltpu.VMEM((H // 2, LANES, LANES), F32)],
        compiler_params=_cparams(("parallel", "arbitrary")),
        name="rwkv",
    )(*args)


def _merge_kernel(oa_ref, ob_ref, oc_ref, ga_ref, gb_ref, gc_ref, x_ref, wpa_ref, wpb_ref, wpc_ref, wout_ref,
                  o_ref):
    pa = _dot(oa_ref[...], wpa_ref[...])
    pb = _dot(ob_ref[...], wpb_ref[...])
    pc = _dot(oc_ref[...], wpc_ref[...])
    sig = lambda ref: _sigmoid(ref[...].astype(F32))
    merged = sig(ga_ref) * pa + sig(gb_ref) * pb + sig(gc_ref) * pc
    o_ref[...] = x_ref[...] + _dot(merged.astype(BF16), wout_ref[...])


def _merge(oa, ob, oc, p, x2, wpa, wpb, wpc, wout, batch, seq, t):
    n = x2.shape[0]
    ns = seq // t
    row = lambda b, i: (b * ns + i, 0)
    const2 = lambda b, i: (0, 0)
    gate = lambda g: pl.BlockSpec((t, D_MODEL), lambda b, i: (b * ns + i, COL_MERGE // D_MODEL + g))
    return pl.pallas_call(
        _merge_kernel,
        grid=(batch, ns),
        in_specs=[
            pl.BlockSpec((t, W_BR), row),
            pl.BlockSpec((t, W_BR), row),
            pl.BlockSpec((t, W_BR), row),
            gate(0), gate(1), gate(2),
            pl.BlockSpec((t, D_MODEL), row),
            pl.BlockSpec((W_BR, D_MODEL), const2),
            pl.BlockSpec((W_BR, D_MODEL), const2),
            pl.BlockSpec((W_BR, D_MODEL), const2),
            pl.BlockSpec((D_MODEL, D_MODEL), const2),
        ],
        out_specs=pl.BlockSpec((t, D_MODEL), row),
        out_shape=jax.ShapeDtypeStruct((n, D_MODEL), F32),
        compiler_params=_cparams(("parallel", "parallel")),
        name="merge",
    )(oa, ob, oc, p, p, p, x2, wpa, wpb, wpc, wout)


def _regroup_pieces():
    sizes = (Q_LORA, KV_LORA, ROPE_D, W_BR, W_BR, W_BR, W_BR, H, W_BR,
             3 * W_BR + DECAY_LORA + AAA_LORA, W_BR, 3 * D_MODEL)
    o = [int(v) for v in np.concatenate([[0], np.cumsum(sizes)])]
    return (
        ("wide", COL_MERGE, o[11], 3 * D_MODEL), ("wide", COL_GATE_A, o[3], 4 * W_BR),
        ("wide", COL_GATE_B, o[8], W_BR), ("wide", COL_SR, o[9], 3 * W_BR), ("wide", COL_GATE_C, o[10], W_BR),
        ("narrow", COL_CQ, o[0], Q_LORA), ("narrow", COL_CKV, o[1], KV_LORA),
        ("narrow", COL_KR + NOPE, o[2], ROPE_D), ("narrow", COL_KR + NOPE + ROPE_D, o[2], ROPE_D // 2),
        ("narrow", COL_FF, o[7], H), ("narrow", COL_WA, o[9] + 3 * W_BR, DECAY_LORA + AAA_LORA),
    )


def _regroup_kernel(w_ref, wide_ref, narrow_ref):
    narrow_ref[...] = jnp.zeros(narrow_ref.shape, narrow_ref.dtype)
    dst = {"wide": wide_ref, "narrow": narrow_ref}
    for name, d0, s0, width in _regroup_pieces():
        dst[name][:, d0:d0 + width] = w_ref[0, :, s0:s0 + width].astype(BF16)


def _regroup_w_in_pallas(w_in, l, tr=256):
    d, cols = w_in.shape[1], w_in.shape[2]
    return pl.pallas_call(
        _regroup_kernel,
        grid=(d // tr,),
        in_specs=[pl.BlockSpec((1, tr, cols), lambda i: (l, i, 0))],
        out_specs=[pl.BlockSpec((tr, NCOLS_WIDE), lambda i: (i, 0)), pl.BlockSpec((tr, NCOLS_NARROW), lambda i: (i, 0))],
        out_shape=[jax.ShapeDtypeStruct((d, NCOLS_WIDE), BF16), jax.ShapeDtypeStruct((d, NCOLS_NARROW), BF16)],
        compiler_params=_cparams(("parallel",)),
        name="regroup",
    )(w_in)


def _regroup_w_in(w_in, l):
    d = w_in.shape[1]
    sizes = (Q_LORA, KV_LORA, ROPE_D, W_BR, W_BR, W_BR, W_BR, H, W_BR,
             3 * W_BR + DECAY_LORA + AAA_LORA, W_BR, 3 * D_MODEL)
    offs = [int(o) for o in np.concatenate([[0], np.cumsum(sizes)])]
    (c_q, c_kv, k_r, gate_a, fq, fk, fv, ff, gate_b, shift, gate_c, merge) = [
        w_in[l, :, offs[i]:offs[i + 1]].astype(BF16) for i in range(len(sizes))]
    z = lambda n: jnp.zeros((d, n), BF16)
    wide = jnp.concatenate([
        merge, gate_a, fq, fk, fv, gate_b,
        shift[:, :W_BR], shift[:, W_BR:2 * W_BR], shift[:, 2 * W_BR:3 * W_BR], gate_c,
    ], axis=1)
    narrow = jnp.concatenate([
        c_q, c_kv,
        z(NOPE), k_r, k_r[:, :ROPE_D // 2], z(LANES - NOPE - ROPE_D - ROPE_D // 2),
        ff, z(LANES - H),
        shift[:, 3 * W_BR:],
    ], axis=1)
    return wide, narrow


def _rope_tables(seq):
    inv =ROPE_THETA ** (-jnp.arange(0, ROPE_D, 2, dtype=F32) / ROPE_D)
    ang = jnp.arange(seq, dtype=F32)[:, None] * inv[None, :]
    cos, sin = jnp.cos(ang), jnp.sin(ang)
    z = lambda n: jnp.zeros((seq, n), F32)
    c = jnp.concatenate([jnp.ones((seq, NOPE), F32), cos, cos, z(LANES - NOPE - ROPE_D)], axis=1)
    s = jnp.concatenate([z(NOPE), -sin, sin, z(LANES - NOPE - ROPE_D)], axis=1)
    return c, s


def _pad_lanes(v, lo, total=LANES):
    return jnp.zeros((1, total), F32).at[0, lo:lo + v.shape[0]].set(v)


def _fox_selectors():
    selq = np.zeros((3, LANES, H * LANES), np.float32)
    selk = np.zeros((3, LANES, H * LANES), np.float32)
    oneq = np.zeros((1, H * LANES), np.float32)
    onek = np.zeros((1, H * LANES), np.float32)
    for h in range(H):
        for j in range(3):
            selq[j, h, h * LANES + HD + j] = 1.0
            selk[j, h, h * LANES + HD + 3 + j] = -1.0
            oneq[0, h * LANES + HD + 3 + j] = 1.0
            onek[0, h * LANES + HD + j] = 1.0
    stacked = lambda sel: jnp.asarray(sel.reshape(3 * LANES, H * LANES), BF16)
    return (stacked(selq), stacked(selk), jnp.asarray(oneq), jnp.asarray(onek))


def _tri(t):
    r = np.arange(t)
    return jnp.asarray((r[None, :] <= r[:, None]).astype(np.float32), BF16)


def _chunk_tri(t):
    r = np.arange(t)
    same = (r[None, :] // CHUNK) == (r[:, None] // CHUNK)
    lower = r[None, :] <= r[:, None]
    return (jnp.asarray((same & lower).astype(np.float32), BF16), jnp.asarray(same.astype(np.float32), BF16))


def _tile(seq, pref):
    return pref if seq % pref == 0 else seq


def kernel(x, norm_g, w_in, mla_qa_g, mla_w_uq, mla_kva_g, mla_w_ukv, mla_q_g, mla_knope_g, mla_krope_g, fox_b_f, fox_q_g, fox_k_g, rwkv_mu, rwkv_w0, rwkv_w_up, rwkv_a0, rwkv_a_up, rwkv_k_k, rwkv_k_a, rwkv_r_k, rwkv_lnx_g, rwkv_lnx_b, rwkv_v0, rwkv_v_down, rwkv_v_up, w_pa, w_pb, w_pc, w_out):
    batch, seq, d = x.shape
    depth = w_in.shape[0]
    n = batch * seq
    t_row = _tile(seq, ROW_TILE)
    t_in = _tile(n, INPROJ_TM)
    t_rwkv = _tile(seq, RWKV_TILE)
    assert seq % (ATTN_QT * t_row) == 0 and seq % t_rwkv == 0 and t_rwkv % CHUNK == 0, seq
    x2 = x.reshape(n, d)

    tabs = _rope_tables(seq)
    selq, selk, oneq, onek = _fox_selectors()
    tri = _tri(t_row)
    btri, bones = _chunk_tri(t_rwkv)
    row = lambda v: v.reshape(1, -1).astype(F32)

    half = ROPE_D // 2
    spare = LANES - NOPE - ROPE_D - half
    with_copy = lambda g, lo: jnp.concatenate([g, g[lo:lo + half]])
    vfirst = None
    for l in range(depth):
        w_wide, w_narrow = _regroup_w_in_pallas(w_in, l)
        pw, pn = _inproj(x2, row(norm_g[l]), w_wide, w_narrow, t_in, INPROJ_TN)

        wuq = mla_w_uq[l].reshape(Q_LORA, H, NOPE + ROPE_D)
        wuq = jnp.concatenate([wuq, wuq[:, :, NOPE:NOPE + half], jnp.zeros((Q_LORA, H, spare), F32)], axis=2)
        wuq = wuq.reshape(Q_LORA, H * LANES).astype(BF16)
        wukv = mla_w_ukv[l].reshape(KV_LORA, H, NOPE + HD)
        wk = jnp.pad(wukv[:, :, :NOPE], ((0, 0), (0, 0), (0, LANES - NOPE))).reshape(KV_LORA, H * LANES).astype(BF16)
        wv = wukv[:, :, NOPE:].reshape(KV_LORA, W_BR).T.astype(BF16)
        mla_scale = float(NOPE + ROPE_D) ** -0.5 * LOG2E
        mla_wts = (row(mla_qa_g[l]), wuq, row(mla_kva_g[l]), wk, wv,
                   _pad_lanes(with_copy(mla_q_g[l], NOPE) * mla_scale, 0), _pad_lanes(mla_knope_g[l], 0),
                   _pad_lanes(with_copy(mla_krope_g[l], 0), NOPE))
        qa, ka, va = _mla_prep(pn, tabs, mla_wts, seq, t_row)
        o_a = _attention(qa, ka, va, pw, COL_GATE_A, batch, seq, t_row)

        fox_wts = (_pad_lanes(fox_b_f[l], 0), row(jnp.tile(fox_q_g[l] * (float(HD) ** -0.5 * LOG2E), H)),
                   row(jnp.tile(fox_k_g[l], H)), tri, selq, selk, oneq, onek)
        qb, kb, vb = _fox_prep(pw, pn, fox_wts, batch, seq, t_row)
        o_b = _attention(qb, kb, vb, pw, COL_GATE_B, batch, seq, t_row)

        mu = rwkv_mu[l]
        wup = jnp.zeros((LANES, W_BR), F32).at[:DECAY_LORA].set(rwkv_w_up[l]).astype(BF16)
        aup = jnp.zeros((LANES, W_BR), F32).at[DECAY_LORA:].set(rwkv_a_up[l]).astype(BF16)
        rw = [row(mu[:W_BR]), row(mu[W_BR:2 * W_BR]), row(mu[2 * W_BR:3 * W_BR]), row(mu[3 * W_BR:]),
              row(rwkv_w0[l]), row(rwkv_a0[l]), wup, aup, row(rwkv_k_k[l]), row(rwkv_k_a[l]),
              row(rwkv_r_k[l])]
        if l > 0:
            vdown = jnp.zeros((W_BR, LANES), F32).at[:, :MV_LORA].set(rwkv_v_down[l - 1]).astype(BF16)
            vup = jnp.zeros((LANES, W_BR), F32).at[:MV_LORA].set(rwkv_v_up[l - 1]).astype(BF16)
            rw += [row(rwkv_v0[l - 1]), vdown, vup]
        rw += [btri, bones]
        outs = _rwkv(pw, pn, vfirst, rw, row(rwkv_lnx_g[l]), row(rwkv_lnx_b[l]), batch, seq, t_rwkv,
                     first_layer=(l == 0))
        o_c = outs[0]
        if l == 0:
            vfirst = outs[1]

        x2 = _merge(o_a, o_b, o_c, pw, x2, w_pa[l].astype(BF16), w_pb[l].astype(BF16),
                    w_pc[l].astype(BF16), w_out[l].astype(BF16), batch, seq, _tile(seq, INPROJ_TM))
    return x2.reshape(batch, seq, d)
```

```python
import functools
import math

import jax
import jax.numpy as jnp
import numpy as np
from jax import lax
from jax.experimental import pallas as pl
from jax.experimental.pallas import tpu as pltpu

F32 = jnp.float32
BF16 = jnp.bfloat16

LANES = 128
H = 8
HD = 64
NOPE = 64
ROPE_D = 32
Q_LORA = 256
KV_LORA = 128
D_MODEL = 1024
W_BR = H * HD
DECAY_LORA = 64
AAA_LORA = 64
MV_LORA = 32
ROPE_THETA = 10000.0
RWKV_DECAY_SCALE = 0.606531
GN_EPS = 64e-5
EPS = 1e-6
NEG_INF = -1e30
LOG2E = math.log2(math.e)
CHUNK = 64
ROW_TILE = 512
INPROJ_TM = 1024
RWKV_TILE = 256
ATTN_QT = 4
ONES_ROWS = 16

COL_MERGE = 0
COL_GATE_A = 3072
COL_FQ = 3584
COL_FK = 4096
COL_FV = 4608
COL_GATE_B = 5120
COL_SR = 5632
COL_SK = 6144
COL_SV = 6656
COL_GATE_C = 7168
NCOLS_WIDE = 7680
INPROJ_TN = 2560
COL_CQ = 0
COL_CKV = 256
COL_KR = 384
COL_FF = 512
COL_WA = 640
NCOLS_NARROW = 768

VMEM_LIMIT = 56 * 1024 * 1024


def _cparams(sem):
    return pltpu.CompilerParams(dimension_semantics=sem, vmem_limit_bytes=VMEM_LIMIT)


def _sigmoid(x):
    return 0.5 * jnp.tanh(0.5 * x) + 0.5


def _dot(a, b):
    return jnp.dot(a, b, preferred_element_type=F32)


def _split3(x):
    hi = x.astype(BF16)
    r1 = x - hi.astype(F32)
    mid = r1.astype(BF16)
    lo = (r1 - mid.astype(F32)).astype(BF16)
    return hi, mid, lo


def _dot_exact_lhs(m_bf16, x):
    hi, mid, lo = _split3(x)
    return _dot(m_bf16, hi) + _dot(m_bf16, mid) + _dot(m_bf16, lo)


def _pair_seg_sum(x):
    lane = lax.broadcasted_iota(jnp.int32, x.shape, 1)
    first = lane < HD
    s0 = jnp.sum(jnp.where(first, x, 0.0), axis=-1, keepdims=True)
    s1 = jnp.sum(jnp.where(first, 0.0, x), axis=-1, keepdims=True)
    return jnp.where(first, s0, s1)


def _inproj_kernel(x_ref, g_ref, w_ref, wn_ref, o_ref, on_ref, h_ref):
    @pl.when(pl.program_id(1) == 0)
    def _():
        x = x_ref[...]
        ms = jnp.mean(x * x, axis=-1, keepdims=True)
        h = (x * lax.rsqrt(ms + EPS) * g_ref[...]).astype(BF16)
        h_ref[...] = h
        on_ref[...] = _dot(h, wn_ref[...])

    o_ref[...] = _dot(h_ref[...], w_ref[...]).astype(o_ref.dtype)


def _inproj(x2, g, w_wide, w_narrow, tm, tn):
    n = x2.shape[0]
    return pl.pallas_call(
        _inproj_kernel,
        grid=(n // tm, NCOLS_WIDE // tn),
        in_specs=[
            pl.BlockSpec((tm, D_MODEL), lambda i, j: (i, 0)),
            pl.BlockSpec((1, D_MODEL), lambda i, j: (0, 0)),
            pl.BlockSpec((D_MODEL, tn), lambda i, j: (0, j)),
            pl.BlockSpec((D_MODEL, NCOLS_NARROW), lambda i, j: (0, 0)),
        ],
        out_specs=[pl.BlockSpec((tm, tn), lambda i, j: (i, j)),
                   pl.BlockSpec((tm, NCOLS_NARROW), lambda i, j: (i, 0))],
        out_shape=[jax.ShapeDtypeStruct((n, NCOLS_WIDE), BF16), jax.ShapeDtypeStruct((n, NCOLS_NARROW), F32)],
        scratch_shapes=[pltpu.VMEM((tm, D_MODEL), BF16)],
        compiler_params=_cparams(("parallel", "arbitrary")),
        name="inproj",
    )(x2, g, w_wide, w_narrow)


def _rms(x, g, n, keep=None):
    sq = x * x
    if keep is not None:
        sq = jnp.where(keep, sq, 0.0)
    ms = jnp.sum(sq, axis=-1, keepdims=True) / n
    return x * lax.rsqrt(ms + EPS) * g


def _rope(x, c, s):
    return x * c + pltpu.roll(x, LANES - ROPE_D // 2, 1) * s


def _mla_prep_kernel(cq_ref, ckv_ref, kr_ref, c_ref, s_ref, qag_ref, wuq_ref, kvag_ref, wk_ref, wv_ref,
                     qg_ref, kng_ref, krg_ref, q_out, k_out, vt_out):
    c, s = c_ref[...], s_ref[...]
    keep = lax.broadcasted_iota(jnp.int32, c.shape, 1) < NOPE + ROPE_D
    cqn = _rms(cq_ref[...], qag_ref[...], Q_LORA).astype(BF16)
    ckvn = _rms(ckv_ref[...], kvag_ref[...], KV_LORA).astype(BF16)
    q_all = _dot(cqn, wuq_ref[...])
    k_all = _dot(ckvn, wk_ref[...])
    vt_out[0, 0] = lax.dot_general(wv_ref[...], ckvn, _NT, preferred_element_type=F32).astype(BF16)
    k_rope = _rope(_rms(kr_ref[...], krg_ref[...], ROPE_D, keep), c, s)
    qg, kng = qg_ref[...], kng_ref[...]
    for h in range(H):
        sl = slice(h * LANES, (h + 1) * LANES)
        q_out[:, sl] = _rope(_rms(q_all[:, sl], qg, NOPE + ROPE_D, keep), c, s).astype(BF16)
        k_out[:, sl] = (_rms(k_all[:, sl], kng, NOPE) + k_rope).astype(BF16)


def _mla_prep(pn, tabs, wts, seq, t):
    n = pn.shape[0]
    ns = seq // t
    row = lambda i: (i, 0)
    const = lambda i: (0, 0)
    tab_spec = pl.BlockSpec((t, LANES), lambda i: (i % ns, 0))
    return pl.pallas_call(
        _mla_prep_kernel,
        grid=(n // t,),
        in_specs=[
            pl.BlockSpec((t, Q_LORA), lambda i: (i, COL_CQ // Q_LORA)),
            pl.BlockSpec((t, LANES), lambda i: (i, COL_CKV // LANES)),
            pl.BlockSpec((t, LANES), lambda i: (i, COL_KR // LANES)),
            tab_spec, tab_spec,
            pl.BlockSpec((1, Q_LORA), const),
            pl.BlockSpec((Q_LORA, H * LANES), const),
            pl.BlockSpec((1, KV_LORA), const),
            pl.BlockSpec((KV_LORA, H * LANES), const),
            pl.BlockSpec((W_BR, KV_LORA), const),
            pl.BlockSpec((1, LANES), const),
            pl.BlockSpec((1, LANES), const),
            pl.BlockSpec((1, LANES), const),
        ],
        out_specs=[
            pl.BlockSpec((t, H * LANES), row),
            pl.BlockSpec((t, H * LANES), row),
            pl.BlockSpec((1, 1, W_BR, t), lambda i: (i // ns, i % ns, 0, 0)),
        ],
        out_shape=[
            jax.ShapeDtypeStruct((n, H * LANES), BF16),
            jax.ShapeDtypeStruct((n, H * LANES), BF16),
            jax.ShapeDtypeStruct((n // seq, ns, W_BR, t), BF16),
        ],
        compiler_params=_cparams(("parallel",)),
        name="mla_prep",
    )(pn, pn, pn, *tabs, *wts)


def _fox_prep_kernel(fq_ref, fk_ref, fv_ref, ff_ref, bf_ref, qg_ref, kg_ref, tri_ref, selq_ref, selk_ref,
                     oneq_ref, onek_ref, q_out, k_out, vt_out, carry_ref):
    t = fq_ref.shape[0]

    @pl.when(pl.program_id(1) == 0)
    def _():
        carry_ref[...] = jnp.zeros_like(carry_ref)

    z = ff_ref[...] + bf_ref[...]
    logf = jnp.minimum(z, 0.0) - jnp.log1p(jnp.exp(-jnp.abs(z)))
    cum = carry_ref[...] + _dot_exact_lhs(tri_ref[...], logf)
    carry_ref[...] = cum[t - 1:t, :]
    hi, mid, lo = _split3(cum * LOG2E)
    terms = jnp.concatenate([hi, mid, lo], axis=1)
    aug_q = _dot(terms, selq_ref[...]) + oneq_ref[...]
    aug_k = _dot(terms, selk_ref[...]) + onek_ref[...]
    vt_out[0, 0] = fv_ref[...].astype(F32).T.astype(BF16)
    lane = lax.broadcasted_iota(jnp.int32, (t, LANES), 1)
    first = lane < HD
    for src_ref, g_ref, aug, dst in ((fq_ref, qg_ref, aug_q, q_out), (fk_ref, kg_ref, aug_k, k_out)):
        for p in range(H // 2):
            sl = slice(p * LANES, (p + 1) * LANES)
            x = src_ref[:, sl].astype(F32)
            ms = _pair_seg_sum(x * x) / HD
            xn = x * lax.rsqrt(ms + EPS) * g_ref[:, sl]
            he, ho = 2 * p, 2 * p + 1
            dst[:, he * LANES:(he + 1) * LANES] = jnp.where(first, xn, aug[:, he * LANES:(he + 1) * LANES]).astype(BF16)
            dst[:, ho * LANES:(ho + 1) * LANES] = jnp.where(
                first, pltpu.roll(xn, HD, 1), aug[:, ho * LANES:(ho + 1) * LANES]).astype(BF16)


def _fox_prep(pw, pn, wts, batch, seq, t):
    n = pw.shape[0]
    ns = seq // t
    row = lambda b, i: (b * ns + i, 0)
    const2 = lambda b, i: (0, 0)
    return pl.pallas_call(
        _fox_prep_kernel,
        grid=(batch, ns),
        in_specs=[
            pl.BlockSpec((t, W_BR), lambda b, i: (b * ns + i, COL_FQ // W_BR)),
            pl.BlockSpec((t, W_BR), lambda b, i: (b * ns + i, COL_FK // W_BR)),
            pl.BlockSpec((t, W_BR), lambda b, i: (b * ns + i, COL_FV // W_BR)),
            pl.BlockSpec((t, LANES), lambda b, i: (b * ns + i, COL_FF // LANES)),
            pl.BlockSpec((1, LANES), const2),
            pl.BlockSpec((1, W_BR), const2),
            pl.BlockSpec((1, W_BR), const2),
            pl.BlockSpec((t, t), const2),
            pl.BlockSpec((3 * LANES, H * LANES), const2),
            pl.BlockSpec((3 * LANES, H * LANES), const2),
            pl.BlockSpec((1, H * LANES), const2),
            pl.BlockSpec((1, H * LANES), const2),
        ],
        out_specs=[
            pl.BlockSpec((t, H * LANES), row),
            pl.BlockSpec((t, H * LANES), row),
            pl.BlockSpec((1, 1, W_BR, t), lambda b, i: (b, i, 0, 0)),
        ],
        out_shape=[
            jax.ShapeDtypeStruct((n, H * LANES), BF16),
            jax.ShapeDtypeStruct((n, H * LANES), BF16),
            jax.ShapeDtypeStruct((batch, ns, W_BR, t), BF16),
        ],
        scratch_shapes=[pltpu.VMEM((1, LANES), F32)],
        compiler_params=_cparams(("parallel", "arbitrary")),
        name="fox_prep",
    )(pw, pw, pw, pn, *wts)


def _attn_kernel(q_ref, k_ref, vt_ref, g_ref, o_ref, m_ref, acc_ref, sa_ref, sb_ref, mxa_ref, mxb_ref, *, tk):
    tq = ATTN_QT * tk
    gi = pl.program_id(2)
    nt = (((1,), (1,)), ((), ()))
    kv_pos = lax.broadcasted_iota(jnp.int32, (tk, tq), 0)
    q_pos = lax.broadcasted_iota(jnp.int32, (tk, tq), 1)
    causal = kv_pos <= q_pos
    both = slice(0, tq)
    ones = jnp.ones((ONES_ROWS, tk), BF16)
    m_ref[...] = jnp.full(m_ref.shape, NEG_INF, F32)
    acc_ref[...] = jnp.zeros(acc_ref.shape, F32)

    def scores(c, s_ref, mx_ref, cols):
        start = pl.multiple_of(c * tk, tk)
        for j in range(2):
            q = q_ref[cols, j * LANES:(j + 1) * LANES]
            k = k_ref[pl.ds(start, tk), j * LANES:(j + 1) * LANES]
            s = lax.dot_general(k, q, nt, preferred_element_type=F32)
            s_ref[j, :, cols] = s
            mx_ref[j, :, cols] = jnp.max(s, axis=0, keepdims=True)

    def consume(c, s_ref, mx_ref, cols, mask):
        for j in range(2):
            vt = jnp.concatenate([vt_ref[0, c, j * HD:(j + 1) * HD, :], ones], axis=0)
            s = s_ref[j, :, cols]
            if mask is not None:
                s = jnp.where(mask, s, NEG_INF)
                m_cur = jnp.max(s, axis=0, keepdims=True)
            else:
                m_cur = mx_ref[j, :, cols]
            m_old = m_ref[j, :, cols]
            m_new = jnp.maximum(m_old, m_cur)
            alpha = jnp.exp2(m_old - m_new)
            p = jnp.exp2(s - m_new)
            acc_ref[j, :, cols] = alpha * acc_ref[j, :, cols] + _dot(vt, p.astype(BF16))
            m_ref[j, :, cols] = m_new

    scores(0, sa_ref, mxa_ref, both)

    def body(i, carry):
        c = 2 * i
        scores(c + 1, sb_ref, mxb_ref, both)
        consume(c, sa_ref, mxa_ref, both, None)
        scores(c + 2, sa_ref, mxa_ref, both)
        consume(c + 1, sb_ref, mxb_ref, both, None)
        return carry

    lax.fori_loop(0, gi * (ATTN_QT // 2), body, 0)
    bufs = ((sa_ref, mxa_ref), (sb_ref, mxb_ref))
    for d in range(ATTN_QT):
        c = ATTN_QT * gi + d
        if d + 1 < ATTN_QT:
            scores(c + 1, *bufs[(d + 1) % 2], slice((d + 1) * tk, tq))
        consume(c, *bufs[d % 2], slice(d * tk, tq), causal[:, :tq - d * tk])

    o_t = jnp.concatenate([acc_ref[j, :HD, :] / acc_ref[j, HD:HD + 1, :] for j in range(2)], axis=0)
    g = g_ref[...].astype(F32)
    o_ref[...] = (o_t.T * (g * _sigmoid(g))).astype(BF16)


def _attention(q, k, vt, pw, gate_col, batch, seq, tk):
    n = q.shape[0]
    tq = ATTN_QT * tk
    nq, nk = seq // tq, seq // tk
    return pl.pallas_call(
        functools.partial(_attn_kernel, tk=tk),
        grid=(batch, H // 2, nq),
        in_specs=[
            pl.BlockSpec((tq, 2 * LANES), lambda b, hp, i: (b * nq + i, hp)),
            pl.BlockSpec((seq, 2 * LANES), lambda b, hp, i: (b, hp)),
            pl.BlockSpec((1, nk, LANES, tk), lambda b, hp, i: (b, 0, hp, 0)),
            pl.BlockSpec((tq, LANES), lambda b, hp, i: (b * nq + i, gate_col // LANES + hp)),
        ],
        out_specs=pl.BlockSpec((tq, LANES), lambda b, hp, i: (b * nq + i, hp)),
        out_shape=jax.ShapeDtypeStruct((n, W_BR), BF16),
        scratch_shapes=[
            pltpu.VMEM((2, 1, tq), F32),
            pltpu.VMEM((2, HD + ONES_ROWS, tq), F32),
            pltpu.VMEM((2, tk, tq), F32),
            pltpu.VMEM((2, tk, tq), F32),
            pltpu.VMEM((2, 1, tq), F32),
            pltpu.VMEM((2, 1, tq), F32),
        ],
        compiler_params=_cparams(("parallel", "parallel", "arbitrary")),
        name="attention",
    )(q, k, vt, pw)


def _rwkv_prep_kernel(*refs, first_layer):
    if first_layer:
        (sr_ref, sk_ref, sv_ref, swa_ref, gc_ref, mur_ref, muk_ref, muv_ref, muwa_ref, w0_ref, a0_ref, wup_ref,
         aup_ref, kk_ref, ka_ref, rk_ref, bt_ref, bo_ref,
         at_o, rt_o, bt_o, kt_o, v_o, bh_o, kh_o, wc_o, bonus_o, sg_o, vfirst_o,
         cr_ref, ck_ref, cv_ref, cwa_ref) = refs
    else:
        (sr_ref, sk_ref, sv_ref, swa_ref, gc_ref, vf_ref, mur_ref, muk_ref, muv_ref, muwa_ref, w0_ref, a0_ref,
         wup_ref, aup_ref, kk_ref, ka_ref, rk_ref, v0_ref, vdown_ref, vup_ref, bt_ref, bo_ref,
         at_o, rt_o, bt_o, kt_o, v_o, bh_o, kh_o, wc_o, bonus_o, sg_o,
         cr_ref, ck_ref, cv_ref, cwa_ref) = refs
    t = sr_ref.shape[0]

    def shift(x_ref, c_ref, mu_ref):
        x = x_ref[...].astype(F32)
        rowid = lax.broadcasted_iota(jnp.int32, x.shape, 0)
        prev = jnp.where(rowid == 0, c_ref[...], pltpu.roll(x, 1, 0))
        c_ref[...] = x[t - 1:t, :]
        return x + mu_ref[...] * (prev - x)

    r = shift(sr_ref, cr_ref, mur_ref)
    k = shift(sk_ref, ck_ref, muk_ref)
    v = shift(sv_ref, cv_ref, muv_ref)
    wa = shift(swa_ref, cwa_ref, muwa_ref)

    logw = -(RWKV_DECAY_SCALE * LOG2E) * _sigmoid(w0_ref[...] + _dot(jnp.tanh(wa).astype(BF16), wup_ref[...]))
    a = _sigmoid(a0_ref[...] + _dot(wa.astype(BF16), aup_ref[...]))
    if first_layer:
        vfirst_o[...] = v
    else:
        low = _dot(v.astype(BF16), vdown_ref[...]).astype(BF16)
        nu = _sigmoid(v0_ref[...] + _dot(low, vup_ref[...]))
        v = v + (vf_ref[...] - v) * nu

    kk = k * kk_ref[...]
    k_mod = k * (1.0 + (a - 1.0) * ka_ref[...])
    rk = r * k_mod * rk_ref[...]
    kk_n, bonus = [], []
    for p in range(H // 2):
        sl = slice(p * LANES, (p + 1) * LANES)
        nrm = jnp.sqrt(_pair_seg_sum(kk[:, sl] * kk[:, sl]))
        kk_n.append(kk[:, sl] / jnp.maximum(nrm, 1e-12))
        bonus.append(_pair_seg_sum(rk[:, sl]) * v[:, sl])
    kk = jnp.concatenate(kk_n, axis=-1)
    bonus = jnp.concatenate(bonus, axis=-1)
    a_vec = -kk
    b_vec = kk * a

    terms = _split3(logw)
    lcum = sum(_dot(bt_ref[...], x) for x in terms)
    ltot = sum(_dot(bo_ref[...], x) for x in terms)
    e_minus = jnp.exp2(-lcum)
    e_rem = jnp.exp2(ltot - lcum)
    g = gc_ref[...].astype(F32)
    for o_ref, val in ((at_o, a_vec * jnp.exp2(lcum - logw)), (rt_o, r * jnp.exp2(lcum)), (bt_o, b_vec * e_minus),
                       (kt_o, k_mod * e_minus), (v_o, v), (bh_o, b_vec * e_rem), (kh_o, k_mod * e_rem),
                       (wc_o, jnp.exp2(ltot)), (bonus_o, bonus), (sg_o, g * _sigmoid(g))):
        o_ref[...] = val.astype(o_ref.dtype)


def _mm(a, b, dims):
    return lax.dot_general(a.astype(BF16), b.astype(BF16), dims, preferred_element_type=F32)


_NN = (((1,), (0,)), ((), ()))
_NT = (((1,), (1,)), ((), ()))
_TN = (((0,), (0,)), ((), ()))


def _rwkv_chunk_kernel(at_ref, rt_ref, bt_ref, kt_ref, v_ref, bh_ref, kh_ref, wc_ref, bonus_ref, sg_ref,
                       lg_ref, lb_ref, o_ref, s_all_ref, *, nchunk):
    mm = _mm
    first = lax.broadcasted_iota(jnp.int32, (CHUNK, LANES), 1) < HD
    r2 = lax.broadcasted_iota(jnp.int32, (LANES, LANES), 0)
    c2 = lax.broadcasted_iota(jnp.int32, (LANES, LANES), 1)
    same_head = (r2 < HD) == (c2 < HD)
    eye2 = r2 == c2
    t_row, t_col = r2 % CHUNK, c2 % CHUNK
    strict2 = jnp.logical_and(same_head, t_col < t_row)
    incl2 = jnp.logical_and(same_head, t_col <= t_row)
    each = lambda f, *cols: [f(*xs) for xs in zip(*cols)]
    pairs = [(c, p) for c in range(nchunk) for p in range(H // 2)]
    npair = range(len(pairs))
    ld = lambda ref: [ref[c * CHUNK:(c + 1) * CHUNK, p * LANES:(p + 1) * LANES] for c, p in pairs]
    at, rt, bt, kt, v, bh, kh = ld(at_ref), ld(rt_ref), ld(bt_ref), ld(kt_ref), ld(v_ref), ld(bh_ref), ld(kh_ref)
    zero = jnp.zeros((CHUNK, LANES), BF16)
    stack = lambda x: jnp.concatenate([jnp.where(first, x, zero), jnp.where(first, zero, x)], axis=0)
    fold = lambda x: x[:CHUNK] + x[CHUNK:]
    a_s, b_s, k_s, v_s = each(stack, at), each(stack, bt), each(stack, kt), each(stack, v)
    x1 = each(lambda a, r: jnp.concatenate([a, stack(r.astype(BF16))], axis=0), a_s, rt)
    gb = each(lambda x, b: mm(x, b, _NT), x1, b_s)
    gk = each(lambda x, k: mm(x, k, _NT), x1, k_s)
    a_ab = [jnp.where(strict2, g[:LANES], 0.0) for g in gb]
    a_rb = [jnp.where(incl2, g[LANES:], 0.0) for g in gb]
    a_ak = [jnp.where(strict2, g[:LANES], 0.0) for g in gk]
    a_rk = [jnp.where(incl2, g[LANES:], 0.0) for g in gk]
    tinv = [jnp.where(eye2, 1.0, a) for a in a_ab]
    pw = a_ab
    for _ in range(int(math.log2(CHUNK)) - 1):
        pw = each(lambda p: mm(p, p, _NN), pw)
        tinv = each(lambda t, p: t + mm(t, p, _NN), tinv, pw)
    av = each(lambda a, x: mm(a, x, _NN), a_ak, v_s)
    tx = each(lambda t, x, a: mm(t, jnp.concatenate([x.astype(BF16), a], axis=1), _NN), tinv, av, a_s)
    ry = each(lambda a, x: mm(a, x, _NN), a_rb, tx)
    yk = each(lambda a, x: mm(a, x, _NN), a_rk, v_s)
    u = [fold(t[:, :LANES]) for t in tx]
    a_til = [fold(t[:, LANES:]) for t in tx]
    y_in = [fold(r[:, :LANES] + k) for r, k in zip(ry, yk)]
    r_hat = [r + fold(x[:, LANES:]) for r, x in zip(rt, ry)]
    m_new = [jnp.where(same_head, mm(a_til[i], bh[i], _TN), 0.0) for i in npair]
    n_new = [jnp.where(same_head, mm(jnp.concatenate([u[i].astype(BF16), v[i]], axis=0),
                                     jnp.concatenate([bh[i], kh[i]], axis=0), _TN), 0.0) for i in npair]
    lg, lb = lg_ref[...], lb_ref[...]
    for c in range(nchunk):
        idx = [i for i, (ci, _) in enumerate(pairs) if ci == c]
        s0 = [s_all_ref[p] for p in range(H // 2)]
        y = [y_in[i] + mm(r_hat[i], s0[p], _NT) for p, i in enumerate(idx)]
        for p, i in enumerate(idx):
            wc = wc_ref[c * CHUNK:c * CHUNK + 1, p * LANES:(p + 1) * LANES]
            s_all_ref[p] = mm(s0[p], jnp.where(eye2, wc, 0.0) + m_new[i], _NN) + n_new[i]
        for p, i in enumerate(idx):
            rows, lanes = slice(c * CHUNK, (c + 1) * CHUNK), slice(p * LANES, (p + 1) * LANES)
            mu = _pair_seg_sum(y[p]) / HD
            d = y[p] - mu
            var = _pair_seg_sum(d * d) / HD
            yn = d * lax.rsqrt(var + GN_EPS) * lg[:, lanes] + lb[:, lanes]
            o_ref[rows, lanes] = ((yn + bonus_ref[rows, lanes]) * sg_ref[rows, lanes]).astype(o_ref.dtype)


_RWKV_STAGE_DTYPES = (BF16, F32, BF16, BF16, BF16, BF16, BF16, F32, F32, F32)


def _rwkv_kernel(*refs, first_layer, nchunk, ns):
    n_in = 18 if first_layer else 22
    ins = refs[:n_in]
    lg_ref, lb_ref, o_ref = refs[n_in:n_in + 3]
    pos = n_in + 3
    vfirst = refs[pos:pos + 1] if first_layer else ()
    pos += len(vfirst)
    stage_a, stage_b = refs[pos:pos + 10], refs[pos + 10:pos + 20]
    carries = refs[pos + 20:pos + 24]
    s_all_ref = refs[pos + 24]
    i = pl.program_id(1)

    @pl.when(i == 0)
    def _():
        for r in stage_b + carries + (s_all_ref,):
            r[...] = jnp.zeros(r.shape, r.dtype)

    def run(done, todo):
        _rwkv_prep_kernel(*ins, *todo, *vfirst, *carries, first_layer=first_layer)
        _rwkv_chunk_kernel(*done, lg_ref, lb_ref, o_ref, s_all_ref, nchunk=nchunk)

    @pl.when(i % 2 == 0)
    def _():
        run(stage_b, stage_a)

    @pl.when(i % 2 == 1)
    def _():
        run(stage_a, stage_b)


def _rwkv(pw, pn, vfirst, wts, lnx_g, lnx_b, batch, seq, t, first_layer):
    n = pw.shape[0]
    ns = seq // t
    const2 = lambda b, i: (0, 0)
    cur = lambda b, i: b * ns + jnp.minimum(i, ns - 1)
    pcol = lambda col, w: pl.BlockSpec((t, w), lambda b, i: (cur(b, i), col // w))
    vec = pl.BlockSpec((1, W_BR), const2)
    in_specs = [pcol(COL_SR, W_BR), pcol(COL_SK, W_BR), pcol(COL_SV, W_BR), pcol(COL_WA, LANES),
                pcol(COL_GATE_C, W_BR)]
    args = [pw, pw, pw, pn, pw]
    if not first_layer:
        in_specs.append(pl.BlockSpec((t, W_BR), lambda b, i: (cur(b, i), 0)))
        args.append(vfirst)
    in_specs += [vec, vec, vec, pl.BlockSpec((1, LANES), const2), vec, vec,
                 pl.BlockSpec((LANES, W_BR), const2), pl.BlockSpec((LANES, W_BR), const2), vec, vec, vec]
    if not first_layer:
        in_specs += [vec, pl.BlockSpec((W_BR, LANES), const2), pl.BlockSpec((LANES, W_BR), const2)]
    in_specs += [pl.BlockSpec((t, t), const2)] * 2 + [vec, vec]
    args += list(wts) + [lnx_g, lnx_b]
    out_specs = [pl.BlockSpec((t, W_BR), lambda b, i: (b * ns + jnp.maximum(i - 1, 0), 0))]
    out_shape = [jax.ShapeDtypeStruct((n, W_BR), BF16)]
    if first_layer:
        out_specs.append(pl.BlockSpec((t, W_BR), lambda b, i: (jnp.where(i < ns, b * ns + i, batch * ns + b), 0)))
        out_shape.append(jax.ShapeDtypeStruct((n + batch * t, W_BR), F32))
    return pl.pallas_call(
        functools.partial(_rwkv_kernel, first_layer=first_layer, nchunk=t // CHUNK, ns=ns),
        grid=(batch, ns + 1),
        in_specs=in_specs,
        out_specs=out_specs,
        out_shape=out_shape,
        scratch_shapes=[pltpu.VMEM((t, W_BR), dt) for dt in _RWKV_STAGE_DTYPES * 2]
        + [pltpu.VMEM((1, W_BR), F32), pltpu.VMEM((1, W_BR), F32), pltpu.VMEM((1, W_BR), F32),
           pltpu.VMEM((1, LANES), F32), pltpu.VMEM((H // 2, LANES, LANES), F32)],
        compiler_params=_cparams(("parallel", "arbitrary")),
        name="rwkv",
    )(*args)


def _merge_kernel(oa_ref, ob_ref, oc_ref, ga_ref, gb_ref, gc_ref, x_ref, wpa_ref, wpb_ref, wpc_ref, wout_ref,
                  o_ref):
    pa = _dot(oa_ref[...], wpa_ref[...])
    pb = _dot(ob_ref[...], wpb_ref[...])
    pc = _dot(oc_ref[...], wpc_ref[...])
    sig = lambda ref: _sigmoid(ref[...].astype(F32))
    merged = sig(ga_ref) * pa + sig(gb_ref) * pb + sig(gc_ref) * pc
    o_ref[...] = x_ref[...] + _dot(merged.astype(BF16), wout_ref[...])


def _merge(oa, ob, oc, p, x2, wpa, wpb, wpc, wout, batch, seq, t):
    n = x2.shape[0]
    ns = seq // t
    row = lambda b, i: (b * ns + i, 0)
    const2 = lambda b, i: (0, 0)
    gate = lambda g: pl.BlockSpec((t, D_MODEL), lambda b, i: (b * ns + i, COL_MERGE // D_MODEL + g))
    return pl.pallas_call(
        _merge_kernel,
        grid=(batch, ns),
        in_specs=[
            pl.BlockSpec((t, W_BR), row),
            pl.BlockSpec((t, W_BR), row),
            pl.BlockSpec((t, W_BR), row),
            gate(0), gate(1), gate(2),
            pl.BlockSpec((t, D_MODEL), row),
            pl.BlockSpec((W_BR, D_MODEL), const2),
            pl.BlockSpec((W_BR, D_MODEL), const2),
            pl.BlockSpec((W_BR, D_MODEL), const2),
            pl.BlockSpec((D_MODEL, D_MODEL), const2),
        ],
        out_specs=pl.BlockSpec((t, D_MODEL), row),
        out_shape=jax.ShapeDtypeStruct((n, D_MODEL), F32),
        compiler_params=_cparams(("parallel", "parallel")),
        name="merge",
    )(oa, ob, oc, p, p, p, x2, wpa, wpb, wpc, wout)


def _regroup_pieces():
    sizes = (Q_LORA, KV_LORA, ROPE_D, W_BR, W_BR, W_BR, W_BR, H, W_BR,
             3 * W_BR + DECAY_LORA + AAA_LORA, W_BR, 3 * D_MODEL)
    o = [int(v) for v in np.concatenate([[0], np.cumsum(sizes)])]
    return (
        ("wide", COL_MERGE, o[11], 3 * D_MODEL), ("wide", COL_GATE_A, o[3], 4 * W_BR),
        ("wide", COL_GATE_B, o[8], W_BR), ("wide", COL_SR, o[9], 3 * W_BR), ("wide", COL_GATE_C, o[10], W_BR),
        ("narrow", COL_CQ, o[0], Q_LORA), ("narrow", COL_CKV, o[1], KV_LORA),
        ("narrow", COL_KR + NOPE, o[2], ROPE_D), ("narrow", COL_KR + NOPE + ROPE_D, o[2], ROPE_D // 2),
        ("narrow", COL_FF, o[7], H), ("narrow", COL_WA, o[9] + 3 * W_BR, DECAY_LORA + AAA_LORA),
    )


def _regroup_kernel(w_ref, wide_ref, narrow_ref):
    narrow_ref[...] = jnp.zeros(narrow_ref.shape, narrow_ref.dtype)
    dst = {"wide": wide_ref, "narrow": narrow_ref}
    for name, d0, s0, width in _regroup_pieces():
        dst[name][:, d0:d0 + width] = w_ref[0, :, s0:s0 + width].astype(BF16)


def _regroup_w_in(w_in, l, tr=256):
    d, cols = w_in.shape[1], w_in.shape[2]
    return pl.pallas_call(
        _regroup_kernel,
        grid=(d // tr,),
        in_specs=[pl.BlockSpec((1, tr, cols), lambda i: (l, i, 0))],
        out_specs=[pl.BlockSpec((tr, NCOLS_WIDE), lambda i: (i, 0)), pl.BlockSpec((tr, NCOLS_NARROW), lambda i: (i, 0))],
        out_shape=[jax.ShapeDtypeStruct((d, NCOLS_WIDE), BF16), jax.ShapeDtypeStruct((d, NCOLS_NARROW), BF16)],
        compiler_params=_cparams(("parallel",)),
        name="regroup",
    )(w_in)


def _rope_tables(seq):
    inv =ROPE_THETA ** (-jnp.arange(0, ROPE_D, 2, dtype=F32) / ROPE_D)
    ang = jnp.arange(seq, dtype=F32)[:, None] * inv[None, :]
    cos, sin = jnp.cos(ang), jnp.sin(ang)
    z = lambda n: jnp.zeros((seq, n), F32)
    c = jnp.concatenate([jnp.ones((seq, NOPE), F32), cos, cos, z(LANES - NOPE - ROPE_D)], axis=1)
    s = jnp.concatenate([z(NOPE), -sin, sin, z(LANES - NOPE - ROPE_D)], axis=1)
    return c, s


def _pad_lanes(v, lo, total=LANES):
    return jnp.zeros((1, total), F32).at[0, lo:lo + v.shape[0]].set(v)


def _fox_selectors():
    selq = np.zeros((3, LANES, H * LANES), np.float32)
    selk = np.zeros((3, LANES, H * LANES), np.float32)
    oneq = np.zeros((1, H * LANES), np.float32)
    onek = np.zeros((1, H * LANES), np.float32)
    for h in range(H):
        for j in range(3):
            selq[j, h, h * LANES + HD + j] = 1.0
            selk[j, h, h * LANES + HD + 3 + j] = -1.0
            oneq[0, h * LANES + HD + 3 + j] = 1.0
            onek[0, h * LANES + HD + j] = 1.0
    stacked = lambda sel: jnp.asarray(sel.reshape(3 * LANES, H * LANES), BF16)
    return (stacked(selq), stacked(selk), jnp.asarray(oneq), jnp.asarray(onek))


def _tri(t):
    r = np.arange(t)
    return jnp.asarray((r[None, :] <= r[:, None]).astype(np.float32), BF16)


def _chunk_tri(t):
    r = np.arange(t)
    same = (r[None, :] // CHUNK) == (r[:, None] // CHUNK)
    lower = r[None, :] <= r[:, None]
    return (jnp.asarray((same & lower).astype(np.float32), BF16), jnp.asarray(same.astype(np.float32), BF16))


def _tile(seq, pref):
    return pref if seq % pref == 0 else seq


def kernel(x, norm_g, w_in, mla_qa_g, mla_w_uq, mla_kva_g, mla_w_ukv, mla_q_g, mla_knope_g, mla_krope_g, fox_b_f, fox_q_g, fox_k_g, rwkv_mu, rwkv_w0, rwkv_w_up, rwkv_a0, rwkv_a_up, rwkv_k_k, rwkv_k_a, rwkv_r_k, rwkv_lnx_g, rwkv_lnx_b, rwkv_v0, rwkv_v_down, rwkv_v_up, w_pa, w_pb, w_pc, w_out):
    batch, seq, d = x.shape
    depth = w_in.shape[0]
    n = batch * seq
    t_row = _tile(seq, ROW_TILE)
    t_in = _tile(n, INPROJ_TM)
    t_rwkv = _tile(seq, RWKV_TILE)
    assert seq % (ATTN_QT * t_row) == 0 and seq % t_rwkv == 0 and t_rwkv % CHUNK == 0, seq
    x2 = x.reshape(n, d)

    tabs = _rope_tables(seq)
    selq, selk, oneq, onek = _fox_selectors()
    tri = _tri(t_row)
    btri, bones = _chunk_tri(t_rwkv)
    row = lambda v: v.reshape(1, -1).astype(F32)

    half = ROPE_D // 2
    spare = LANES - NOPE - ROPE_D - half
    with_copy = lambda g, lo: jnp.concatenate([g, g[lo:lo + half]])
    vfirst = None
    for l in range(depth):
        w_wide, w_narrow = _regroup_w_in(w_in, l)
        pw, pn = _inproj(x2, row(norm_g[l]), w_wide, w_narrow, t_in, INPROJ_TN)

        wuq = mla_w_uq[l].reshape(Q_LORA, H, NOPE + ROPE_D)
        wuq = jnp.concatenate([wuq, wuq[:, :, NOPE:NOPE + half], jnp.zeros((Q_LORA, H, spare), F32)], axis=2)
        wuq = wuq.reshape(Q_LORA, H * LANES).astype(BF16)
        wukv = mla_w_ukv[l].reshape(KV_LORA, H, NOPE + HD)
        wk = jnp.pad(wukv[:, :, :NOPE], ((0, 0), (0, 0), (0, LANES - NOPE))).reshape(KV_LORA, H * LANES).astype(BF16)
        wv = wukv[:, :, NOPE:].reshape(KV_LORA, W_BR).T.astype(BF16)
        mla_scale = float(NOPE + ROPE_D) ** -0.5 * LOG2E
        mla_wts = (row(mla_qa_g[l]), wuq, row(mla_kva_g[l]), wk, wv,
                   _pad_lanes(with_copy(mla_q_g[l], NOPE) * mla_scale, 0), _pad_lanes(mla_knope_g[l], 0),
                   _pad_lanes(with_copy(mla_krope_g[l], 0), NOPE))
        qa, ka, va = _mla_prep(pn, tabs, mla_wts, seq, t_row)
        o_a = _attention(qa, ka, va, pw, COL_GATE_A, batch, seq, t_row)

        fox_wts = (_pad_lanes(fox_b_f[l], 0), row(jnp.tile(fox_q_g[l] * (float(HD) ** -0.5 * LOG2E), H)),
                   row(jnp.tile(fox_k_g[l], H)), tri, selq, selk, oneq, onek)
        qb, kb, vb = _fox_prep(pw, pn, fox_wts, batch, seq, t_row)
        o_b = _attention(qb, kb, vb, pw, COL_GATE_B, batch, seq, t_row)

        mu = rwkv_mu[l]
        wup = jnp.zeros((LANES, W_BR), F32).at[:DECAY_LORA].set(rwkv_w_up[l]).astype(BF16)
        aup = jnp.zeros((LANES, W_BR), F32).at[DECAY_LORA:].set(rwkv_a_up[l]).astype(BF16)
        rw = [row(mu[:W_BR]), row(mu[W_BR:2 * W_BR]), row(mu[2 * W_BR:3 * W_BR]), row(mu[3 * W_BR:]),
              row(rwkv_w0[l]), row(rwkv_a0[l]), wup, aup, row(rwkv_k_k[l]), row(rwkv_k_a[l]),
              row(rwkv_r_k[l])]
        if l > 0:
            vdown = jnp.zeros((W_BR, LANES), F32).at[:, :MV_LORA].set(rwkv_v_down[l - 1]).astype(BF16)
            vup = jnp.zeros((LANES, W_BR), F32).at[:MV_LORA].set(rwkv_v_up[l - 1]).astype(BF16)
            rw += [row(rwkv_v0[l - 1]), vdown, vup]
        rw += [btri, bones]
        outs = _rwkv(pw, pn, vfirst, rw, row(rwkv_lnx_g[l]), row(rwkv_lnx_b[l]), batch, seq, t_rwkv,
                     first_layer=(l == 0))
        o_c = outs[0]
        if l == 0:
            vfirst = outs[1]

        x2 = _merge(o_a, o_b, o_c, pw, x2, w_pa[l].astype(BF16), w_pb[l].astype(BF16),
                    w_pc[l].astype(BF16), w_out[l].astype(BF16), batch, seq, _tile(seq, INPROJ_TM))
    return x2.reshape(batch, seq, d)
```

```python
import functools
import math

import jax
import jax.numpy as jnp
import numpy as np
from jax import lax
from jax.experimental import pallas as pl
from jax.experimental.pallas import tpu as pltpu

F32 = jnp.float32
BF16 = jnp.bfloat16

LANES = 128
H = 8
HD = 64
NOPE = 64
ROPE_D = 32
Q_LORA = 256
KV_LORA = 128
D_MODEL = 1024
W_BR = H * HD
DECAY_LORA = 64
AAA_LORA = 64
MV_LORA = 32
ROPE_THETA = 10000.0
RWKV_DECAY_SCALE = 0.606531
GN_EPS = 64e-5
EPS = 1e-6
NEG_INF = -1e30
LOG2E = math.log2(math.e)
CHUNK = 64
ROW_TILE = 512
INPROJ_TM = 1024
RWKV_TILE = 256
ATTN_QT = 4
ONES_ROWS = 16

COL_MERGE = 0
COL_GATE_A = 3072
COL_FQ = 3584
COL_FK = 4096
COL_FV = 4608
COL_GATE_B = 5120
COL_SR = 5632
COL_SK = 6144
COL_SV = 6656
COL_GATE_C = 7168
NCOLS_WIDE = 7680
INPROJ_TN = 2560
COL_CQ = 0
COL_CKV = 256
COL_KR = 384
COL_FF = 512
COL_WA = 640
NCOLS_NARROW = 768

VMEM_LIMIT = 56 * 1024 * 1024


def _cparams(sem):
    return pltpu.CompilerParams(dimension_semantics=sem, vmem_limit_bytes=VMEM_LIMIT)


def _sigmoid(x):
    return 0.5 * jnp.tanh(0.5 * x) + 0.5


def _dot(a, b):
    return jnp.dot(a, b, preferred_element_type=F32)


def _split3(x):
    hi = x.astype(BF16)
    r1 = x - hi.astype(F32)
    mid = r1.astype(BF16)
    lo = (r1 - mid.astype(F32)).astype(BF16)
    return hi, mid, lo


def _dot_exact_lhs(m_bf16, x):
    hi, mid, lo = _split3(x)
    return _dot(m_bf16, hi) + _dot(m_bf16, mid) + _dot(m_bf16, lo)


def _pair_seg_sum(x):
    lane = lax.broadcasted_iota(jnp.int32, x.shape, 1)
    first = lane < HD
    s0 = jnp.sum(jnp.where(first, x, 0.0), axis=-1, keepdims=True)
    s1 = jnp.sum(jnp.where(first, 0.0, x), axis=-1, keepdims=True)
    return jnp.where(first, s0, s1)


def _inproj_kernel(x_ref, g_ref, w_ref, wn_ref, o_ref, on_ref, h_ref):
    @pl.when(pl.program_id(1) == 0)
    def _():
        x = x_ref[...]
        ms = jnp.mean(x * x, axis=-1, keepdims=True)
        h = (x * lax.rsqrt(ms + EPS) * g_ref[...]).astype(BF16)
        h_ref[...] = h
        on_ref[...] = _dot(h, wn_ref[...])

    o_ref[...] = _dot(h_ref[...], w_ref[...]).astype(o_ref.dtype)


def _inproj(x2, g, w_wide, w_narrow, tm, tn):
    n = x2.shape[0]
    return pl.pallas_call(
        _inproj_kernel,
        grid=(n // tm, NCOLS_WIDE // tn),
        in_specs=[
            pl.BlockSpec((tm, D_MODEL), lambda i, j: (i, 0)),
            pl.BlockSpec((1, D_MODEL), lambda i, j: (0, 0)),
            pl.BlockSpec((D_MODEL, tn), lambda i, j: (0, j)),
            pl.BlockSpec((D_MODEL, NCOLS_NARROW), lambda i, j: (0, 0)),
        ],
        out_specs=[pl.BlockSpec((tm, tn), lambda i, j: (i, j)),
                   pl.BlockSpec((tm, NCOLS_NARROW), lambda i, j: (i, 0))],
        out_shape=[jax.ShapeDtypeStruct((n, NCOLS_WIDE), BF16), jax.ShapeDtypeStruct((n, NCOLS_NARROW), F32)],
        scratch_shapes=[pltpu.VMEM((tm, D_MODEL), BF16)],
        compiler_params=_cparams(("parallel", "arbitrary")),
        name="inproj",
    )(x2, g, w_wide, w_narrow)


def _rms(x, g, n, keep=None):
    sq = x * x
    if keep is not None:
        sq = jnp.where(keep, sq, 0.0)
    ms = jnp.sum(sq, axis=-1, keepdims=True) / n
    return x * lax.rsqrt(ms + EPS) * g


def _rope(x, c, s):
    return x * c + pltpu.roll(x, LANES - ROPE_D // 2, 1) * s


def _mla_prep_kernel(cq_ref, ckv_ref, kr_ref, c_ref, s_ref, qag_ref, wuq_ref, kvag_ref, wk_ref, wv_ref,
                     qg_ref, kng_ref, krg_ref, q_out, k_out, vt_out):
    c, s = c_ref[...], s_ref[...]
    keep = lax.broadcasted_iota(jnp.int32, c.shape, 1) < NOPE + ROPE_D
    cqn = _rms(cq_ref[...], qag_ref[...], Q_LORA).astype(BF16)
    ckvn = _rms(ckv_ref[...], kvag_ref[...], KV_LORA).astype(BF16)
    q_all = _dot(cqn, wuq_ref[...])
    k_all = _dot(ckvn, wk_ref[...])
    vt_out[0, 0] = lax.dot_general(wv_ref[...], ckvn, _NT, preferred_element_type=F32).astype(BF16)
    k_rope = _rope(_rms(kr_ref[...], krg_ref[...], ROPE_D, keep), c, s)
    qg, kng = qg_ref[...], kng_ref[...]
    for h in range(H):
        sl = slice(h * LANES, (h + 1) * LANES)
        q_out[:, sl] = _rope(_rms(q_all[:, sl], qg, NOPE + ROPE_D, keep), c, s).astype(BF16)
        k_out[:, sl] = (_rms(k_all[:, sl], kng, NOPE) + k_rope).astype(BF16)


def _mla_prep(pn, tabs, wts, seq, t):
    n = pn.shape[0]
    ns = seq // t
    row = lambda i: (i, 0)
    const = lambda i: (0, 0)
    tab_spec = pl.BlockSpec((t, LANES), lambda i: (i % ns, 0))
    return pl.pallas_call(
        _mla_prep_kernel,
        grid=(n // t,),
        in_specs=[
            pl.BlockSpec((t, Q_LORA), lambda i: (i, COL_CQ // Q_LORA)),
            pl.BlockSpec((t, LANES), lambda i: (i, COL_CKV // LANES)),
            pl.BlockSpec((t, LANES), lambda i: (i, COL_KR // LANES)),
            tab_spec, tab_spec,
            pl.BlockSpec((1, Q_LORA), const),
            pl.BlockSpec((Q_LORA, H * LANES), const),
            pl.BlockSpec((1, KV_LORA), const),
            pl.BlockSpec((KV_LORA, H * LANES), const),
            pl.BlockSpec((W_BR, KV_LORA), const),
            pl.BlockSpec((1, LANES), const),
            pl.BlockSpec((1, LANES), const),
            pl.BlockSpec((1, LANES), const),
        ],
        out_specs=[
            pl.BlockSpec((t, H * LANES), row),
            pl.BlockSpec((t, H * LANES), row),
            pl.BlockSpec((1, 1, W_BR, t), lambda i: (i // ns, i % ns, 0, 0)),
        ],
        out_shape=[
            jax.ShapeDtypeStruct((n, H * LANES), BF16),
            jax.ShapeDtypeStruct((n, H * LANES), BF16),
            jax.ShapeDtypeStruct((n // seq, ns, W_BR, t), BF16),
        ],
        compiler_params=_cparams(("parallel",)),
        name="mla_prep",
    )(pn, pn, pn, *tabs, *wts)


def _fox_prep_kernel(fq_ref, fk_ref, fv_ref, ff_ref, bf_ref, qg_ref, kg_ref, tri_ref, selq_ref, selk_ref,
                     oneq_ref, onek_ref, q_out, k_out, vt_out, carry_ref):
    t = fq_ref.shape[0]

    @pl.when(pl.program_id(1) == 0)
    def _():
        carry_ref[...] = jnp.zeros_like(carry_ref)

    z = ff_ref[...] + bf_ref[...]
    logf = jnp.minimum(z, 0.0) - jnp.log1p(jnp.exp(-jnp.abs(z)))
    cum = carry_ref[...] + _dot_exact_lhs(tri_ref[...], logf)
    carry_ref[...] = cum[t - 1:t, :]
    hi, mid, lo = _split3(cum * LOG2E)
    terms = jnp.concatenate([hi, mid, lo], axis=1)
    aug_q = _dot(terms, selq_ref[...]) + oneq_ref[...]
    aug_k = _dot(terms, selk_ref[...]) + onek_ref[...]
    vt_out[0, 0] = fv_ref[...].astype(F32).T.astype(BF16)
    lane = lax.broadcasted_iota(jnp.int32, (t, LANES), 1)
    first = lane < HD
    for src_ref, g_ref, aug, dst in ((fq_ref, qg_ref, aug_q, q_out), (fk_ref, kg_ref, aug_k, k_out)):
        for p in range(H // 2):
            sl = slice(p * LANES, (p + 1) * LANES)
            x = src_ref[:, sl].astype(F32)
            ms = _pair_seg_sum(x * x) / HD
            xn = x * lax.rsqrt(ms + EPS) * g_ref[:, sl]
            he, ho = 2 * p, 2 * p + 1
            dst[:, he * LANES:(he + 1) * LANES] = jnp.where(first, xn, aug[:, he * LANES:(he + 1) * LANES]).astype(BF16)
            dst[:, ho * LANES:(ho + 1) * LANES] = jnp.where(
                first, pltpu.roll(xn, HD, 1), aug[:, ho * LANES:(ho + 1) * LANES]).astype(BF16)


def _fox_prep(pw, pn, wts, batch, seq, t):
    n = pw.shape[0]
    ns = seq // t
    row = lambda b, i: (b * ns + i, 0)
    const2 = lambda b, i: (0, 0)
    return pl.pallas_call(
        _fox_prep_kernel,
        grid=(batch, ns),
        in_specs=[
            pl.BlockSpec((t, W_BR), lambda b, i: (b * ns + i, COL_FQ // W_BR)),
            pl.BlockSpec((t, W_BR), lambda b, i: (b * ns + i, COL_FK // W_BR)),
            pl.BlockSpec((t, W_BR), lambda b, i: (b * ns + i, COL_FV // W_BR)),
            pl.BlockSpec((t, LANES), lambda b, i: (b * ns + i, COL_FF // LANES)),
            pl.BlockSpec((1, LANES), const2),
            pl.BlockSpec((1, W_BR), const2),
            pl.BlockSpec((1, W_BR), const2),
            pl.BlockSpec((t, t), const2),
            pl.BlockSpec((3 * LANES, H * LANES), const2),
            pl.BlockSpec((3 * LANES, H * LANES), const2),
            pl.BlockSpec((1, H * LANES), const2),
            pl.BlockSpec((1, H * LANES), const2),
        ],
        out_specs=[
            pl.BlockSpec((t, H * LANES), row),
            pl.BlockSpec((t, H * LANES), row),
            pl.BlockSpec((1, 1, W_BR, t), lambda b, i: (b, i, 0, 0)),
        ],
        out_shape=[
            jax.ShapeDtypeStruct((n, H * LANES), BF16),
            jax.ShapeDtypeStruct((n, H * LANES), BF16),
            jax.ShapeDtypeStruct((batch, ns, W_BR, t), BF16),
        ],
        scratch_shapes=[pltpu.VMEM((1, LANES), F32)],
        compiler_params=_cparams(("parallel", "arbitrary")),
        name="fox_prep",
    )(pw, pw, pw, pn, *wts)


def _attn_kernel(q_ref, k_ref, vt_ref, g_ref, o_ref, m_ref, acc_ref, sa_ref, sb_ref, mxa_ref, mxb_ref, *, tk):
    tq = ATTN_QT * tk
    gi = pl.program_id(2)
    nt = (((1,), (1,)), ((), ()))
    kv_pos = lax.broadcasted_iota(jnp.int32, (tk, tq), 0)
    q_pos = lax.broadcasted_iota(jnp.int32, (tk, tq), 1)
    causal = kv_pos <= q_pos
    both = slice(0, tq)
    ones = jnp.ones((ONES_ROWS, tk), BF16)
    m_ref[...] = jnp.full(m_ref.shape, NEG_INF, F32)
    acc_ref[...] = jnp.zeros(acc_ref.shape, F32)

    def scores(c, s_ref, mx_ref, cols):
        start = pl.multiple_of(c * tk, tk)
        for j in range(2):
            q = q_ref[cols, j * LANES:(j + 1) * LANES]
            k = k_ref[pl.ds(start, tk), j * LANES:(j + 1) * LANES]
            s = lax.dot_general(k, q, nt, preferred_element_type=F32)
            s_ref[j, :, cols] = s
            mx_ref[j, :, cols] = jnp.max(s, axis=0, keepdims=True)

    def consume(c, s_ref, mx_ref, cols, mask):
        for j in range(2):
            vt = jnp.concatenate([vt_ref[0, c, j * HD:(j + 1) * HD, :], ones], axis=0)
            s = s_ref[j, :, cols]
            if mask is not None:
                s = jnp.where(mask, s, NEG_INF)
                m_cur = jnp.max(s, axis=0, keepdims=True)
            else:
                m_cur = mx_ref[j, :, cols]
            m_old = m_ref[j, :, cols]
            m_new = jnp.maximum(m_old, m_cur)
            alpha = jnp.exp2(m_old - m_new)
            p = jnp.exp2(s - m_new)
            acc_ref[j, :, cols] = alpha * acc_ref[j, :, cols] + _dot(vt, p.astype(BF16))
            m_ref[j, :, cols] = m_new

    scores(0, sa_ref, mxa_ref, both)

    def body(i, carry):
        c = 2 * i
        scores(c + 1, sb_ref, mxb_ref, both)
        consume(c, sa_ref, mxa_ref, both, None)
        scores(c + 2, sa_ref, mxa_ref, both)
        consume(c + 1, sb_ref, mxb_ref, both, None)
        return carry

    lax.fori_loop(0, gi * (ATTN_QT // 2), body, 0)
    bufs = ((sa_ref, mxa_ref), (sb_ref, mxb_ref))
    for d in range(ATTN_QT):
        c = ATTN_QT * gi + d
        if d + 1 < ATTN_QT:
            scores(c + 1, *bufs[(d + 1) % 2], slice((d + 1) * tk, tq))
        consume(c, *bufs[d % 2], slice(d * tk, tq), causal[:, :tq - d * tk])

    o_t = jnp.concatenate([acc_ref[j, :HD, :] / acc_ref[j, HD:HD + 1, :] for j in range(2)], axis=0)
    g = g_ref[...].astype(F32)
    o_ref[...] = (o_t.T * (g * _sigmoid(g))).astype(BF16)


def _attention(q, k, vt, pw, gate_col, batch, seq, tk):
    n = q.shape[0]
    tq = ATTN_QT * tk
    nq, nk = seq // tq, seq // tk
    return pl.pallas_call(
        functools.partial(_attn_kernel, tk=tk),
        grid=(batch, H // 2, nq),
        in_specs=[
            pl.BlockSpec((tq, 2 * LANES), lambda b, hp, i: (b * nq + i, hp)),
            pl.BlockSpec((seq, 2 * LANES), lambda b, hp, i: (b, hp)),
            pl.BlockSpec((1, nk, LANES, tk), lambda b, hp, i: (b, 0, hp, 0)),
            pl.BlockSpec((tq, LANES), lambda b, hp, i: (b * nq + i, gate_col // LANES + hp)),
        ],
        out_specs=pl.BlockSpec((tq, LANES), lambda b, hp, i: (b * nq + i, hp)),
        out_shape=jax.ShapeDtypeStruct((n, W_BR), BF16),
        scratch_shapes=[
            pltpu.VMEM((2, 1, tq), F32),
            pltpu.VMEM((2, HD + ONES_ROWS, tq), F32),
            pltpu.VMEM((2, tk, tq), F32),
            pltpu.VMEM((2, tk, tq), F32),
            pltpu.VMEM((2, 1, tq), F32),
            pltpu.VMEM((2, 1, tq), F32),
        ],
        compiler_params=_cparams(("parallel", "parallel", "arbitrary")),
        name="attention",
    )(q, k, vt, pw)


def _rwkv_prep_kernel(*refs, first_layer):
    if first_layer:
        (sr_ref, sk_ref, sv_ref, swa_ref, gc_ref, mur_ref, muk_ref, muv_ref, muwa_ref, w0_ref, a0_ref, wup_ref,
         aup_ref, kk_ref, ka_ref, rk_ref, bt_ref, bo_ref,
         at_o, rt_o, bt_o, kt_o, v_o, bh_o, kh_o, wc_o, bonus_o, sg_o, vfirst_o,
         cr_ref, ck_ref, cv_ref, cwa_ref) = refs
    else:
        (sr_ref, sk_ref, sv_ref, swa_ref, gc_ref, vf_ref, mur_ref, muk_ref, muv_ref, muwa_ref, w0_ref, a0_ref,
         wup_ref, aup_ref, kk_ref, ka_ref, rk_ref, v0_ref, vdown_ref, vup_ref, bt_ref, bo_ref,
         at_o, rt_o, bt_o, kt_o, v_o, bh_o, kh_o, wc_o, bonus_o, sg_o,
         cr_ref, ck_ref, cv_ref, cwa_ref) = refs
    t = sr_ref.shape[0]

    def shift(x_ref, c_ref, mu_ref):
        x = x_ref[...].astype(F32)
        rowid = lax.broadcasted_iota(jnp.int32, x.shape, 0)
        prev = jnp.where(rowid == 0, c_ref[...], pltpu.roll(x, 1, 0))
        c_ref[...] = x[t - 1:t, :]
        return x + mu_ref[...] * (prev - x)

    r = shift(sr_ref, cr_ref, mur_ref)
    k = shift(sk_ref, ck_ref, muk_ref)
    v = shift(sv_ref, cv_ref, muv_ref)
    wa = shift(swa_ref, cwa_ref, muwa_ref)

    logw = -(RWKV_DECAY_SCALE * LOG2E) * _sigmoid(w0_ref[...] + _dot(jnp.tanh(wa).astype(BF16), wup_ref[...]))
    a = _sigmoid(a0_ref[...] + _dot(wa.astype(BF16), aup_ref[...]))
    if first_layer:
        vfirst_o[...] = v
    else:
        low = _dot(v.astype(BF16), vdown_ref[...]).astype(BF16)
        nu = _sigmoid(v0_ref[...] + _dot(low, vup_ref[...]))
        v = v + (vf_ref[...] - v) * nu

    kk = k * kk_ref[...]
    k_mod = k * (1.0 + (a - 1.0) * ka_ref[...])
    rk = r * k_mod * rk_ref[...]
    kk_n, bonus = [], []
    for p in range(H // 2):
        sl = slice(p * LANES, (p + 1) * LANES)
        nrm = jnp.sqrt(_pair_seg_sum(kk[:, sl] * kk[:, sl]))
        kk_n.append(kk[:, sl] / jnp.maximum(nrm, 1e-12))
        bonus.append(_pair_seg_sum(rk[:, sl]) * v[:, sl])
    kk = jnp.concatenate(kk_n, axis=-1)
    bonus = jnp.concatenate(bonus, axis=-1)
    a_vec = -kk
    b_vec = kk * a

    terms = _split3(logw)
    lcum = sum(_dot(bt_ref[...], x) for x in terms)
    ltot = sum(_dot(bo_ref[...], x) for x in terms)
    e_minus = jnp.exp2(-lcum)
    e_rem = jnp.exp2(ltot - lcum)
    g = gc_ref[...].astype(F32)
    for o_ref, val in ((at_o, a_vec * jnp.exp2(lcum - logw)), (rt_o, r * jnp.exp2(lcum)), (bt_o, b_vec * e_minus),
                       (kt_o, k_mod * e_minus), (v_o, v), (bh_o, b_vec * e_rem), (kh_o, k_mod * e_rem),
                       (wc_o, jnp.exp2(ltot)), (bonus_o, bonus), (sg_o, g * _sigmoid(g))):
        o_ref[...] = val.astype(o_ref.dtype)


def _mm(a, b, dims):
    return lax.dot_general(a.astype(BF16), b.astype(BF16), dims, preferred_element_type=F32)


_NN = (((1,), (0,)), ((), ()))
_NT = (((1,), (1,)), ((), ()))
_TN = (((0,), (0,)), ((), ()))


def _rwkv_chunk_kernel(at_ref, rt_ref, bt_ref, kt_ref, v_ref, bh_ref, kh_ref, wc_ref, bonus_ref, sg_ref,
                       lg_ref, lb_ref, o_ref, s_all_ref, *, nchunk):
    mm = _mm
    first = lax.broadcasted_iota(jnp.int32, (CHUNK, LANES), 1) < HD
    r2 = lax.broadcasted_iota(jnp.int32, (LANES, LANES), 0)
    c2 = lax.broadcasted_iota(jnp.int32, (LANES, LANES), 1)
    same_head = (r2 < HD) == (c2 < HD)
    eye2 = r2 == c2
    t_row, t_col = r2 % CHUNK, c2 % CHUNK
    strict2 = jnp.logical_and(same_head, t_col < t_row)
    incl2 = jnp.logical_and(same_head, t_col <= t_row)
    each = lambda f, *cols: [f(*xs) for xs in zip(*cols)]
    pairs = [(c, p) for c in range(nchunk) for p in range(H // 2)]
    npair = range(len(pairs))
    ld = lambda ref: [ref[c * CHUNK:(c + 1) * CHUNK, p * LANES:(p + 1) * LANES] for c, p in pairs]
    at, rt, bt, kt, v, bh, kh = ld(at_ref), ld(rt_ref), ld(bt_ref), ld(kt_ref), ld(v_ref), ld(bh_ref), ld(kh_ref)
    zero = jnp.zeros((CHUNK, LANES), BF16)
    stack = lambda x: jnp.concatenate([jnp.where(first, x, zero), jnp.where(first, zero, x)], axis=0)
    fold = lambda x: x[:CHUNK] + x[CHUNK:]
    a_s, b_s, k_s, v_s = each(stack, at), each(stack, bt), each(stack, kt), each(stack, v)
    x1 = each(lambda a, r: jnp.concatenate([a, stack(r.astype(BF16))], axis=0), a_s, rt)
    gb = each(lambda x, b: mm(x, b, _NT), x1, b_s)
    gk = each(lambda x, k: mm(x, k, _NT), x1, k_s)
    a_ab = [jnp.where(strict2, g[:LANES], 0.0) for g in gb]
    a_rb = [jnp.where(incl2, g[LANES:], 0.0) for g in gb]
    a_ak = [jnp.where(strict2, g[:LANES], 0.0) for g in gk]
    a_rk = [jnp.where(incl2, g[LANES:], 0.0) for g in gk]
    tinv = [jnp.where(eye2, 1.0, a) for a in a_ab]
    pw = a_ab
    for _ in range(int(math.log2(CHUNK)) - 1):
        pw = each(lambda p: mm(p, p, _NN), pw)
        tinv = each(lambda t, p: t + mm(t, p, _NN), tinv, pw)
    av = each(lambda a, x: mm(a, x, _NN), a_ak, v_s)
    tx = each(lambda t, x, a: mm(t, jnp.concatenate([x.astype(BF16), a], axis=1), _NN), tinv, av, a_s)
    ry = each(lambda a, x: mm(a, x, _NN), a_rb, tx)
    yk = each(lambda a, x: mm(a, x, _NN), a_rk, v_s)
    u = [fold(t[:, :LANES]) for t in tx]
    a_til = [fold(t[:, LANES:]) for t in tx]
    y_in = [fold(r[:, :LANES] + k) for r, k in zip(ry, yk)]
    r_hat = [r + fold(x[:, LANES:]) for r, x in zip(rt, ry)]
    m_new = [jnp.where(same_head, mm(a_til[i], bh[i], _TN), 0.0) for i in npair]
    n_new = [jnp.where(same_head, mm(jnp.concatenate([u[i].astype(BF16), v[i]], axis=0),
                                     jnp.concatenate([bh[i], kh[i]], axis=0), _TN), 0.0) for i in npair]
    lg, lb = lg_ref[...], lb_ref[...]
    for c in range(nchunk):
        idx = [i for i, (ci, _) in enumerate(pairs) if ci == c]
        s0 = [s_all_ref[p] for p in range(H // 2)]
        y = [y_in[i] + mm(r_hat[i], s0[p], _NT) for p, i in enumerate(idx)]
        for p, i in enumerate(idx):
            wc = wc_ref[c * CHUNK:c * CHUNK + 1, p * LANES:(p + 1) * LANES]
            s_all_ref[p] = mm(s0[p], jnp.where(eye2, wc, 0.0) + m_new[i], _NN) + n_new[i]
        for p, i in enumerate(idx):
            rows, lanes = slice(c * CHUNK, (c + 1) * CHUNK), slice(p * LANES, (p + 1) * LANES)
            mu = _pair_seg_sum(y[p]) / HD
            d = y[p] - mu
            var = _pair_seg_sum(d * d) / HD
            yn = d * lax.rsqrt(var + GN_EPS) * lg[:, lanes] + lb[:, lanes]
            o_ref[rows, lanes] = ((yn + bonus_ref[rows, lanes]) * sg_ref[rows, lanes]).astype(o_ref.dtype)


_RWKV_STAGE_DTYPES = (BF16, F32, BF16, BF16, BF16, BF16, BF16, F32, F32, F32)


def _rwkv_kernel(*refs, first_layer, nchunk, ns):
    n_in = 18 if first_layer else 22
    ins = refs[:n_in]
    lg_ref, lb_ref, o_ref = refs[n_in:n_in + 3]
    pos = n_in + 3
    vfirst = refs[pos:pos + 1] if first_layer else ()
    pos += len(vfirst)
    stage_a, stage_b = refs[pos:pos + 10], refs[pos + 10:pos + 20]
    carries = refs[pos + 20:pos + 24]
    s_all_ref = refs[pos + 24]
    i = pl.program_id(0)

    def zero(rs):
        for r in rs:
            r[...] = jnp.zeros(r.shape, r.dtype)

    @pl.when(i == 0)
    def _():
        zero(stage_b)

    @pl.when(i % ns == 0)
    def _():
        zero(carries)

    @pl.when((i + ns - 1) % ns == 0)
    def _():
        zero((s_all_ref,))

    def run(done, todo):
        _rwkv_prep_kernel(*ins, *todo, *vfirst, *carries, first_layer=first_layer)
        _rwkv_chunk_kernel(*done, lg_ref, lb_ref, o_ref, s_all_ref, nchunk=nchunk)

    @pl.when(i % 2 == 0)
    def _():
        run(stage_b, stage_a)

    @pl.when(i % 2 == 1)
    def _():
        run(stage_a, stage_b)


def _rwkv(pw, pn, vfirst, wts, lnx_g, lnx_b, batch, seq, t, first_layer):
    n = pw.shape[0]
    ns = seq // t
    nb = batch * ns
    const2 = lambda i: (0, 0)
    cur = lambda i: jnp.minimum(i, nb - 1)
    pcol = lambda col, w: pl.BlockSpec((t, w), lambda i: (cur(i), col // w))
    vec = pl.BlockSpec((1, W_BR), const2)
    in_specs = [pcol(COL_SR, W_BR), pcol(COL_SK, W_BR), pcol(COL_SV, W_BR), pcol(COL_WA, LANES),
                pcol(COL_GATE_C, W_BR)]
    args = [pw, pw, pw, pn, pw]
    if not first_layer:
        in_specs.append(pl.BlockSpec((t, W_BR), lambda i: (cur(i), 0)))
        args.append(vfirst)
    in_specs += [vec, vec, vec, pl.BlockSpec((1, LANES), const2), vec, vec,
                 pl.BlockSpec((LANES, W_BR), const2), pl.BlockSpec((LANES, W_BR), const2), vec, vec, vec]
    if not first_layer:
        in_specs += [vec, pl.BlockSpec((W_BR, LANES), const2), pl.BlockSpec((LANES, W_BR), const2)]
    in_specs += [pl.BlockSpec((t, t), const2)] * 2 + [vec, vec]
    args += list(wts) + [lnx_g, lnx_b]
    out_specs = [pl.BlockSpec((t, W_BR), lambda i: (jnp.maximum(i - 1, 0), 0))]
    out_shape = [jax.ShapeDtypeStruct((n, W_BR), BF16)]
    if first_layer:
        out_specs.append(pl.BlockSpec((t, W_BR), lambda i: (i, 0)))
        out_shape.append(jax.ShapeDtypeStruct((n + t, W_BR), F32))
    return pl.pallas_call(
        functools.partial(_rwkv_kernel, first_layer=first_layer, nchunk=t // CHUNK, ns=ns),
        grid=(nb + 1,),
        in_specs=in_specs,
        out_specs=out_specs,
        out_shape=out_shape,
        scratch_shapes=[pltpu.VMEM((t, W_BR), dt) for dt in _RWKV_STAGE_DTYPES * 2]
        + [pltpu.VMEM((1, W_BR), F32), pltpu.VMEM((1, W_BR), F32), pltpu.VMEM((1, W_BR), F32),
           pltpu.VMEM((1, LANES), F32), pltpu.VMEM((H // 2, LANES, LANES), F32)],
        compiler_params=_cparams(("arbitrary",)),
        name="rwkv",
    )(*args)


def _merge_kernel(oa_ref, ob_ref, oc_ref, ga_ref, gb_ref, gc_ref, x_ref, wpa_ref, wpb_ref, wpc_ref, wout_ref,
                  o_ref):
    pa = _dot(oa_ref[...], wpa_ref[...])
    pb = _dot(ob_ref[...], wpb_ref[...])
    pc = _dot(oc_ref[...], wpc_ref[...])
    sig = lambda ref: _sigmoid(ref[...].astype(F32))
    merged = sig(ga_ref) * pa + sig(gb_ref) * pb + sig(gc_ref) * pc
    o_ref[...] = x_ref[...] + _dot(merged.astype(BF16), wout_ref[...])


def _merge(oa, ob, oc, p, x2, wpa, wpb, wpc, wout, batch, seq, t):
    n = x2.shape[0]
    ns = seq // t
    row = lambda b, i: (b * ns + i, 0)
    const2 = lambda b, i: (0, 0)
    gate = lambda g: pl.BlockSpec((t, D_MODEL), lambda b, i: (b * ns + i, COL_MERGE // D_MODEL + g))
    return pl.pallas_call(
        _merge_kernel,
        grid=(batch, ns),
        in_specs=[
            pl.BlockSpec((t, W_BR), row),
            pl.BlockSpec((t, W_BR), row),
            pl.BlockSpec((t, W_BR), row),
            gate(0), gate(1), gate(2),
            pl.BlockSpec((t, D_MODEL), row),
            pl.BlockSpec((W_BR, D_MODEL), const2),
            pl.BlockSpec((W_BR, D_MODEL), const2),
            pl.BlockSpec((W_BR, D_MODEL), const2),
            pl.BlockSpec((D_MODEL, D_MODEL), const2),
        ],
        out_specs=pl.BlockSpec((t, D_MODEL), row),
        out_shape=jax.ShapeDtypeStruct((n, D_MODEL), F32),
        compiler_params=_cparams(("parallel", "parallel")),
        name="merge",
    )(oa, ob, oc, p, p, p, x2, wpa, wpb, wpc, wout)


def _regroup_pieces():
    sizes = (Q_LORA, KV_LORA, ROPE_D, W_BR, W_BR, W_BR, W_BR, H, W_BR,
             3 * W_BR + DECAY_LORA + AAA_LORA, W_BR, 3 * D_MODEL)
    o = [int(v) for v in np.concatenate([[0], np.cumsum(sizes)])]
    return (
        ("wide", COL_MERGE, o[11], 3 * D_MODEL), ("wide", COL_GATE_A, o[3], 4 * W_BR),
        ("wide", COL_GATE_B, o[8], W_BR), ("wide", COL_SR, o[9], 3 * W_BR), ("wide", COL_GATE_C, o[10], W_BR),
        ("narrow", COL_CQ, o[0], Q_LORA), ("narrow", COL_CKV, o[1], KV_LORA),
        ("narrow", COL_KR + NOPE, o[2], ROPE_D), ("narrow", COL_KR + NOPE + ROPE_D, o[2], ROPE_D // 2),
        ("narrow", COL_FF, o[7], H), ("narrow", COL_WA, o[9] + 3 * W_BR, DECAY_LORA + AAA_LORA),
    )


def _regroup_kernel(w_ref, wide_ref, narrow_ref):
    narrow_ref[...] = jnp.zeros(narrow_ref.shape, narrow_ref.dtype)
    dst = {"wide": wide_ref, "narrow": narrow_ref}
    for name, d0, s0, width in _regroup_pieces():
        dst[name][:, d0:d0 + width] = w_ref[0, :, s0:s0 + width].astype(BF16)


def _regroup_w_in(w_in, l, tr=256):
    d, cols = w_in.shape[1], w_in.shape[2]
    return pl.pallas_call(
        _regroup_kernel,
        grid=(d // tr,),
        in_specs=[pl.BlockSpec((1, tr, cols), lambda i: (l, i, 0))],
        out_specs=[pl.BlockSpec((tr, NCOLS_WIDE), lambda i: (i, 0)), pl.BlockSpec((tr, NCOLS_NARROW), lambda i: (i, 0))],
        out_shape=[jax.ShapeDtypeStruct((d, NCOLS_WIDE), BF16), jax.ShapeDtypeStruct((d, NCOLS_NARROW), BF16)],
        compiler_params=_cparams(("parallel",)),
        name="regroup",
    )(w_in)


def _rope_tables(seq):
    inv =ROPE_THETA ** (-jnp.arange(0, ROPE_D, 2, dtype=F32) / ROPE_D)
    ang = jnp.arange(seq, dtype=F32)[:, None] * inv[None, :]
    cos, sin = jnp.cos(ang), jnp.sin(ang)
    z = lambda n: jnp.zeros((seq, n), F32)
    c = jnp.concatenate([jnp.ones((seq, NOPE), F32), cos, cos, z(LANES - NOPE - ROPE_D)], axis=1)
    s = jnp.concatenate([z(NOPE), -sin, sin, z(LANES - NOPE - ROPE_D)], axis=1)
    return c, s


def _pad_lanes(v, lo, total=LANES):
    return jnp.zeros((1, total), F32).at[0, lo:lo + v.shape[0]].set(v)


def _fox_selectors():
    selq = np.zeros((3, LANES, H * LANES), np.float32)
    selk = np.zeros((3, LANES, H * LANES), np.float32)
    oneq = np.zeros((1, H * LANES), np.float32)
    onek = np.zeros((1, H * LANES), np.float32)
    for h in range(H):
        for j in range(3):
            selq[j, h, h * LANES + HD + j] = 1.0
            selk[j, h, h * LANES + HD + 3 + j] = -1.0
            oneq[0, h * LANES + HD + 3 + j] = 1.0
            onek[0, h * LANES + HD + j] = 1.0
    stacked = lambda sel: jnp.asarray(sel.reshape(3 * LANES, H * LANES), BF16)
    return (stacked(selq), stacked(selk), jnp.asarray(oneq), jnp.asarray(onek))


def _tri(t):
    r = np.arange(t)
    return jnp.asarray((r[None, :] <= r[:, None]).astype(np.float32), BF16)


def _chunk_tri(t):
    r = np.arange(t)
    same = (r[None, :] // CHUNK) == (r[:, None] // CHUNK)
    lower = r[None, :] <= r[:, None]
    return (jnp.asarray((same & lower).astype(np.float32), BF16), jnp.asarray(same.astype(np.float32), BF16))


def _tile(seq, pref):
    return pref if seq % pref == 0 else seq


def kernel(x, norm_g, w_in, mla_qa_g, mla_w_uq, mla_kva_g, mla_w_ukv, mla_q_g, mla_knope_g, mla_krope_g, fox_b_f, fox_q_g, fox_k_g, rwkv_mu, rwkv_w0, rwkv_w_up, rwkv_a0, rwkv_a_up, rwkv_k_k, rwkv_k_a, rwkv_r_k, rwkv_lnx_g, rwkv_lnx_b, rwkv_v0, rwkv_v_down, rwkv_v_up, w_pa, w_pb, w_pc, w_out):
    batch, seq, d = x.shape
    depth = w_in.shape[0]
    n = batch * seq
    t_row = _tile(seq, ROW_TILE)
    t_in = _tile(n, INPROJ_TM)
    t_rwkv = _tile(seq, RWKV_TILE)
    assert seq % (ATTN_QT * t_row) == 0 and seq % t_rwkv == 0 and t_rwkv % CHUNK == 0, seq
    x2 = x.reshape(n, d)

    tabs = _rope_tables(seq)
    selq, selk, oneq, onek = _fox_selectors()
    tri = _tri(t_row)
    btri, bones = _chunk_tri(t_rwkv)
    row = lambda v: v.reshape(1, -1).astype(F32)

    half = ROPE_D // 2
    spare = LANES - NOPE - ROPE_D - half
    with_copy = lambda g, lo: jnp.concatenate([g, g[lo:lo + half]])
    vfirst = None
    for l in range(depth):
        w_wide, w_narrow = _regroup_w_in(w_in, l)
        pw, pn = _inproj(x2, row(norm_g[l]), w_wide, w_narrow, t_in, INPROJ_TN)

        wuq = mla_w_uq[l].reshape(Q_LORA, H, NOPE + ROPE_D)
        wuq = jnp.concatenate([wuq, wuq[:, :, NOPE:NOPE + half], jnp.zeros((Q_LORA, H, spare), F32)], axis=2)
        wuq = wuq.reshape(Q_LORA, H * LANES).astype(BF16)
        wukv = mla_w_ukv[l].reshape(KV_LORA, H, NOPE + HD)
        wk = jnp.pad(wukv[:, :, :NOPE], ((0, 0), (0, 0), (0, LANES - NOPE))).reshape(KV_LORA, H * LANES).astype(BF16)
        wv = wukv[:, :, NOPE:].reshape(KV_LORA, W_BR).T.astype(BF16)
        mla_scale = float(NOPE + ROPE_D) ** -0.5 * LOG2E
        mla_wts = (row(mla_qa_g[l]), wuq, row(mla_kva_g[l]), wk, wv,
                   _pad_lanes(with_copy(mla_q_g[l], NOPE) * mla_scale, 0), _pad_lanes(mla_knope_g[l], 0),
                   _pad_lanes(with_copy(mla_krope_g[l], 0), NOPE))
        qa, ka, va = _mla_prep(pn, tabs, mla_wts, seq, t_row)
        o_a = _attention(qa, ka, va, pw, COL_GATE_A, batch, seq, t_row)

        fox_wts = (_pad_lanes(fox_b_f[l], 0), row(jnp.tile(fox_q_g[l] * (float(HD) ** -0.5 * LOG2E), H)),
                   row(jnp.tile(fox_k_g[l], H)), tri, selq, selk, oneq, onek)
        qb, kb, vb = _fox_prep(pw, pn, fox_wts, batch, seq, t_row)
        o_b = _attention(qb, kb, vb, pw, COL_GATE_B, batch, seq, t_row)

        mu = rwkv_mu[l]
        wup = jnp.zeros((LANES, W_BR), F32).at[:DECAY_LORA].set(rwkv_w_up[l]).astype(BF16)
        aup = jnp.zeros((LANES, W_BR), F32).at[DECAY_LORA:].set(rwkv_a_up[l]).astype(BF16)
        rw = [row(mu[:W_BR]), row(mu[W_BR:2 * W_BR]), row(mu[2 * W_BR:3 * W_BR]), row(mu[3 * W_BR:]),
              row(rwkv_w0[l]), row(rwkv_a0[l]), wup, aup, row(rwkv_k_k[l]), row(rwkv_k_a[l]),
              row(rwkv_r_k[l])]
        if l > 0:
            vdown = jnp.zeros((W_BR, LANES), F32).at[:, :MV_LORA].set(rwkv_v_down[l - 1]).astype(BF16)
            vup = jnp.zeros((LANES, W_BR), F32).at[:MV_LORA].set(rwkv_v_up[l - 1]).astype(BF16)
            rw += [row(rwkv_v0[l - 1]), vdown, vup]
        rw += [btri, bones]
        outs = _rwkv(pw, pn, vfirst, rw, row(rwkv_lnx_g[l]), row(rwkv_lnx_b[l]), batch, seq, t_rwkv,
                     first_layer=(l == 0))
        o_c = outs[0]
        if l == 0:
            vfirst = outs[1]

        x2 = _merge(o_a, o_b, o_c, pw, x2, w_pa[l].astype(BF16), w_pb[l].astype(BF16),
                    w_pc[l].astype(BF16), w_out[l].astype(BF16), batch, seq, _tile(seq, INPROJ_TM))
    return x2.reshape(batch, seq, d)
```

```python
import functools
import math

import jax
import jax.numpy as jnp
import numpy as np
from jax import lax
from jax.experimental import pallas as pl
from jax.experimental.pallas import tpu as pltpu

F32 = jnp.float32
BF16 = jnp.bfloat16

LANES = 128
H = 8
HD = 64
NOPE = 64
ROPE_D = 32
Q_LORA = 256
KV_LORA = 128
D_MODEL = 1024
W_BR = H * HD
DECAY_LORA = 64
AAA_LORA = 64
MV_LORA = 32
ROPE_THETA = 10000.0
RWKV_DECAY_SCALE = 0.606531
GN_EPS = 64e-5
EPS = 1e-6
NEG_INF = -1e30
LOG2E = math.log2(math.e)
CHUNK = 64
ROW_TILE = 512
INPROJ_TM = 1024
RWKV_TILE = 256
ATTN_QT = 4
ONES_ROWS = 16

COL_MERGE = 0
COL_GATE_A = 3072
COL_FQ = 3584
COL_FK = 4096
COL_FV = 4608
COL_GATE_B = 5120
COL_SR = 5632
COL_SK = 6144
COL_SV = 6656
COL_GATE_C = 7168
NCOLS_WIDE = 7680
INPROJ_TN = 3840
COL_CQ = 0
COL_CKV = 256
COL_KR = 384
COL_FF = 512
COL_WA = 640
NCOLS_NARROW = 768

VMEM_LIMIT = 56 * 1024 * 1024


def _cparams(sem):
    return pltpu.CompilerParams(dimension_semantics=sem, vmem_limit_bytes=VMEM_LIMIT)


def _sigmoid(x):
    return 0.5 * jnp.tanh(0.5 * x) + 0.5


def _dot(a, b):
    return jnp.dot(a, b, preferred_element_type=F32)


def _split3(x):
    hi = x.astype(BF16)
    r1 = x - hi.astype(F32)
    mid = r1.astype(BF16)
    lo = (r1 - mid.astype(F32)).astype(BF16)
    return hi, mid, lo


def _dot_exact_lhs(m_bf16, x):
    hi, mid, lo = _split3(x)
    return _dot(m_bf16, hi) + _dot(m_bf16, mid) + _dot(m_bf16, lo)


def _pair_seg_sum(x):
    lane = lax.broadcasted_iota(jnp.int32, x.shape, 1)
    first = lane < HD
    s0 = jnp.sum(jnp.where(first, x, 0.0), axis=-1, keepdims=True)
    s1 = jnp.sum(jnp.where(first, 0.0, x), axis=-1, keepdims=True)
    return jnp.where(first, s0, s1)


def _inproj_kernel(x_ref, g_ref, w_ref, wn_ref, o_ref, on_ref, h_ref):
    @pl.when(pl.program_id(1) == 0)
    def _():
        x = x_ref[...]
        ms = jnp.mean(x * x, axis=-1, keepdims=True)
        h = (x * lax.rsqrt(ms + EPS) * g_ref[...]).astype(BF16)
        h_ref[...] = h
        on_ref[...] = _dot(h, wn_ref[...])

    o_ref[...] = _dot(h_ref[...], w_ref[...]).astype(o_ref.dtype)


def _inproj(x2, g, w_wide, w_narrow, tm, tn):
    n = x2.shape[0]
    return pl.pallas_call(
        _inproj_kernel,
        grid=(n // tm, NCOLS_WIDE // tn),
        in_specs=[
            pl.BlockSpec((tm, D_MODEL), lambda i, j: (i, 0)),
            pl.BlockSpec((1, D_MODEL), lambda i, j: (0, 0)),
            pl.BlockSpec((D_MODEL, tn), lambda i, j: (0, j)),
            pl.BlockSpec((D_MODEL, NCOLS_NARROW), lambda i, j: (0, 0)),
        ],
        out_specs=[pl.BlockSpec((tm, tn), lambda i, j: (i, j)),
                   pl.BlockSpec((tm, NCOLS_NARROW), lambda i, j: (i, 0))],
        out_shape=[jax.ShapeDtypeStruct((n, NCOLS_WIDE), BF16), jax.ShapeDtypeStruct((n, NCOLS_NARROW), F32)],
        scratch_shapes=[pltpu.VMEM((tm, D_MODEL), BF16)],
        compiler_params=_cparams(("parallel", "arbitrary")),
        name="inproj",
    )(x2, g, w_wide, w_narrow)


def _rms(x, g, n, keep=None):
    sq = x * x
    if keep is not None:
        sq = jnp.where(keep, sq, 0.0)
    ms = jnp.sum(sq, axis=-1, keepdims=True) / n
    return x * lax.rsqrt(ms + EPS) * g


def _rope(x, c, s):
    return x * c + pltpu.roll(x, LANES - ROPE_D // 2, 1) * s


def _mla_prep_kernel(cq_ref, ckv_ref, kr_ref, c_ref, s_ref, qag_ref, wuq_ref, kvag_ref, wk_ref, wv_ref,
                     qg_ref, kng_ref, krg_ref, q_out, k_out, vt_out):
    c, s = c_ref[...], s_ref[...]
    keep = lax.broadcasted_iota(jnp.int32, c.shape, 1) < NOPE + ROPE_D
    cqn = _rms(cq_ref[...], qag_ref[...], Q_LORA).astype(BF16)
    ckvn = _rms(ckv_ref[...], kvag_ref[...], KV_LORA).astype(BF16)
    q_all = _dot(cqn, wuq_ref[...])
    k_all = _dot(ckvn, wk_ref[...])
    vt_out[0, 0] = lax.dot_general(wv_ref[...], ckvn, _NT, preferred_element_type=F32).astype(BF16)
    k_rope = _rope(_rms(kr_ref[...], krg_ref[...], ROPE_D, keep), c, s)
    qg, kng = qg_ref[...], kng_ref[...]
    for h in range(H):
        sl = slice(h * LANES, (h + 1) * LANES)
        q_out[:, sl] = _rope(_rms(q_all[:, sl], qg, NOPE + ROPE_D, keep), c, s).astype(BF16)
        k_out[:, sl] = (_rms(k_all[:, sl], kng, NOPE) + k_rope).astype(BF16)


def _mla_prep(pn, tabs, wts, seq, t):
    n = pn.shape[0]
    ns = seq // t
    row = lambda i: (i, 0)
    const = lambda i: (0, 0)
    tab_spec = pl.BlockSpec((t, LANES), lambda i: (i % ns, 0))
    return pl.pallas_call(
        _mla_prep_kernel,
        grid=(n // t,),
        in_specs=[
            pl.BlockSpec((t, Q_LORA), lambda i: (i, COL_CQ // Q_LORA)),
            pl.BlockSpec((t, LANES), lambda i: (i, COL_CKV // LANES)),
            pl.BlockSpec((t, LANES), lambda i: (i, COL_KR // LANES)),
            tab_spec, tab_spec,
            pl.BlockSpec((1, Q_LORA), const),
            pl.BlockSpec((Q_LORA, H * LANES), const),
            pl.BlockSpec((1, KV_LORA), const),
            pl.BlockSpec((KV_LORA, H * LANES), const),
            pl.BlockSpec((W_BR, KV_LORA), const),
            pl.BlockSpec((1, LANES), const),
            pl.BlockSpec((1, LANES), const),
            pl.BlockSpec((1, LANES), const),
        ],
        out_specs=[
            pl.BlockSpec((t, H * LANES), row),
            pl.BlockSpec((t, H * LANES), row),
            pl.BlockSpec((1, 1, W_BR, t), lambda i: (i // ns, i % ns, 0, 0)),
        ],
        out_shape=[
            jax.ShapeDtypeStruct((n, H * LANES), BF16),
            jax.ShapeDtypeStruct((n, H * LANES), BF16),
            jax.ShapeDtypeStruct((n // seq, ns, W_BR, t), BF16),
        ],
        compiler_params=_cparams(("parallel",)),
        name="mla_prep",
    )(pn, pn, pn, *tabs, *wts)


def _fox_prep_kernel(fq_ref, fk_ref, fv_ref, ff_ref, bf_ref, qg_ref, kg_ref, tri_ref, selq_ref, selk_ref,
                     oneq_ref, onek_ref, q_out, k_out, vt_out, carry_ref):
    t = fq_ref.shape[0]

    @pl.when(pl.program_id(1) == 0)
    def _():
        carry_ref[...] = jnp.zeros_like(carry_ref)

    z = ff_ref[...] + bf_ref[...]
    logf = jnp.minimum(z, 0.0) - jnp.log1p(jnp.exp(-jnp.abs(z)))
    cum = carry_ref[...] + _dot_exact_lhs(tri_ref[...], logf)
    carry_ref[...] = cum[t - 1:t, :]
    hi, mid, lo = _split3(cum * LOG2E)
    terms = jnp.concatenate([hi, mid, lo], axis=1)
    aug_q = _dot(terms, selq_ref[...]) + oneq_ref[...]
    aug_k = _dot(terms, selk_ref[...]) + onek_ref[...]
    vt_out[0, 0] = fv_ref[...].astype(F32).T.astype(BF16)
    lane = lax.broadcasted_iota(jnp.int32, (t, LANES), 1)
    first = lane < HD
    for src_ref, g_ref, aug, dst in ((fq_ref, qg_ref, aug_q, q_out), (fk_ref, kg_ref, aug_k, k_out)):
        for p in range(H // 2):
            sl = slice(p * LANES, (p + 1) * LANES)
            x = src_ref[:, sl].astype(F32)
            ms = _pair_seg_sum(x * x) / HD
            xn = x * lax.rsqrt(ms + EPS) * g_ref[:, sl]
            he, ho = 2 * p, 2 * p + 1
            dst[:, he * LANES:(he + 1) * LANES] = jnp.where(first, xn, aug[:, he * LANES:(he + 1) * LANES]).astype(BF16)
            dst[:, ho * LANES:(ho + 1) * LANES] = jnp.where(
                first, pltpu.roll(xn, HD, 1), aug[:, ho * LANES:(ho + 1) * LANES]).astype(BF16)


def _fox_prep(pw, pn, wts, batch, seq, t):
    n = pw.shape[0]
    ns = seq // t
    row = lambda b, i: (b * ns + i, 0)
    const2 = lambda b, i: (0, 0)
    return pl.pallas_call(
        _fox_prep_kernel,
        grid=(batch, ns),
        in_specs=[
            pl.BlockSpec((t, W_BR), lambda b, i: (b * ns + i, COL_FQ // W_BR)),
            pl.BlockSpec((t, W_BR), lambda b, i: (b * ns + i, COL_FK // W_BR)),
            pl.BlockSpec((t, W_BR), lambda b, i: (b * ns + i, COL_FV // W_BR)),
            pl.BlockSpec((t, LANES), lambda b, i: (b * ns + i, COL_FF // LANES)),
            pl.BlockSpec((1, LANES), const2),
            pl.BlockSpec((1, W_BR), const2),
            pl.BlockSpec((1, W_BR), const2),
            pl.BlockSpec((t, t), const2),
            pl.BlockSpec((3 * LANES, H * LANES), const2),
            pl.BlockSpec((3 * LANES, H * LANES), const2),
            pl.BlockSpec((1, H * LANES), const2),
            pl.BlockSpec((1, H * LANES), const2),
        ],
        out_specs=[
            pl.BlockSpec((t, H * LANES), row),
            pl.BlockSpec((t, H * LANES), row),
            pl.BlockSpec((1, 1, W_BR, t), lambda b, i: (b, i, 0, 0)),
        ],
        out_shape=[
            jax.ShapeDtypeStruct((n, H * LANES), BF16),
            jax.ShapeDtypeStruct((n, H * LANES), BF16),
            jax.ShapeDtypeStruct((batch, ns, W_BR, t), BF16),
        ],
        scratch_shapes=[pltpu.VMEM((1, LANES), F32)],
        compiler_params=_cparams(("parallel", "arbitrary")),
        name="fox_prep",
    )(pw, pw, pw, pn, *wts)


def _attn_kernel(q_ref, k_ref, vt_ref, g_ref, o_ref, m_ref, acc_ref, sa_ref, sb_ref, mxa_ref, mxb_ref, *, tk):
    tq = ATTN_QT * tk
    gi = pl.program_id(2)
    nt = (((1,), (1,)), ((), ()))
    kv_pos = lax.broadcasted_iota(jnp.int32, (tk, tq), 0)
    q_pos = lax.broadcasted_iota(jnp.int32, (tk, tq), 1)
    causal = kv_pos <= q_pos
    both = slice(0, tq)
    ones = jnp.ones((ONES_ROWS, tk), BF16)
    m_ref[...] = jnp.full(m_ref.shape, NEG_INF, F32)
    acc_ref[...] = jnp.zeros(acc_ref.shape, F32)

    def scores(c, s_ref, mx_ref, cols):
        start = pl.multiple_of(c * tk, tk)
        for j in range(2):
            q = q_ref[cols, j * LANES:(j + 1) * LANES]
            k = k_ref[pl.ds(start, tk), j * LANES:(j + 1) * LANES]
            s = lax.dot_general(k, q, nt, preferred_element_type=F32)
            s_ref[j, :, cols] = s
            mx_ref[j, :, cols] = jnp.max(s, axis=0, keepdims=True)

    def consume(c, s_ref, mx_ref, cols, mask):
        for j in range(2):
            vt = jnp.concatenate([vt_ref[0, c, j * HD:(j + 1) * HD, :], ones], axis=0)
            s = s_ref[j, :, cols]
            if mask is not None:
                s = jnp.where(mask, s, NEG_INF)
                m_cur = jnp.max(s, axis=0, keepdims=True)
            else:
                m_cur = mx_ref[j, :, cols]
            m_old = m_ref[j, :, cols]
            m_new = jnp.maximum(m_old, m_cur)
            alpha = jnp.exp2(m_old - m_new)
            p = jnp.exp2(s - m_new)
            acc_ref[j, :, cols] = alpha * acc_ref[j, :, cols] + _dot(vt, p.astype(BF16))
            m_ref[j, :, cols] = m_new

    scores(0, sa_ref, mxa_ref, both)

    def body(i, carry):
        c = 2 * i
        scores(c + 1, sb_ref, mxb_ref, both)
        consume(c, sa_ref, mxa_ref, both, None)
        scores(c + 2, sa_ref, mxa_ref, both)
        consume(c + 1, sb_ref, mxb_ref, both, None)
        return carry

    lax.fori_loop(0, gi * (ATTN_QT // 2), body, 0)
    bufs = ((sa_ref, mxa_ref), (sb_ref, mxb_ref))
    for d in range(ATTN_QT):
        c = ATTN_QT * gi + d
        if d + 1 < ATTN_QT:
            scores(c + 1, *bufs[(d + 1) % 2], slice((d + 1) * tk, tq))
        consume(c, *bufs[d % 2], slice(d * tk, tq), causal[:, :tq - d * tk])

    o_t = jnp.concatenate([acc_ref[j, :HD, :] / acc_ref[j, HD:HD + 1, :] for j in range(2)], axis=0)
    g = g_ref[...].astype(F32)
    o_ref[...] = (o_t.T * (g * _sigmoid(g))).astype(BF16)


def _attention(q, k, vt, pw, gate_col, batch, seq, tk):
    n = q.shape[0]
    tq = ATTN_QT * tk
    nq, nk = seq // tq, seq // tk
    return pl.pallas_call(
        functools.partial(_attn_kernel, tk=tk),
        grid=(batch, H // 2, nq),
        in_specs=[
            pl.BlockSpec((tq, 2 * LANES), lambda b, hp, i: (b * nq + i, hp)),
            pl.BlockSpec((seq, 2 * LANES), lambda b, hp, i: (b, hp)),
            pl.BlockSpec((1, nk, LANES, tk), lambda b, hp, i: (b, 0, hp, 0)),
            pl.BlockSpec((tq, LANES), lambda b, hp, i: (b * nq + i, gate_col // LANES + hp)),
        ],
        out_specs=pl.BlockSpec((tq, LANES), lambda b, hp, i: (b * nq + i, hp)),
        out_shape=jax.ShapeDtypeStruct((n, W_BR), BF16),
        scratch_shapes=[
            pltpu.VMEM((2, 1, tq), F32),
            pltpu.VMEM((2, HD + ONES_ROWS, tq), F32),
            pltpu.VMEM((2, tk, tq), F32),
            pltpu.VMEM((2, tk, tq), F32),
            pltpu.VMEM((2, 1, tq), F32),
            pltpu.VMEM((2, 1, tq), F32),
        ],
        compiler_params=_cparams(("parallel", "parallel", "arbitrary")),
        name="attention",
    )(q, k, vt, pw)


def _rwkv_prep_kernel(*refs, first_layer):
    if first_layer:
        (sr_ref, sk_ref, sv_ref, swa_ref, gc_ref, mur_ref, muk_ref, muv_ref, muwa_ref, w0_ref, a0_ref, wup_ref,
         aup_ref, kk_ref, ka_ref, rk_ref, bt_ref, bo_ref,
         at_o, rt_o, bt_o, kt_o, v_o, bh_o, kh_o, wc_o, bonus_o, sg_o, vfirst_o,
         cr_ref, ck_ref, cv_ref, cwa_ref) = refs
    else:
        (sr_ref, sk_ref, sv_ref, swa_ref, gc_ref, vf_ref, mur_ref, muk_ref, muv_ref, muwa_ref, w0_ref, a0_ref,
         wup_ref, aup_ref, kk_ref, ka_ref, rk_ref, v0_ref, vdown_ref, vup_ref, bt_ref, bo_ref,
         at_o, rt_o, bt_o, kt_o, v_o, bh_o, kh_o, wc_o, bonus_o, sg_o,
         cr_ref, ck_ref, cv_ref, cwa_ref) = refs
    t = sr_ref.shape[0]

    def shift(x_ref, c_ref, mu_ref):
        x = x_ref[...].astype(F32)
        rowid = lax.broadcasted_iota(jnp.int32, x.shape, 0)
        prev = jnp.where(rowid == 0, c_ref[...], pltpu.roll(x, 1, 0))
        c_ref[...] = x[t - 1:t, :]
        return x + mu_ref[...] * (prev - x)

    r = shift(sr_ref, cr_ref, mur_ref)
    k = shift(sk_ref, ck_ref, muk_ref)
    v = shift(sv_ref, cv_ref, muv_ref)
    wa = shift(swa_ref, cwa_ref, muwa_ref)

    logw = -(RWKV_DECAY_SCALE * LOG2E) * _sigmoid(w0_ref[...] + _dot(jnp.tanh(wa).astype(BF16), wup_ref[...]))
    a = _sigmoid(a0_ref[...] + _dot(wa.astype(BF16), aup_ref[...]))
    if first_layer:
        vfirst_o[...] = v
    else:
        low = _dot(v.astype(BF16), vdown_ref[...]).astype(BF16)
        nu = _sigmoid(v0_ref[...] + _dot(low, vup_ref[...]))
        v = v + (vf_ref[...] - v) * nu

    kk = k * kk_ref[...]
    k_mod = k * (1.0 + (a - 1.0) * ka_ref[...])
    rk = r * k_mod * rk_ref[...]
    kk_n, bonus = [], []
    for p in range(H // 2):
        sl = slice(p * LANES, (p + 1) * LANES)
        nrm = jnp.sqrt(_pair_seg_sum(kk[:, sl] * kk[:, sl]))
        kk_n.append(kk[:, sl] / jnp.maximum(nrm, 1e-12))
        bonus.append(_pair_seg_sum(rk[:, sl]) * v[:, sl])
    kk = jnp.concatenate(kk_n, axis=-1)
    bonus = jnp.concatenate(bonus, axis=-1)
    a_vec = -kk
    b_vec = kk * a

    terms = _split3(logw)
    lcum = sum(_dot(bt_ref[...], x) for x in terms)
    ltot = sum(_dot(bo_ref[...], x) for x in terms)
    e_minus = jnp.exp2(-lcum)
    e_rem = jnp.exp2(ltot - lcum)
    g = gc_ref[...].astype(F32)
    for o_ref, val in ((at_o, a_vec * jnp.exp2(lcum - logw)), (rt_o, r * jnp.exp2(lcum)), (bt_o, b_vec * e_minus),
                       (kt_o, k_mod * e_minus), (v_o, v), (bh_o, b_vec * e_rem), (kh_o, k_mod * e_rem),
                       (wc_o, jnp.exp2(ltot)), (bonus_o, bonus), (sg_o, g * _sigmoid(g))):
        o_ref[...] = val.astype(o_ref.dtype)


def _mm(a, b, dims):
    return lax.dot_general(a.astype(BF16), b.astype(BF16), dims, preferred_element_type=F32)


_NN = (((1,), (0,)), ((), ()))
_NT = (((1,), (1,)), ((), ()))
_TN = (((0,), (0,)), ((), ()))


def _rwkv_chunk_kernel(at_ref, rt_ref, bt_ref, kt_ref, v_ref, bh_ref, kh_ref, wc_ref, bonus_ref, sg_ref,
                       lg_ref, lb_ref, o_ref, s_all_ref, *, nchunk):
    mm = _mm
    first = lax.broadcasted_iota(jnp.int32, (CHUNK, LANES), 1) < HD
    r2 = lax.broadcasted_iota(jnp.int32, (LANES, LANES), 0)
    c2 = lax.broadcasted_iota(jnp.int32, (LANES, LANES), 1)
    same_head = (r2 < HD) == (c2 < HD)
    eye2 = r2 == c2
    t_row, t_col = r2 % CHUNK, c2 % CHUNK
    strict2 = jnp.logical_and(same_head, t_col < t_row)
    incl2 = jnp.logical_and(same_head, t_col <= t_row)
    each = lambda f, *cols: [f(*xs) for xs in zip(*cols)]
    pairs = [(c, p) for c in range(nchunk) for p in range(H // 2)]
    npair = range(len(pairs))
    ld = lambda ref: [ref[c * CHUNK:(c + 1) * CHUNK, p * LANES:(p + 1) * LANES] for c, p in pairs]
    at, rt, bt, kt, v, bh, kh = ld(at_ref), ld(rt_ref), ld(bt_ref), ld(kt_ref), ld(v_ref), ld(bh_ref), ld(kh_ref)
    zero = jnp.zeros((CHUNK, LANES), BF16)
    stack = lambda x: jnp.concatenate([jnp.where(first, x, zero), jnp.where(first, zero, x)], axis=0)
    fold = lambda x: x[:CHUNK] + x[CHUNK:]
    a_s, b_s, k_s, v_s = each(stack, at), each(stack, bt), each(stack, kt), each(stack, v)
    x1 = each(lambda a, r: jnp.concatenate([a, stack(r.astype(BF16))], axis=0), a_s, rt)
    gb = each(lambda x, b: mm(x, b, _NT), x1, b_s)
    gk = each(lambda x, k: mm(x, k, _NT), x1, k_s)
    a_ab = [jnp.where(strict2, g[:LANES], 0.0) for g in gb]
    a_rb = [jnp.where(incl2, g[LANES:], 0.0) for g in gb]
    a_ak = [jnp.where(strict2, g[:LANES], 0.0) for g in gk]
    a_rk = [jnp.where(incl2, g[LANES:], 0.0) for g in gk]
    tinv = [jnp.where(eye2, 1.0, a) for a in a_ab]
    pw = a_ab
    for _ in range(int(math.log2(CHUNK)) - 1):
        pw = each(lambda p: mm(p, p, _NN), pw)
        tinv = each(lambda t, p: t + mm(t, p, _NN), tinv, pw)
    av = each(lambda a, x: mm(a, x, _NN), a_ak, v_s)
    tx = each(lambda t, x, a: mm(t, jnp.concatenate([x.astype(BF16), a], axis=1), _NN), tinv, av, a_s)
    ry = each(lambda a, x: mm(a, x, _NN), a_rb, tx)
    yk = each(lambda a, x: mm(a, x, _NN), a_rk, v_s)
    u = [fold(t[:, :LANES]) for t in tx]
    a_til = [fold(t[:, LANES:]) for t in tx]
    y_in = [fold(r[:, :LANES] + k) for r, k in zip(ry, yk)]
    r_hat = [r + fold(x[:, LANES:]) for r, x in zip(rt, ry)]
    m_new = [jnp.where(same_head, mm(a_til[i], bh[i], _TN), 0.0) for i in npair]
    n_new = [jnp.where(same_head, mm(jnp.concatenate([u[i].astype(BF16), v[i]], axis=0),
                                     jnp.concatenate([bh[i], kh[i]], axis=0), _TN), 0.0) for i in npair]
    lg, lb = lg_ref[...], lb_ref[...]
    for c in range(nchunk):
        idx = [i for i, (ci, _) in enumerate(pairs) if ci == c]
        s0 = [s_all_ref[p] for p in range(H // 2)]
        y = [y_in[i] + mm(r_hat[i], s0[p], _NT) for p, i in enumerate(idx)]
        for p, i in enumerate(idx):
            wc = wc_ref[c * CHUNK:c * CHUNK + 1, p * LANES:(p + 1) * LANES]
            s_all_ref[p] = mm(s0[p], jnp.where(eye2, wc, 0.0) + m_new[i], _NN) + n_new[i]
        for p, i in enumerate(idx):
            rows, lanes = slice(c * CHUNK, (c + 1) * CHUNK), slice(p * LANES, (p + 1) * LANES)
            mu = _pair_seg_sum(y[p]) / HD
            d = y[p] - mu
            var = _pair_seg_sum(d * d) / HD
            yn = d * lax.rsqrt(var + GN_EPS) * lg[:, lanes] + lb[:, lanes]
            o_ref[rows, lanes] = ((yn + bonus_ref[rows, lanes]) * sg_ref[rows, lanes]).astype(o_ref.dtype)


_RWKV_STAGE_DTYPES = (BF16, F32, BF16, BF16, BF16, BF16, BF16, F32, F32, F32)


def _rwkv_kernel(*refs, first_layer, nchunk, ns):
    n_in = 18 if first_layer else 22
    ins = refs[:n_in]
    lg_ref, lb_ref, o_ref = refs[n_in:n_in + 3]
    pos = n_in + 3
    vfirst = refs[pos:pos + 1] if first_layer else ()
    pos += len(vfirst)
    stage_a, stage_b = refs[pos:pos + 10], refs[pos + 10:pos + 20]
    carries = refs[pos + 20:pos + 24]
    s_all_ref = refs[pos + 24]
    i = pl.program_id(0)

    def zero(rs):
        for r in rs:
            r[...] = jnp.zeros(r.shape, r.dtype)

    @pl.when(i == 0)
    def _():
        zero(stage_b)

    @pl.when(i % ns == 0)
    def _():
        zero(carries)

    @pl.when((i + ns - 1) % ns == 0)
    def _():
        zero((s_all_ref,))

    def run(done, todo):
        _rwkv_prep_kernel(*ins, *todo, *vfirst, *carries, first_layer=first_layer)
        _rwkv_chunk_kernel(*done, lg_ref, lb_ref, o_ref, s_all_ref, nchunk=nchunk)

    @pl.when(i % 2 == 0)
    def _():
        run(stage_b, stage_a)

    @pl.when(i % 2 == 1)
    def _():
        run(stage_a, stage_b)


def _rwkv(pw, pn, vfirst, wts, lnx_g, lnx_b, batch, seq, t, first_layer):
    n = pw.shape[0]
    ns = seq // t
    nb = batch * ns
    const2 = lambda i: (0, 0)
    cur = lambda i: jnp.minimum(i, nb - 1)
    pcol = lambda col, w: pl.BlockSpec((t, w), lambda i: (cur(i), col // w))
    vec = pl.BlockSpec((1, W_BR), const2)
    in_specs = [pcol(COL_SR, W_BR), pcol(COL_SK, W_BR), pcol(COL_SV, W_BR), pcol(COL_WA, LANES),
                pcol(COL_GATE_C, W_BR)]
    args = [pw, pw, pw, pn, pw]
    if not first_layer:
        in_specs.append(pl.BlockSpec((t, W_BR), lambda i: (cur(i), 0)))
        args.append(vfirst)
    in_specs += [vec, vec, vec, pl.BlockSpec((1, LANES), const2), vec, vec,
                 pl.BlockSpec((LANES, W_BR), const2), pl.BlockSpec((LANES, W_BR), const2), vec, vec, vec]
    if not first_layer:
        in_specs += [vec, pl.BlockSpec((W_BR, LANES), const2), pl.BlockSpec((LANES, W_BR), const2)]
    in_specs += [pl.BlockSpec((t, t), const2)] * 2 + [vec, vec]
    args += list(wts) + [lnx_g, lnx_b]
    out_specs = [pl.BlockSpec((t, W_BR), lambda i: (jnp.maximum(i - 1, 0), 0))]
    out_shape = [jax.ShapeDtypeStruct((n, W_BR), BF16)]
    if first_layer:
        out_specs.append(pl.BlockSpec((t, W_BR), lambda i: (i, 0)))
        out_shape.append(jax.ShapeDtypeStruct((n + t, W_BR), F32))
    return pl.pallas_call(
        functools.partial(_rwkv_kernel, first_layer=first_layer, nchunk=t // CHUNK, ns=ns),
        grid=(nb + 1,),
        in_specs=in_specs,
        out_specs=out_specs,
        out_shape=out_shape,
        scratch_shapes=[pltpu.VMEM((t, W_BR), dt) for dt in _RWKV_STAGE_DTYPES * 2]
        + [pltpu.VMEM((1, W_BR), F32), pltpu.VMEM((1, W_BR), F32), pltpu.VMEM((1, W_BR), F32),
           pltpu.VMEM((1, LANES), F32), pltpu.VMEM((H // 2, LANES, LANES), F32)],
        compiler_params=_cparams(("arbitrary",)),
        name="rwkv",
    )(*args)


def _merge_kernel(oa_ref, ob_ref, oc_ref, ga_ref, gb_ref, gc_ref, x_ref, wpa_ref, wpb_ref, wpc_ref, wout_ref,
                  o_ref):
    pa = _dot(oa_ref[...], wpa_ref[...])
    pb = _dot(ob_ref[...], wpb_ref[...])
    pc = _dot(oc_ref[...], wpc_ref[...])
    sig = lambda ref: _sigmoid(ref[...].astype(F32))
    merged = sig(ga_ref) * pa + sig(gb_ref) * pb + sig(gc_ref) * pc
    o_ref[...] = x_ref[...] + _dot(merged.astype(BF16), wout_ref[...])


def _merge(oa, ob, oc, p, x2, wpa, wpb, wpc, wout, batch, seq, t):
    n = x2.shape[0]
    ns = seq // t
    row = lambda b, i: (b * ns + i, 0)
    const2 = lambda b, i: (0, 0)
    gate = lambda g: pl.BlockSpec((t, D_MODEL), lambda b, i: (b * ns + i, COL_MERGE // D_MODEL + g))
    return pl.pallas_call(
        _merge_kernel,
        grid=(batch, ns),
        in_specs=[
            pl.BlockSpec((t, W_BR), row),
            pl.BlockSpec((t, W_BR), row),
            pl.BlockSpec((t, W_BR), row),
            gate(0), gate(1), gate(2),
            pl.BlockSpec((t, D_MODEL), row),
            pl.BlockSpec((W_BR, D_MODEL), const2),
            pl.BlockSpec((W_BR, D_MODEL), const2),
            pl.BlockSpec((W_BR, D_MODEL), const2),
            pl.BlockSpec((D_MODEL, D_MODEL), const2),
        ],
        out_specs=pl.BlockSpec((t, D_MODEL), row),
        out_shape=jax.ShapeDtypeStruct((n, D_MODEL), F32),
        compiler_params=_cparams(("parallel", "parallel")),
        name="merge",
    )(oa, ob, oc, p, p, p, x2, wpa, wpb, wpc, wout)


def _regroup_pieces():
    sizes = (Q_LORA, KV_LORA, ROPE_D, W_BR, W_BR, W_BR, W_BR, H, W_BR,
             3 * W_BR + DECAY_LORA + AAA_LORA, W_BR, 3 * D_MODEL)
    o = [int(v) for v in np.concatenate([[0], np.cumsum(sizes)])]
    return (
        ("wide", COL_MERGE, o[11], 3 * D_MODEL), ("wide", COL_GATE_A, o[3], 4 * W_BR),
        ("wide", COL_GATE_B, o[8], W_BR), ("wide", COL_SR, o[9], 3 * W_BR), ("wide", COL_GATE_C, o[10], W_BR),
        ("narrow", COL_CQ, o[0], Q_LORA), ("narrow", COL_CKV, o[1], KV_LORA),
        ("narrow", COL_KR + NOPE, o[2], ROPE_D), ("narrow", COL_KR + NOPE + ROPE_D, o[2], ROPE_D // 2),
        ("narrow", COL_FF, o[7], H), ("narrow", COL_WA, o[9] + 3 * W_BR, DECAY_LORA + AAA_LORA),
    )


def _regroup_kernel(w_ref, wide_ref, narrow_ref):
    narrow_ref[...] = jnp.zeros(narrow_ref.shape, narrow_ref.dtype)
    dst = {"wide": wide_ref, "narrow": narrow_ref}
    for name, d0, s0, width in _regroup_pieces():
        dst[name][:, d0:d0 + width] = w_ref[0, :, s0:s0 + width].astype(BF16)


def _regroup_w_in(w_in, l, tr=256):
    d, cols = w_in.shape[1], w_in.shape[2]
    return pl.pallas_call(
        _regroup_kernel,
        grid=(d // tr,),
        in_specs=[pl.BlockSpec((1, tr, cols), lambda i: (l, i, 0))],
        out_specs=[pl.BlockSpec((tr, NCOLS_WIDE), lambda i: (i, 0)), pl.BlockSpec((tr, NCOLS_NARROW), lambda i: (i, 0))],
        out_shape=[jax.ShapeDtypeStruct((d, NCOLS_WIDE), BF16), jax.ShapeDtypeStruct((d, NCOLS_NARROW), BF16)],
        compiler_params=_cparams(("parallel",)),
        name="regroup",
    )(w_in)


def _rope_tables(seq):
    inv =ROPE_THETA ** (-jnp.arange(0, ROPE_D, 2, dtype=F32) / ROPE_D)
    ang = jnp.arange(seq, dtype=F32)[:, None] * inv[None, :]
    cos, sin = jnp.cos(ang), jnp.sin(ang)
    z = lambda n: jnp.zeros((seq, n), F32)
    c = jnp.concatenate([jnp.ones((seq, NOPE), F32), cos, cos, z(LANES - NOPE - ROPE_D)], axis=1)
    s = jnp.concatenate([z(NOPE), -sin, sin, z(LANES - NOPE - ROPE_D)], axis=1)
    return c, s


def _pad_lanes(v, lo, total=LANES):
    return jnp.zeros((1, total), F32).at[0, lo:lo + v.shape[0]].set(v)


def _fox_selectors():
    selq = np.zeros((3, LANES, H * LANES), np.float32)
    selk = np.zeros((3, LANES, H * LANES), np.float32)
    oneq = np.zeros((1, H * LANES), np.float32)
    onek = np.zeros((1, H * LANES), np.float32)
    for h in range(H):
        for j in range(3):
            selq[j, h, h * LANES + HD + j] = 1.0
            selk[j, h, h * LANES + HD + 3 + j] = -1.0
            oneq[0, h * LANES + HD + 3 + j] = 1.0
            onek[0, h * LANES + HD + j] = 1.0
    stacked = lambda sel: jnp.asarray(sel.reshape(3 * LANES, H * LANES), BF16)
    return (stacked(selq), stacked(selk), jnp.asarray(oneq), jnp.asarray(onek))


def _tri(t):
    r = np.arange(t)
    return jnp.asarray((r[None, :] <= r[:, None]).astype(np.float32), BF16)


def _chunk_tri(t):
    r = np.arange(t)
    same = (r[None, :] // CHUNK) == (r[:, None] // CHUNK)
    lower = r[None, :] <= r[:, None]
    return (jnp.asarray((same & lower).astype(np.float32), BF16), jnp.asarray(same.astype(np.float32), BF16))


def _tile(seq, pref):
    return pref if seq % pref == 0 else seq


def kernel(x, norm_g, w_in, mla_qa_g, mla_w_uq, mla_kva_g, mla_w_ukv, mla_q_g, mla_knope_g, mla_krope_g, fox_b_f, fox_q_g, fox_k_g, rwkv_mu, rwkv_w0, rwkv_w_up, rwkv_a0, rwkv_a_up, rwkv_k_k, rwkv_k_a, rwkv_r_k, rwkv_lnx_g, rwkv_lnx_b, rwkv_v0, rwkv_v_down, rwkv_v_up, w_pa, w_pb, w_pc, w_out):
    batch, seq, d = x.shape
    depth = w_in.shape[0]
    n = batch * seq
    t_row = _tile(seq, ROW_TILE)
    t_in = _tile(n, INPROJ_TM)
    t_rwkv = _tile(seq, RWKV_TILE)
    assert seq % (ATTN_QT * t_row) == 0 and seq % t_rwkv == 0 and t_rwkv % CHUNK == 0, seq
    x2 = x.reshape(n, d)

    tabs = _rope_tables(seq)
    selq, selk, oneq, onek = _fox_selectors()
    tri = _tri(t_row)
    btri, bones = _chunk_tri(t_rwkv)
    row = lambda v: v.reshape(1, -1).astype(F32)

    half = ROPE_D // 2
    spare = LANES - NOPE - ROPE_D - half
    with_copy = lambda g, lo: jnp.concatenate([g, g[lo:lo + half]])
    vfirst = None
    for l in range(depth):
        w_wide, w_narrow = _regroup_w_in(w_in, l)
        pw, pn = _inproj(x2, row(norm_g[l]), w_wide, w_narrow, t_in, INPROJ_TN)

        wuq = mla_w_uq[l].reshape(Q_LORA, H, NOPE + ROPE_D)
        wuq = jnp.concatenate([wuq, wuq[:, :, NOPE:NOPE + half], jnp.zeros((Q_LORA, H, spare), F32)], axis=2)
        wuq = wuq.reshape(Q_LORA, H * LANES).astype(BF16)
        wukv = mla_w_ukv[l].reshape(KV_LORA, H, NOPE + HD)
        wk = jnp.pad(wukv[:, :, :NOPE], ((0, 0), (0, 0), (0, LANES - NOPE))).reshape(KV_LORA, H * LANES).astype(BF16)
        wv = wukv[:, :, NOPE:].reshape(KV_LORA, W_BR).T.astype(BF16)
        mla_scale = float(NOPE + ROPE_D) ** -0.5 * LOG2E
        mla_wts = (row(mla_qa_g[l]), wuq, row(mla_kva_g[l]), wk, wv,
                   _pad_lanes(with_copy(mla_q_g[l], NOPE) * mla_scale, 0), _pad_lanes(mla_knope_g[l], 0),
                   _pad_lanes(with_copy(mla_krope_g[l], 0), NOPE))
        qa, ka, va = _mla_prep(pn, tabs, mla_wts, seq, t_row)
        o_a = _attention(qa, ka, va, pw, COL_GATE_A, batch, seq, t_row)

        fox_wts = (_pad_lanes(fox_b_f[l], 0), row(jnp.tile(fox_q_g[l] * (float(HD) ** -0.5 * LOG2E), H)),
                   row(jnp.tile(fox_k_g[l], H)), tri, selq, selk, oneq, onek)
        qb, kb, vb = _fox_prep(pw, pn, fox_wts, batch, seq, t_row)
        o_b = _attention(qb, kb, vb, pw, COL_GATE_B, batch, seq, t_row)

        mu = rwkv_mu[l]
        wup = jnp.zeros((LANES, W_BR), F32).at[:DECAY_LORA].set(rwkv_w_up[l]).astype(BF16)
        aup = jnp.zeros((LANES, W_BR), F32).at[DECAY_LORA:].set(rwkv_a_up[l]).astype(BF16)
        rw = [row(mu[:W_BR]), row(mu[W_BR:2 * W_BR]), row(mu[2 * W_BR:3 * W_BR]), row(mu[3 * W_BR:]),
              row(rwkv_w0[l]), row(rwkv_a0[l]), wup, aup, row(rwkv_k_k[l]), row(rwkv_k_a[l]),
              row(rwkv_r_k[l])]
        if l > 0:
            vdown = jnp.zeros((W_BR, LANES), F32).at[:, :MV_LORA].set(rwkv_v_down[l - 1]).astype(BF16)
            vup = jnp.zeros((LANES, W_BR), F32).at[:MV_LORA].set(rwkv_v_up[l - 1]).astype(BF16)
            rw += [row(rwkv_v0[l - 1]), vdown, vup]
        rw += [btri, bones]
        outs = _rwkv(pw, pn, vfirst, rw, row(rwkv_lnx_g[l]), row(rwkv_lnx_b[l]), batch, seq, t_rwkv,
                     first_layer=(l == 0))
        o_c = outs[0]
        if l == 0:
            vfirst = outs[1]

        x2 = _merge(o_a, o_b, o_c, pw, x2, w_pa[l].astype(BF16), w_pb[l].astype(BF16),
                    w_pc[l].astype(BF16), w_out[l].astype(BF16), batch, seq, _tile(seq, INPROJ_TM))
    return x2.reshape(batch, seq, d)
```

```python
import functools
import math

import jax
import jax.numpy as jnp
import numpy as np
from jax import lax
from jax.experimental import pallas as pl
from jax.experimental.pallas import tpu as pltpu

F32 = jnp.float32
BF16 = jnp.bfloat16

LANES = 128
H = 8
HD = 64
NOPE = 64
ROPE_D = 32
Q_LORA = 256
KV_LORA = 128
D_MODEL = 1024
W_BR = H * HD
DECAY_LORA = 64
AAA_LORA = 64
MV_LORA = 32
ROPE_THETA = 10000.0
RWKV_DECAY_SCALE = 0.606531
GN_EPS = 64e-5
EPS = 1e-6
NEG_INF = -1e30
LOG2E = math.log2(math.e)
CHUNK = 64
ROW_TILE = 512
MERGE_TILE = 1024
RWKV_TILE = 256
ATTN_QT = 4
ONES_ROWS = 16

COL_MERGE = 0
COL_GATE_A = 3072
COL_FQ = 3584
COL_FK = 4096
COL_FV = 4608
COL_GATE_B = 5120
COL_SR = 5632
COL_SK = 6144
COL_SV = 6656
COL_GATE_C = 7168
NCOLS_WIDE = 7680
COL_CQ = 0
COL_CKV = 256
COL_KR = 384
COL_FF = 512
COL_WA = 640
NCOLS_NARROW = 768

VMEM_LIMIT = 56 * 1024 * 1024


def _cparams(sem):
    return pltpu.CompilerParams(dimension_semantics=sem, vmem_limit_bytes=VMEM_LIMIT)


def _sigmoid(x):
    return 0.5 * jnp.tanh(0.5 * x) + 0.5


def _dot(a, b):
    return jnp.dot(a, b, preferred_element_type=F32)


def _split3(x):
    hi = x.astype(BF16)
    r1 = x - hi.astype(F32)
    mid = r1.astype(BF16)
    lo = (r1 - mid.astype(F32)).astype(BF16)
    return hi, mid, lo


def _dot_exact_lhs(m_bf16, x):
    hi, mid, lo = _split3(x)
    return _dot(m_bf16, hi) + _dot(m_bf16, mid) + _dot(m_bf16, lo)


def _pair_seg_sum(x):
    lane = lax.broadcasted_iota(jnp.int32, x.shape, 1)
    first = lane < HD
    s0 = jnp.sum(jnp.where(first, x, 0.0), axis=-1, keepdims=True)
    s1 = jnp.sum(jnp.where(first, 0.0, x), axis=-1, keepdims=True)
    return jnp.where(first, s0, s1)


def _inproj_kernel(x_ref, g_ref, w_ref, wn_ref, o_ref, on_ref):
    x = x_ref[...]
    ms = jnp.mean(x * x, axis=-1, keepdims=True)
    h = (x * lax.rsqrt(ms + EPS) * g_ref[...]).astype(BF16)
    on_ref[...] = _dot(h, wn_ref[...])
    o_ref[...] = _dot(h, w_ref[...]).astype(o_ref.dtype)


def _inproj(x2, g, w_wide, w_narrow, tm):
    n = x2.shape[0]
    resident = lambda cols: pl.BlockSpec((D_MODEL, cols), lambda i: (0, 0), pipeline_mode=pl.Buffered(1))
    return pl.pallas_call(
        _inproj_kernel,
        grid=(n // tm,),
        in_specs=[
            pl.BlockSpec((tm, D_MODEL), lambda i: (i, 0)),
            pl.BlockSpec((1, D_MODEL), lambda i: (0, 0)),
            resident(NCOLS_WIDE),
            resident(NCOLS_NARROW),
        ],
        out_specs=[pl.BlockSpec((tm, NCOLS_WIDE), lambda i: (i, 0)),
                   pl.BlockSpec((tm, NCOLS_NARROW), lambda i: (i, 0))],
        out_shape=[jax.ShapeDtypeStruct((n, NCOLS_WIDE), BF16), jax.ShapeDtypeStruct((n, NCOLS_NARROW), F32)],
        compiler_params=_cparams(("parallel",)),
        name="inproj",
    )(x2, g, w_wide, w_narrow)


def _rms(x, g, n, keep=None):
    sq = x * x
    if keep is not None:
        sq = jnp.where(keep, sq, 0.0)
    ms = jnp.sum(sq, axis=-1, keepdims=True) / n
    return x * lax.rsqrt(ms + EPS) * g


def _rope(x, c, s):
    return x * c + pltpu.roll(x, LANES - ROPE_D // 2, 1) * s


def _mla_prep_kernel(cq_ref, ckv_ref, kr_ref, c_ref, s_ref, qag_ref, wuq_ref, kvag_ref, wk_ref, wv_ref,
                     qg_ref, kng_ref, krg_ref, q_out, k_out, vt_out):
    c, s = c_ref[...], s_ref[...]
    keep = lax.broadcasted_iota(jnp.int32, c.shape, 1) < NOPE + ROPE_D
    cqn = _rms(cq_ref[...], qag_ref[...], Q_LORA).astype(BF16)
    ckvn = _rms(ckv_ref[...], kvag_ref[...], KV_LORA).astype(BF16)
    q_all = _dot(cqn, wuq_ref[...])
    k_all = _dot(ckvn, wk_ref[...])
    vt_out[0, 0] = lax.dot_general(wv_ref[...], ckvn, _NT, preferred_element_type=F32).astype(BF16)
    k_rope = _rope(_rms(kr_ref[...], krg_ref[...], ROPE_D, keep), c, s)
    qg, kng = qg_ref[...], kng_ref[...]
    for h in range(H):
        sl = slice(h * LANES, (h + 1) * LANES)
        q_out[:, sl] = _rope(_rms(q_all[:, sl], qg, NOPE + ROPE_D, keep), c, s).astype(BF16)
        k_out[:, sl] = (_rms(k_all[:, sl], kng, NOPE) + k_rope).astype(BF16)


def _mla_prep(pn, tabs, wts, seq, t):
    n = pn.shape[0]
    ns = seq // t
    row = lambda i: (i, 0)
    const = lambda i: (0, 0)
    tab_spec = pl.BlockSpec((t, LANES), lambda i: (i % ns, 0))
    return pl.pallas_call(
        _mla_prep_kernel,
        grid=(n // t,),
        in_specs=[
            pl.BlockSpec((t, Q_LORA), lambda i: (i, COL_CQ // Q_LORA)),
            pl.BlockSpec((t, LANES), lambda i: (i, COL_CKV // LANES)),
            pl.BlockSpec((t, LANES), lambda i: (i, COL_KR // LANES)),
            tab_spec, tab_spec,
            pl.BlockSpec((1, Q_LORA), const),
            pl.BlockSpec((Q_LORA, H * LANES), const),
            pl.BlockSpec((1, KV_LORA), const),
            pl.BlockSpec((KV_LORA, H * LANES), const),
            pl.BlockSpec((W_BR, KV_LORA), const),
            pl.BlockSpec((1, LANES), const),
            pl.BlockSpec((1, LANES), const),
            pl.BlockSpec((1, LANES), const),
        ],
        out_specs=[
            pl.BlockSpec((t, H * LANES), row),
            pl.BlockSpec((t, H * LANES), row),
            pl.BlockSpec((1, 1, W_BR, t), lambda i: (i // ns, i % ns, 0, 0)),
        ],
        out_shape=[
            jax.ShapeDtypeStruct((n, H * LANES), BF16),
            jax.ShapeDtypeStruct((n, H * LANES), BF16),
            jax.ShapeDtypeStruct((n // seq, ns, W_BR, t), BF16),
        ],
        compiler_params=_cparams(("parallel",)),
        name="mla_prep",
    )(pn, pn, pn, *tabs, *wts)


def _fox_prep_kernel(fq_ref, fk_ref, fv_ref, ff_ref, bf_ref, qg_ref, kg_ref, tri_ref, selq_ref, selk_ref,
                     oneq_ref, onek_ref, q_out, k_out, vt_out, carry_ref):
    t = fq_ref.shape[0]

    @pl.when(pl.program_id(1) == 0)
    def _():
        carry_ref[...] = jnp.zeros_like(carry_ref)

    z = ff_ref[...] + bf_ref[...]
    logf = jnp.minimum(z, 0.0) - jnp.log1p(jnp.exp(-jnp.abs(z)))
    cum = carry_ref[...] + _dot_exact_lhs(tri_ref[...], logf)
    carry_ref[...] = cum[t - 1:t, :]
    hi, mid, lo = _split3(cum * LOG2E)
    terms = jnp.concatenate([hi, mid, lo], axis=1)
    aug_q = _dot(terms, selq_ref[...]) + oneq_ref[...]
    aug_k = _dot(terms, selk_ref[...]) + onek_ref[...]
    vt_out[0, 0] = fv_ref[...].astype(F32).T.astype(BF16)
    lane = lax.broadcasted_iota(jnp.int32, (t, LANES), 1)
    first = lane < HD
    for src_ref, g_ref, aug, dst in ((fq_ref, qg_ref, aug_q, q_out), (fk_ref, kg_ref, aug_k, k_out)):
        for p in range(H // 2):
            sl = slice(p * LANES, (p + 1) * LANES)
            x = src_ref[:, sl].astype(F32)
            ms = _pair_seg_sum(x * x) / HD
            xn = x * lax.rsqrt(ms + EPS) * g_ref[:, sl]
            he, ho = 2 * p, 2 * p + 1
            dst[:, he * LANES:(he + 1) * LANES] = jnp.where(first, xn, aug[:, he * LANES:(he + 1) * LANES]).astype(BF16)
            dst[:, ho * LANES:(ho + 1) * LANES] = jnp.where(
                first, pltpu.roll(xn, HD, 1), aug[:, ho * LANES:(ho + 1) * LANES]).astype(BF16)


def _fox_prep(pw, pn, wts, batch, seq, t):
    n = pw.shape[0]
    ns = seq // t
    row = lambda b, i: (b * ns + i, 0)
    const2 = lambda b, i: (0, 0)
    return pl.pallas_call(
        _fox_prep_kernel,
        grid=(batch, ns),
        in_specs=[
            pl.BlockSpec((t, W_BR), lambda b, i: (b * ns + i, COL_FQ // W_BR)),
            pl.BlockSpec((t, W_BR), lambda b, i: (b * ns + i, COL_FK // W_BR)),
            pl.BlockSpec((t, W_BR), lambda b, i: (b * ns + i, COL_FV // W_BR)),
            pl.BlockSpec((t, LANES), lambda b, i: (b * ns + i, COL_FF // LANES)),
            pl.BlockSpec((1, LANES), const2),
            pl.BlockSpec((1, W_BR), const2),
            pl.BlockSpec((1, W_BR), const2),
            pl.BlockSpec((t, t), const2),
            pl.BlockSpec((3 * LANES, H * LANES), const2),
            pl.BlockSpec((3 * LANES, H * LANES), const2),
            pl.BlockSpec((1, H * LANES), const2),
            pl.BlockSpec((1, H * LANES), const2),
        ],
        out_specs=[
            pl.BlockSpec((t, H * LANES), row),
            pl.BlockSpec((t, H * LANES), row),
            pl.BlockSpec((1, 1, W_BR, t), lambda b, i: (b, i, 0, 0)),
        ],
        out_shape=[
            jax.ShapeDtypeStruct((n, H * LANES), BF16),
            jax.ShapeDtypeStruct((n, H * LANES), BF16),
            jax.ShapeDtypeStruct((batch, ns, W_BR, t), BF16),
        ],
        scratch_shapes=[pltpu.VMEM((1, LANES), F32)],
        compiler_params=_cparams(("parallel", "arbitrary")),
        name="fox_prep",
    )(pw, pw, pw, pn, *wts)


def _attn_kernel(q_ref, k_ref, vt_ref, g_ref, o_ref, m_ref, acc_ref, sa_ref, sb_ref, mxa_ref, mxb_ref, *, tk):
    tq = ATTN_QT * tk
    gi = pl.program_id(2)
    nt = (((1,), (1,)), ((), ()))
    kv_pos = lax.broadcasted_iota(jnp.int32, (tk, tq), 0)
    q_pos = lax.broadcasted_iota(jnp.int32, (tk, tq), 1)
    causal = kv_pos <= q_pos
    both = slice(0, tq)
    ones = jnp.ones((ONES_ROWS, tk), BF16)
    m_ref[...] = jnp.full(m_ref.shape, NEG_INF, F32)
    acc_ref[...] = jnp.zeros(acc_ref.shape, F32)

    def scores(c, s_ref, mx_ref, cols):
        start = pl.multiple_of(c * tk, tk)
        for j in range(2):
            q = q_ref[cols, j * LANES:(j + 1) * LANES]
            k = k_ref[pl.ds(start, tk), j * LANES:(j + 1) * LANES]
            s = lax.dot_general(k, q, nt, preferred_element_type=F32)
            s_ref[j, :, cols] = s
            mx_ref[j, :, cols] = jnp.max(s, axis=0, keepdims=True)

    def consume(c, s_ref, mx_ref, cols, mask):
        for j in range(2):
            vt = jnp.concatenate([vt_ref[0, c, j * HD:(j + 1) * HD, :], ones], axis=0)
            s = s_ref[j, :, cols]
            if mask is not None:
                s = jnp.where(mask, s, NEG_INF)
                m_cur = jnp.max(s, axis=0, keepdims=True)
            else:
                m_cur = mx_ref[j, :, cols]
            m_old = m_ref[j, :, cols]
            m_new = jnp.maximum(m_old, m_cur)
            alpha = jnp.exp2(m_old - m_new)
            p = jnp.exp2(s - m_new)
            acc_ref[j, :, cols] = alpha * acc_ref[j, :, cols] + _dot(vt, p.astype(BF16))
            m_ref[j, :, cols] = m_new

    scores(0, sa_ref, mxa_ref, both)

    def body(i, carry):
        c = 2 * i
        scores(c + 1, sb_ref, mxb_ref, both)
        consume(c, sa_ref, mxa_ref, both, None)
        scores(c + 2, sa_ref, mxa_ref, both)
        consume(c + 1, sb_ref, mxb_ref, both, None)
        return carry

    lax.fori_loop(0, gi * (ATTN_QT // 2), body, 0)
    bufs = ((sa_ref, mxa_ref), (sb_ref, mxb_ref))
    for d in range(ATTN_QT):
        c = ATTN_QT * gi + d
        if d + 1 < ATTN_QT:
            scores(c + 1, *bufs[(d + 1) % 2], slice((d + 1) * tk, tq))
        consume(c, *bufs[d % 2], slice(d * tk, tq), causal[:, :tq - d * tk])

    o_t = jnp.concatenate([acc_ref[j, :HD, :] / acc_ref[j, HD:HD + 1, :] for j in range(2)], axis=0)
    g = g_ref[...].astype(F32)
    o_ref[...] = (o_t.T * (g * _sigmoid(g))).astype(BF16)


def _attention(q, k, vt, pw, gate_col, batch, seq, tk):
    n = q.shape[0]
    tq = ATTN_QT * tk
    nq, nk = seq // tq, seq // tk
    return pl.pallas_call(
        functools.partial(_attn_kernel, tk=tk),
        grid=(batch, H // 2, nq),
        in_specs=[
            pl.BlockSpec((tq, 2 * LANES), lambda b, hp, i: (b * nq + i, hp)),
            pl.BlockSpec((seq, 2 * LANES), lambda b, hp, i: (b, hp)),
            pl.BlockSpec((1, nk, LANES, tk), lambda b, hp, i: (b, 0, hp, 0)),
            pl.BlockSpec((tq, LANES), lambda b, hp, i: (b * nq + i, gate_col // LANES + hp)),
        ],
        out_specs=pl.BlockSpec((tq, LANES), lambda b, hp, i: (b * nq + i, hp)),
        out_shape=jax.ShapeDtypeStruct((n, W_BR), BF16),
        scratch_shapes=[
            pltpu.VMEM((2, 1, tq), F32),
            pltpu.VMEM((2, HD + ONES_ROWS, tq), F32),
            pltpu.VMEM((2, tk, tq), F32),
            pltpu.VMEM((2, tk, tq), F32),
            pltpu.VMEM((2, 1, tq), F32),
            pltpu.VMEM((2, 1, tq), F32),
        ],
        compiler_params=_cparams(("parallel", "parallel", "arbitrary")),
        name="attention",
    )(q, k, vt, pw)


def _rwkv_prep_kernel(*refs, first_layer):
    if first_layer:
        (sr_ref, sk_ref, sv_ref, swa_ref, gc_ref, mur_ref, muk_ref, muv_ref, muwa_ref, w0_ref, a0_ref, wup_ref,
         aup_ref, kk_ref, ka_ref, rk_ref, bt_ref, bo_ref,
         at_o, rt_o, bt_o, kt_o, v_o, bh_o, kh_o, wc_o, bonus_o, sg_o, vfirst_o,
         cr_ref, ck_ref, cv_ref, cwa_ref) = refs
    else:
        (sr_ref, sk_ref, sv_ref, swa_ref, gc_ref, vf_ref, mur_ref, muk_ref, muv_ref, muwa_ref, w0_ref, a0_ref,
         wup_ref, aup_ref, kk_ref, ka_ref, rk_ref, v0_ref, vdown_ref, vup_ref, bt_ref, bo_ref,
         at_o, rt_o, bt_o, kt_o, v_o, bh_o, kh_o, wc_o, bonus_o, sg_o,
         cr_ref, ck_ref, cv_ref, cwa_ref) = refs
    t = sr_ref.shape[0]

    def shift(x_ref, c_ref, mu_ref):
        x = x_ref[...].astype(F32)
        rowid = lax.broadcasted_iota(jnp.int32, x.shape, 0)
        prev = jnp.where(rowid == 0, c_ref[...], pltpu.roll(x, 1, 0))
        c_ref[...] = x[t - 1:t, :]
        return x + mu_ref[...] * (prev - x)

    r = shift(sr_ref, cr_ref, mur_ref)
    k = shift(sk_ref, ck_ref, muk_ref)
    v = shift(sv_ref, cv_ref, muv_ref)
    wa = shift(swa_ref, cwa_ref, muwa_ref)

    logw = -(RWKV_DECAY_SCALE * LOG2E) * _sigmoid(w0_ref[...] + _dot(jnp.tanh(wa).astype(BF16), wup_ref[...]))
    a = _sigmoid(a0_ref[...] + _dot(wa.astype(BF16), aup_ref[...]))
    if first_layer:
        vfirst_o[...] = v
    else:
        low = _dot(v.astype(BF16), vdown_ref[...]).astype(BF16)
        nu = _sigmoid(v0_ref[...] + _dot(low, vup_ref[...]))
        v = v + (vf_ref[...] - v) * nu

    kk = k * kk_ref[...]
    k_mod = k * (1.0 + (a - 1.0) * ka_ref[...])
    rk = r * k_mod * rk_ref[...]
    kk_n, bonus = [], []
    for p in range(H // 2):
        sl = slice(p * LANES, (p + 1) * LANES)
        nrm = jnp.sqrt(_pair_seg_sum(kk[:, sl] * kk[:, sl]))
        kk_n.append(kk[:, sl] / jnp.maximum(nrm, 1e-12))
        bonus.append(_pair_seg_sum(rk[:, sl]) * v[:, sl])
    kk = jnp.concatenate(kk_n, axis=-1)
    bonus = jnp.concatenate(bonus, axis=-1)
    a_vec = -kk
    b_vec = kk * a

    terms = _split3(logw)
    lcum = sum(_dot(bt_ref[...], x) for x in terms)
    ltot = sum(_dot(bo_ref[...], x) for x in terms)
    e_minus = jnp.exp2(-lcum)
    e_rem = jnp.exp2(ltot - lcum)
    g = gc_ref[...].astype(F32)
    for o_ref, val in ((at_o, a_vec * jnp.exp2(lcum - logw)), (rt_o, r * jnp.exp2(lcum)), (bt_o, b_vec * e_minus),
                       (kt_o, k_mod * e_minus), (v_o, v), (bh_o, b_vec * e_rem), (kh_o, k_mod * e_rem),
                       (wc_o, jnp.exp2(ltot)), (bonus_o, bonus), (sg_o, g * _sigmoid(g))):
        o_ref[...] = val.astype(o_ref.dtype)


def _mm(a, b, dims):
    return lax.dot_general(a.astype(BF16), b.astype(BF16), dims, preferred_element_type=F32)


_NN = (((1,), (0,)), ((), ()))
_NT = (((1,), (1,)), ((), ()))
_TN = (((0,), (0,)), ((), ()))


def _rwkv_chunk_kernel(at_ref, rt_ref, bt_ref, kt_ref, v_ref, bh_ref, kh_ref, wc_ref, bonus_ref, sg_ref,
                       lg_ref, lb_ref, o_ref, s_all_ref, *, nchunk):
    mm = _mm
    first = lax.broadcasted_iota(jnp.int32, (CHUNK, LANES), 1) < HD
    r2 = lax.broadcasted_iota(jnp.int32, (LANES, LANES), 0)
    c2 = lax.broadcasted_iota(jnp.int32, (LANES, LANES), 1)
    same_head = (r2 < HD) == (c2 < HD)
    eye2 = r2 == c2
    t_row, t_col = r2 % CHUNK, c2 % CHUNK
    strict2 = jnp.logical_and(same_head, t_col < t_row)
    incl2 = jnp.logical_and(same_head, t_col <= t_row)
    each = lambda f, *cols: [f(*xs) for xs in zip(*cols)]
    pairs = [(c, p) for c in range(nchunk) for p in range(H // 2)]
    npair = range(len(pairs))
    ld = lambda ref: [ref[c * CHUNK:(c + 1) * CHUNK, p * LANES:(p + 1) * LANES] for c, p in pairs]
    at, rt, bt, kt, v, bh, kh = ld(at_ref), ld(rt_ref), ld(bt_ref), ld(kt_ref), ld(v_ref), ld(bh_ref), ld(kh_ref)
    zero = jnp.zeros((CHUNK, LANES), BF16)
    stack = lambda x: jnp.concatenate([jnp.where(first, x, zero), jnp.where(first, zero, x)], axis=0)
    fold = lambda x: x[:CHUNK] + x[CHUNK:]
    a_s, b_s, k_s, v_s = each(stack, at), each(stack, bt), each(stack, kt), each(stack, v)
    x1 = each(lambda a, r: jnp.concatenate([a, stack(r.astype(BF16))], axis=0), a_s, rt)
    gb = each(lambda x, b: mm(x, b, _NT), x1, b_s)
    gk = each(lambda x, k: mm(x, k, _NT), x1, k_s)
    a_ab = [jnp.where(strict2, g[:LANES], 0.0) for g in gb]
    a_rb = [jnp.where(incl2, g[LANES:], 0.0) for g in gb]
    a_ak = [jnp.where(strict2, g[:LANES], 0.0) for g in gk]
    a_rk = [jnp.where(incl2, g[LANES:], 0.0) for g in gk]
    tinv = [jnp.where(eye2, 1.0, a) for a in a_ab]
    pw = a_ab
    for _ in range(int(math.log2(CHUNK)) - 1):
        pw = each(lambda p: mm(p, p, _NN), pw)
        tinv = each(lambda t, p: t + mm(t, p, _NN), tinv, pw)
    av = each(lambda a, x: mm(a, x, _NN), a_ak, v_s)
    tx = each(lambda t, x, a: mm(t, jnp.concatenate([x.astype(BF16), a], axis=1), _NN), tinv, av, a_s)
    ry = each(lambda a, x: mm(a, x, _NN), a_rb, tx)
    yk = each(lambda a, x: mm(a, x, _NN), a_rk, v_s)
    u = [fold(t[:, :LANES]) for t in tx]
    a_til = [fold(t[:, LANES:]) for t in tx]
    y_in = [fold(r[:, :LANES] + k) for r, k in zip(ry, yk)]
    r_hat = [r + fold(x[:, LANES:]) for r, x in zip(rt, ry)]
    m_new = [jnp.where(same_head, mm(a_til[i], bh[i], _TN), 0.0) for i in npair]
    n_new = [jnp.where(same_head, mm(jnp.concatenate([u[i].astype(BF16), v[i]], axis=0),
                                     jnp.concatenate([bh[i], kh[i]], axis=0), _TN), 0.0) for i in npair]
    lg, lb = lg_ref[...], lb_ref[...]
    for c in range(nchunk):
        idx = [i for i, (ci, _) in enumerate(pairs) if ci == c]
        s0 = [s_all_ref[p] for p in range(H // 2)]
        y = [y_in[i] + mm(r_hat[i], s0[p], _NT) for p, i in enumerate(idx)]
        for p, i in enumerate(idx):
            wc = wc_ref[c * CHUNK:c * CHUNK + 1, p * LANES:(p + 1) * LANES]
            s_all_ref[p] = mm(s0[p], jnp.where(eye2, wc, 0.0) + m_new[i], _NN) + n_new[i]
        for p, i in enumerate(idx):
            rows, lanes = slice(c * CHUNK, (c + 1) * CHUNK), slice(p * LANES, (p + 1) * LANES)
            mu = _pair_seg_sum(y[p]) / HD
            d = y[p] - mu
            var = _pair_seg_sum(d * d) / HD
            yn = d * lax.rsqrt(var + GN_EPS) * lg[:, lanes] + lb[:, lanes]
            o_ref[rows, lanes] = ((yn + bonus_ref[rows, lanes]) * sg_ref[rows, lanes]).astype(o_ref.dtype)


_RWKV_STAGE_DTYPES = (BF16, F32, BF16, BF16, BF16, BF16, BF16, F32, F32, F32)


def _rwkv_kernel(*refs, first_layer, nchunk, ns):
    n_in = 18 if first_layer else 22
    ins = refs[:n_in]
    lg_ref, lb_ref, o_ref = refs[n_in:n_in + 3]
    pos = n_in + 3
    vfirst = refs[pos:pos + 1] if first_layer else ()
    pos += len(vfirst)
    stage_a, stage_b = refs[pos:pos + 10], refs[pos + 10:pos + 20]
    carries = refs[pos + 20:pos + 24]
    s_all_ref = refs[pos + 24]
    i = pl.program_id(0)

    def zero(rs):
        for r in rs:
            r[...] = jnp.zeros(r.shape, r.dtype)

    @pl.when(i == 0)
    def _():
        zero(stage_b)

    @pl.when(i % ns == 0)
    def _():
        zero(carries)

    @pl.when((i + ns - 1) % ns == 0)
    def _():
        zero((s_all_ref,))

    def run(done, todo):
        _rwkv_prep_kernel(*ins, *todo, *vfirst, *carries, first_layer=first_layer)
        _rwkv_chunk_kernel(*done, lg_ref, lb_ref, o_ref, s_all_ref, nchunk=nchunk)

    @pl.when(i % 2 == 0)
    def _():
        run(stage_b, stage_a)

    @pl.when(i % 2 == 1)
    def _():
        run(stage_a, stage_b)


def _rwkv(pw, pn, vfirst, wts, lnx_g, lnx_b, batch, seq, t, first_layer):
    n = pw.shape[0]
    ns = seq // t
    nb = batch * ns
    const2 = lambda i: (0, 0)
    cur = lambda i: jnp.minimum(i, nb - 1)
    pcol = lambda col, w: pl.BlockSpec((t, w), lambda i: (cur(i), col // w))
    vec = pl.BlockSpec((1, W_BR), const2)
    in_specs = [pcol(COL_SR, W_BR), pcol(COL_SK, W_BR), pcol(COL_SV, W_BR), pcol(COL_WA, LANES),
                pcol(COL_GATE_C, W_BR)]
    args = [pw, pw, pw, pn, pw]
    if not first_layer:
        in_specs.append(pl.BlockSpec((t, W_BR), lambda i: (cur(i), 0)))
        args.append(vfirst)
    in_specs += [vec, vec, vec, pl.BlockSpec((1, LANES), const2), vec, vec,
                 pl.BlockSpec((LANES, W_BR), const2), pl.BlockSpec((LANES, W_BR), const2), vec, vec, vec]
    if not first_layer:
        in_specs += [vec, pl.BlockSpec((W_BR, LANES), const2), pl.BlockSpec((LANES, W_BR), const2)]
    in_specs += [pl.BlockSpec((t, t), const2)] * 2 + [vec, vec]
    args += list(wts) + [lnx_g, lnx_b]
    out_specs = [pl.BlockSpec((t, W_BR), lambda i: (jnp.maximum(i - 1, 0), 0))]
    out_shape = [jax.ShapeDtypeStruct((n, W_BR), BF16)]
    if first_layer:
        out_specs.append(pl.BlockSpec((t, W_BR), lambda i: (i, 0)))
        out_shape.append(jax.ShapeDtypeStruct((n + t, W_BR), F32))
    return pl.pallas_call(
        functools.partial(_rwkv_kernel, first_layer=first_layer, nchunk=t // CHUNK, ns=ns),
        grid=(nb + 1,),
        in_specs=in_specs,
        out_specs=out_specs,
        out_shape=out_shape,
        scratch_shapes=[pltpu.VMEM((t, W_BR), dt) for dt in _RWKV_STAGE_DTYPES * 2]
        + [pltpu.VMEM((1, W_BR), F32), pltpu.VMEM((1, W_BR), F32), pltpu.VMEM((1, W_BR), F32),
           pltpu.VMEM((1, LANES), F32), pltpu.VMEM((H // 2, LANES, LANES), F32)],
        compiler_params=_cparams(("arbitrary",)),
        name="rwkv",
    )(*args)


def _merge_kernel(oa_ref, ob_ref, oc_ref, ga_ref, gb_ref, gc_ref, x_ref, wpa_ref, wpb_ref, wpc_ref, wout_ref,
                  o_ref):
    pa = _dot(oa_ref[...], wpa_ref[...])
    pb = _dot(ob_ref[...], wpb_ref[...])
    pc = _dot(oc_ref[...], wpc_ref[...])
    sig = lambda ref: _sigmoid(ref[...].astype(F32))
    merged = sig(ga_ref) * pa + sig(gb_ref) * pb + sig(gc_ref) * pc
    o_ref[...] = x_ref[...] + _dot(merged.astype(BF16), wout_ref[...])


def _merge(oa, ob, oc, p, x2, wpa, wpb, wpc, wout, batch, seq, t):
    n = x2.shape[0]
    ns = seq // t
    row = lambda b, i: (b * ns + i, 0)
    const2 = lambda b, i: (0, 0)
    gate = lambda g: pl.BlockSpec((t, D_MODEL), lambda b, i: (b * ns + i, COL_MERGE // D_MODEL + g))
    return pl.pallas_call(
        _merge_kernel,
        grid=(batch, ns),
        in_specs=[
            pl.BlockSpec((t, W_BR), row),
            pl.BlockSpec((t, W_BR), row),
            pl.BlockSpec((t, W_BR), row),
            gate(0), gate(1), gate(2),
            pl.BlockSpec((t, D_MODEL), row),
            pl.BlockSpec((W_BR, D_MODEL), const2),
            pl.BlockSpec((W_BR, D_MODEL), const2),
            pl.BlockSpec((W_BR, D_MODEL), const2),
            pl.BlockSpec((D_MODEL, D_MODEL), const2),
        ],
        out_specs=pl.BlockSpec((t, D_MODEL), row),
        out_shape=jax.ShapeDtypeStruct((n, D_MODEL), F32),
        compiler_params=_cparams(("parallel", "parallel")),
        name="merge",
    )(oa, ob, oc, p, p, p, x2, wpa, wpb, wpc, wout)


def _regroup_pieces():
    sizes = (Q_LORA, KV_LORA, ROPE_D, W_BR, W_BR, W_BR, W_BR, H, W_BR,
             3 * W_BR + DECAY_LORA + AAA_LORA, W_BR, 3 * D_MODEL)
    o = [int(v) for v in np.concatenate([[0], np.cumsum(sizes)])]
    return (
        ("wide", COL_MERGE, o[11], 3 * D_MODEL), ("wide", COL_GATE_A, o[3], 4 * W_BR),
        ("wide", COL_GATE_B, o[8], W_BR), ("wide", COL_SR, o[9], 3 * W_BR), ("wide", COL_GATE_C, o[10], W_BR),
        ("narrow", COL_CQ, o[0], Q_LORA), ("narrow", COL_CKV, o[1], KV_LORA),
        ("narrow", COL_KR + NOPE, o[2], ROPE_D), ("narrow", COL_KR + NOPE + ROPE_D, o[2], ROPE_D // 2),
        ("narrow", COL_FF, o[7], H), ("narrow", COL_WA, o[9] + 3 * W_BR, DECAY_LORA + AAA_LORA),
    )


def _regroup_kernel(w_ref, wide_ref, narrow_ref):
    narrow_ref[...] = jnp.zeros(narrow_ref.shape, narrow_ref.dtype)
    dst = {"wide": wide_ref, "narrow": narrow_ref}
    for name, d0, s0, width in _regroup_pieces():
        dst[name][:, d0:d0 + width] = w_ref[0, :, s0:s0 + width].astype(BF16)


def _regroup_w_in(w_in, l, tr=256):
    d, cols = w_in.shape[1], w_in.shape[2]
    return pl.pallas_call(
        _regroup_kernel,
        grid=(d // tr,),
        in_specs=[pl.BlockSpec((1, tr, cols), lambda i: (l, i, 0))],
        out_specs=[pl.BlockSpec((tr, NCOLS_WIDE), lambda i: (i, 0)), pl.BlockSpec((tr, NCOLS_NARROW), lambda i: (i, 0))],
        out_shape=[jax.ShapeDtypeStruct((d, NCOLS_WIDE), BF16), jax.ShapeDtypeStruct((d, NCOLS_NARROW), BF16)],
        compiler_params=_cparams(("parallel",)),
        name="regroup",
    )(w_in)


def _rope_tables(seq):
    inv =ROPE_THETA ** (-jnp.arange(0, ROPE_D, 2, dtype=F32) / ROPE_D)
    ang = jnp.arange(seq, dtype=F32)[:, None] * inv[None, :]
    cos, sin = jnp.cos(ang), jnp.sin(ang)
    z = lambda n: jnp.zeros((seq, n), F32)
    c = jnp.concatenate([jnp.ones((seq, NOPE), F32), cos, cos, z(LANES - NOPE - ROPE_D)], axis=1)
    s = jnp.concatenate([z(NOPE), -sin, sin, z(LANES - NOPE - ROPE_D)], axis=1)
    return c, s


def _pad_lanes(v, lo, total=LANES):
    return jnp.zeros((1, total), F32).at[0, lo:lo + v.shape[0]].set(v)


def _fox_selectors():
    selq = np.zeros((3, LANES, H * LANES), np.float32)
    selk = np.zeros((3, LANES, H * LANES), np.float32)
    oneq = np.zeros((1, H * LANES), np.float32)
    onek = np.zeros((1, H * LANES), np.float32)
    for h in range(H):
        for j in range(3):
            selq[j, h, h * LANES + HD + j] = 1.0
            selk[j, h, h * LANES + HD + 3 + j] = -1.0
            oneq[0, h * LANES + HD + 3 + j] = 1.0
            onek[0, h * LANES + HD + j] = 1.0
    stacked = lambda sel: jnp.asarray(sel.reshape(3 * LANES, H * LANES), BF16)
    return (stacked(selq), stacked(selk), jnp.asarray(oneq), jnp.asarray(onek))


def _tri(t):
    r = np.arange(t)
    return jnp.asarray((r[None, :] <= r[:, None]).astype(np.float32), BF16)


def _chunk_tri(t):
    r = np.arange(t)
    same = (r[None, :] // CHUNK) == (r[:, None] // CHUNK)
    lower = r[None, :] <= r[:, None]
    return (jnp.asarray((same & lower).astype(np.float32), BF16), jnp.asarray(same.astype(np.float32), BF16))


def _tile(seq, pref):
    return pref if seq % pref == 0 else seq


def kernel(x, norm_g, w_in, mla_qa_g, mla_w_uq, mla_kva_g, mla_w_ukv, mla_q_g, mla_knope_g, mla_krope_g, fox_b_f, fox_q_g, fox_k_g, rwkv_mu, rwkv_w0, rwkv_w_up, rwkv_a0, rwkv_a_up, rwkv_k_k, rwkv_k_a, rwkv_r_k, rwkv_lnx_g, rwkv_lnx_b, rwkv_v0, rwkv_v_down, rwkv_v_up, w_pa, w_pb, w_pc, w_out):
    batch, seq, d = x.shape
    depth = w_in.shape[0]
    n = batch * seq
    t_row = _tile(seq, ROW_TILE)
    t_rwkv = _tile(seq, RWKV_TILE)
    assert seq % (ATTN_QT * t_row) == 0 and seq % t_rwkv == 0 and t_rwkv % CHUNK == 0, seq
    x2 = x.reshape(n, d)

    tabs = _rope_tables(seq)
    selq, selk, oneq, onek = _fox_selectors()
    tri = _tri(t_row)
    btri, bones = _chunk_tri(t_rwkv)
    row = lambda v: v.reshape(1, -1).astype(F32)

    half = ROPE_D // 2
    spare = LANES - NOPE - ROPE_D - half
    with_copy = lambda g, lo: jnp.concatenate([g, g[lo:lo + half]])
    vfirst = None
    for l in range(depth):
        w_wide, w_narrow = _regroup_w_in(w_in, l)
        pw, pn = _inproj(x2, row(norm_g[l]), w_wide, w_narrow, _tile(n, ROW_TILE))

        wuq = mla_w_uq[l].reshape(Q_LORA, H, NOPE + ROPE_D)
        wuq = jnp.concatenate([wuq, wuq[:, :, NOPE:NOPE + half], jnp.zeros((Q_LORA, H, spare), F32)], axis=2)
        wuq = wuq.reshape(Q_LORA, H * LANES).astype(BF16)
        wukv = mla_w_ukv[l].reshape(KV_LORA, H, NOPE + HD)
        wk = jnp.pad(wukv[:, :, :NOPE], ((0, 0), (0, 0), (0, LANES - NOPE))).reshape(KV_LORA, H * LANES).astype(BF16)
        wv = wukv[:, :, NOPE:].reshape(KV_LORA, W_BR).T.astype(BF16)
        mla_scale = float(NOPE + ROPE_D) ** -0.5 * LOG2E
        mla_wts = (row(mla_qa_g[l]), wuq, row(mla_kva_g[l]), wk, wv,
                   _pad_lanes(with_copy(mla_q_g[l], NOPE) * mla_scale, 0), _pad_lanes(mla_knope_g[l], 0),
                   _pad_lanes(with_copy(mla_krope_g[l], 0), NOPE))
        qa, ka, va = _mla_prep(pn, tabs, mla_wts, seq, t_row)
        o_a = _attention(qa, ka, va, pw, COL_GATE_A, batch, seq, t_row)

        fox_wts = (_pad_lanes(fox_b_f[l], 0), row(jnp.tile(fox_q_g[l] * (float(HD) ** -0.5 * LOG2E), H)),
                   row(jnp.tile(fox_k_g[l], H)), tri, selq, selk, oneq, onek)
        qb, kb, vb = _fox_prep(pw, pn, fox_wts, batch, seq, t_row)
        o_b = _attention(qb, kb, vb, pw, COL_GATE_B, batch, seq, t_row)

        mu = rwkv_mu[l]
        wup = jnp.zeros((LANES, W_BR), F32).at[:DECAY_LORA].set(rwkv_w_up[l]).astype(BF16)
        aup = jnp.zeros((LANES, W_BR), F32).at[DECAY_LORA:].set(rwkv_a_up[l]).astype(BF16)
        rw = [row(mu[:W_BR]), row(mu[W_BR:2 * W_BR]), row(mu[2 * W_BR:3 * W_BR]), row(mu[3 * W_BR:]),
              row(rwkv_w0[l]), row(rwkv_a0[l]), wup, aup, row(rwkv_k_k[l]), row(rwkv_k_a[l]),
              row(rwkv_r_k[l])]
        if l > 0:
            vdown = jnp.zeros((W_BR, LANES), F32).at[:, :MV_LORA].set(rwkv_v_down[l - 1]).astype(BF16)
            vup = jnp.zeros((LANES, W_BR), F32).at[:MV_LORA].set(rwkv_v_up[l - 1]).astype(BF16)
            rw += [row(rwkv_v0[l - 1]), vdown, vup]
        rw += [btri, bones]
        outs = _rwkv(pw, pn, vfirst, rw, row(rwkv_lnx_g[l]), row(rwkv_lnx_b[l]), batch, seq, t_rwkv,
                     first_layer=(l == 0))
        o_c = outs[0]
        if l == 0:
            vfirst = outs[1]

        x2 = _merge(o_a, o_b, o_c, pw, x2, w_pa[l].astype(BF16), w_pb[l].astype(BF16),
                    w_pc[l].astype(BF16), w_out[l].astype(BF16), batch, seq, _tile(seq, MERGE_TILE))
    return x2.reshape(batch, seq, d)
```
